```python
import jax, jax.numpy as jnp
from jax import lax
import numpy as np

D_MODEL = 1024
BATCH = 8
SEQ = 4096
DEPTH = 4

GRID_W = 64
CTX_LEN = 256
EPS = 1e-6
HEAD_DIM = 64
ATT_HEADS = D_MODEL // 128
ATT_KV_HEADS = ATT_HEADS // 4
ATT_GROUP = ATT_HEADS // ATT_KV_HEADS
WINDOW = 128
BLOCK = 128
ROPE_THETA = 10000.0
ATT_W = ATT_HEADS * HEAD_DIM
KV_W = ATT_KV_HEADS * HEAD_DIM
GLA_HEADS = D_MODEL // 256
GLA_DK = 64
GLA_DV = 64
GLA_LOWRANK = 16
GLA_TAU = 16.0
GLA_CHUNK = 64
GLA_KW = GLA_HEADS * GLA_DK
GLA_VW = GLA_HEADS * GLA_DV
CONV_CH = D_MODEL // 4
CONV_K = 3
MIX_W = ATT_W + GLA_VW + CONV_CH
IN_SIZES = (ATT_W, KV_W, KV_W, GLA_KW, GLA_KW, GLA_VW, GLA_VW, GLA_VW, GLA_LOWRANK, GLA_LOWRANK, CONV_CH, CONV_CH, CONV_CH)
IN_W = ATT_W + 2 * KV_W + 2 * GLA_KW + 3 * GLA_VW + 2 * GLA_LOWRANK + 3 * CONV_CH
D_FF = ((8 * D_MODEL // 3 + 127) // 128) * 128
N_EXPERTS = 8
TOP_K = 2
D_FF_EXPERT = D_FF // 2
N_DENSE = (DEPTH + 1) // 2
N_MOE = DEPTH // 2

kernel_name = 'hybrid_parallel_heads_dit_block'

F32 = jnp.float32


def rmsnorm(t, g):
    t32 = t.astype(F32)
    return (t32 * lax.rsqrt(jnp.mean(t32 * t32, axis=-1, keepdims=True) + EPS) * g.astype(F32)).astype(t.dtype)


def axial_rope(rows, cols):
    nf = HEAD_DIM // 4
    inv = ROPE_THETA ** (-jnp.arange(nf, dtype=F32) / nf)
    ang = jnp.concatenate([rows.astype(F32)[:, None] * inv, cols.astype(F32)[:, None] * inv], axis=-1)
    return jnp.cos(ang), jnp.sin(ang)


def apply_rope(t, cos, sin):
    half = HEAD_DIM // 2
    t1, t2 = t[..., :half], t[..., half:]
    c, s = cos[None, :, None, :], sin[None, :, None, :]
    return jnp.concatenate([t1 * c - t2 * s, t2 * c + t1 * s], axis=-1).astype(t.dtype)


def softmax_with_sink(scores, sink):
    sk = jnp.broadcast_to(sink.astype(F32)[:, :, None, None], scores.shape[:-1] + (1,))
    p = jax.nn.softmax(jnp.concatenate([sk, scores], axis=-1), axis=-1)
    return p[..., 1:]


def window_attention(q, k, v, k_ctx, v_ctx, sink):
    B, S = q.shape[:2]
    L = k_ctx.shape[1]
    nb = S // BLOCK
    qb = q.reshape(B, nb, BLOCK, ATT_KV_HEADS, ATT_GROUP, HEAD_DIM) * HEAD_DIM ** -0.5
    pad = ((0, 0), (BLOCK, BLOCK), (0, 0), (0, 0))
    kp = jnp.pad(k, pad).reshape(B, nb + 2, BLOCK, ATT_KV_HEADS, HEAD_DIM)
    vp = jnp.pad(v, pad).reshape(B, nb + 2, BLOCK, ATT_KV_HEADS, HEAD_DIM)
    kw = jnp.concatenate([kp[:, :-2], kp[:, 1:-1], kp[:, 2:]], axis=2)
    vw = jnp.concatenate([vp[:, :-2], vp[:, 1:-1], vp[:, 2:]], axis=2)
    qpos = jnp.arange(S).reshape(nb, BLOCK)
    kpos = (jnp.arange(nb)[:, None] - 1) * BLOCK + jnp.arange(3 * BLOCK)[None, :]
    valid = ((kpos[:, None, :] >= 0) & (kpos[:, None, :] < S)
             & (jnp.abs(qpos[:, :, None] - kpos[:, None, :]) <= WINDOW))
    s_loc = jnp.einsum('bnqhgd,bnkhd->bnhgqk', qb, kw).astype(F32)
    s_loc = jnp.where(valid[None, :, None, None], s_loc, -jnp.inf)
    s_ctx = jnp.einsum('bnqhgd,bkhd->bnhgqk', qb, k_ctx).astype(F32)
    p = softmax_with_sink(jnp.concatenate([s_ctx, s_loc], axis=-1),
                          sink.reshape(ATT_KV_HEADS, ATT_GROUP)).astype(v.dtype)
    o = (jnp.einsum('bnhgqk,bkhd->bnqhgd', p[..., :L], v_ctx)
         + jnp.einsum('bnhgqk,bnkhd->bnqhgd', p[..., L:], vw))
    return o.reshape(B, S, ATT_W)


def context_attention(q, k, v, sink):
    B, L = q.shape[:2]
    qg = q.reshape(B, L, ATT_KV_HEADS, ATT_GROUP, HEAD_DIM) * HEAD_DIM ** -0.5
    s = jnp.einsum('bqhgd,bkhd->bhgqk', qg, k).astype(F32)
    p = softmax_with_sink(s, sink.reshape(ATT_KV_HEADS, ATT_GROUP)).astype(v.dtype)
    return jnp.einsum('bhgqk,bkhd->bqhgd', p, v).reshape(B, L, ATT_W)


def gla_chunked(q, k, v, log_a, s0):
    B, T, H, DK = q.shape
    DV = v.shape[-1]
    C = GLA_CHUNK
    n = T // C
    q32 = q.astype(F32).reshape(B, n, C, H, DK)
    k32 = k.astype(F32).reshape(B, n, C, H, DK)
    v32 = v.astype(F32).reshape(B, n, C, H, DV)
    b = jnp.cumsum(log_a.astype(F32).reshape(B, n, C, H, DK), axis=2)
    b_ref = b[:, :, C // 2:C // 2 + 1]
    b_last = b[:, :, -1]
    lower = jnp.tril(jnp.ones((C, C), bool))
    a = jnp.einsum('bnthd,bnshd->bnhts', q32 * jnp.exp(b - b_ref), k32 * jnp.exp(b_ref - b))
    a = jnp.where(lower, a, 0.0)
    o = jnp.einsum('bnhts,bnshe->bnthe', a, v32)
    u = jnp.einsum('bnshd,bnshe->bnhde', k32 * jnp.exp(b_last[:, :, None] - b), v32)

    def step(s, xs):
        d, ui = xs
        return d[..., None] * s + ui, s

    s_fin, s_in = lax.scan(step, s0.astype(F32), (jnp.moveaxis(jnp.exp(b_last), 1, 0), jnp.moveaxis(u, 1, 0)))
    s_in = jnp.moveaxis(s_in, 0, 1)
    o = o + jnp.einsum('bnthd,bnhde->bnthe', q32 * jnp.exp(b), s_in)
    return o.reshape(B, T, H, DV).astype(v.dtype), s_fin


def gla_final_state(k, v, log_a):
    b = jnp.cumsum(log_a.astype(F32), axis=1)
    w = jnp.exp(b[:, -1:] - b)
    return jnp.einsum('bthd,bthe->bhde', k.astype(F32) * w, v.astype(F32))


def head_rmsnorm(o, g):
    o32 = o.astype(F32)
    return (o32 * lax.rsqrt(jnp.mean(o32 * o32, axis=-1, keepdims=True) + EPS) * g.astype(F32)).astype(o.dtype)


def gla_merge(o_f, o_b, g_f, g_b, norm_g):
    B, T = g_f.shape[:2]
    gf = jax.nn.silu(g_f).reshape(B, T, GLA_HEADS, GLA_DV)
    gb = jax.nn.silu(g_b).reshape(B, T, GLA_HEADS, GLA_DV)
    y = head_rmsnorm(o_f, norm_g) * gf + head_rmsnorm(o_b, norm_g) * gb
    return y.reshape(B, T, GLA_VW)


def short_conv(u, w):
    return lax.conv_general_dilated(u, w[:, None, :].astype(u.dtype), window_strides=(1,),
                                    padding=((CONV_K // 2, CONV_K // 2),),
                                    dimension_numbers=('NWC', 'WIO', 'NWC'),
                                    feature_group_count=u.shape[-1])


def token_mixers(hc, hl, w_in, w_out, sink, w2_f, b_f, w2_b, b_b, gla_g, conv_w, cos, sin, ctx_out):
    B, L, _ = hc.shape
    T = L + hl.shape[1]
    offs = np.cumsum(np.array(IN_SIZES))[:-1].tolist()
    proj = jnp.concatenate([hc, hl], axis=1) @ w_in
    (a_q, a_k, a_v, l_q, l_k, l_v, l_gf, l_gb, l_af, l_ab, s_b, s_c, s_x) = jnp.split(proj, offs, axis=-1)
    flip = lambda t: jnp.flip(t, axis=1)

    q = a_q.reshape(B, T, ATT_HEADS, HEAD_DIM)
    k = a_k.reshape(B, T, ATT_KV_HEADS, HEAD_DIM)
    v = a_v.reshape(B, T, ATT_KV_HEADS, HEAD_DIM)
    att_lat = window_attention(apply_rope(q[:, L:], cos, sin), apply_rope(k[:, L:], cos, sin), v[:, L:],
                               k[:, :L], v[:, :L], sink)

    gq = l_q.reshape(B, T, GLA_HEADS, GLA_DK) * GLA_DK ** -0.5
    gk = l_k.reshape(B, T, GLA_HEADS, GLA_DK)
    gv = l_v.reshape(B, T, GLA_HEADS, GLA_DV)
    la_f = (jax.nn.log_sigmoid((l_af @ w2_f + b_f).astype(F32)) / GLA_TAU).reshape(B, T, GLA_HEADS, GLA_DK)
    la_b = (jax.nn.log_sigmoid((l_ab @ w2_b + b_b).astype(F32)) / GLA_TAU).reshape(B, T, GLA_HEADS, GLA_DK)
    zero = jnp.zeros((B, GLA_HEADS, GLA_DK, GLA_DV), F32)
    if ctx_out:
        oc_f, st_f = gla_chunked(gq[:, :L], gk[:, :L], gv[:, :L], la_f[:, :L], zero)
        oc_b, st_b = gla_chunked(flip(gq[:, :L]), flip(gk[:, :L]), flip(gv[:, :L]), flip(la_b[:, :L]), zero)
        oc_b = flip(oc_b)
    else:
        st_f = gla_final_state(gk[:, :L], gv[:, :L], la_f[:, :L])
        st_b = gla_final_state(flip(gk[:, :L]), flip(gv[:, :L]), flip(la_b[:, :L]))
    ol_f, _ = gla_chunked(gq[:, L:], gk[:, L:], gv[:, L:], la_f[:, L:], st_f)
    ol_b, _ = gla_chunked(flip(gq[:, L:]), flip(gk[:, L:]), flip(gv[:, L:]), flip(la_b[:, L:]), st_b)
    gla_lat = gla_merge(ol_f, flip(ol_b), l_gf[:, L:], l_gb[:, L:], gla_g)

    conv_lat = s_b[:, L:] * short_conv(s_c[:, L:] * s_x[:, L:], conv_w)

    lat = jnp.concatenate([att_lat, gla_lat, conv_lat], axis=-1)
    if not ctx_out:
        return lat @ w_out, None
    att_ctx = context_attention(q[:, :L], k[:, :L], v[:, :L], sink)
    gla_ctx = gla_merge(oc_f, oc_b, l_gf[:, :L], l_gb[:, :L], gla_g)
    conv_ctx = s_b[:, :L] * short_conv(s_c[:, :L] * s_x[:, :L], conv_w)
    cmix = jnp.concatenate([att_ctx, gla_ctx, conv_ctx], axis=-1)
    y = jnp.concatenate([cmix, lat], axis=1) @ w_out
    return y[:, L:], y[:, :L]


def swiglu(t, w_gu, w_down):
    g, u = jnp.split(t @ w_gu, 2, axis=-1)
    return (jax.nn.silu(g) * u) @ w_down


def moe_swiglu(h, router, w_gu, w_down):
    shp = h.shape
    t = h.reshape(-1, shp[-1])
    logits = (t @ router).astype(F32)
    top_v, top_i = lax.top_k(logits, TOP_K)
    gates = jax.nn.softmax(top_v, axis=-1)
    combine = jnp.sum(jax.nn.one_hot(top_i, N_EXPERTS, dtype=F32) * gates[..., None], axis=1).astype(t.dtype)
    y = jnp.zeros_like(t)
    for e in range(N_EXPERTS):
        y = y + combine[:, e:e + 1] * swiglu(t, w_gu[e], w_down[e])
    return y.reshape(shp)


def setup_inputs(seed: int = 0) -> dict:
    key = jax.random.key(seed)
    ks = jax.random.split(key, 24)
    D = D_MODEL
    nrm = lambda k, shape, s: jax.random.normal(k, shape, F32) * s
    return {
        'x': nrm(ks[0], (BATCH, SEQ, D), 1.0),
        'c': nrm(ks[1], (BATCH, D), 1.0),
        'ctx': nrm(ks[2], (BATCH, CTX_LEN, D), 1.0),
        'c_ctx': nrm(ks[3], (D,), 1.0),
        'w_mod': nrm(ks[4], (DEPTH, D, 6 * D), 0.5 * D ** -0.5),
        'b_mod': nrm(ks[5], (DEPTH, 6 * D), 0.02),
        'norm1_g': 1.0 + nrm(ks[6], (DEPTH, D), 0.05),
        'norm2_g': 1.0 + nrm(ks[7], (DEPTH, D), 0.05),
        'w_in': nrm(ks[8], (DEPTH, D, IN_W), D ** -0.5),
        'w_out': nrm(ks[9], (DEPTH, MIX_W, D), MIX_W ** -0.5),
        'attn_sink': nrm(ks[10], (DEPTH, ATT_HEADS), 0.5),
        'gla_w2_f': nrm(ks[11], (DEPTH, GLA_LOWRANK, GLA_KW), GLA_LOWRANK ** -0.5),
        'gla_b_f': nrm(ks[12], (DEPTH, GLA_KW), 0.5),
        'gla_w2_b': nrm(ks[13], (DEPTH, GLA_LOWRANK, GLA_KW), GLA_LOWRANK ** -0.5),
        'gla_b_b': nrm(ks[14], (DEPTH, GLA_KW), 0.5),
        'gla_norm_g': 1.0 + nrm(ks[15], (DEPTH, GLA_DV), 0.05),
        'conv_w': nrm(ks[16], (DEPTH, CONV_K, CONV_CH), CONV_K ** -0.5),
        'ffn_w_gu': nrm(ks[17], (N_DENSE, D, 2 * D_FF), D ** -0.5),
        'ffn_w_down': nrm(ks[18], (N_DENSE, D_FF, D), D_FF ** -0.5),
        'router_w': nrm(ks[19], (N_MOE, D, N_EXPERTS), D ** -0.5),
        'expert_w_gu': nrm(ks[20], (N_MOE, N_EXPERTS, D, 2 * D_FF_EXPERT), D ** -0.5),
        'expert_w_down': nrm(ks[21], (N_MOE, N_EXPERTS, D_FF_EXPERT, D), D_FF_EXPERT ** -0.5),
        'final_norm_g': 1.0 + nrm(ks[22], (D,), 0.05),
    }


def reference(x, c, ctx, c_ctx, w_mod, b_mod, norm1_g, norm2_g, w_in, w_out, attn_sink,
              gla_w2_f, gla_b_f, gla_w2_b, gla_b_b, gla_norm_g, conv_w,
              ffn_w_gu, ffn_w_down, router_w, expert_w_gu, expert_w_down, final_norm_g):
    S = x.shape[1]
    L = ctx.shape[1]
    ROWS = S // GRID_W
    rows = jnp.repeat(jnp.arange(ROWS, dtype=jnp.int32), GRID_W)
    cols = jnp.tile(jnp.arange(GRID_W, dtype=jnp.int32), ROWS)
    cos, sin = axial_rope(rows, cols)
    silu_c = jax.nn.silu(c)
    silu_cc = jax.nn.silu(c_ctx)
    xl, xc = x, ctx
    for l in range(DEPTH):
        last = l == DEPTH - 1
        mod_l = (silu_c @ w_mod[l] + b_mod[l])[:, None, :]
        mod_c = silu_cc @ w_mod[l] + b_mod[l]
        sh1, sc1, g1, sh2, sc2, g2 = jnp.split(mod_l, 6, axis=-1)
        ch1, cs1, cg1, ch2, cs2, cg2 = jnp.split(mod_c, 6, axis=-1)
        hl = rmsnorm(xl, norm1_g[l]) * (1 + sc1) + sh1
        hc = rmsnorm(xc, norm1_g[l]) * (1 + cs1) + ch1
        yl, yc = token_mixers(hc, hl, w_in[l], w_out[l], attn_sink[l], gla_w2_f[l], gla_b_f[l],
                              gla_w2_b[l], gla_b_b[l], gla_norm_g[l], conv_w[l], cos, sin, not last)
        xl = xl + g1 * yl
        if not last:
            xc = xc + cg1 * yc
        hl = rmsnorm(xl, norm2_g[l]) * (1 + sc2) + sh2
        if last:
            h2 = hl
        else:
            hc = rmsnorm(xc, norm2_g[l]) * (1 + cs2) + ch2
            h2 = jnp.concatenate([hc, hl], axis=1)
        if l % 2 == 0:
            f = swiglu(h2, ffn_w_gu[l // 2], ffn_w_down[l // 2])
        else:
            f = moe_swiglu(h2, router_w[l // 2], expert_w_gu[l // 2], expert_w_down[l // 2])
        xl = xl + g2 * f[:, h2.shape[1] - S:]
        if not last:
            xc = xc + cg2 * f[:, :L]
    return rmsnorm(xl, final_norm_g)
```

```python
import functools

import jax
import jax.numpy as jnp
from jax import lax
from jax.experimental import pallas as pl
from jax.experimental.pallas import tpu as pltpu

F32 = jnp.float32
BF16 = jnp.bfloat16

D_MODEL = 1024
EPS = 1e-6
GRID_W = 64
ROPE_THETA = 10000.0
HEAD_DIM = 64
ATT_HEADS = 8
ATT_KV_HEADS = 2
ATT_GROUP = ATT_HEADS // ATT_KV_HEADS
ATT_W = ATT_HEADS * HEAD_DIM
KV_W = ATT_KV_HEADS * HEAD_DIM
WINDOW = 128
GLA_HEADS = 4
GLA_DK = 64
GLA_W = GLA_HEADS * GLA_DK
GLA_LOWRANK = 16
GLA_TAU = 16.0
GLA_CHUNK = 64
CONV_CH = 256
N_EXPERTS = 8

LANES = 128
SUBLANES_BF16 = 16

COL_Q = 0
COL_K = ATT_W
COL_V = COL_K + KV_W
ROPE_W = COL_V
COL_GQ = COL_V + KV_W
COL_GK = COL_GQ + GLA_W
COL_GV = COL_GK + GLA_W
COL_GGF = COL_GV + GLA_W
COL_GGB = COL_GGF + GLA_W
COL_CB = COL_GGB + GLA_W
COL_CC = COL_CB + CONV_CH
COL_CX = COL_CC + CONV_CH
COL_LR = COL_CX + CONV_CH
PROJ_W = COL_LR + LANES

SEQ_TILE = 256
VMEM_LIMIT = 56 * 1024 * 1024


def _silu(v):
    return v / (1.0 + jnp.exp(-v))


def _pick_tile(total, cap):
    best = None
    for t in range(SUBLANES_BF16, cap + 1, SUBLANES_BF16):
        if total % t == 0:
            best = t
    assert best is not None
    return best


def _params(sem):
    return pltpu.CompilerParams(dimension_semantics=sem, vmem_limit_bytes=VMEM_LIMIT)


def _adaln_kernel(c_ref, w_ref, b_ref, o_ref):
    s = _silu(c_ref[...]).astype(BF16)
    o_ref[...] = jnp.dot(s, w_ref[...].astype(BF16), preferred_element_type=F32) + b_ref[...]


def _adaln(cc, w_mod, b_mod):
    depth, d, n = w_mod.shape
    tn = 1024
    rows = cc.shape[0]
    return pl.pallas_call(
        _adaln_kernel,
        grid=(depth, n // tn),
        in_specs=[pl.BlockSpec((rows, d), lambda l, j: (0, 0)),
                  pl.BlockSpec((None, d, tn), lambda l, j: (l, 0, j)),
                  pl.BlockSpec((None, 1, tn), lambda l, j: (l, 0, j))],
        out_specs=pl.BlockSpec((None, rows, tn), lambda l, j: (l, 0, j)),
        out_shape=jax.ShapeDtypeStruct((depth, rows, n), F32),
        compiler_params=_params(("parallel", "parallel")),
    )(cc, w_mod, b_mod.reshape(depth, 1, n))


def _mod_specs(k, d):
    return [pl.BlockSpec((None, 1, d), lambda b, j, *_: (b, 0, k)),
            pl.BlockSpec((1, d), lambda b, j, *_: (0, k))]


def _row_is_ctx(j, tm, ctx_len):
    return (j * tm + lax.broadcasted_iota(jnp.int32, (tm, 1), 0)) < ctx_len


def _inproj_kernel(x_ref, g_ref, sc_ref, csc_ref, sh_ref, csh_ref, cos_ref, sin_ref, w_ref, o_ref, *, tm, ctx_len):
    x = x_ref[...]
    n = x * lax.rsqrt(jnp.mean(x * x, axis=-1, keepdims=True) + EPS) * g_ref[...]
    is_ctx = _row_is_ctx(pl.program_id(1), tm, ctx_len)
    scale = jnp.where(is_ctx, csc_ref[...], sc_ref[...])
    shift = jnp.where(is_ctx, csh_ref[...], sh_ref[...])
    h = (n * (1.0 + scale) + shift).astype(BF16)
    qk = jnp.dot(h, w_ref[:, :ROPE_W], preferred_element_type=F32)
    lower_half = (lax.broadcasted_iota(jnp.int32, (tm, LANES), 1) % HEAD_DIM) < (HEAD_DIM // 2)
    cos = cos_ref[...]
    sin = sin_ref[...]
    for i in range(ROPE_W // LANES):
        t = qk[:, i * LANES:(i + 1) * LANES]
        partner = jnp.where(lower_half, pltpu.roll(t, LANES - HEAD_DIM // 2, axis=1), pltpu.roll(t, HEAD_DIM // 2, axis=1))
        o_ref[:, i * LANES:(i + 1) * LANES] = (t * cos + partner * sin).astype(BF16)
    o_ref[:, ROPE_W:] = jnp.dot(h, w_ref[:, ROPE_W:], preferred_element_type=F32).astype(BF16)


def _inproj(xt, g, mod_l, mod_c, cos_t, sin_t, w, ctx_len):
    B, T, D = xt.shape
    tm = _pick_tile(T, 640)
    kern = functools.partial(_inproj_kernel, tm=tm, ctx_len=ctx_len)
    return pl.pallas_call(
        kern,
        grid=(B, T // tm),
        in_specs=[pl.BlockSpec((None, tm, D), lambda b, j: (b, j, 0)),
                  pl.BlockSpec((1, D), lambda b, j: (0, 0)),
                  *_mod_specs(1, D), *_mod_specs(0, D),
                  pl.BlockSpec((tm, LANES), lambda b, j: (j, 0)),
                  pl.BlockSpec((tm, LANES), lambda b, j: (j, 0)),
                  pl.BlockSpec((D, PROJ_W), lambda b, j: (0, 0))],
        out_specs=pl.BlockSpec((None, tm, PROJ_W), lambda b, j: (b, j, 0)),
        out_shape=jax.ShapeDtypeStruct((B, T, PROJ_W), BF16),
        compiler_params=_params(("parallel", "parallel")),
    )(xt, g, mod_l, mod_c, mod_l, mod_c, cos_t, sin_t, w)


def _softmax_pv(s, sink, v):
    m = jnp.maximum(jnp.max(s, axis=-1, keepdims=True), sink)
    p = jnp.exp(s - m)
    den = jnp.sum(p, axis=-1, keepdims=True) + jnp.exp(sink - m)
    return jnp.dot(p.astype(BF16), v, preferred_element_type=F32) / den


def _attn_kernel(sink_ref, q_ref, kc_ref, vc_ref, km_ref, vm_ref, kp_ref, vp_ref, kn_ref, vn_ref, o_ref, *, n_tiles):
    j = pl.program_id(1)
    nt_dims = (((1,), (1,)), ((), ()))

    @pl.when(j == 0)
    def _context_queries():
        for h in range(ATT_HEADS):
            g = h // ATT_GROUP
            hs = slice(h * HEAD_DIM, (h + 1) * HEAD_DIM)
            gs = slice(g * HEAD_DIM, (g + 1) * HEAD_DIM)
            s = lax.dot_general(q_ref[:, hs], kc_ref[:, gs], nt_dims, preferred_element_type=F32)
            o_ref[:, hs] = _softmax_pv(s, sink_ref[h], vc_ref[:, gs]).astype(BF16)

    @pl.when(j > 0)
    def _latent_queries():
        n_ctx = kc_ref.shape[0]
        n_keys = n_ctx + 3 * WINDOW
        r = lax.broadcasted_iota(jnp.int32, (WINDOW, n_keys), 0)
        kcol = lax.broadcasted_iota(jnp.int32, (WINDOW, n_keys), 1)
        w = kcol - n_ctx
        band = (w >= r) & (w <= r + 2 * WINDOW)
        for sub in range(SEQ_TILE // WINDOW):
            rows = slice(sub * WINDOW, (sub + 1) * WINDOW)
            if sub == 0:
                kprev, vprev = kp_ref[...], vp_ref[...]
                kmid, vmid = km_ref[:WINDOW], vm_ref[:WINDOW]
                knext, vnext = km_ref[WINDOW:], vm_ref[WINDOW:]
                w_lo = jnp.where(j > 1, 0, WINDOW)
                w_hi = 3 * WINDOW
            else:
                kprev, vprev = km_ref[:WINDOW], vm_ref[:WINDOW]
                kmid, vmid = km_ref[WINDOW:], vm_ref[WINDOW:]
                knext, vnext = kn_ref[...], vn_ref[...]
                w_lo = 0
                w_hi = jnp.where(j < n_tiles - 1, 3 * WINDOW, 2 * WINDOW)
            mask = (kcol < n_ctx) | (band & (w >= w_lo) & (w < w_hi))
            for g in range(ATT_KV_HEADS):
                gs = slice(g * HEAD_DIM, (g + 1) * HEAD_DIM)
                kcat = jnp.concatenate([kc_ref[:, gs], kprev[:, gs], kmid[:, gs], knext[:, gs]], axis=0)
                vcat = jnp.concatenate([vc_ref[:, gs], vprev[:, gs], vmid[:, gs], vnext[:, gs]], axis=0)
                for hh in range(ATT_GROUP):
                    h = g * ATT_GROUP + hh
                    hs = slice(h * HEAD_DIM, (h + 1) * HEAD_DIM)
                    s = lax.dot_general(q_ref[rows, hs], kcat, nt_dims, preferred_element_type=F32)
                    s = jnp.where(mask, s, -1e30)
                    o_ref[rows, hs] = _softmax_pv(s, sink_ref[h], vcat).astype(BF16)


def _attention(proj, sink, ctx_len):
    B, T, _ = proj.shape
    assert ctx_len == SEQ_TILE and T % SEQ_TILE == 0
    n_tiles = T // SEQ_TILE
    n_win = T // WINDOW
    kcol, vcol = COL_K // KV_W, COL_V // KV_W
    per_tile = SEQ_TILE // WINDOW
    tile = lambda col: pl.BlockSpec((None, SEQ_TILE, KV_W), lambda b, j: (b, j, col))
    ctx = lambda col: pl.BlockSpec((None, SEQ_TILE, KV_W), lambda b, j: (b, 0, col))
    prev = lambda col: pl.BlockSpec((None, WINDOW, KV_W), lambda b, j: (b, jnp.maximum(j * per_tile - 1, 0), col))
    nxt = lambda col: pl.BlockSpec((None, WINDOW, KV_W), lambda b, j: (b, jnp.minimum((j + 1) * per_tile, n_win - 1), col))
    return pl.pallas_call(
        functools.partial(_attn_kernel, n_tiles=n_tiles),
        grid=(B, n_tiles),
        in_specs=[pl.BlockSpec(memory_space=pltpu.SMEM),
                  pl.BlockSpec((None, SEQ_TILE, ATT_W), lambda b, j: (b, j, 0)),
                  ctx(kcol), ctx(vcol), tile(kcol), tile(vcol), prev(kcol), prev(vcol), nxt(kcol), nxt(vcol)],
        out_specs=pl.BlockSpec((None, SEQ_TILE, ATT_W), lambda b, j: (b, j, 0)),
        out_shape=jax.ShapeDtypeStruct((B, T, ATT_W), BF16),
        compiler_params=_params(("parallel", "parallel")),
    )(sink, *([proj] * 9))


def _conv_kernel(b_ref, c_ref, x_ref, cp_ref, xp_ref, cn_ref, xn_ref, w_ref, o_ref, u_scr, *, n_tiles):
    j = pl.program_id(1)
    last = SUBLANES_BF16 - 1
    pad = 8
    u = c_ref[...].astype(F32) * x_ref[...].astype(F32)
    u_prev_row = (cp_ref[...].astype(F32) * xp_ref[...].astype(F32))[last:, :]
    u_next_row = (cn_ref[...].astype(F32) * xn_ref[...].astype(F32))[:1, :]
    jv = jnp.full((1, CONV_CH), j, jnp.int32)
    u_scr[pad:pad + SEQ_TILE, :] = u
    u_scr[pad - 1:pad, :] = jnp.where(jv > 1, u_prev_row, 0.0)
    u_scr[pad + SEQ_TILE:pad + SEQ_TILE + 1, :] = jnp.where((jv > 0) & (jv < n_tiles - 1), u_next_row, 0.0)
    y = (w_ref[0:1, :] * u_scr[pad - 1:pad - 1 + SEQ_TILE, :] + w_ref[1:2, :] * u
         + w_ref[2:3, :] * u_scr[pad + 1:pad + 1 + SEQ_TILE, :])
    o_ref[...] = (b_ref[...].astype(F32) * y).astype(BF16)


def _short_conv(proj, conv_w, ctx_len):
    B, T, _ = proj.shape
    assert ctx_len == SEQ_TILE and T % SEQ_TILE == 0
    n_tiles = T // SEQ_TILE
    n_halo = T // SUBLANES_BF16
    per_tile = SEQ_TILE // SUBLANES_BF16
    bcol, ccol, xcol = COL_CB // CONV_CH, COL_CC // CONV_CH, COL_CX // CONV_CH
    tile = lambda col: pl.BlockSpec((None, SEQ_TILE, CONV_CH), lambda b, j: (b, j, col))
    prev = lambda col: pl.BlockSpec((None, SUBLANES_BF16, CONV_CH), lambda b, j: (b, jnp.maximum(j * per_tile - 1, 0), col))
    nxt = lambda col: pl.BlockSpec((None, SUBLANES_BF16, CONV_CH), lambda b, j: (b, jnp.minimum((j + 1) * per_tile, n_halo - 1), col))
    return pl.pallas_call(
        functools.partial(_conv_kernel, n_tiles=n_tiles),
        grid=(B, n_tiles),
        in_specs=[tile(bcol), tile(ccol), tile(xcol), prev(ccol), prev(xcol), nxt(ccol), nxt(xcol),
                  pl.BlockSpec((3, CONV_CH), lambda b, j: (0, 0))],
        out_specs=pl.BlockSpec((None, SEQ_TILE, CONV_CH), lambda b, j: (b, j, 0)),
        out_shape=jax.ShapeDtypeStruct((B, T, CONV_CH), BF16),
        scratch_shapes=[pltpu.VMEM((SEQ_TILE + 16, CONV_CH), F32)],
        compiler_params=_params(("parallel", "parallel")),
    )(*([proj] * 7), conv_w)


def _split2(v):
    hi = v.astype(BF16)
    lo = (v - hi.astype(F32)).astype(BF16)
    return hi, lo


def _gla_kernel(q_ref, k_ref, v_ref, gate_ref, lr_ref, w2_ref, bias_ref, ng_ref, o_ref,
                oacc_scr, qs_scr, u_scr, dec_scr, sin_scr, *, reverse, n_ctx_chunks):
    T = q_ref.shape[0]
    C = GLA_CHUNK
    n_tiles = T // SEQ_TILE
    n_chunks = T // C
    per_tile = SEQ_TILE // C
    nt_dims = (((1,), (1,)), ((), ()))
    tn_dims = (((0,), (0,)), ((), ()))

    r = lax.broadcasted_iota(jnp.int32, (SEQ_TILE, SEQ_TILE), 0)
    c = lax.broadcasted_iota(jnp.int32, (SEQ_TILE, SEQ_TILE), 1)
    same_chunk = (r // C) == (c // C)
    causal = same_chunk & ((c >= r) if reverse else (c <= r))
    cum_mat = jnp.where(causal, 1.0, 0.0).astype(BF16)
    tot_mat = jnp.where(same_chunk, 1.0, 0.0).astype(BF16)
    head_mean = jnp.where(same_chunk, 1.0 / GLA_DK, 0.0).astype(BF16)
    lane_head = lax.broadcasted_iota(jnp.int32, (C, GLA_W), 1) // GLA_DK

    def phase1(i, carry):
        rows = pl.ds(pl.multiple_of(i * SEQ_TILE, SEQ_TILE), SEQ_TILE)
        z = jnp.dot(lr_ref[rows, :], w2_ref[...], preferred_element_type=F32) + bias_ref[...]
        la = (jnp.minimum(z, 0.0) - jnp.log(1.0 + jnp.exp(-jnp.abs(z)))) / GLA_TAU
        hi, lo = _split2(la)
        b = jnp.dot(cum_mat, hi, preferred_element_type=F32) + jnp.dot(cum_mat, lo, preferred_element_type=F32)
        b_last = jnp.dot(tot_mat, hi, preferred_element_type=F32) + jnp.dot(tot_mat, lo, preferred_element_type=F32)
        b_ref = 0.5 * b_last
        q = q_ref[rows, :].astype(F32)
        k = k_ref[rows, :].astype(F32)
        v = v_ref[rows, :]
        qe = (q * jnp.exp(b - b_ref)).astype(BF16)
        ke = (k * jnp.exp(b_ref - b)).astype(BF16)
        ku = (k * jnp.exp(b_last - b)).astype(BF16)
        qs_scr[rows, :] = (q * jnp.exp(b)).astype(BF16)
        dec = jnp.exp(b_last)
        for h in range(GLA_HEADS):
            hs = slice(h * GLA_DK, (h + 1) * GLA_DK)
            s = lax.dot_general(qe[:, hs], ke[:, hs], nt_dims, preferred_element_type=F32)
            a = jnp.where(causal, s, 0.0).astype(BF16)
            oacc_scr[rows, hs] = jnp.dot(a, v[:, hs], preferred_element_type=F32)
        for ci in range(per_tile):
            cs = slice(ci * C, (ci + 1) * C)
            full = lax.dot_general(v[cs, :], ku[cs, :], tn_dims, preferred_element_type=F32)
            ut = full[0:C, :]
            for h in range(1, GLA_HEADS):
                ut = jnp.where(lane_head == h, full[h * C:(h + 1) * C, :], ut)
            u_scr[i * per_tile + ci] = ut
            dec_scr[i * per_tile + ci] = dec[ci * C:ci * C + 1, :]
        return carry

    lax.fori_loop(0, n_tiles, phase1, 0)

    def phase2(i, st):
        if reverse:
            ci = jnp.where(i < n_ctx_chunks, n_ctx_chunks - 1 - i, n_chunks - 1 + n_ctx_chunks - i)
        else:
            ci = i
        sin_scr[ci] = st.astype(BF16)
        return st * dec_scr[ci] + u_scr[ci]

    lax.fori_loop(0, n_chunks, phase2, jnp.zeros((C, GLA_W), F32))

    def phase3(i, carry):
        rows = pl.ds(pl.multiple_of(i * SEQ_TILE, SEQ_TILE), SEQ_TILE)
        qs = qs_scr[rows, :]
        parts = []
        for ci in range(per_tile):
            st = sin_scr[i * per_tile + ci]
            heads = []
            for h in range(GLA_HEADS):
                hs = slice(h * GLA_DK, (h + 1) * GLA_DK)
                heads.append(lax.dot_general(qs[ci * C:(ci + 1) * C, hs], st[:, hs], nt_dims,
                                             preferred_element_type=F32))
            parts.append(jnp.concatenate(heads, axis=1))
        o = oacc_scr[rows, :] + jnp.concatenate(parts, axis=0)
        hi, lo = _split2(o * o)
        ms = jnp.dot(hi, head_mean, preferred_element_type=F32) + jnp.dot(lo, head_mean, preferred_element_type=F32)
        y = o * lax.rsqrt(ms + EPS) * ng_ref[...]
        o_ref[rows, :] = (y * _silu(gate_ref[rows, :].astype(F32))).astype(BF16)
        return carry

    lax.fori_loop(0, n_tiles, phase3, 0)


def _gla(proj, w2pad, bias, ng, ctx_len, reverse):
    B, T, _ = proj.shape
    assert T % SEQ_TILE == 0 and ctx_len % GLA_CHUNK == 0
    n_chunks = T // GLA_CHUNK
    col = lambda c0: pl.BlockSpec((None, T, GLA_W), lambda b: (b, 0, c0 // GLA_W))
    gate_col = COL_GGB if reverse else COL_GGF
    kern = functools.partial(_gla_kernel, reverse=reverse, n_ctx_chunks=ctx_len // GLA_CHUNK)
    return pl.pallas_call(
        kern,
        grid=(B,),
        in_specs=[col(COL_GQ), col(COL_GK), col(COL_GV), col(gate_col),
                  pl.BlockSpec((None, T, LANES), lambda b: (b, 0, COL_LR // LANES)),
                  pl.BlockSpec((LANES, GLA_W), lambda b: (0, 0)),
                  pl.BlockSpec((1, GLA_W), lambda b: (0, 0)),
                  pl.BlockSpec((1, GLA_W), lambda b: (0, 0))],
        out_specs=pl.BlockSpec((None, T, GLA_W), lambda b: (b, 0, 0)),
        out_shape=jax.ShapeDtypeStruct((B, T, GLA_W), BF16),
        scratch_shapes=[pltpu.VMEM((T, GLA_W), F32),
                        pltpu.VMEM((T, GLA_W), BF16),
                        pltpu.VMEM((n_chunks, GLA_CHUNK, GLA_W), F32),
                        pltpu.VMEM((n_chunks, 1, GLA_W), F32),
                        pltpu.VMEM((n_chunks, GLA_CHUNK, GLA_W), BF16)],
        compiler_params=_params(("parallel",)),
    )(proj, proj, proj, proj, proj, w2pad, bias, ng)


def _outproj_kernel(*refs, tm, ctx_len, moe):
    if moe:
        (x_ref, att_ref, gf_ref, gb_ref, cv_ref, w_ref, g1_ref, cg1_ref, ng_ref, sc_ref, csc_ref, sh_ref, csh_ref,
         rhi_ref, rlo_ref, xo_ref, h_ref, cw_ref) = refs
    else:
        (x_ref, att_ref, gf_ref, gb_ref, cv_ref, w_ref, g1_ref, cg1_ref, ng_ref, sc_ref, csc_ref, sh_ref, csh_ref,
         xo_ref, h_ref) = refs
    is_ctx = _row_is_ctx(pl.program_id(1), tm, ctx_len)
    gla = (gf_ref[...].astype(F32) + gb_ref[...].astype(F32)).astype(BF16)
    y = jnp.dot(att_ref[...], w_ref[:ATT_W, :], preferred_element_type=F32)
    y += jnp.dot(gla, w_ref[ATT_W:ATT_W + GLA_W, :], preferred_element_type=F32)
    y += jnp.dot(cv_ref[...], w_ref[ATT_W + GLA_W:, :], preferred_element_type=F32)
    x = x_ref[...] + jnp.where(is_ctx, cg1_ref[...], g1_ref[...]) * y
    xo_ref[...] = x
    n = x * lax.rsqrt(jnp.mean(x * x, axis=-1, keepdims=True) + EPS) * ng_ref[...]
    scale = jnp.where(is_ctx, csc_ref[...], sc_ref[...])
    shift = jnp.where(is_ctx, csh_ref[...], sh_ref[...])
    h = n * (1.0 + scale) + shift
    hi = h.astype(BF16)
    h_ref[...] = hi
    if moe:
        lo = (h - hi.astype(F32)).astype(BF16)
        logits = (jnp.dot(hi, rhi_ref[...], preferred_element_type=F32)
                  + jnp.dot(lo, rhi_ref[...], preferred_element_type=F32)
                  + jnp.dot(hi, rlo_ref[...], preferred_element_type=F32))
        lane = lax.broadcasted_iota(jnp.int32, logits.shape, 1)
        neg = -jnp.inf
        lg = jnp.where(lane < N_EXPERTS, logits, neg)
        m1 = jnp.max(lg, axis=-1, keepdims=True)
        i1 = jnp.min(jnp.where(lg == m1, lane, LANES), axis=-1, keepdims=True)
        lg2 = jnp.where(lane == i1, neg, lg)
        m2 = jnp.max(lg2, axis=-1, keepdims=True)
        i2 = jnp.min(jnp.where(lg2 == m2, lane, LANES), axis=-1, keepdims=True)
        e2 = jnp.exp(m2 - m1)
        gate1 = 1.0 / (1.0 + e2)
        gate2 = e2 / (1.0 + e2)
        cw_ref[...] = jnp.where(lane == i1, gate1, 0.0) + jnp.where(lane == i2, gate2, 0.0)


def _outproj(xt, att, gf, gb, cv, w_out, ng, mod_l, mod_c, ctx_len, router=None):
    B, T, D = xt.shape
    tm = _pick_tile(T, 640)
    moe = router is not None
    row = lambda w: pl.BlockSpec((None, tm, w), lambda b, j: (b, j, 0))
    const = lambda shape: pl.BlockSpec(shape, lambda b, j: (0,) * len(shape))
    in_specs = [row(D), row(ATT_W), row(GLA_W), row(GLA_W), row(CONV_CH), const((D, D)),
                *_mod_specs(2, D), const((1, D)), *_mod_specs(4, D), *_mod_specs(3, D)]
    args = [xt, att, gf, gb, cv, w_out, mod_l, mod_c, ng, mod_l, mod_c, mod_l, mod_c]
    out_specs = [row(D), row(D)]
    out_shape = [jax.ShapeDtypeStruct((B, T, D), F32), jax.ShapeDtypeStruct((B, T, D), BF16)]
    if moe:
        in_specs += [const((D, LANES)), const((D, LANES))]
        args += list(router)
        out_specs.append(row(LANES))
        out_shape.append(jax.ShapeDtypeStruct((B, T, LANES), F32))
    return pl.pallas_call(
        functools.partial(_outproj_kernel, tm=tm, ctx_len=ctx_len, moe=moe),
        grid=(B, T // tm),
        in_specs=in_specs, out_specs=out_specs, out_shape=out_shape,
        compiler_params=_params(("parallel", "parallel")),
    )(*args)


def _ffn_kernel(*refs, tm, ctx_len, moe):
    if moe:
        h_ref, x_ref, g2_ref, cg2_ref, cw_ref, wg_ref, wu_ref, wd_ref, o_ref, acc_ref = refs
    else:
        h_ref, x_ref, g2_ref, cg2_ref, wg_ref, wu_ref, wd_ref, o_ref, acc_ref = refs
    j = pl.program_id(1)
    e = pl.program_id(2)
    f = pl.program_id(3)
    last_step = (e == pl.num_programs(2) - 1) & (f == pl.num_programs(3) - 1)

    @pl.when((e == 0) & (f == 0))
    def _init():
        acc_ref[...] = jnp.zeros_like(acc_ref)

    h = h_ref[...]
    g = jnp.dot(h, wg_ref[...], preferred_element_type=F32)
    u = jnp.dot(h, wu_ref[...], preferred_element_type=F32)
    y = jnp.dot((_silu(g) * u).astype(BF16), wd_ref[...], preferred_element_type=F32)
    if moe:
        lane = lax.broadcasted_iota(jnp.int32, cw_ref.shape, 1)
        y = y * jnp.sum(jnp.where(lane == e, cw_ref[...], 0.0), axis=-1, keepdims=True)
    acc_ref[...] += y

    @pl.when(last_step)
    def _finish():
        is_ctx = _row_is_ctx(j, tm, ctx_len)
        o_ref[...] = x_ref[...] + jnp.where(is_ctx, cg2_ref[...], g2_ref[...]) * acc_ref[...]


def _ffn(h2, xt, mod_l, mod_c, w_gu, w_down, ctx_len, cw=None):
    B, T, D = xt.shape
    E, F, _ = w_down.shape
    fc = 1408
    assert F % fc == 0
    nf = F // fc
    tm = _pick_tile(T, 640)
    moe = cw is not None
    row = lambda w: pl.BlockSpec((None, tm, w), lambda b, j, e, f: (b, j, 0))
    in_specs = [row(D), row(D), *_mod_specs(5, D)]
    args = [h2, xt, mod_l, mod_c]
    if moe:
        in_specs.append(row(LANES))
        args.append(cw)
    in_specs += [pl.BlockSpec((None, D, fc), lambda b, j, e, f: (e, 0, f)),
                 pl.BlockSpec((None, D, fc), lambda b, j, e, f: (e, 0, nf + f)),
                 pl.BlockSpec((None, fc, D), lambda b, j, e, f: (e, f, 0))]
    args += [w_gu, w_gu, w_down]
    return pl.pallas_call(
        functools.partial(_ffn_kernel, tm=tm, ctx_len=ctx_len, moe=moe),
        grid=(B, T // tm, E, nf),
        in_specs=in_specs,
        out_specs=row(D),
        out_shape=jax.ShapeDtypeStruct((B, T, D), F32),
        scratch_shapes=[pltpu.VMEM((tm, D), F32)],
        compiler_params=_params(("parallel", "parallel", "arbitrary", "arbitrary")),
    )(*args)


def _final_norm_kernel(x_ref, g_ref, o_ref):
    x = x_ref[...]
    o_ref[...] = x * lax.rsqrt(jnp.mean(x * x, axis=-1, keepdims=True) + EPS) * g_ref[...]


def _final_norm(xt, g, ctx_len):
    B, T, D = xt.shape
    S = T - ctx_len
    tm = SEQ_TILE
    assert ctx_len % tm == 0 and S % tm == 0
    skip = ctx_len // tm
    return pl.pallas_call(
        _final_norm_kernel,
        grid=(B, S // tm),
        in_specs=[pl.BlockSpec((None, tm, D), lambda b, j: (b, j + skip, 0)),
                  pl.BlockSpec((1, D), lambda b, j: (0, 0))],
        out_specs=pl.BlockSpec((None, tm, D), lambda b, j: (b, j, 0)),
        out_shape=jax.ShapeDtypeStruct((B, S, D), F32),
        compiler_params=_params(("parallel", "parallel")),
    )(xt, g)


def _rope_tables(seq, ctx_len):
    pos = jnp.arange(seq, dtype=jnp.int32)
    nf = HEAD_DIM // 4
    inv = ROPE_THETA ** (-jnp.arange(nf, dtype=F32) / nf)
    ang = jnp.concatenate([(pos // GRID_W).astype(F32)[:, None] * inv, (pos % GRID_W).astype(F32)[:, None] * inv], axis=-1)
    cos, sin = jnp.cos(ang), jnp.sin(ang)
    reps = LANES // HEAD_DIM
    cos_t = jnp.tile(jnp.concatenate([cos, cos], axis=-1), (1, reps))
    sin_t = jnp.tile(jnp.concatenate([-sin, sin], axis=-1), (1, reps))
    cos_t = jnp.concatenate([jnp.ones((ctx_len, LANES), F32), cos_t], axis=0)
    sin_t = jnp.concatenate([jnp.zeros((ctx_len, LANES), F32), sin_t], axis=0)
    return cos_t, sin_t


def _relayout_w_in(w_in):
    o_lr = ATT_W + 2 * KV_W + 5 * GLA_W
    q_scale = HEAD_DIM ** -0.5
    gq_scale = GLA_DK ** -0.5
    parts = [w_in[..., :ATT_W] * q_scale, w_in[..., ATT_W:COL_GQ], w_in[..., COL_GQ:COL_GK] * gq_scale,
             w_in[..., COL_GK:o_lr], w_in[..., o_lr + 2 * GLA_LOWRANK:], w_in[..., o_lr:o_lr + 2 * GLA_LOWRANK],
             jnp.zeros(w_in.shape[:-1] + (LANES - 2 * GLA_LOWRANK,), w_in.dtype)]
    return jnp.concatenate(parts, axis=-1).astype(BF16)


def kernel(x, c, ctx, c_ctx, w_mod, b_mod, norm1_g, norm2_g, w_in, w_out, attn_sink, gla_w2_f, gla_b_f, gla_w2_b, gla_b_b, gla_norm_g, conv_w, ffn_w_gu, ffn_w_down, router_w, expert_w_gu, expert_w_down, final_norm_g):
    B, S, D = x.shape
    L = ctx.shape[1]
    depth = w_in.shape[0]
    assert D == D_MODEL and S % GRID_W == 0

    xt = jnp.concatenate([ctx, x], axis=1)
    rows = 16
    cc = jnp.concatenate([c, c_ctx[None, :], jnp.zeros((rows - B - 1, D), F32)], axis=0)
    mod = _adaln(cc, w_mod, b_mod)
    cos_t, sin_t = _rope_tables(S, L)

    w_in_p = _relayout_w_in(w_in)
    w_out_b = w_out.astype(BF16)
    zpad = jnp.zeros((depth, LANES - 2 * GLA_LOWRANK, GLA_W), F32)
    w2_f = jnp.concatenate([gla_w2_f, jnp.zeros_like(gla_w2_b), zpad], axis=1).astype(BF16)
    w2_b = jnp.concatenate([jnp.zeros_like(gla_w2_f), gla_w2_b, zpad], axis=1).astype(BF16)
    gla_ng = jnp.tile(gla_norm_g, (1, GLA_HEADS))
    ffn_gu_b = ffn_w_gu.astype(BF16)
    ffn_down_b = ffn_w_down.astype(BF16)
    exp_gu_b = expert_w_gu.astype(BF16)
    exp_down_b = expert_w_down.astype(BF16)
    router_p = jnp.pad(router_w, ((0, 0), (0, 0), (0, LANES - N_EXPERTS)))
    router_hi = router_p.astype(BF16)
    router_lo = (router_p - router_hi.astype(F32)).astype(BF16)

    for l in range(depth):
        mod_l = mod[l, :B].reshape(B, 1, 6 * D)
        mod_c = mod[l, B:B + 1]
        proj = _inproj(xt, norm1_g[l][None, :], mod_l, mod_c, cos_t, sin_t, w_in_p[l], L)
        att = _attention(proj, attn_sink[l], L)
        cv = _short_conv(proj, conv_w[l], L)
        gf = _gla(proj, w2_f[l], gla_b_f[l][None, :], gla_ng[l][None, :], L, reverse=False)
        gb = _gla(proj, w2_b[l], gla_b_b[l][None, :], gla_ng[l][None, :], L, reverse=True)
        if l % 2 == 0:
            xt, h2 = _outproj(xt, att, gf, gb, cv, w_out_b[l], norm2_g[l][None, :], mod_l, mod_c, L)
            xt = _ffn(h2, xt, mod_l, mod_c, ffn_gu_b[l // 2][None], ffn_down_b[l // 2][None], L)
        else:
            xt, h2, cw = _outproj(xt, att, gf, gb, cv, w_out_b[l], norm2_g[l][None, :], mod_l, mod_c, L,
                                  router=(router_hi[l // 2], router_lo[l // 2]))
            xt = _ffn(h2, xt, mod_l, mod_c, exp_gu_b[l // 2], exp_down_b[l // 2], L, cw=cw)
    return _final_norm(xt, final_norm_g[None, :], L)
```

```python
import functools

import jax
import jax.numpy as jnp
from jax import lax
from jax.experimental import pallas as pl
from jax.experimental.pallas import tpu as pltpu
from jax.experimental.pallas import tpu_sc as plsc

F32 = jnp.float32
BF16 = jnp.bfloat16

D_MODEL = 1024
EPS = 1e-6
GRID_W = 64
ROPE_THETA = 10000.0
HEAD_DIM = 64
ATT_HEADS = 8
ATT_KV_HEADS = 2
ATT_GROUP = ATT_HEADS // ATT_KV_HEADS
ATT_W = ATT_HEADS * HEAD_DIM
KV_W = ATT_KV_HEADS * HEAD_DIM
WINDOW = 128
GLA_HEADS = 4
GLA_DK = 64
GLA_W = GLA_HEADS * GLA_DK
GLA_LOWRANK = 16
GLA_TAU = 16.0
GLA_CHUNK = 64
CONV_CH = 256
N_EXPERTS = 8

LANES = 128
SUBLANES_BF16 = 16

COL_Q = 0
COL_K = ATT_W
COL_V = COL_K + KV_W
ROPE_W = COL_V
COL_GQ = COL_V + KV_W
COL_GK = COL_GQ + GLA_W
COL_GV = COL_GK + GLA_W
COL_GGF = COL_GV + GLA_W
COL_GGB = COL_GGF + GLA_W
COL_CB = COL_GGB + GLA_W
COL_CC = COL_CB + CONV_CH
COL_CX = COL_CC + CONV_CH
COL_LR = COL_CX + CONV_CH
PROJ_W = COL_LR + LANES

SEQ_TILE = 256
VMEM_LIMIT = 56 * 1024 * 1024

PACK_W = D_MODEL // 2
PACK_CHUNKS = PACK_W // LANES
EXPERT_TILE = 512
SC_CORES = 2
SC_SUBCORES = 16
SC_INDEX_CHUNK = 128


def _silu(v):
    return v / (1.0 + jnp.exp(-v))


def _pick_tile(total, cap):
    best = None
    for t in range(SUBLANES_BF16, cap + 1, SUBLANES_BF16):
        if total % t == 0:
            best = t
    assert best is not None
    return best


def _params(sem):
    return pltpu.CompilerParams(dimension_semantics=sem, vmem_limit_bytes=VMEM_LIMIT)


def _pack_bf16_pairs(v):
    bits = lambda t: lax.bitcast_convert_type(t.astype(BF16).astype(F32), jnp.int32)
    return ((bits(v[:, :PACK_W]) >> 16) & 0xFFFF) | (bits(v[:, PACK_W:]) & -65536)


def _unpack_bf16_pairs(w):
    return lax.bitcast_convert_type(w << 16, F32), lax.bitcast_convert_type(w & -65536, F32)


def _load_packed(ref):
    return jnp.concatenate([ref[k] for k in range(PACK_CHUNKS)], axis=1)


def _store_packed(ref, packed):
    for k in range(PACK_CHUNKS):
        ref[k] = packed[:, k * LANES:(k + 1) * LANES]


def _adaln_kernel(c_ref, w_ref, b_ref, o_ref):
    s = _silu(c_ref[...]).astype(BF16)
    o_ref[...] = jnp.dot(s, w_ref[...].astype(BF16), preferred_element_type=F32) + b_ref[...]


def _adaln(cc, w_mod, b_mod):
    depth, d, n = w_mod.shape
    tn = 1024
    rows = cc.shape[0]
    return pl.pallas_call(
        _adaln_kernel,
        grid=(depth, n // tn),
        in_specs=[pl.BlockSpec((rows, d), lambda l, j: (0, 0)),
                  pl.BlockSpec((None, d, tn), lambda l, j: (l, 0, j)),
                  pl.BlockSpec((None, 1, tn), lambda l, j: (l, 0, j))],
        out_specs=pl.BlockSpec((None, rows, tn), lambda l, j: (l, 0, j)),
        out_shape=jax.ShapeDtypeStruct((depth, rows, n), F32),
        compiler_params=_params(("parallel", "parallel")),
    )(cc, w_mod, b_mod.reshape(depth, 1, n))


def _mod_specs(k, d):
    return [pl.BlockSpec((None, 1, d), lambda b, j, *_: (b, 0, k)),
            pl.BlockSpec((1, d), lambda b, j, *_: (0, k))]


def _row_is_ctx(j, tm, ctx_len):
    return (j * tm + lax.broadcasted_iota(jnp.int32, (tm, 1), 0)) < ctx_len


def _inproj_kernel(x_ref, g_ref, sc_ref, csc_ref, sh_ref, csh_ref, cos_ref, sin_ref, w_ref, o_ref, *, tm, ctx_len):
    x = x_ref[...]
    n = x * lax.rsqrt(jnp.mean(x * x, axis=-1, keepdims=True) + EPS) * g_ref[...]
    is_ctx = _row_is_ctx(pl.program_id(1), tm, ctx_len)
    scale = jnp.where(is_ctx, csc_ref[...], sc_ref[...])
    shift = jnp.where(is_ctx, csh_ref[...], sh_ref[...])
    h = (n * (1.0 + scale) + shift).astype(BF16)
    qk = jnp.dot(h, w_ref[:, :ROPE_W], preferred_element_type=F32)
    lower_half = (lax.broadcasted_iota(jnp.int32, (tm, LANES), 1) % HEAD_DIM) < (HEAD_DIM // 2)
    cos = cos_ref[...]
    sin = sin_ref[...]
    for i in range(ROPE_W // LANES):
        t = qk[:, i * LANES:(i + 1) * LANES]
        partner = jnp.where(lower_half, pltpu.roll(t, LANES - HEAD_DIM // 2, axis=1), pltpu.roll(t, HEAD_DIM // 2, axis=1))
        o_ref[:, i * LANES:(i + 1) * LANES] = (t * cos + partner * sin).astype(BF16)
    o_ref[:, ROPE_W:] = jnp.dot(h, w_ref[:, ROPE_W:], preferred_element_type=F32).astype(BF16)


def _inproj(xt, g, mod_l, mod_c, cos_t, sin_t, w, ctx_len):
    B, T, D = xt.shape
    tm = _pick_tile(T, 640)
    kern = functools.partial(_inproj_kernel, tm=tm, ctx_len=ctx_len)
    return pl.pallas_call(
        kern,
        grid=(B, T // tm),
        in_specs=[pl.BlockSpec((None, tm, D), lambda b, j: (b, j, 0)),
                  pl.BlockSpec((1, D), lambda b, j: (0, 0)),
                  *_mod_specs(1, D), *_mod_specs(0, D),
                  pl.BlockSpec((tm, LANES), lambda b, j: (j, 0)),
                  pl.BlockSpec((tm, LANES), lambda b, j: (j, 0)),
                  pl.BlockSpec((D, PROJ_W), lambda b, j: (0, 0))],
        out_specs=pl.BlockSpec((None, tm, PROJ_W), lambda b, j: (b, j, 0)),
        out_shape=jax.ShapeDtypeStruct((B, T, PROJ_W), BF16),
        compiler_params=_params(("parallel", "parallel")),
    )(xt, g, mod_l, mod_c, mod_l, mod_c, cos_t, sin_t, w)


def _softmax_pv(s, sink, v):
    m = jnp.maximum(jnp.max(s, axis=-1, keepdims=True), sink)
    p = jnp.exp(s - m)
    den = jnp.sum(p, axis=-1, keepdims=True) + jnp.exp(sink - m)
    return jnp.dot(p.astype(BF16), v, preferred_element_type=F32) / den


def _attn_kernel(sink_ref, q_ref, kc_ref, vc_ref, km_ref, vm_ref, kp_ref, vp_ref, kn_ref, vn_ref, o_ref, *, n_tiles):
    j = pl.program_id(1)
    nt_dims = (((1,), (1,)), ((), ()))

    @pl.when(j == 0)
    def _context_queries():
        for h in range(ATT_HEADS):
            g = h // ATT_GROUP
            hs = slice(h * HEAD_DIM, (h + 1) * HEAD_DIM)
            gs = slice(g * HEAD_DIM, (g + 1) * HEAD_DIM)
            s = lax.dot_general(q_ref[:, hs], kc_ref[:, gs], nt_dims, preferred_element_type=F32)
            o_ref[:, hs] = _softmax_pv(s, sink_ref[h], vc_ref[:, gs]).astype(BF16)

    @pl.when(j > 0)
    def _latent_queries():
        n_ctx = kc_ref.shape[0]
        n_keys = n_ctx + 3 * WINDOW
        r = lax.broadcasted_iota(jnp.int32, (WINDOW, n_keys), 0)
        kcol = lax.broadcasted_iota(jnp.int32, (WINDOW, n_keys), 1)
        w = kcol - n_ctx
        band = (w >= r) & (w <= r + 2 * WINDOW)
        for sub in range(SEQ_TILE // WINDOW):
            rows = slice(sub * WINDOW, (sub + 1) * WINDOW)
            if sub == 0:
                kprev, vprev = kp_ref[...], vp_ref[...]
                kmid, vmid = km_ref[:WINDOW], vm_ref[:WINDOW]
                knext, vnext = km_ref[WINDOW:], vm_ref[WINDOW:]
                w_lo = jnp.where(j > 1, 0, WINDOW)
                w_hi = 3 * WINDOW
            else:
                kprev, vprev = km_ref[:WINDOW], vm_ref[:WINDOW]
                kmid, vmid = km_ref[WINDOW:], vm_ref[WINDOW:]
                knext, vnext = kn_ref[...], vn_ref[...]
                w_lo = 0
                w_hi = jnp.where(j < n_tiles - 1, 3 * WINDOW, 2 * WINDOW)
            mask = (kcol < n_ctx) | (band & (w >= w_lo) & (w < w_hi))
            for g in range(ATT_KV_HEADS):
                gs = slice(g * HEAD_DIM, (g + 1) * HEAD_DIM)
                kcat = jnp.concatenate([kc_ref[:, gs], kprev[:, gs], kmid[:, gs], knext[:, gs]], axis=0)
                vcat = jnp.concatenate([vc_ref[:, gs], vprev[:, gs], vmid[:, gs], vnext[:, gs]], axis=0)
                for hh in range(ATT_GROUP):
                    h = g * ATT_GROUP + hh
                    hs = slice(h * HEAD_DIM, (h + 1) * HEAD_DIM)
                    s = lax.dot_general(q_ref[rows, hs], kcat, nt_dims, preferred_element_type=F32)
                    s = jnp.where(mask, s, -1e30)
                    o_ref[rows, hs] = _softmax_pv(s, sink_ref[h], vcat).astype(BF16)


def _attention(proj, sink, ctx_len):
    B, T, _ = proj.shape
    assert ctx_len == SEQ_TILE and T % SEQ_TILE == 0
    n_tiles = T // SEQ_TILE
    n_win = T // WINDOW
    kcol, vcol = COL_K // KV_W, COL_V // KV_W
    per_tile = SEQ_TILE // WINDOW
    tile = lambda col: pl.BlockSpec((None, SEQ_TILE, KV_W), lambda b, j: (b, j, col))
    ctx = lambda col: pl.BlockSpec((None, SEQ_TILE, KV_W), lambda b, j: (b, 0, col))
    prev = lambda col: pl.BlockSpec((None, WINDOW, KV_W), lambda b, j: (b, jnp.maximum(j * per_tile - 1, 0), col))
    nxt = lambda col: pl.BlockSpec((None, WINDOW, KV_W), lambda b, j: (b, jnp.minimum((j + 1) * per_tile, n_win - 1), col))
    return pl.pallas_call(
        functools.partial(_attn_kernel, n_tiles=n_tiles),
        grid=(B, n_tiles),
        in_specs=[pl.BlockSpec(memory_space=pltpu.SMEM),
                  pl.BlockSpec((None, SEQ_TILE, ATT_W), lambda b, j: (b, j, 0)),
                  ctx(kcol), ctx(vcol), tile(kcol), tile(vcol), prev(kcol), prev(vcol), nxt(kcol), nxt(vcol)],
        out_specs=pl.BlockSpec((None, SEQ_TILE, ATT_W), lambda b, j: (b, j, 0)),
        out_shape=jax.ShapeDtypeStruct((B, T, ATT_W), BF16),
        compiler_params=_params(("parallel", "parallel")),
    )(sink, *([proj] * 9))


def _conv_kernel(b_ref, c_ref, x_ref, cp_ref, xp_ref, cn_ref, xn_ref, w_ref, o_ref, u_scr, *, n_tiles):
    j = pl.program_id(1)
    last = SUBLANES_BF16 - 1
    pad = 8
    u = c_ref[...].astype(F32) * x_ref[...].astype(F32)
    u_prev_row = (cp_ref[...].astype(F32) * xp_ref[...].astype(F32))[last:, :]
    u_next_row = (cn_ref[...].astype(F32) * xn_ref[...].astype(F32))[:1, :]
    jv = jnp.full((1, CONV_CH), j, jnp.int32)
    u_scr[pad:pad + SEQ_TILE, :] = u
    u_scr[pad - 1:pad, :] = jnp.where(jv > 1, u_prev_row, 0.0)
    u_scr[pad + SEQ_TILE:pad + SEQ_TILE + 1, :] = jnp.where((jv > 0) & (jv < n_tiles - 1), u_next_row, 0.0)
    y = (w_ref[0:1, :] * u_scr[pad - 1:pad - 1 + SEQ_TILE, :] + w_ref[1:2, :] * u
         + w_ref[2:3, :] * u_scr[pad + 1:pad + 1 + SEQ_TILE, :])
    o_ref[...] = (b_ref[...].astype(F32) * y).astype(BF16)


def _short_conv(proj, conv_w, ctx_len):
    B, T, _ = proj.shape
    assert ctx_len == SEQ_TILE and T % SEQ_TILE == 0
    n_tiles = T // SEQ_TILE
    n_halo = T // SUBLANES_BF16
    per_tile = SEQ_TILE // SUBLANES_BF16
    bcol, ccol, xcol = COL_CB // CONV_CH, COL_CC // CONV_CH, COL_CX // CONV_CH
    tile = lambda col: pl.BlockSpec((None, SEQ_TILE, CONV_CH), lambda b, j: (b, j, col))
    prev = lambda col: pl.BlockSpec((None, SUBLANES_BF16, CONV_CH), lambda b, j: (b, jnp.maximum(j * per_tile - 1, 0), col))
    nxt = lambda col: pl.BlockSpec((None, SUBLANES_BF16, CONV_CH), lambda b, j: (b, jnp.minimum((j + 1) * per_tile, n_halo - 1), col))
    return pl.pallas_call(
        functools.partial(_conv_kernel, n_tiles=n_tiles),
        grid=(B, n_tiles),
        in_specs=[tile(bcol), tile(ccol), tile(xcol), prev(ccol), prev(xcol), nxt(ccol), nxt(xcol),
                  pl.BlockSpec((3, CONV_CH), lambda b, j: (0, 0))],
        out_specs=pl.BlockSpec((None, SEQ_TILE, CONV_CH), lambda b, j: (b, j, 0)),
        out_shape=jax.ShapeDtypeStruct((B, T, CONV_CH), BF16),
        scratch_shapes=[pltpu.VMEM((SEQ_TILE + 16, CONV_CH), F32)],
        compiler_params=_params(("parallel", "parallel")),
    )(*([proj] * 7), conv_w)


def _split2(v):
    hi = v.astype(BF16)
    lo = (v - hi.astype(F32)).astype(BF16)
    return hi, lo


def _gla_kernel(q_ref, k_ref, v_ref, gate_ref, lr_ref, w2_ref, bias_ref, ng_ref, o_ref,
                oacc_scr, qs_scr, u_scr, dec_scr, sin_scr, *, reverse, n_ctx_chunks):
    T = q_ref.shape[0]
    C = GLA_CHUNK
    n_tiles = T // SEQ_TILE
    n_chunks = T // C
    per_tile = SEQ_TILE // C
    nt_dims = (((1,), (1,)), ((), ()))
    tn_dims = (((0,), (0,)), ((), ()))

    r = lax.broadcasted_iota(jnp.int32, (SEQ_TILE, SEQ_TILE), 0)
    c = lax.broadcasted_iota(jnp.int32, (SEQ_TILE, SEQ_TILE), 1)
    same_chunk = (r // C) == (c // C)
    causal = same_chunk & ((c >= r) if reverse else (c <= r))
    cum_mat = jnp.where(causal, 1.0, 0.0).astype(BF16)
    tot_mat = jnp.where(same_chunk, 1.0, 0.0).astype(BF16)
    head_mean = jnp.where(same_chunk, 1.0 / GLA_DK, 0.0).astype(BF16)
    lane_head = lax.broadcasted_iota(jnp.int32, (C, GLA_W), 1) // GLA_DK

    def phase1(i, carry):
        rows = pl.ds(pl.multiple_of(i * SEQ_TILE, SEQ_TILE), SEQ_TILE)
        z = jnp.dot(lr_ref[rows, :], w2_ref[...], preferred_element_type=F32) + bias_ref[...]
        la = (jnp.minimum(z, 0.0) - jnp.log(1.0 + jnp.exp(-jnp.abs(z)))) / GLA_TAU
        hi, lo = _split2(la)
        b = jnp.dot(cum_mat, hi, preferred_element_type=F32) + jnp.dot(cum_mat, lo, preferred_element_type=F32)
        b_last = jnp.dot(tot_mat, hi, preferred_element_type=F32) + jnp.dot(tot_mat, lo, preferred_element_type=F32)
        b_ref = 0.5 * b_last
        q = q_ref[rows, :].astype(F32)
        k = k_ref[rows, :].astype(F32)
        v = v_ref[rows, :]
        qe = (q * jnp.exp(b - b_ref)).astype(BF16)
        ke = (k * jnp.exp(b_ref - b)).astype(BF16)
        ku = (k * jnp.exp(b_last - b)).astype(BF16)
        qs_scr[rows, :] = (q * jnp.exp(b)).astype(BF16)
        dec = jnp.exp(b_last)
        for h in range(GLA_HEADS):
            hs = slice(h * GLA_DK, (h + 1) * GLA_DK)
            s = lax.dot_general(qe[:, hs], ke[:, hs], nt_dims, preferred_element_type=F32)
            a = jnp.where(causal, s, 0.0).astype(BF16)
            oacc_scr[rows, hs] = jnp.dot(a, v[:, hs], preferred_element_type=F32)
        for ci in range(per_tile):
            cs = slice(ci * C, (ci + 1) * C)
            full = lax.dot_general(v[cs, :], ku[cs, :], tn_dims, preferred_element_type=F32)
            ut = full[0:C, :]
            for h in range(1, GLA_HEADS):
                ut = jnp.where(lane_head == h, full[h * C:(h + 1) * C, :], ut)
            u_scr[i * per_tile + ci] = ut
            dec_scr[i * per_tile + ci] = dec[ci * C:ci * C + 1, :]
        return carry

    lax.fori_loop(0, n_tiles, phase1, 0)

    def phase2(i, st):
        if reverse:
            ci = jnp.where(i < n_ctx_chunks, n_ctx_chunks - 1 - i, n_chunks - 1 + n_ctx_chunks - i)
        else:
            ci = i
        sin_scr[ci] = st.astype(BF16)
        return st * dec_scr[ci] + u_scr[ci]

    lax.fori_loop(0, n_chunks, phase2, jnp.zeros((C, GLA_W), F32))

    def phase3(i, carry):
        rows = pl.ds(pl.multiple_of(i * SEQ_TILE, SEQ_TILE), SEQ_TILE)
        qs = qs_scr[rows, :]
        parts = []
        for ci in range(per_tile):
            st = sin_scr[i * per_tile + ci]
            heads = []
            for h in range(GLA_HEADS):
                hs = slice(h * GLA_DK, (h + 1) * GLA_DK)
                heads.append(lax.dot_general(qs[ci * C:(ci + 1) * C, hs], st[:, hs], nt_dims,
                                             preferred_element_type=F32))
            parts.append(jnp.concatenate(heads, axis=1))
        o = oacc_scr[rows, :] + jnp.concatenate(parts, axis=0)
        hi, lo = _split2(o * o)
        ms = jnp.dot(hi, head_mean, preferred_element_type=F32) + jnp.dot(lo, head_mean, preferred_element_type=F32)
        y = o * lax.rsqrt(ms + EPS) * ng_ref[...]
        o_ref[rows, :] = (y * _silu(gate_ref[rows, :].astype(F32))).astype(BF16)
        return carry

    lax.fori_loop(0, n_tiles, phase3, 0)


def _gla(proj, w2pad, bias, ng, ctx_len, reverse):
    B, T, _ = proj.shape
    assert T % SEQ_TILE == 0 and ctx_len % GLA_CHUNK == 0
    n_chunks = T // GLA_CHUNK
    col = lambda c0: pl.BlockSpec((None, T, GLA_W), lambda b: (b, 0, c0 // GLA_W))
    gate_col = COL_GGB if reverse else COL_GGF
    kern = functools.partial(_gla_kernel, reverse=reverse, n_ctx_chunks=ctx_len // GLA_CHUNK)
    return pl.pallas_call(
        kern,
        grid=(B,),
        in_specs=[col(COL_GQ), col(COL_GK), col(COL_GV), col(gate_col),
                  pl.BlockSpec((None, T, LANES), lambda b: (b, 0, COL_LR // LANES)),
                  pl.BlockSpec((LANES, GLA_W), lambda b: (0, 0)),
                  pl.BlockSpec((1, GLA_W), lambda b: (0, 0)),
                  pl.BlockSpec((1, GLA_W), lambda b: (0, 0))],
        out_specs=pl.BlockSpec((None, T, GLA_W), lambda b: (b, 0, 0)),
        out_shape=jax.ShapeDtypeStruct((B, T, GLA_W), BF16),
        scratch_shapes=[pltpu.VMEM((T, GLA_W), F32),
                        pltpu.VMEM((T, GLA_W), BF16),
                        pltpu.VMEM((n_chunks, GLA_CHUNK, GLA_W), F32),
                        pltpu.VMEM((n_chunks, 1, GLA_W), F32),
                        pltpu.VMEM((n_chunks, GLA_CHUNK, GLA_W), BF16)],
        compiler_params=_params(("parallel",)),
    )(proj, proj, proj, proj, proj, w2pad, bias, ng)


def _outproj_kernel(*refs, tm, ctx_len, moe):
    if moe:
        (x_ref, att_ref, gf_ref, gb_ref, cv_ref, w_ref, g1_ref, cg1_ref, ng_ref, sc_ref, csc_ref, sh_ref, csh_ref,
         rhi_ref, rlo_ref, xo_ref, hp_ref, route_ref) = refs
    else:
        (x_ref, att_ref, gf_ref, gb_ref, cv_ref, w_ref, g1_ref, cg1_ref, ng_ref, sc_ref, csc_ref, sh_ref, csh_ref,
         xo_ref, h_ref) = refs
    is_ctx = _row_is_ctx(pl.program_id(1), tm, ctx_len)
    gla = (gf_ref[...].astype(F32) + gb_ref[...].astype(F32)).astype(BF16)
    y = jnp.dot(att_ref[...], w_ref[:ATT_W, :], preferred_element_type=F32)
    y += jnp.dot(gla, w_ref[ATT_W:ATT_W + GLA_W, :], preferred_element_type=F32)
    y += jnp.dot(cv_ref[...], w_ref[ATT_W + GLA_W:, :], preferred_element_type=F32)
    x = x_ref[...] + jnp.where(is_ctx, cg1_ref[...], g1_ref[...]) * y
    xo_ref[...] = x
    n = x * lax.rsqrt(jnp.mean(x * x, axis=-1, keepdims=True) + EPS) * ng_ref[...]
    scale = jnp.where(is_ctx, csc_ref[...], sc_ref[...])
    shift = jnp.where(is_ctx, csh_ref[...], sh_ref[...])
    h = n * (1.0 + scale) + shift
    hi = h.astype(BF16)
    if not moe:
        h_ref[...] = hi
    else:
        _store_packed(hp_ref, _pack_bf16_pairs(h))
        lo =(h - hi.astype(F32)).astype(BF16)
        logits = (jnp.dot(hi, rhi_ref[...], preferred_element_type=F32)
                  + jnp.dot(lo, rhi_ref[...], preferred_element_type=F32)
                  + jnp.dot(hi, rlo_ref[...], preferred_element_type=F32))
        lane = lax.broadcasted_iota(jnp.int32, logits.shape, 1)
        neg = -jnp.inf
        lg = jnp.where(lane < N_EXPERTS, logits, neg)
        m1 = jnp.max(lg, axis=-1, keepdims=True)
        i1 = jnp.min(jnp.where(lg == m1, lane, LANES), axis=-1, keepdims=True)
        lg2 = jnp.where(lane == i1, neg, lg)
        m2 = jnp.max(lg2, axis=-1, keepdims=True)
        i2 = jnp.min(jnp.where(lg2 == m2, lane, LANES), axis=-1, keepdims=True)
        e2 = jnp.exp(m2 - m1)
        gate1 = 1.0 / (1.0 + e2)
        gate2 = e2 / (1.0 + e2)
        route_ref[...] = jnp.where(lane == 0, i1.astype(F32), jnp.where(lane == 1, i2.astype(F32),
                                   jnp.where(lane == 2, gate1, jnp.where(lane == 3, gate2, 0.0))))


def _outproj(xt, att, gf, gb, cv, w_out, ng, mod_l, mod_c, ctx_len, router=None):
    B, T, D = xt.shape
    tm = _pick_tile(T, 640)
    moe = router is not None
    row = lambda w: pl.BlockSpec((None, tm, w), lambda b, j: (b, j, 0))
    const = lambda shape: pl.BlockSpec(shape, lambda b, j: (0,) * len(shape))
    in_specs = [row(D), row(ATT_W), row(GLA_W), row(GLA_W), row(CONV_CH), const((D, D)),
                *_mod_specs(2, D), const((1, D)), *_mod_specs(4, D), *_mod_specs(3, D)]
    args = [xt, att, gf, gb, cv, w_out, mod_l, mod_c, ng, mod_l, mod_c, mod_l, mod_c]
    nt = T // tm
    if moe:
        in_specs += [const((D, LANES)), const((D, LANES))]
        args += list(router)
        out_specs = [row(D), pl.BlockSpec((PACK_CHUNKS, tm, LANES), lambda b, j: (0, b * nt + j, 0)), row(LANES)]
        out_shape = [jax.ShapeDtypeStruct((B, T, D), F32), jax.ShapeDtypeStruct((PACK_CHUNKS, B * T, LANES), jnp.int32),
                     jax.ShapeDtypeStruct((B, T, LANES), F32)]
    else:
        out_specs = [row(D), row(D)]
        out_shape = [jax.ShapeDtypeStruct((B, T, D), F32), jax.ShapeDtypeStruct((B, T, D), BF16)]
    return pl.pallas_call(
        functools.partial(_outproj_kernel, tm=tm, ctx_len=ctx_len, moe=moe),
        grid=(B, T // tm),
        in_specs=in_specs, out_specs=out_specs, out_shape=out_shape,
        compiler_params=_params(("parallel", "parallel")),
    )(*args)


def _ffn_kernel(h_ref, x_ref, g2_ref, cg2_ref, wg_ref, wu_ref, wd_ref, o_ref, acc_ref, *, tm, ctx_len):
    j = pl.program_id(1)
    f = pl.program_id(2)

    @pl.when(f == 0)
    def _init():
        acc_ref[...] = jnp.zeros_like(acc_ref)

    h = h_ref[...]
    g = jnp.dot(h, wg_ref[...], preferred_element_type=F32)
    u = jnp.dot(h, wu_ref[...], preferred_element_type=F32)
    acc_ref[...] += jnp.dot((_silu(g) * u).astype(BF16), wd_ref[...], preferred_element_type=F32)

    @pl.when(f == pl.num_programs(2) - 1)
    def _finish():
        is_ctx = _row_is_ctx(j, tm, ctx_len)
        o_ref[...] = x_ref[...] + jnp.where(is_ctx, cg2_ref[...], g2_ref[...]) * acc_ref[...]


def _ffn(h2, xt, mod_l, mod_c, w_gu, w_down, ctx_len):
    B, T, D = xt.shape
    F = w_down.shape[0]
    fc = 1408
    assert F % fc == 0
    nf = F // fc
    tm = _pick_tile(T, 640)
    row = lambda w: pl.BlockSpec((None, tm, w), lambda b, j, f: (b, j, 0))
    return pl.pallas_call(
        functools.partial(_ffn_kernel, tm=tm, ctx_len=ctx_len),
        grid=(B, T // tm, nf),
        in_specs=[row(D), row(D), *_mod_specs(5, D),
                  pl.BlockSpec((D, fc), lambda b, j, f: (0, f)),
                  pl.BlockSpec((D, fc), lambda b, j, f: (0, nf + f)),
                  pl.BlockSpec((fc, D), lambda b, j, f: (f, 0))],
        out_specs=row(D),
        out_shape=jax.ShapeDtypeStruct((B, T, D), F32),
        scratch_shapes=[pltpu.VMEM((tm, D), F32)],
        compiler_params=_params(("parallel", "parallel", "arbitrary")),
    )(h2, xt, mod_l, mod_c, w_gu, w_gu, w_down)


def _sc_gather(table, idx):
    _, width = table.shape
    m = idx.shape[0]
    workers = SC_CORES * SC_SUBCORES
    ch = SC_INDEX_CHUNK
    assert width == LANES and m % (workers * ch) == 0
    n_chunks = m // (workers * ch)
    nb = max(b for b in (4, 3, 2, 1) if n_chunks % b == 0)
    mesh = plsc.VectorSubcoreMesh(core_axis_name="c", subcore_axis_name="s",
                                  num_cores=SC_CORES, num_subcores=SC_SUBCORES)

    @functools.partial(
        pl.kernel, mesh=mesh, out_type=jax.ShapeDtypeStruct((m, width), table.dtype),
        scratch_types=[pltpu.VMEM((n_chunks, ch), jnp.int32), pltpu.VMEM((nb, ch, width), table.dtype)]
        + [pltpu.SemaphoreType.DMA] * (2 * nb))
    def gather(table_hbm, idx_hbm, out_hbm, idx_v, rows_v, *sems):
        wid = lax.axis_index("s") * SC_CORES + lax.axis_index("c")
        base = wid * (n_chunks * ch)
        pltpu.sync_copy(idx_hbm.at[wid], idx_v)

        def fetch(c, b):
            return pltpu.make_async_copy(table_hbm.at[idx_v.at[c]], rows_v.at[b], sems[b])

        def flush(c, b):
            return pltpu.make_async_copy(rows_v.at[b], out_hbm.at[pl.ds(base + c * ch, ch)], sems[nb + b])

        @pl.loop(0, n_chunks, step=nb)
        def _(c0):
            for b in range(nb):
                @pl.when(c0 > 0)
                def _():
                    flush(c0 - nb + b, b).wait()
                fetch(c0 + b, b).start()
            for b in range(nb):
                fetch(c0 + b, b).wait()
                flush(c0 + b, b).start()

        for b in range(nb):
            flush(n_chunks - nb + b, b).wait()

    return gather(table, idx.reshape(workers, n_chunks, ch))


def _routing(route, tile):
    B, T, _ = route.shape
    n = B * T
    assert (2 * n) % tile == 0
    ids = route[..., :2].astype(jnp.int32).reshape(n, 2)
    e_cat = jnp.concatenate([ids[:, 0], ids[:, 1]])
    onehot = (e_cat[:, None] == jnp.arange(N_EXPERTS, dtype=jnp.int32)[None, :]).astype(jnp.int32)
    csum = jnp.cumsum(onehot, axis=0)
    pos = jnp.sum(onehot * csum, axis=1) - 1
    padded = (csum[-1] + tile - 1) // tile * tile
    ends = jnp.cumsum(padded)
    dest = ((ends - padded)[e_cat] + pos).astype(jnp.int32)
    n_rows = 2 * n + N_EXPERTS * tile
    token = jnp.arange(2 * n, dtype=jnp.int32) % n
    row_token = jnp.zeros((n_rows,), jnp.int32).at[dest].set(token, unique_indices=True)
    tile_start = jnp.arange(n_rows // tile, dtype=jnp.int32) * tile
    tile_expert = jnp.minimum(jnp.searchsorted(ends, tile_start, side="right"), N_EXPERTS - 1).astype(jnp.int32)
    tile_valid = (tile_start < ends[-1]).astype(jnp.int32)
    return dest, row_token, tile_expert, tile_valid


def _expert_kernel(te_ref, tv_ref, xs_ref, wg_ref, wu_ref, wd_ref, ys_ref):
    i = pl.program_id(0)

    @pl.when(tv_ref[i] > 0)
    def _compute():
        lo, hi = _unpack_bf16_pairs(_load_packed(xs_ref))
        lo = lo.astype(BF16)
        hi = hi.astype(BF16)
        g = (jnp.dot(lo, wg_ref[:PACK_W, :], preferred_element_type=F32)
             + jnp.dot(hi, wg_ref[PACK_W:, :], preferred_element_type=F32))
        u = (jnp.dot(lo, wu_ref[:PACK_W, :], preferred_element_type=F32)
             + jnp.dot(hi, wu_ref[PACK_W:, :], preferred_element_type=F32))
        y = jnp.dot((_silu(g) * u).astype(BF16), wd_ref[...], preferred_element_type=F32)
        _store_packed(ys_ref, _pack_bf16_pairs(y))

    @pl.when(tv_ref[i] == 0)
    def _unused_tile():
        ys_ref[...] = jnp.zeros_like(ys_ref)


def _experts(xs, tile_expert, tile_valid, w_gu, w_down):
    _, n_rows, _ = xs.shape
    E, F, D = w_down.shape
    tile = EXPERT_TILE
    rows = pl.BlockSpec((PACK_CHUNKS, tile, LANES), lambda i, te, tv: (0, i, 0))
    return pl.pallas_call(
        _expert_kernel,
        grid_spec=pltpu.PrefetchScalarGridSpec(
            num_scalar_prefetch=2,
            grid=(n_rows // tile,),
            in_specs=[rows,
                      pl.BlockSpec((None, D, F), lambda i, te, tv: (te[i], 0, 0)),
                      pl.BlockSpec((None, D, F), lambda i, te, tv: (te[i], 0, 1)),
                      pl.BlockSpec((None, F, D), lambda i, te, tv: (te[i], 0, 0))],
            out_specs=rows),
        out_shape=jax.ShapeDtypeStruct(xs.shape, jnp.int32),
        compiler_params=_params(("arbitrary",)),
    )(tile_expert, tile_valid, xs, w_gu, w_gu, w_down)


def _combine_kernel(x_ref, g2_ref, cg2_ref, route_ref, ya_ref, yb_ref, o_ref, *, tm, ctx_len):
    is_ctx = _row_is_ctx(pl.program_id(1), tm, ctx_len)
    route = route_ref[...]
    lane = lax.broadcasted_iota(jnp.int32, route.shape, 1)
    gate_a = jnp.sum(jnp.where(lane == 2, route, 0.0), axis=-1, keepdims=True)
    gate_b = jnp.sum(jnp.where(lane == 3, route, 0.0), axis=-1, keepdims=True)
    unpack = lambda ref: jnp.concatenate(_unpack_bf16_pairs(_load_packed(ref)), axis=1)
    f = gate_a * unpack(ya_ref) + gate_b * unpack(yb_ref)
    o_ref[...] = x_ref[...] + jnp.where(is_ctx, cg2_ref[...], g2_ref[...]) * f


def _combine(xt, mod_l, mod_c, route, yg, ctx_len):
    B, T, D = xt.shape
    tm = _pick_tile(T, 640)
    nt = T // tm
    row = lambda w: pl.BlockSpec((None, tm, w), lambda b, j: (b, j, 0))
    slot = lambda s: pl.BlockSpec((PACK_CHUNKS, None, tm, LANES), lambda b, j: (0, s, b * nt + j, 0))
    return pl.pallas_call(
        functools.partial(_combine_kernel, tm=tm, ctx_len=ctx_len),
        grid=(B, nt),
        in_specs=[row(D), *_mod_specs(5, D), row(LANES), slot(0), slot(1)],
        out_specs=row(D),
        out_shape=jax.ShapeDtypeStruct((B, T, D), F32),
        compiler_params=_params(("parallel", "parallel")),
    )(xt, mod_l, mod_c, route, yg, yg)


def _moe(hp, route, xt, mod_l, mod_c, w_gu, w_down, ctx_len):
    B, T, _ = xt.shape
    n = B * T
    dest, row_token, tile_expert, tile_valid = _routing(route, EXPERT_TILE)
    n_rows = row_token.shape[0]
    chunk = jnp.arange(PACK_CHUNKS, dtype=jnp.int32)[:, None]
    xs = _sc_gather(hp.reshape(PACK_CHUNKS * n, LANES), (chunk * n + row_token[None, :]).reshape(-1))
    ys = _experts(xs.reshape(PACK_CHUNKS, n_rows, LANES), tile_expert, tile_valid, w_gu, w_down)
    yg = _sc_gather(ys.reshape(PACK_CHUNKS * n_rows, LANES), (chunk * n_rows + dest[None, :]).reshape(-1))
    return _combine(xt, mod_l, mod_c, route, yg.reshape(PACK_CHUNKS, 2, n, LANES), ctx_len)


def _final_norm_kernel(x_ref, g_ref, o_ref):
    x = x_ref[...]
    o_ref[...] = x * lax.rsqrt(jnp.mean(x * x, axis=-1, keepdims=True) + EPS) * g_ref[...]


def _final_norm(xt, g, ctx_len):
    B, T, D = xt.shape
    S = T - ctx_len
    tm = SEQ_TILE
    assert ctx_len % tm == 0 and S % tm == 0
    skip = ctx_len // tm
    return pl.pallas_call(
        _final_norm_kernel,
        grid=(B, S // tm),
        in_specs=[pl.BlockSpec((None, tm, D), lambda b, j: (b, j + skip, 0)),
                  pl.BlockSpec((1, D), lambda b, j: (0, 0))],
        out_specs=pl.BlockSpec((None, tm, D), lambda b, j: (b, j, 0)),
        out_shape=jax.ShapeDtypeStruct((B, S, D), F32),
        compiler_params=_params(("parallel", "parallel")),
    )(xt, g)


def _rope_tables(seq, ctx_len):
    pos = jnp.arange(seq, dtype=jnp.int32)
    nf = HEAD_DIM // 4
    inv = ROPE_THETA ** (-jnp.arange(nf, dtype=F32) / nf)
    ang = jnp.concatenate([(pos // GRID_W).astype(F32)[:, None] * inv, (pos % GRID_W).astype(F32)[:, None] * inv], axis=-1)
    cos, sin = jnp.cos(ang), jnp.sin(ang)
    reps = LANES // HEAD_DIM
    cos_t = jnp.tile(jnp.concatenate([cos, cos], axis=-1), (1, reps))
    sin_t = jnp.tile(jnp.concatenate([-sin, sin], axis=-1), (1, reps))
    cos_t = jnp.concatenate([jnp.ones((ctx_len, LANES), F32), cos_t], axis=0)
    sin_t = jnp.concatenate([jnp.zeros((ctx_len, LANES), F32), sin_t], axis=0)
    return cos_t, sin_t


def _relayout_w_in(w_in):
    o_lr = ATT_W + 2 * KV_W + 5 * GLA_W
    q_scale = HEAD_DIM ** -0.5
    gq_scale = GLA_DK ** -0.5
    parts = [w_in[..., :ATT_W] * q_scale, w_in[..., ATT_W:COL_GQ], w_in[..., COL_GQ:COL_GK] * gq_scale,
             w_in[..., COL_GK:o_lr], w_in[..., o_lr + 2 * GLA_LOWRANK:], w_in[..., o_lr:o_lr + 2 * GLA_LOWRANK],
             jnp.zeros(w_in.shape[:-1] + (LANES - 2 * GLA_LOWRANK,), w_in.dtype)]
    return jnp.concatenate(parts, axis=-1).astype(BF16)


def kernel(x, c, ctx, c_ctx, w_mod, b_mod, norm1_g, norm2_g, w_in, w_out, attn_sink, gla_w2_f, gla_b_f, gla_w2_b, gla_b_b, gla_norm_g, conv_w, ffn_w_gu, ffn_w_down, router_w, expert_w_gu, expert_w_down, final_norm_g):
    B, S, D = x.shape
    L = ctx.shape[1]
    depth = w_in.shape[0]
    assert D == D_MODEL and S % GRID_W == 0

    xt = jnp.concatenate([ctx, x], axis=1)
    rows = 16
    cc = jnp.concatenate([c, c_ctx[None, :], jnp.zeros((rows - B - 1, D), F32)], axis=0)
    mod = _adaln(cc, w_mod, b_mod)
    cos_t, sin_t = _rope_tables(S, L)

    w_in_p = _relayout_w_in(w_in)
    w_out_b = w_out.astype(BF16)
    zpad = jnp.zeros((depth, LANES - 2 * GLA_LOWRANK, GLA_W), F32)
    w2_f = jnp.concatenate([gla_w2_f, jnp.zeros_like(gla_w2_b), zpad], axis=1).astype(BF16)
    w2_b = jnp.concatenate([jnp.zeros_like(gla_w2_f), gla_w2_b, zpad], axis=1).astype(BF16)
    gla_ng = jnp.tile(gla_norm_g, (1, GLA_HEADS))
    ffn_gu_b = ffn_w_gu.astype(BF16)
    ffn_down_b = ffn_w_down.astype(BF16)
    exp_gu_b = expert_w_gu.astype(BF16)
    exp_down_b = expert_w_down.astype(BF16)
    router_p = jnp.pad(router_w, ((0, 0), (0, 0), (0, LANES - N_EXPERTS)))
    router_hi = router_p.astype(BF16)
    router_lo = (router_p - router_hi.astype(F32)).astype(BF16)

    for l in range(depth):
        mod_l = mod[l, :B].reshape(B, 1, 6 * D)
        mod_c = mod[l, B:B + 1]
        proj = _inproj(xt, norm1_g[l][None, :], mod_l, mod_c, cos_t, sin_t, w_in_p[l], L)
        att = _attention(proj, attn_sink[l], L)
        cv = _short_conv(proj, conv_w[l], L)
        gf = _gla(proj, w2_f[l], gla_b_f[l][None, :], gla_ng[l][None, :], L, reverse=False)
        gb = _gla(proj, w2_b[l], gla_b_b[l][None, :], gla_ng[l][None, :], L, reverse=True)
        if l % 2 == 0:
            xt, h2 = _outproj(xt, att, gf, gb, cv, w_out_b[l], norm2_g[l][None, :], mod_l, mod_c, L)
            xt = _ffn(h2, xt, mod_l, mod_c, ffn_gu_b[l // 2], ffn_down_b[l // 2], L)
        else:
            xt, hp, route = _outproj(xt, att, gf, gb, cv, w_out_b[l], norm2_g[l][None, :], mod_l, mod_c, L,
                                     router=(router_hi[l // 2], router_lo[l // 2]))
            xt = _moe(hp, route, xt, mod_l, mod_c, exp_gu_b[l // 2], exp_down_b[l // 2], L)
    return _final_norm(xt, final_norm_g[None, :], L)
```

```python
import functools

import jax
import jax.numpy as jnp
from jax import lax
from jax.experimental import pallas as pl
from jax.experimental.pallas import tpu as pltpu
from jax.experimental.pallas import tpu_sc as plsc

F32 = jnp.float32
BF16 = jnp.bfloat16

D_MODEL = 1024
EPS = 1e-6
GRID_W = 64
ROPE_THETA = 10000.0
HEAD_DIM = 64
ATT_HEADS = 8
ATT_KV_HEADS = 2
ATT_GROUP = ATT_HEADS // ATT_KV_HEADS
ATT_W = ATT_HEADS * HEAD_DIM
KV_W = ATT_KV_HEADS * HEAD_DIM
WINDOW = 128
GLA_HEADS = 4
GLA_DK = 64
GLA_W = GLA_HEADS * GLA_DK
GLA_LOWRANK = 16
GLA_TAU = 16.0
GLA_CHUNK = 64
CONV_CH = 256
N_EXPERTS = 8

LANES = 128
SUBLANES_BF16 = 16

COL_Q = 0
COL_K = ATT_W
COL_V = COL_K + KV_W
ROPE_W = COL_V
COL_GQ = COL_V + KV_W
COL_GK = COL_GQ + GLA_W
COL_GV = COL_GK + GLA_W
COL_GGF = COL_GV + GLA_W
COL_GGB = COL_GGF + GLA_W
COL_CB = COL_GGB + GLA_W
COL_CC = COL_CB + CONV_CH
COL_CX = COL_CC + CONV_CH
COL_LR = COL_CX + CONV_CH
PROJ_W = COL_LR + LANES

SEQ_TILE = 256
VMEM_LIMIT = 56 * 1024 * 1024

PACK_W = D_MODEL // 2
PACK_CHUNKS = PACK_W // LANES
EXPERT_TILE = 512
SC_CORES = 2
SC_SUBCORES = 16
SC_INDEX_CHUNK = 128


def _silu(v):
    return v / (1.0 + jnp.exp(-v))


def _pick_tile(total, cap):
    best = None
    for t in range(SUBLANES_BF16, cap + 1, SUBLANES_BF16):
        if total % t == 0:
            best = t
    assert best is not None
    return best


def _params(sem):
    return pltpu.CompilerParams(dimension_semantics=sem, vmem_limit_bytes=VMEM_LIMIT)


def _pack_bf16_pairs(v):
    bits = lambda t: lax.bitcast_convert_type(t.astype(BF16).astype(F32), jnp.int32)
    return ((bits(v[:, :PACK_W]) >> 16) & 0xFFFF) | (bits(v[:, PACK_W:]) & -65536)


def _unpack_bf16_pairs(w):
    return lax.bitcast_convert_type(w << 16, F32), lax.bitcast_convert_type(w & -65536, F32)


def _load_packed(ref):
    return jnp.concatenate([ref[k] for k in range(PACK_CHUNKS)], axis=1)


def _store_packed(ref, packed):
    for k in range(PACK_CHUNKS):
        ref[k] = packed[:, k * LANES:(k + 1) * LANES]


def _adaln_kernel(c_ref, w_ref, b_ref, o_ref):
    s = _silu(c_ref[...]).astype(BF16)
    o_ref[...] = jnp.dot(s, w_ref[...].astype(BF16), preferred_element_type=F32) + b_ref[...]


def _adaln(cc, w_mod, b_mod):
    depth, d, n = w_mod.shape
    tn = 1024
    rows = cc.shape[0]
    return pl.pallas_call(
        _adaln_kernel,
        grid=(depth, n // tn),
        in_specs=[pl.BlockSpec((rows, d), lambda l, j: (0, 0)),
                  pl.BlockSpec((None, d, tn), lambda l, j: (l, 0, j)),
                  pl.BlockSpec((None, 1, tn), lambda l, j: (l, 0, j))],
        out_specs=pl.BlockSpec((None, rows, tn), lambda l, j: (l, 0, j)),
        out_shape=jax.ShapeDtypeStruct((depth, rows, n), F32),
        compiler_params=_params(("parallel", "parallel")),
    )(cc, w_mod, b_mod.reshape(depth, 1, n))


def _mod_specs(k, d):
    return [pl.BlockSpec((None, 1, d), lambda b, j, *_: (b, 0, k)),
            pl.BlockSpec((1, d), lambda b, j, *_: (0, k))]


def _row_is_ctx(j, tm, ctx_len):
    return (j * tm + lax.broadcasted_iota(jnp.int32, (tm, 1), 0)) < ctx_len


def _inproj_kernel(x_ref, g_ref, sc_ref, csc_ref, sh_ref, csh_ref, cos_ref, sin_ref, w_ref, o_ref, *, tm, ctx_len):
    x = x_ref[...]
    n = x * lax.rsqrt(jnp.mean(x * x, axis=-1, keepdims=True) + EPS) * g_ref[...]
    is_ctx = _row_is_ctx(pl.program_id(1), tm, ctx_len)
    scale = jnp.where(is_ctx, csc_ref[...], sc_ref[...])
    shift = jnp.where(is_ctx, csh_ref[...], sh_ref[...])
    h = (n * (1.0 + scale) + shift).astype(BF16)
    qk = jnp.dot(h, w_ref[:, :ROPE_W], preferred_element_type=F32)
    lower_half = (lax.broadcasted_iota(jnp.int32, (tm, LANES), 1) % HEAD_DIM) < (HEAD_DIM // 2)
    cos = cos_ref[...]
    sin = sin_ref[...]
    for i in range(ROPE_W // LANES):
        t = qk[:, i * LANES:(i + 1) * LANES]
        partner = jnp.where(lower_half, pltpu.roll(t, LANES - HEAD_DIM // 2, axis=1), pltpu.roll(t, HEAD_DIM // 2, axis=1))
        o_ref[:, i * LANES:(i + 1) * LANES] = (t * cos + partner * sin).astype(BF16)
    o_ref[:, ROPE_W:] = jnp.dot(h, w_ref[:, ROPE_W:], preferred_element_type=F32).astype(BF16)


def _inproj(xt, g, mod_l, mod_c, cos_t, sin_t, w, ctx_len):
    B, T, D = xt.shape
    tm = _pick_tile(T, 640)
    kern = functools.partial(_inproj_kernel, tm=tm, ctx_len=ctx_len)
    return pl.pallas_call(
        kern,
        grid=(B, T // tm),
        in_specs=[pl.BlockSpec((None, tm, D), lambda b, j: (b, j, 0)),
                  pl.BlockSpec((1, D), lambda b, j: (0, 0)),
                  *_mod_specs(1, D), *_mod_specs(0, D),
                  pl.BlockSpec((tm, LANES), lambda b, j: (j, 0)),
                  pl.BlockSpec((tm, LANES), lambda b, j: (j, 0)),
                  pl.BlockSpec((D, PROJ_W), lambda b, j: (0, 0))],
        out_specs=pl.BlockSpec((None, tm, PROJ_W), lambda b, j: (b, j, 0)),
        out_shape=jax.ShapeDtypeStruct((B, T, PROJ_W), BF16),
        compiler_params=_params(("parallel", "parallel")),
    )(xt, g, mod_l, mod_c, mod_l, mod_c, cos_t, sin_t, w)


def _softmax_pv(s, sink, v):
    m = jnp.maximum(jnp.max(s, axis=-1, keepdims=True), sink)
    p = jnp.exp(s - m)
    den = jnp.sum(p, axis=-1, keepdims=True) + jnp.exp(sink - m)
    return jnp.dot(p.astype(BF16), v, preferred_element_type=F32) / den


def _attn_kernel(sink_ref, q_ref, kc_ref, vc_ref, km_ref, vm_ref, kp_ref, vp_ref, kn_ref, vn_ref, o_ref, *, n_tiles):
    j = pl.program_id(1)
    nt_dims = (((1,), (1,)), ((), ()))

    def group_attention(rows, n_rows, g, k, v, mask):
        heads = [g * ATT_GROUP + hh for hh in range(ATT_GROUP)]
        q = jnp.concatenate([q_ref[rows, h * HEAD_DIM:(h + 1) * HEAD_DIM] for h in heads], axis=0)
        s = lax.dot_general(q, k, nt_dims, preferred_element_type=F32)
        if mask is not None:
            s = jnp.where(mask, s, -1e30)
        head_row = lax.broadcasted_iota(jnp.int32, (ATT_GROUP * n_rows, 1), 0) // n_rows
        sink = jnp.full((ATT_GROUP * n_rows, 1), sink_ref[heads[0]], F32)
        for hh in range(1, ATT_GROUP):
            sink = jnp.where(head_row == hh, sink_ref[heads[hh]], sink)
        o = _softmax_pv(s, sink, v)
        for hh, h in enumerate(heads):
            o_ref[rows, h * HEAD_DIM:(h + 1) * HEAD_DIM] = o[hh * n_rows:(hh + 1) * n_rows].astype(BF16)

    @pl.when(j == 0)
    def _context_queries():
        for g in range(ATT_KV_HEADS):
            gs = slice(g * HEAD_DIM, (g + 1) * HEAD_DIM)
            group_attention(slice(0, SEQ_TILE), SEQ_TILE, g, kc_ref[:, gs], vc_ref[:, gs], None)

    @pl.when(j > 0)
    def _latent_queries():
        n_ctx = kc_ref.shape[0]
        n_keys = n_ctx + 3 * WINDOW
        r = lax.broadcasted_iota(jnp.int32, (ATT_GROUP * WINDOW, n_keys), 0) % WINDOW
        kcol = lax.broadcasted_iota(jnp.int32, (ATT_GROUP * WINDOW, n_keys), 1)
        w = kcol - n_ctx
        band = (w >= r) & (w <= r + 2 * WINDOW)
        for sub in range(SEQ_TILE // WINDOW):
            rows = slice(sub * WINDOW, (sub + 1) * WINDOW)
            if sub == 0:
                kprev, vprev = kp_ref[...], vp_ref[...]
                kmid, vmid = km_ref[:WINDOW], vm_ref[:WINDOW]
                knext, vnext = km_ref[WINDOW:], vm_ref[WINDOW:]
                w_lo = jnp.where(j > 1, 0, WINDOW)
                w_hi = 3 * WINDOW
            else:
                kprev, vprev = km_ref[:WINDOW], vm_ref[:WINDOW]
                kmid, vmid = km_ref[WINDOW:], vm_ref[WINDOW:]
                knext, vnext = kn_ref[...], vn_ref[...]
                w_lo = 0
                w_hi = jnp.where(j < n_tiles - 1, 3 * WINDOW, 2 * WINDOW)
            mask = (kcol < n_ctx) | (band & (w >= w_lo) & (w < w_hi))
            for g in range(ATT_KV_HEADS):
                gs = slice(g * HEAD_DIM, (g + 1) * HEAD_DIM)
                kcat = jnp.concatenate([kc_ref[:, gs], kprev[:, gs], kmid[:, gs], knext[:, gs]], axis=0)
                vcat = jnp.concatenate([vc_ref[:, gs], vprev[:, gs], vmid[:, gs], vnext[:, gs]], axis=0)
                group_attention(rows, WINDOW, g, kcat, vcat, mask)


def _attention(proj, sink, ctx_len):
    B, T, _ = proj.shape
    assert ctx_len == SEQ_TILE and T % SEQ_TILE == 0
    n_tiles = T // SEQ_TILE
    n_win = T // WINDOW
    kcol, vcol = COL_K // KV_W, COL_V // KV_W
    per_tile = SEQ_TILE // WINDOW
    tile = lambda col: pl.BlockSpec((None, SEQ_TILE, KV_W), lambda b, j: (b, j, col))
    ctx = lambda col: pl.BlockSpec((None, SEQ_TILE, KV_W), lambda b, j: (b, 0, col))
    prev = lambda col: pl.BlockSpec((None, WINDOW, KV_W), lambda b, j: (b, jnp.maximum(j * per_tile - 1, 0), col))
    nxt = lambda col: pl.BlockSpec((None, WINDOW, KV_W), lambda b, j: (b, jnp.minimum((j + 1) * per_tile, n_win - 1), col))
    return pl.pallas_call(
        functools.partial(_attn_kernel, n_tiles=n_tiles),
        grid=(B, n_tiles),
        in_specs=[pl.BlockSpec(memory_space=pltpu.SMEM),
                  pl.BlockSpec((None, SEQ_TILE, ATT_W), lambda b, j: (b, j, 0)),
                  ctx(kcol), ctx(vcol), tile(kcol), tile(vcol), prev(kcol), prev(vcol), nxt(kcol), nxt(vcol)],
        out_specs=pl.BlockSpec((None, SEQ_TILE, ATT_W), lambda b, j: (b, j, 0)),
        out_shape=jax.ShapeDtypeStruct((B, T, ATT_W), BF16),
        compiler_params=_params(("parallel", "parallel")),
    )(sink, *([proj] * 9))


def _conv_kernel(b_ref, c_ref, x_ref, cp_ref, xp_ref, cn_ref, xn_ref, w_ref, o_ref, u_scr, *, n_tiles):
    j = pl.program_id(1)
    last = SUBLANES_BF16 - 1
    pad = 8
    u = c_ref[...].astype(F32) * x_ref[...].astype(F32)
    u_prev_row = (cp_ref[...].astype(F32) * xp_ref[...].astype(F32))[last:, :]
    u_next_row = (cn_ref[...].astype(F32) * xn_ref[...].astype(F32))[:1, :]
    jv = jnp.full((1, CONV_CH), j, jnp.int32)
    u_scr[pad:pad + SEQ_TILE, :] = u
    u_scr[pad - 1:pad, :] = jnp.where(jv > 1, u_prev_row, 0.0)
    u_scr[pad + SEQ_TILE:pad + SEQ_TILE + 1, :] = jnp.where((jv > 0) & (jv < n_tiles - 1), u_next_row, 0.0)
    y = (w_ref[0:1, :] * u_scr[pad - 1:pad - 1 + SEQ_TILE, :] + w_ref[1:2, :] * u
         + w_ref[2:3, :] * u_scr[pad + 1:pad + 1 + SEQ_TILE, :])
    o_ref[...] = (b_ref[...].astype(F32) * y).astype(BF16)


def _short_conv(proj, conv_w, ctx_len):
    B, T, _ = proj.shape
    assert ctx_len == SEQ_TILE and T % SEQ_TILE == 0
    n_tiles = T // SEQ_TILE
    n_halo = T // SUBLANES_BF16
    per_tile = SEQ_TILE // SUBLANES_BF16
    bcol, ccol, xcol = COL_CB // CONV_CH, COL_CC // CONV_CH, COL_CX // CONV_CH
    tile = lambda col: pl.BlockSpec((None, SEQ_TILE, CONV_CH), lambda b, j: (b, j, col))
    prev = lambda col: pl.BlockSpec((None, SUBLANES_BF16, CONV_CH), lambda b, j: (b, jnp.maximum(j * per_tile - 1, 0), col))
    nxt = lambda col: pl.BlockSpec((None, SUBLANES_BF16, CONV_CH), lambda b, j: (b, jnp.minimum((j + 1) * per_tile, n_halo - 1), col))
    return pl.pallas_call(
        functools.partial(_conv_kernel, n_tiles=n_tiles),
        grid=(B, n_tiles),
        in_specs=[tile(bcol), tile(ccol), tile(xcol), prev(ccol), prev(xcol), nxt(ccol), nxt(xcol),
                  pl.BlockSpec((3, CONV_CH), lambda b, j: (0, 0))],
        out_specs=pl.BlockSpec((None, SEQ_TILE, CONV_CH), lambda b, j: (b, j, 0)),
        out_shape=jax.ShapeDtypeStruct((B, T, CONV_CH), BF16),
        scratch_shapes=[pltpu.VMEM((SEQ_TILE + 16, CONV_CH), F32)],
        compiler_params=_params(("parallel", "parallel")),
    )(*([proj] * 7), conv_w)


def _split2(v):
    hi = v.astype(BF16)
    lo = (v - hi.astype(F32)).astype(BF16)
    return hi, lo


def _gla_kernel(q_ref, k_ref, v_ref, gate_ref, lr_ref, w2_ref, bias_ref, ng_ref, o_ref,
                oacc_scr, qs_scr, u_scr, dec_scr, sin_scr, *, reverse, n_ctx_chunks):
    T = q_ref.shape[0]
    C = GLA_CHUNK
    n_tiles = T // SEQ_TILE
    n_chunks = T // C
    per_tile = SEQ_TILE // C
    nt_dims = (((1,), (1,)), ((), ()))
    tn_dims = (((0,), (0,)), ((), ()))

    r = lax.broadcasted_iota(jnp.int32, (SEQ_TILE, SEQ_TILE), 0)
    c = lax.broadcasted_iota(jnp.int32, (SEQ_TILE, SEQ_TILE), 1)
    same_chunk = (r // C) == (c // C)
    causal = same_chunk & ((c >= r) if reverse else (c <= r))
    cum_mat = jnp.where(causal, 1.0, 0.0).astype(BF16)
    head_mean = jnp.where(same_chunk, 1.0 / GLA_DK, 0.0).astype(BF16)
    lane_head = lax.broadcasted_iota(jnp.int32, (C, GLA_W), 1) // GLA_DK

    def phase1(i, carry):
        rows = pl.ds(pl.multiple_of(i * SEQ_TILE, SEQ_TILE), SEQ_TILE)
        z = jnp.dot(lr_ref[rows, :], w2_ref[...], preferred_element_type=F32) + bias_ref[...]
        la = (jnp.minimum(z, 0.0) - jnp.log(1.0 + jnp.exp(-jnp.abs(z)))) / GLA_TAU
        hi, lo = _split2(la)
        b = jnp.dot(cum_mat, hi, preferred_element_type=F32) + jnp.dot(cum_mat, lo, preferred_element_type=F32)
        b3 = b.reshape(per_tile, C, GLA_W)
        total = b3[:, 0:1, :] if reverse else b3[:, C - 1:C, :]
        b_last = jnp.broadcast_to(total, (per_tile, C, GLA_W)).reshape(SEQ_TILE, GLA_W)
        b_ref = 0.5 * b_last
        q = q_ref[rows, :].astype(F32)
        k = k_ref[rows, :].astype(F32)
        v = v_ref[rows, :]
        qe = (q * jnp.exp(b - b_ref)).astype(BF16)
        ke = (k * jnp.exp(b_ref - b)).astype(BF16)
        ku = (k * jnp.exp(b_last - b)).astype(BF16)
        qs_scr[rows, :] = (q * jnp.exp(b)).astype(BF16)
        dec = jnp.exp(b_last)
        for h in range(GLA_HEADS):
            hs = slice(h * GLA_DK, (h + 1) * GLA_DK)
            s = lax.dot_general(qe[:, hs], ke[:, hs], nt_dims, preferred_element_type=F32)
            a = jnp.where(causal, s, 0.0).astype(BF16)
            oacc_scr[rows, hs] = jnp.dot(a, v[:, hs], preferred_element_type=F32)
        for ci in range(per_tile):
            cs = slice(ci * C, (ci + 1) * C)
            full = lax.dot_general(v[cs, :], ku[cs, :], tn_dims, preferred_element_type=F32)
            ut = full[0:C, :]
            for h in range(1, GLA_HEADS):
                ut = jnp.where(lane_head == h, full[h * C:(h + 1) * C, :], ut)
            u_scr[i * per_tile + ci] = ut
            dec_scr[i * per_tile + ci] = dec[ci * C:ci * C + 1, :]
        return carry

    lax.fori_loop(0, n_tiles, phase1, 0)

    def phase2(i, st):
        if reverse:
            ci = jnp.where(i < n_ctx_chunks, n_ctx_chunks - 1 - i, n_chunks - 1 + n_ctx_chunks - i)
        else:
            ci = i
        sin_scr[ci] = st.astype(BF16)
        return st * dec_scr[ci] + u_scr[ci]

    lax.fori_loop(0, n_chunks, phase2, jnp.zeros((C, GLA_W), F32))

    def phase3(i, carry):
        rows = pl.ds(pl.multiple_of(i * SEQ_TILE, SEQ_TILE), SEQ_TILE)
        qs = qs_scr[rows, :]
        parts = []
        for ci in range(per_tile):
            st = sin_scr[i * per_tile + ci]
            heads = []
            for h in range(GLA_HEADS):
                hs = slice(h * GLA_DK, (h + 1) * GLA_DK)
                heads.append(lax.dot_general(qs[ci * C:(ci + 1) * C, hs], st[:, hs], nt_dims,
                                             preferred_element_type=F32))
            parts.append(jnp.concatenate(heads, axis=1))
        o = oacc_scr[rows, :] + jnp.concatenate(parts, axis=0)
        ms = jnp.dot((o * o).astype(BF16), head_mean, preferred_element_type=F32)
        y = o * lax.rsqrt(ms + EPS) * ng_ref[...]
        o_ref[rows, :] = (y * _silu(gate_ref[rows, :].astype(F32))).astype(BF16)
        return carry

    lax.fori_loop(0, n_tiles, phase3, 0)


def _gla(proj, w2pad, bias, ng, ctx_len, reverse):
    B, T, _ = proj.shape
    assert T % SEQ_TILE == 0 and ctx_len % GLA_CHUNK == 0
    n_chunks = T // GLA_CHUNK
    col = lambda c0: pl.BlockSpec((None, T, GLA_W), lambda b: (b, 0, c0 // GLA_W))
    gate_col = COL_GGB if reverse else COL_GGF
    kern = functools.partial(_gla_kernel, reverse=reverse, n_ctx_chunks=ctx_len // GLA_CHUNK)
    return pl.pallas_call(
        kern,
        grid=(B,),
        in_specs=[col(COL_GQ), col(COL_GK), col(COL_GV), col(gate_col),
                  pl.BlockSpec((None, T, LANES), lambda b: (b, 0, COL_LR // LANES)),
                  pl.BlockSpec((LANES, GLA_W), lambda b: (0, 0)),
                  pl.BlockSpec((1, GLA_W), lambda b: (0, 0)),
                  pl.BlockSpec((1, GLA_W), lambda b: (0, 0))],
        out_specs=pl.BlockSpec((None, T, GLA_W), lambda b: (b, 0, 0)),
        out_shape=jax.ShapeDtypeStruct((B, T, GLA_W), BF16),
        scratch_shapes=[pltpu.VMEM((T, GLA_W), F32),
                        pltpu.VMEM((T, GLA_W), BF16),
                        pltpu.VMEM((n_chunks, GLA_CHUNK, GLA_W), F32),
                        pltpu.VMEM((n_chunks, 1, GLA_W), F32),
                        pltpu.VMEM((n_chunks, GLA_CHUNK, GLA_W), BF16)],
        compiler_params=_params(("parallel",)),
    )(proj, proj, proj, proj, proj, w2pad, bias, ng)


def _outproj_kernel(*refs, tm, ctx_len, moe):
    if moe:
        (x_ref, att_ref, gf_ref, gb_ref, cv_ref, w_ref, g1_ref, cg1_ref, ng_ref, sc_ref, csc_ref, sh_ref, csh_ref,
         rhi_ref, rlo_ref, xo_ref, hp_ref, route_ref) = refs
    else:
        (x_ref, att_ref, gf_ref, gb_ref, cv_ref, w_ref, g1_ref, cg1_ref, ng_ref, sc_ref, csc_ref, sh_ref, csh_ref,
         xo_ref, h_ref) = refs
    is_ctx = _row_is_ctx(pl.program_id(1), tm, ctx_len)
    gla = (gf_ref[...].astype(F32) + gb_ref[...].astype(F32)).astype(BF16)
    y = jnp.dot(att_ref[...], w_ref[:ATT_W, :], preferred_element_type=F32)
    y += jnp.dot(gla, w_ref[ATT_W:ATT_W + GLA_W, :], preferred_element_type=F32)
    y += jnp.dot(cv_ref[...], w_ref[ATT_W + GLA_W:, :], preferred_element_type=F32)
    x = x_ref[...] + jnp.where(is_ctx, cg1_ref[...], g1_ref[...]) * y
    xo_ref[...] = x
    n = x * lax.rsqrt(jnp.mean(x * x, axis=-1, keepdims=True) + EPS) * ng_ref[...]
    scale = jnp.where(is_ctx, csc_ref[...], sc_ref[...])
    shift = jnp.where(is_ctx, csh_ref[...], sh_ref[...])
    h = n * (1.0 + scale) + shift
    hi = h.astype(BF16)
    if not moe:
        h_ref[...] = hi
    else:
        _store_packed(hp_ref, _pack_bf16_pairs(h))
        lo =(h - hi.astype(F32)).astype(BF16)
        logits = (jnp.dot(hi, rhi_ref[...], preferred_element_type=F32)
                  + jnp.dot(lo, rhi_ref[...], preferred_element_type=F32)
                  + jnp.dot(hi, rlo_ref[...], preferred_element_type=F32))
        lane = lax.broadcasted_iota(jnp.int32, logits.shape, 1)
        neg = -jnp.inf
        lg = jnp.where(lane < N_EXPERTS, logits, neg)
        m1 = jnp.max(lg, axis=-1, keepdims=True)
        i1 = jnp.min(jnp.where(lg == m1, lane, LANES), axis=-1, keepdims=True)
        lg2 = jnp.where(lane == i1, neg, lg)
        m2 = jnp.max(lg2, axis=-1, keepdims=True)
        i2 = jnp.min(jnp.where(lg2 == m2, lane, LANES), axis=-1, keepdims=True)
        e2 = jnp.exp(m2 - m1)
        gate1 = 1.0 / (1.0 + e2)
        gate2 = e2 / (1.0 + e2)
        route_ref[...] = jnp.where(lane == 0, i1.astype(F32), jnp.where(lane == 1, i2.astype(F32),
                                   jnp.where(lane == 2, gate1, jnp.where(lane == 3, gate2, 0.0))))


def _outproj(xt, att, gf, gb, cv, w_out, ng, mod_l, mod_c, ctx_len, router=None):
    B, T, D = xt.shape
    tm = _pick_tile(T, 640)
    moe = router is not None
    row = lambda w: pl.BlockSpec((None, tm, w), lambda b, j: (b, j, 0))
    const = lambda shape: pl.BlockSpec(shape, lambda b, j: (0,) * len(shape))
    in_specs = [row(D), row(ATT_W), row(GLA_W), row(GLA_W), row(CONV_CH), const((D, D)),
                *_mod_specs(2, D), const((1, D)), *_mod_specs(4, D), *_mod_specs(3, D)]
    args = [xt, att, gf, gb, cv, w_out, mod_l, mod_c, ng, mod_l, mod_c, mod_l, mod_c]
    nt = T // tm
    if moe:
        in_specs += [const((D, LANES)), const((D, LANES))]
        args += list(router)
        out_specs = [row(D), pl.BlockSpec((PACK_CHUNKS, tm, LANES), lambda b, j: (0, b * nt + j, 0)), row(LANES)]
        out_shape = [jax.ShapeDtypeStruct((B, T, D), F32), jax.ShapeDtypeStruct((PACK_CHUNKS, B * T, LANES), jnp.int32),
                     jax.ShapeDtypeStruct((B, T, LANES), F32)]
    else:
        out_specs = [row(D), row(D)]
        out_shape = [jax.ShapeDtypeStruct((B, T, D), F32), jax.ShapeDtypeStruct((B, T, D), BF16)]
    return pl.pallas_call(
        functools.partial(_outproj_kernel, tm=tm, ctx_len=ctx_len, moe=moe),
        grid=(B, T // tm),
        in_specs=in_specs, out_specs=out_specs, out_shape=out_shape,
        compiler_params=_params(("parallel", "parallel")),
    )(*args)


def _ffn_kernel(h_ref, x_ref, g2_ref, cg2_ref, wg_ref, wu_ref, wd_ref, o_ref, acc_ref, *, tm, ctx_len):
    j = pl.program_id(1)
    f = pl.program_id(2)

    @pl.when(f == 0)
    def _init():
        acc_ref[...] = jnp.zeros_like(acc_ref)

    h = h_ref[...]
    g = jnp.dot(h, wg_ref[...], preferred_element_type=F32)
    u = jnp.dot(h, wu_ref[...], preferred_element_type=F32)
    acc_ref[...] += jnp.dot((_silu(g) * u).astype(BF16), wd_ref[...], preferred_element_type=F32)

    @pl.when(f == pl.num_programs(2) - 1)
    def _finish():
        is_ctx = _row_is_ctx(j, tm, ctx_len)
        o_ref[...] = x_ref[...] + jnp.where(is_ctx, cg2_ref[...], g2_ref[...]) * acc_ref[...]


def _ffn(h2, xt, mod_l, mod_c, w_gu, w_down, ctx_len):
    B, T, D = xt.shape
    F = w_down.shape[0]
    fc = 1408
    assert F % fc == 0
    nf = F // fc
    tm = _pick_tile(T, 640)
    row = lambda w: pl.BlockSpec((None, tm, w), lambda b, j, f: (b, j, 0))
    return pl.pallas_call(
        functools.partial(_ffn_kernel, tm=tm, ctx_len=ctx_len),
        grid=(B, T // tm, nf),
        in_specs=[row(D), row(D), *_mod_specs(5, D),
                  pl.BlockSpec((D, fc), lambda b, j, f: (0, f)),
                  pl.BlockSpec((D, fc), lambda b, j, f: (0, nf + f)),
                  pl.BlockSpec((fc, D), lambda b, j, f: (f, 0))],
        out_specs=row(D),
        out_shape=jax.ShapeDtypeStruct((B, T, D), F32),
        scratch_shapes=[pltpu.VMEM((tm, D), F32)],
        compiler_params=_params(("parallel", "parallel", "arbitrary")),
    )(h2, xt, mod_l, mod_c, w_gu, w_gu, w_down)


def _sc_gather(table, idx):
    _, width = table.shape
    m = idx.shape[0]
    workers = SC_CORES * SC_SUBCORES
    ch = SC_INDEX_CHUNK
    assert width == LANES and m % (workers * ch) == 0
    n_chunks = m // (workers * ch)
    nb = max(b for b in (4, 3, 2, 1) if n_chunks % b == 0)
    mesh = plsc.VectorSubcoreMesh(core_axis_name="c", subcore_axis_name="s",
                                  num_cores=SC_CORES, num_subcores=SC_SUBCORES)

    @functools.partial(
        pl.kernel, mesh=mesh, out_type=jax.ShapeDtypeStruct((m, width), table.dtype),
        scratch_types=[pltpu.VMEM((n_chunks, ch), jnp.int32), pltpu.VMEM((nb, ch, width), table.dtype)]
        + [pltpu.SemaphoreType.DMA] * (2 * nb))
    def gather(table_hbm, idx_hbm, out_hbm, idx_v, rows_v, *sems):
        wid = lax.axis_index("s") * SC_CORES + lax.axis_index("c")
        base = wid * (n_chunks * ch)
        pltpu.sync_copy(idx_hbm.at[wid], idx_v)

        def fetch(c, b):
            return pltpu.make_async_copy(table_hbm.at[idx_v.at[c]], rows_v.at[b], sems[b])

        def flush(c, b):
            return pltpu.make_async_copy(rows_v.at[b], out_hbm.at[pl.ds(base + c * ch, ch)], sems[nb + b])

        @pl.loop(0, n_chunks, step=nb)
        def _(c0):
            for b in range(nb):
                @pl.when(c0 > 0)
                def _():
                    flush(c0 - nb + b, b).wait()
                fetch(c0 + b, b).start()
            for b in range(nb):
                fetch(c0 + b, b).wait()
                flush(c0 + b, b).start()

        for b in range(nb):
            flush(n_chunks - nb + b, b).wait()

    return gather(table, idx.reshape(workers, n_chunks, ch))


def _routing(route, tile):
    B, T, _ = route.shape
    n = B * T
    assert (2 * n) % tile == 0
    ids = route[..., :2].astype(jnp.int32).reshape(n, 2)
    e_cat = jnp.concatenate([ids[:, 0], ids[:, 1]])
    onehot = (e_cat[:, None] == jnp.arange(N_EXPERTS, dtype=jnp.int32)[None, :]).astype(jnp.int32)
    csum = jnp.cumsum(onehot, axis=0)
    pos = jnp.sum(onehot * csum, axis=1) - 1
    padded = (csum[-1] + tile - 1) // tile * tile
    ends = jnp.cumsum(padded)
    dest = ((ends - padded)[e_cat] + pos).astype(jnp.int32)
    n_rows = 2 * n + N_EXPERTS * tile
    id_bits = (2 * n - 1).bit_length()
    assert N_EXPERTS << id_bits < 2 ** 31
    order = jnp.sort((e_cat << id_bits) | jnp.arange(2 * n, dtype=jnp.int32)) & ((1 << id_bits) - 1)
    max_shift = N_EXPERTS * tile
    base = jnp.concatenate([jnp.zeros((max_shift,), jnp.int32), order, jnp.zeros((max_shift,), jnp.int32)])
    shift = (ends - padded) - (jnp.cumsum(csum[-1]) - csum[-1])
    row = jnp.arange(n_rows, dtype=jnp.int32)
    row_src = jnp.zeros((n_rows,), jnp.int32)
    for e in range(N_EXPERTS):
        shifted = lax.dynamic_slice(base, (max_shift - shift[e],), (n_rows,))
        row_src = jnp.where(row >= (ends - padded)[e], shifted, row_src)
    row_token = row_src % n
    tile_start = jnp.arange(n_rows // tile, dtype=jnp.int32) * tile
    tile_expert = jnp.minimum(jnp.searchsorted(ends, tile_start, side="right"), N_EXPERTS - 1).astype(jnp.int32)
    tile_valid = (tile_start < ends[-1]).astype(jnp.int32)
    return dest, row_token, tile_expert, tile_valid


def _expert_kernel(te_ref, tv_ref, xs_ref, wg_ref, wu_ref, wd_ref, ys_ref):
    i = pl.program_id(0)

    @pl.when(tv_ref[i] > 0)
    def _compute():
        lo, hi = _unpack_bf16_pairs(_load_packed(xs_ref))
        lo = lo.astype(BF16)
        hi = hi.astype(BF16)
        g = (jnp.dot(lo, wg_ref[:PACK_W, :], preferred_element_type=F32)
             + jnp.dot(hi, wg_ref[PACK_W:, :], preferred_element_type=F32))
        u = (jnp.dot(lo, wu_ref[:PACK_W, :], preferred_element_type=F32)
             + jnp.dot(hi, wu_ref[PACK_W:, :], preferred_element_type=F32))
        y = jnp.dot((_silu(g) * u).astype(BF16), wd_ref[...], preferred_element_type=F32)
        _store_packed(ys_ref, _pack_bf16_pairs(y))

    @pl.when(tv_ref[i] == 0)
    def _unused_tile():
        ys_ref[...] = jnp.zeros_like(ys_ref)


def _experts(xs, tile_expert, tile_valid, w_gu, w_down):
    _, n_rows, _ = xs.shape
    E, F, D = w_down.shape
    tile = EXPERT_TILE
    rows = pl.BlockSpec((PACK_CHUNKS, tile, LANES), lambda i, te, tv: (0, i, 0))
    return pl.pallas_call(
        _expert_kernel,
        grid_spec=pltpu.PrefetchScalarGridSpec(
            num_scalar_prefetch=2,
            grid=(n_rows // tile,),
            in_specs=[rows,
                      pl.BlockSpec((None, D, F), lambda i, te, tv: (te[i], 0, 0)),
                      pl.BlockSpec((None, D, F), lambda i, te, tv: (te[i], 0, 1)),
                      pl.BlockSpec((None, F, D), lambda i, te, tv: (te[i], 0, 0))],
            out_specs=rows),
        out_shape=jax.ShapeDtypeStruct(xs.shape, jnp.int32),
        compiler_params=_params(("arbitrary",)),
    )(tile_expert, tile_valid, xs, w_gu, w_gu, w_down)


def _combine_kernel(x_ref, g2_ref, cg2_ref, route_ref, ya_ref, yb_ref, o_ref, *, tm, ctx_len):
    is_ctx = _row_is_ctx(pl.program_id(1), tm, ctx_len)
    route = route_ref[...]
    lane = lax.broadcasted_iota(jnp.int32, route.shape, 1)
    gate_a = jnp.sum(jnp.where(lane == 2, route, 0.0), axis=-1, keepdims=True)
    gate_b = jnp.sum(jnp.where(lane == 3, route, 0.0), axis=-1, keepdims=True)
    unpack = lambda ref: jnp.concatenate(_unpack_bf16_pairs(_load_packed(ref)), axis=1)
    f = gate_a * unpack(ya_ref) + gate_b * unpack(yb_ref)
    o_ref[...] = x_ref[...] + jnp.where(is_ctx, cg2_ref[...], g2_ref[...]) * f


def _combine(xt, mod_l, mod_c, route, yg, ctx_len):
    B, T, D = xt.shape
    tm = _pick_tile(T, 640)
    nt = T // tm
    row = lambda w: pl.BlockSpec((None, tm, w), lambda b, j: (b, j, 0))
    slot = lambda s: pl.BlockSpec((PACK_CHUNKS, None, tm, LANES), lambda b, j: (0, s, b * nt + j, 0))
    return pl.pallas_call(
        functools.partial(_combine_kernel, tm=tm, ctx_len=ctx_len),
        grid=(B, nt),
        in_specs=[row(D), *_mod_specs(5, D), row(LANES), slot(0), slot(1)],
        out_specs=row(D),
        out_shape=jax.ShapeDtypeStruct((B, T, D), F32),
        compiler_params=_params(("parallel", "parallel")),
    )(xt, mod_l, mod_c, route, yg, yg)


def _moe(hp, route, xt, mod_l, mod_c, w_gu, w_down, ctx_len):
    B, T, _ = xt.shape
    n = B * T
    dest, row_token, tile_expert, tile_valid = _routing(route, EXPERT_TILE)
    n_rows = row_token.shape[0]
    chunk = jnp.arange(PACK_CHUNKS, dtype=jnp.int32)[:, None]
    xs = _sc_gather(hp.reshape(PACK_CHUNKS * n, LANES), (chunk * n + row_token[None, :]).reshape(-1))
    ys = _experts(xs.reshape(PACK_CHUNKS, n_rows, LANES), tile_expert, tile_valid, w_gu, w_down)
    yg = _sc_gather(ys.reshape(PACK_CHUNKS * n_rows, LANES), (chunk * n_rows + dest[None, :]).reshape(-1))
    return _combine(xt, mod_l, mod_c, route, yg.reshape(PACK_CHUNKS, 2, n, LANES), ctx_len)


def _final_norm_kernel(x_ref, g_ref, o_ref):
    x = x_ref[...]
    o_ref[...] = x * lax.rsqrt(jnp.mean(x * x, axis=-1, keepdims=True) + EPS) * g_ref[...]


def _final_norm(xt, g, ctx_len):
    B, T, D = xt.shape
    S = T - ctx_len
    tm = SEQ_TILE
    assert ctx_len % tm == 0 and S % tm == 0
    skip = ctx_len // tm
    return pl.pallas_call(
        _final_norm_kernel,
        grid=(B, S // tm),
        in_specs=[pl.BlockSpec((None, tm, D), lambda b, j: (b, j + skip, 0)),
                  pl.BlockSpec((1, D), lambda b, j: (0, 0))],
        out_specs=pl.BlockSpec((None, tm, D), lambda b, j: (b, j, 0)),
        out_shape=jax.ShapeDtypeStruct((B, S, D), F32),
        compiler_params=_params(("parallel", "parallel")),
    )(xt, g)


def _rope_tables(seq, ctx_len):
    pos = jnp.arange(seq, dtype=jnp.int32)
    nf = HEAD_DIM // 4
    inv = ROPE_THETA ** (-jnp.arange(nf, dtype=F32) / nf)
    ang = jnp.concatenate([(pos // GRID_W).astype(F32)[:, None] * inv, (pos % GRID_W).astype(F32)[:, None] * inv], axis=-1)
    cos, sin = jnp.cos(ang), jnp.sin(ang)
    reps = LANES // HEAD_DIM
    cos_t = jnp.tile(jnp.concatenate([cos, cos], axis=-1), (1, reps))
    sin_t = jnp.tile(jnp.concatenate([-sin, sin], axis=-1), (1, reps))
    cos_t = jnp.concatenate([jnp.ones((ctx_len, LANES), F32), cos_t], axis=0)
    sin_t = jnp.concatenate([jnp.zeros((ctx_len, LANES), F32), sin_t], axis=0)
    return cos_t, sin_t


def _relayout_w_in(w_in):
    o_lr = ATT_W + 2 * KV_W + 5 * GLA_W
    q_scale = HEAD_DIM ** -0.5
    gq_scale = GLA_DK ** -0.5
    parts = [w_in[..., :ATT_W] * q_scale, w_in[..., ATT_W:COL_GQ], w_in[..., COL_GQ:COL_GK] * gq_scale,
             w_in[..., COL_GK:o_lr], w_in[..., o_lr + 2 * GLA_LOWRANK:], w_in[..., o_lr:o_lr + 2 * GLA_LOWRANK],
             jnp.zeros(w_in.shape[:-1] + (LANES - 2 * GLA_LOWRANK,), w_in.dtype)]
    return jnp.concatenate(parts, axis=-1).astype(BF16)


def kernel(x, c, ctx, c_ctx, w_mod, b_mod, norm1_g, norm2_g, w_in, w_out, attn_sink, gla_w2_f, gla_b_f, gla_w2_b, gla_b_b, gla_norm_g, conv_w, ffn_w_gu, ffn_w_down, router_w, expert_w_gu, expert_w_down, final_norm_g):
    B, S, D = x.shape
    L = ctx.shape[1]
    depth = w_in.shape[0]
    assert D == D_MODEL and S % GRID_W == 0

    xt = jnp.concatenate([ctx, x], axis=1)
    rows = 16
    cc = jnp.concatenate([c, c_ctx[None, :], jnp.zeros((rows - B - 1, D), F32)], axis=0)
    mod = _adaln(cc, w_mod, b_mod)
    cos_t, sin_t = _rope_tables(S, L)

    w_in_p = _relayout_w_in(w_in)
    w_out_b = w_out.astype(BF16)
    zpad = jnp.zeros((depth, LANES - 2 * GLA_LOWRANK, GLA_W), F32)
    w2_f = jnp.concatenate([gla_w2_f, jnp.zeros_like(gla_w2_b), zpad], axis=1).astype(BF16)
    w2_b = jnp.concatenate([jnp.zeros_like(gla_w2_f), gla_w2_b, zpad], axis=1).astype(BF16)
    gla_ng = jnp.tile(gla_norm_g, (1, GLA_HEADS))
    ffn_gu_b = ffn_w_gu.astype(BF16)
    ffn_down_b = ffn_w_down.astype(BF16)
    exp_gu_b = expert_w_gu.astype(BF16)
    exp_down_b = expert_w_down.astype(BF16)
    router_p = jnp.pad(router_w, ((0, 0), (0, 0), (0, LANES - N_EXPERTS)))
    router_hi = router_p.astype(BF16)
    router_lo = (router_p - router_hi.astype(F32)).astype(BF16)

    for l in range(depth):
        mod_l = mod[l, :B].reshape(B, 1, 6 * D)
        mod_c = mod[l, B:B + 1]
        proj = _inproj(xt, norm1_g[l][None, :], mod_l, mod_c, cos_t, sin_t, w_in_p[l], L)
        att = _attention(proj, attn_sink[l], L)
        cv = _short_conv(proj, conv_w[l], L)
        gf = _gla(proj, w2_f[l], gla_b_f[l][None, :], gla_ng[l][None, :], L, reverse=False)
        gb = _gla(proj, w2_b[l], gla_b_b[l][None, :], gla_ng[l][None, :], L, reverse=True)
        if l % 2 == 0:
            xt, h2 = _outproj(xt, att, gf, gb, cv, w_out_b[l], norm2_g[l][None, :], mod_l, mod_c, L)
            xt = _ffn(h2, xt, mod_l, mod_c, ffn_gu_b[l // 2], ffn_down_b[l // 2], L)
        else:
            xt, hp, route = _outproj(xt, att, gf, gb, cv, w_out_b[l], norm2_g[l][None, :], mod_l, mod_c, L,
                                     router=(router_hi[l // 2], router_lo[l // 2]))
            xt = _moe(hp, route, xt, mod_l, mod_c, exp_gu_b[l // 2], exp_down_b[l // 2], L)
    return _final_norm(xt, final_norm_g[None, :], L)
```

```python
import functools

import jax
import jax.numpy as jnp
from jax import lax
from jax.experimental import pallas as pl
from jax.experimental.pallas import tpu as pltpu
from jax.experimental.pallas import tpu_sc as plsc

F32 = jnp.float32
BF16 = jnp.bfloat16

D_MODEL = 1024
EPS = 1e-6
GRID_W = 64
ROPE_THETA = 10000.0
HEAD_DIM = 64
ATT_HEADS = 8
ATT_KV_HEADS = 2
ATT_GROUP = ATT_HEADS // ATT_KV_HEADS
ATT_W = ATT_HEADS * HEAD_DIM
KV_W = ATT_KV_HEADS * HEAD_DIM
WINDOW = 128
GLA_HEADS = 4
GLA_DK = 64
GLA_W = GLA_HEADS * GLA_DK
GLA_LOWRANK = 16
GLA_TAU = 16.0
GLA_CHUNK = 64
CONV_CH = 256
N_EXPERTS = 8

LANES = 128
SUBLANES_BF16 = 16

COL_Q = 0
COL_K = ATT_W
COL_V = COL_K + KV_W
ROPE_W = COL_V
COL_GQ = COL_V + KV_W
COL_GK = COL_GQ + GLA_W
COL_GV = COL_GK + GLA_W
COL_GGF = COL_GV + GLA_W
COL_GGB = COL_GGF + GLA_W
COL_CB = COL_GGB + GLA_W
COL_CC = COL_CB + CONV_CH
COL_CX = COL_CC + CONV_CH
COL_LR = COL_CX + CONV_CH
PROJ_W = COL_LR + LANES

SEQ_TILE = 256
VMEM_LIMIT = 56 * 1024 * 1024
CAST_BLOCK_BYTES = 4 * 1024 * 1024

PACK_W = D_MODEL // 2
PACK_CHUNKS = PACK_W // LANES
EXPERT_TILE = 512
SC_CORES = 2
SC_SUBCORES = 16
SC_INDEX_CHUNK = 128


def _silu(v):
    return v / (1.0 + jnp.exp(-v))


def _pick_tile(total, cap):
    best = None
    for t in range(SUBLANES_BF16, cap + 1, SUBLANES_BF16):
        if total % t == 0:
            best = t
    assert best is not None
    return best


def _params(sem):
    return pltpu.CompilerParams(dimension_semantics=sem, vmem_limit_bytes=VMEM_LIMIT)


def _pack_bf16_pairs(v):
    bits = lambda t: lax.bitcast_convert_type(t.astype(BF16).astype(F32), jnp.int32)
    return ((bits(v[:, :PACK_W]) >> 16) & 0xFFFF) | (bits(v[:, PACK_W:]) & -65536)


def _unpack_bf16_pairs(w):
    return lax.bitcast_convert_type(w << 16, F32), lax.bitcast_convert_type(w & -65536, F32)


def _cast_kernel(x_ref, o_ref):
    o_ref[...] = x_ref[...].astype(BF16)


def _to_bf16(w):
    cols = w.shape[-1]
    rows = w.size // cols
    tr = _pick_tile(rows, max(SUBLANES_BF16, CAST_BLOCK_BYTES // (4 * cols)))
    out = pl.pallas_call(
        _cast_kernel,
        grid=(rows // tr,),
        in_specs=[pl.BlockSpec((tr, cols), lambda i: (i, 0))],
        out_specs=pl.BlockSpec((tr, cols), lambda i: (i, 0)),
        out_shape=jax.ShapeDtypeStruct((rows, cols), BF16),
        compiler_params=_params(("parallel",)),
    )(w.reshape(rows, cols))
    return out.reshape(w.shape)


def _load_packed(ref):
    return jnp.concatenate([ref[k] for k in range(PACK_CHUNKS)], axis=1)


def _store_packed(ref, packed):
    for k in range(PACK_CHUNKS):
        ref[k] = packed[:, k * LANES:(k + 1) * LANES]


def _adaln_kernel(c_ref, w_ref, b_ref, o_ref):
    s = _silu(c_ref[...]).astype(BF16)
    o_ref[...] = jnp.dot(s, w_ref[...].astype(BF16), preferred_element_type=F32) + b_ref[...]


def _adaln(cc, w_mod, b_mod):
    depth, d, n = w_mod.shape
    tn = 1024
    rows = cc.shape[0]
    return pl.pallas_call(
        _adaln_kernel,
        grid=(depth, n // tn),
        in_specs=[pl.BlockSpec((rows, d), lambda l, j: (0, 0)),
                  pl.BlockSpec((None, d, tn), lambda l, j: (l, 0, j)),
                  pl.BlockSpec((None, 1, tn), lambda l, j: (l, 0, j))],
        out_specs=pl.BlockSpec((None, rows, tn), lambda l, j: (l, 0, j)),
        out_shape=jax.ShapeDtypeStruct((depth, rows, n), F32),
        compiler_params=_params(("parallel", "parallel")),
    )(cc, w_mod, b_mod.reshape(depth, 1, n))


def _mod_specs(k, d):
    return [pl.BlockSpec((None, 1, d), lambda b, j, *_: (b, 0, k)),
            pl.BlockSpec((1, d), lambda b, j, *_: (0, k))]


def _row_is_ctx(j, tm, ctx_len):
    return (j * tm + lax.broadcasted_iota(jnp.int32, (tm, 1), 0)) < ctx_len


def _inproj_kernel(x_ref, g_ref, sc_ref, csc_ref, sh_ref, csh_ref, cos_ref, sin_ref, w_ref, o_ref, *, tm, ctx_len):
    x = x_ref[...]
    n = x * lax.rsqrt(jnp.mean(x * x, axis=-1, keepdims=True) + EPS) * g_ref[...]
    is_ctx = _row_is_ctx(pl.program_id(1), tm, ctx_len)
    scale = jnp.where(is_ctx, csc_ref[...], sc_ref[...])
    shift = jnp.where(is_ctx, csh_ref[...], sh_ref[...])
    h = (n * (1.0 + scale) + shift).astype(BF16)
    qk = jnp.dot(h, w_ref[:, :ROPE_W], preferred_element_type=F32)
    lower_half = (lax.broadcasted_iota(jnp.int32, (tm, LANES), 1) % HEAD_DIM) < (HEAD_DIM // 2)
    cos = cos_ref[...]
    sin = sin_ref[...]
    for i in range(ROPE_W // LANES):
        t = qk[:, i * LANES:(i + 1) * LANES]
        partner = jnp.where(lower_half, pltpu.roll(t, LANES - HEAD_DIM // 2, axis=1), pltpu.roll(t, HEAD_DIM // 2, axis=1))
        o_ref[:, i * LANES:(i + 1) * LANES] = (t * cos + partner * sin).astype(BF16)
    o_ref[:, ROPE_W:] = jnp.dot(h, w_ref[:, ROPE_W:], preferred_element_type=F32).astype(BF16)


def _inproj(xt, g, mod_l, mod_c, cos_t, sin_t, w, ctx_len):
    B, T, D = xt.shape
    tm = _pick_tile(T, 640)
    kern = functools.partial(_inproj_kernel, tm=tm, ctx_len=ctx_len)
    return pl.pallas_call(
        kern,
        grid=(B, T // tm),
        in_specs=[pl.BlockSpec((None, tm, D), lambda b, j: (b, j, 0)),
                  pl.BlockSpec((1, D), lambda b, j: (0, 0)),
                  *_mod_specs(1, D), *_mod_specs(0, D),
                  pl.BlockSpec((tm, LANES), lambda b, j: (j, 0)),
                  pl.BlockSpec((tm, LANES), lambda b, j: (j, 0)),
                  pl.BlockSpec((D, PROJ_W), lambda b, j: (0, 0))],
        out_specs=pl.BlockSpec((None, tm, PROJ_W), lambda b, j: (b, j, 0)),
        out_shape=jax.ShapeDtypeStruct((B, T, PROJ_W), BF16),
        compiler_params=_params(("parallel", "parallel")),
    )(xt, g, mod_l, mod_c, mod_l, mod_c, cos_t, sin_t, w)


def _softmax_pv(s, sink, v):
    m = jnp.maximum(jnp.max(s, axis=-1, keepdims=True), sink)
    p = jnp.exp(s - m)
    den = jnp.sum(p, axis=-1, keepdims=True) + jnp.exp(sink - m)
    return jnp.dot(p.astype(BF16), v, preferred_element_type=F32) / den


def _attn_kernel(sink_ref, q_ref, kc_ref, vc_ref, km_ref, vm_ref, kp_ref, vp_ref, kn_ref, vn_ref, o_ref, *, n_tiles):
    j = pl.program_id(1)
    nt_dims = (((1,), (1,)), ((), ()))

    def group_attention(rows, n_rows, g, k, v, mask):
        heads = [g * ATT_GROUP + hh for hh in range(ATT_GROUP)]
        q = jnp.concatenate([q_ref[rows, h * HEAD_DIM:(h + 1) * HEAD_DIM] for h in heads], axis=0)
        s = lax.dot_general(q, k, nt_dims, preferred_element_type=F32)
        if mask is not None:
            s = jnp.where(mask, s, -1e30)
        head_row = lax.broadcasted_iota(jnp.int32, (ATT_GROUP * n_rows, 1), 0) // n_rows
        sink = jnp.full((ATT_GROUP * n_rows, 1), sink_ref[heads[0]], F32)
        for hh in range(1, ATT_GROUP):
            sink = jnp.where(head_row == hh, sink_ref[heads[hh]], sink)
        o = _softmax_pv(s, sink, v)
        for hh, h in enumerate(heads):
            o_ref[rows, h * HEAD_DIM:(h + 1) * HEAD_DIM] = o[hh * n_rows:(hh + 1) * n_rows].astype(BF16)

    @pl.when(j == 0)
    def _context_queries():
        for g in range(ATT_KV_HEADS):
            gs = slice(g * HEAD_DIM, (g + 1) * HEAD_DIM)
            group_attention(slice(0, SEQ_TILE), SEQ_TILE, g, kc_ref[:, gs], vc_ref[:, gs], None)

    @pl.when(j > 0)
    def _latent_queries():
        n_ctx = kc_ref.shape[0]
        n_keys = n_ctx + 3 * WINDOW
        r = lax.broadcasted_iota(jnp.int32, (ATT_GROUP * WINDOW, n_keys), 0) % WINDOW
        kcol = lax.broadcasted_iota(jnp.int32, (ATT_GROUP * WINDOW, n_keys), 1)
        w = kcol - n_ctx
        band = (w >= r) & (w <= r + 2 * WINDOW)
        for sub in range(SEQ_TILE // WINDOW):
            rows = slice(sub * WINDOW, (sub + 1) * WINDOW)
            if sub == 0:
                kprev, vprev = kp_ref[...], vp_ref[...]
                kmid, vmid = km_ref[:WINDOW], vm_ref[:WINDOW]
                knext, vnext = km_ref[WINDOW:], vm_ref[WINDOW:]
                w_lo = jnp.where(j > 1, 0, WINDOW)
                w_hi = 3 * WINDOW
            else:
                kprev, vprev = km_ref[:WINDOW], vm_ref[:WINDOW]
                kmid, vmid = km_ref[WINDOW:], vm_ref[WINDOW:]
                knext, vnext = kn_ref[...], vn_ref[...]
                w_lo = 0
                w_hi = jnp.where(j < n_tiles - 1, 3 * WINDOW, 2 * WINDOW)
            mask = (kcol < n_ctx) | (band & (w >= w_lo) & (w < w_hi))
            for g in range(ATT_KV_HEADS):
                gs = slice(g * HEAD_DIM, (g + 1) * HEAD_DIM)
                kcat = jnp.concatenate([kc_ref[:, gs], kprev[:, gs], kmid[:, gs], knext[:, gs]], axis=0)
                vcat = jnp.concatenate([vc_ref[:, gs], vprev[:, gs], vmid[:, gs], vnext[:, gs]], axis=0)
                group_attention(rows, WINDOW, g, kcat, vcat, mask)


def _attention(proj, sink, ctx_len):
    B, T, _ = proj.shape
    assert ctx_len == SEQ_TILE and T % SEQ_TILE == 0
    n_tiles = T // SEQ_TILE
    n_win = T // WINDOW
    kcol, vcol = COL_K // KV_W, COL_V // KV_W
    per_tile = SEQ_TILE // WINDOW
    tile = lambda col: pl.BlockSpec((None, SEQ_TILE, KV_W), lambda b, j: (b, j, col))
    ctx = lambda col: pl.BlockSpec((None, SEQ_TILE, KV_W), lambda b, j: (b, 0, col))
    prev = lambda col: pl.BlockSpec((None, WINDOW, KV_W), lambda b, j: (b, jnp.maximum(j * per_tile - 1, 0), col))
    nxt = lambda col: pl.BlockSpec((None, WINDOW, KV_W), lambda b, j: (b, jnp.minimum((j + 1) * per_tile, n_win - 1), col))
    return pl.pallas_call(
        functools.partial(_attn_kernel, n_tiles=n_tiles),
        grid=(B, n_tiles),
        in_specs=[pl.BlockSpec(memory_space=pltpu.SMEM),
                  pl.BlockSpec((None, SEQ_TILE, ATT_W), lambda b, j: (b, j, 0)),
                  ctx(kcol), ctx(vcol), tile(kcol), tile(vcol), prev(kcol), prev(vcol), nxt(kcol), nxt(vcol)],
        out_specs=pl.BlockSpec((None, SEQ_TILE, ATT_W), lambda b, j: (b, j, 0)),
        out_shape=jax.ShapeDtypeStruct((B, T, ATT_W), BF16),
        compiler_params=_params(("parallel", "parallel")),
    )(sink, *([proj] * 9))


def _conv_kernel(b_ref, c_ref, x_ref, w_ref, o_ref, u_scr, *, ctx_len):
    T = c_ref.shape[0]
    pad = 8
    u = c_ref[...].astype(F32) * x_ref[...].astype(F32)
    zero_row = jnp.zeros((1, CONV_CH), F32)
    u_scr[pad - 1:pad, :] = zero_row
    u_scr[pad:pad + T, :] = u
    u_scr[pad + T:pad + T + 1, :] = zero_row
    row = lax.broadcasted_iota(jnp.int32, (T, 1), 0)
    u_prev = jnp.where(row == ctx_len, 0.0, u_scr[pad - 1:pad - 1 + T, :])
    u_next = jnp.where(row == ctx_len - 1, 0.0, u_scr[pad + 1:pad + 1 + T, :])
    y = w_ref[0:1, :] * u_prev + w_ref[1:2, :] * u + w_ref[2:3, :] * u_next
    o_ref[...] = (b_ref[...].astype(F32) * y).astype(BF16)


def _short_conv(proj, conv_w, ctx_len):
    B, T, _ = proj.shape
    col = lambda c0: pl.BlockSpec((None, T, CONV_CH), lambda b: (b, 0, c0 // CONV_CH))
    return pl.pallas_call(
        functools.partial(_conv_kernel, ctx_len=ctx_len),
        grid=(B,),
        in_specs=[col(COL_CB), col(COL_CC), col(COL_CX), pl.BlockSpec((3, CONV_CH), lambda b: (0, 0))],
        out_specs=pl.BlockSpec((None, T, CONV_CH), lambda b: (b, 0, 0)),
        out_shape=jax.ShapeDtypeStruct((B, T, CONV_CH), BF16),
        scratch_shapes=[pltpu.VMEM((T + 16, CONV_CH), F32)],
        compiler_params=_params(("parallel",)),
    )(proj, proj, proj, conv_w)


def _split2(v):
    hi = v.astype(BF16)
    lo = (v - hi.astype(F32)).astype(BF16)
    return hi, lo


def _gla_kernel(q_ref, k_ref, v_ref, gate_ref, lr_ref, w2_ref, bias_ref, ng_ref, o_ref,
                oacc_scr, qs_scr, u_scr, dec_scr, sin_scr, *, reverse, n_ctx_chunks):
    T = q_ref.shape[0]
    C = GLA_CHUNK
    n_tiles = T // SEQ_TILE
    n_chunks = T // C
    per_tile = SEQ_TILE // C
    nt_dims = (((1,), (1,)), ((), ()))
    tn_dims = (((0,), (0,)), ((), ()))

    r = lax.broadcasted_iota(jnp.int32, (SEQ_TILE, SEQ_TILE), 0)
    c = lax.broadcasted_iota(jnp.int32, (SEQ_TILE, SEQ_TILE), 1)
    same_chunk = (r // C) == (c // C)
    causal = same_chunk & ((c >= r) if reverse else (c <= r))
    cum_mat = jnp.where(causal, 1.0, 0.0).astype(BF16)
    head_mean = jnp.where(same_chunk, 1.0 / GLA_DK, 0.0).astype(BF16)
    lane_head = lax.broadcasted_iota(jnp.int32, (C, GLA_W), 1) // GLA_DK

    def phase1(i, carry):
        rows = pl.ds(pl.multiple_of(i * SEQ_TILE, SEQ_TILE), SEQ_TILE)
        z = jnp.dot(lr_ref[rows, :], w2_ref[...], preferred_element_type=F32) + bias_ref[...]
        la = (jnp.minimum(z, 0.0) - jnp.log(1.0 + jnp.exp(-jnp.abs(z)))) / GLA_TAU
        hi, lo = _split2(la)
        b = jnp.dot(cum_mat, hi, preferred_element_type=F32) + jnp.dot(cum_mat, lo, preferred_element_type=F32)
        b3 = b.reshape(per_tile, C, GLA_W)
        total = b3[:, 0:1, :] if reverse else b3[:, C - 1:C, :]
        b_last = jnp.broadcast_to(total, (per_tile, C, GLA_W)).reshape(SEQ_TILE, GLA_W)
        b_ref = 0.5 * b_last
        q = q_ref[rows, :].astype(F32)
        k = k_ref[rows, :].astype(F32)
        v = v_ref[rows, :]
        e_fwd = jnp.exp(b - b_ref)
        e_bwd = jnp.exp(b_ref - b)
        e_half = jnp.exp(b_ref)
        qe = (q * e_fwd).astype(BF16)
        ke = (k * e_bwd).astype(BF16)
        ku = (k * (e_bwd * e_half)).astype(BF16)
        qs_scr[rows, :] = (q * (e_fwd * e_half)).astype(BF16)
        dec = e_half * e_half
        for h in range(GLA_HEADS):
            hs = slice(h * GLA_DK, (h + 1) * GLA_DK)
            s = lax.dot_general(qe[:, hs], ke[:, hs], nt_dims, preferred_element_type=F32)
            a = jnp.where(causal, s, 0.0).astype(BF16)
            oacc_scr[rows, hs] = jnp.dot(a, v[:, hs], preferred_element_type=F32)
        for ci in range(per_tile):
            cs = slice(ci * C, (ci + 1) * C)
            full = lax.dot_general(v[cs, :], ku[cs, :], tn_dims, preferred_element_type=F32)
            ut = full[0:C, :]
            for h in range(1, GLA_HEADS):
                ut = jnp.where(lane_head == h, full[h * C:(h + 1) * C, :], ut)
            u_scr[i * per_tile + ci] = ut
            dec_scr[i * per_tile + ci] = dec[ci * C:ci * C + 1, :]
        return carry

    lax.fori_loop(0, n_tiles, phase1, 0)

    def phase2(i, st):
        if reverse:
            ci = jnp.where(i < n_ctx_chunks, n_ctx_chunks - 1 - i, n_chunks - 1 + n_ctx_chunks - i)
        else:
            ci = i
        sin_scr[ci] = st.astype(BF16)
        return st * dec_scr[ci] + u_scr[ci]

    lax.fori_loop(0, n_chunks, phase2, jnp.zeros((C, GLA_W), F32))

    def phase3(i, carry):
        rows = pl.ds(pl.multiple_of(i * SEQ_TILE, SEQ_TILE), SEQ_TILE)
        qs = qs_scr[rows, :]
        parts = []
        for ci in range(per_tile):
            st = sin_scr[i * per_tile + ci]
            heads = []
            for h in range(GLA_HEADS):
                hs = slice(h * GLA_DK, (h + 1) * GLA_DK)
                heads.append(lax.dot_general(qs[ci * C:(ci + 1) * C, hs], st[:, hs], nt_dims,
                                             preferred_element_type=F32))
            parts.append(jnp.concatenate(heads, axis=1))
        o = oacc_scr[rows, :] + jnp.concatenate(parts, axis=0)
        ms = jnp.dot((o * o).astype(BF16), head_mean, preferred_element_type=F32)
        y = o * lax.rsqrt(ms + EPS) * ng_ref[...]
        o_ref[rows, :] = (y * _silu(gate_ref[rows, :].astype(F32))).astype(BF16)
        return carry

    lax.fori_loop(0, n_tiles, phase3, 0)


def _gla(proj, w2pad, bias, ng, ctx_len, reverse):
    B, T, _ = proj.shape
    assert T % SEQ_TILE == 0 and ctx_len % GLA_CHUNK == 0
    n_chunks = T // GLA_CHUNK
    col = lambda c0: pl.BlockSpec((None, T, GLA_W), lambda b: (b, 0, c0 // GLA_W))
    gate_col = COL_GGB if reverse else COL_GGF
    kern = functools.partial(_gla_kernel, reverse=reverse, n_ctx_chunks=ctx_len // GLA_CHUNK)
    return pl.pallas_call(
        kern,
        grid=(B,),
        in_specs=[col(COL_GQ), col(COL_GK), col(COL_GV), col(gate_col),
                  pl.BlockSpec((None, T, LANES), lambda b: (b, 0, COL_LR // LANES)),
                  pl.BlockSpec((LANES, GLA_W), lambda b: (0, 0)),
                  pl.BlockSpec((1, GLA_W), lambda b: (0, 0)),
                  pl.BlockSpec((1, GLA_W), lambda b: (0, 0))],
        out_specs=pl.BlockSpec((None, T, GLA_W), lambda b: (b, 0, 0)),
        out_shape=jax.ShapeDtypeStruct((B, T, GLA_W), BF16),
        scratch_shapes=[pltpu.VMEM((T, GLA_W), F32),
                        pltpu.VMEM((T, GLA_W), BF16),
                        pltpu.VMEM((n_chunks, GLA_CHUNK, GLA_W), F32),
                        pltpu.VMEM((n_chunks, 1, GLA_W), F32),
                        pltpu.VMEM((n_chunks, GLA_CHUNK, GLA_W), BF16)],
        compiler_params=_params(("parallel",)),
    )(proj, proj, proj, proj, proj, w2pad, bias, ng)


def _outproj_kernel(*refs, tm, ctx_len, moe):
    if moe:
        (x_ref, att_ref, gf_ref, gb_ref, cv_ref, w_ref, g1_ref, cg1_ref, ng_ref, sc_ref, csc_ref, sh_ref, csh_ref,
         router_ref, xo_ref, hp_ref, route_ref) = refs
    else:
        (x_ref, att_ref, gf_ref, gb_ref, cv_ref, w_ref, g1_ref, cg1_ref, ng_ref, sc_ref, csc_ref, sh_ref, csh_ref,
         xo_ref, h_ref) = refs
    is_ctx = _row_is_ctx(pl.program_id(1), tm, ctx_len)
    gla = (gf_ref[...].astype(F32) + gb_ref[...].astype(F32)).astype(BF16)
    y = jnp.dot(att_ref[...], w_ref[:ATT_W, :], preferred_element_type=F32)
    y += jnp.dot(gla, w_ref[ATT_W:ATT_W + GLA_W, :], preferred_element_type=F32)
    y += jnp.dot(cv_ref[...], w_ref[ATT_W + GLA_W:, :], preferred_element_type=F32)
    x = x_ref[...] + jnp.where(is_ctx, cg1_ref[...], g1_ref[...]) * y
    xo_ref[...] = x
    n = x * lax.rsqrt(jnp.mean(x * x, axis=-1, keepdims=True) + EPS) * ng_ref[...]
    scale = jnp.where(is_ctx, csc_ref[...], sc_ref[...])
    shift = jnp.where(is_ctx, csh_ref[...], sh_ref[...])
    h = n * (1.0 + scale) + shift
    hi = h.astype(BF16)
    if not moe:
        h_ref[...] = hi
    else:
        _store_packed(hp_ref, _pack_bf16_pairs(h))
        logits = jnp.dot(hi, router_ref[...], preferred_element_type=F32)
        lane = lax.broadcasted_iota(jnp.int32, logits.shape, 1)
        neg = -jnp.inf
        lg = jnp.where(lane < N_EXPERTS, logits, neg)
        m1 = jnp.max(lg, axis=-1, keepdims=True)
        i1 = jnp.min(jnp.where(lg == m1, lane, LANES), axis=-1, keepdims=True)
        lg2 = jnp.where(lane == i1, neg, lg)
        m2 = jnp.max(lg2, axis=-1, keepdims=True)
        i2 = jnp.min(jnp.where(lg2 == m2, lane, LANES), axis=-1, keepdims=True)
        e2 = jnp.exp(m2 - m1)
        gate1 = 1.0 / (1.0 + e2)
        gate2 = e2 / (1.0 + e2)
        route_ref[...] = jnp.where(lane == 0, i1.astype(F32), jnp.where(lane == 1, i2.astype(F32),
                                   jnp.where(lane == 2, gate1, jnp.where(lane == 3, gate2, 0.0))))


def _outproj(xt, att, gf, gb, cv, w_out, ng, mod_l, mod_c, ctx_len, router=None):
    B, T, D = xt.shape
    tm = _pick_tile(T, 640)
    moe = router is not None
    row = lambda w: pl.BlockSpec((None, tm, w), lambda b, j: (b, j, 0))
    const = lambda shape: pl.BlockSpec(shape, lambda b, j: (0,) * len(shape))
    in_specs = [row(D), row(ATT_W), row(GLA_W), row(GLA_W), row(CONV_CH), const((D, D)),
                *_mod_specs(2, D), const((1, D)), *_mod_specs(4, D), *_mod_specs(3, D)]
    args = [xt, att, gf, gb, cv, w_out, mod_l, mod_c, ng, mod_l, mod_c, mod_l, mod_c]
    nt = T // tm
    if moe:
        in_specs.append(const((D, LANES)))
        args.append(router)
        out_specs = [row(D), pl.BlockSpec((PACK_CHUNKS, tm, LANES), lambda b, j: (0, b * nt + j, 0)), row(LANES)]
        out_shape = [jax.ShapeDtypeStruct((B, T, D), F32), jax.ShapeDtypeStruct((PACK_CHUNKS, B * T, LANES), jnp.int32),
                     jax.ShapeDtypeStruct((B, T, LANES), F32)]
    else:
        out_specs = [row(D), row(D)]
        out_shape = [jax.ShapeDtypeStruct((B, T, D), F32), jax.ShapeDtypeStruct((B, T, D), BF16)]
    return pl.pallas_call(
        functools.partial(_outproj_kernel, tm=tm, ctx_len=ctx_len, moe=moe),
        grid=(B, T // tm),
        in_specs=in_specs, out_specs=out_specs, out_shape=out_shape,
        compiler_params=_params(("parallel", "parallel")),
    )(*args)


def _ffn_kernel(h_ref, x_ref, g2_ref, cg2_ref, wg_ref, wu_ref, wd_ref, o_ref, acc_ref, *, tm, ctx_len):
    j = pl.program_id(1)
    f = pl.program_id(2)

    @pl.when(f == 0)
    def _init():
        acc_ref[...] = jnp.zeros_like(acc_ref)

    h = h_ref[...]
    g = jnp.dot(h, wg_ref[...], preferred_element_type=F32)
    u = jnp.dot(h, wu_ref[...], preferred_element_type=F32)
    acc_ref[...] += jnp.dot((_silu(g) * u).astype(BF16), wd_ref[...], preferred_element_type=F32)

    @pl.when(f == pl.num_programs(2) - 1)
    def _finish():
        is_ctx = _row_is_ctx(j, tm, ctx_len)
        o_ref[...] = x_ref[...] + jnp.where(is_ctx, cg2_ref[...], g2_ref[...]) * acc_ref[...]


def _ffn(h2, xt, mod_l, mod_c, w_gu, w_down, ctx_len):
    B, T, D = xt.shape
    F = w_down.shape[0]
    fc = 1408
    assert F % fc == 0
    nf = F // fc
    tm = _pick_tile(T, 640)
    row = lambda w: pl.BlockSpec((None, tm, w), lambda b, j, f: (b, j, 0))
    return pl.pallas_call(
        functools.partial(_ffn_kernel, tm=tm, ctx_len=ctx_len),
        grid=(B, T // tm, nf),
        in_specs=[row(D), row(D), *_mod_specs(5, D),
                  pl.BlockSpec((D, fc), lambda b, j, f: (0, f)),
                  pl.BlockSpec((D, fc), lambda b, j, f: (0, nf + f)),
                  pl.BlockSpec((fc, D), lambda b, j, f: (f, 0))],
        out_specs=row(D),
        out_shape=jax.ShapeDtypeStruct((B, T, D), F32),
        scratch_shapes=[pltpu.VMEM((tm, D), F32)],
        compiler_params=_params(("parallel", "parallel", "arbitrary")),
    )(h2, xt, mod_l, mod_c, w_gu, w_gu, w_down)


def _sc_gather(table, idx):
    _, width = table.shape
    m = idx.shape[0]
    workers = SC_CORES * SC_SUBCORES
    ch = SC_INDEX_CHUNK
    assert width == LANES and m % (workers * ch) == 0
    n_chunks = m // (workers * ch)
    nb = max(b for b in (4, 3, 2, 1) if n_chunks % b == 0)
    mesh = plsc.VectorSubcoreMesh(core_axis_name="c", subcore_axis_name="s",
                                  num_cores=SC_CORES, num_subcores=SC_SUBCORES)

    @functools.partial(
        pl.kernel, mesh=mesh, out_type=jax.ShapeDtypeStruct((m, width), table.dtype),
        scratch_types=[pltpu.VMEM((n_chunks, ch), jnp.int32), pltpu.VMEM((nb, ch, width), table.dtype)]
        + [pltpu.SemaphoreType.DMA] * (2 * nb))
    def gather(table_hbm, idx_hbm, out_hbm, idx_v, rows_v, *sems):
        wid = lax.axis_index("s") * SC_CORES + lax.axis_index("c")
        base = wid * (n_chunks * ch)
        pltpu.sync_copy(idx_hbm.at[wid], idx_v)

        def fetch(c, b):
            return pltpu.make_async_copy(table_hbm.at[idx_v.at[c]], rows_v.at[b], sems[b])

        def flush(c, b):
            return pltpu.make_async_copy(rows_v.at[b], out_hbm.at[pl.ds(base + c * ch, ch)], sems[nb + b])

        @pl.loop(0, n_chunks, step=nb)
        def _(c0):
            for b in range(nb):
                @pl.when(c0 > 0)
                def _():
                    flush(c0 - nb + b, b).wait()
                fetch(c0 + b, b).start()
            for b in range(nb):
                fetch(c0 + b, b).wait()
                flush(c0 + b, b).start()

        for b in range(nb):
            flush(n_chunks - nb + b, b).wait()

    return gather(table, idx.reshape(workers, n_chunks, ch))


def _routing(route, tile):
    B, T, _ = route.shape
    n = B * T
    assert (2 * n) % tile == 0
    ids = route[..., :2].astype(jnp.int32).reshape(n, 2)
    e_cat = jnp.concatenate([ids[:, 0], ids[:, 1]])
    onehot = (e_cat[:, None] == jnp.arange(N_EXPERTS, dtype=jnp.int32)[None, :]).astype(jnp.int32)
    csum = jnp.cumsum(onehot, axis=0)
    pos = jnp.sum(onehot * csum, axis=1) - 1
    padded = (csum[-1] + tile - 1) // tile * tile
    ends = jnp.cumsum(padded)
    dest = ((ends - padded)[e_cat] + pos).astype(jnp.int32)
    n_rows = 2 * n + N_EXPERTS * tile
    id_bits = (2 * n - 1).bit_length()
    assert N_EXPERTS << id_bits < 2 ** 31
    order = jnp.sort((e_cat << id_bits) | jnp.arange(2 * n, dtype=jnp.int32)) & ((1 << id_bits) - 1)
    max_shift = N_EXPERTS * tile
    base = jnp.concatenate([jnp.zeros((max_shift,), jnp.int32), order, jnp.zeros((max_shift,), jnp.int32)])
    shift = (ends - padded) - (jnp.cumsum(csum[-1]) - csum[-1])
    row = jnp.arange(n_rows, dtype=jnp.int32)
    row_src = jnp.zeros((n_rows,), jnp.int32)
    for e in range(N_EXPERTS):
        shifted = lax.dynamic_slice(base, (max_shift - shift[e],), (n_rows,))
        row_src = jnp.where(row >= (ends - padded)[e], shifted, row_src)
    row_token = row_src % n
    tile_start = jnp.arange(n_rows // tile, dtype=jnp.int32) * tile
    tile_expert = jnp.minimum(jnp.searchsorted(ends, tile_start, side="right"), N_EXPERTS - 1).astype(jnp.int32)
    tile_valid = (tile_start < ends[-1]).astype(jnp.int32)
    return dest, row_token, tile_expert, tile_valid


def _expert_kernel(te_ref, tv_ref, xs_ref, wg_ref, wu_ref, wd_ref, ys_ref):
    i = pl.program_id(0)

    @pl.when(tv_ref[i] > 0)
    def _compute():
        lo, hi = _unpack_bf16_pairs(_load_packed(xs_ref))
        lo = lo.astype(BF16)
        hi = hi.astype(BF16)
        g = (jnp.dot(lo, wg_ref[:PACK_W, :], preferred_element_type=F32)
             + jnp.dot(hi, wg_ref[PACK_W:, :], preferred_element_type=F32))
        u = (jnp.dot(lo, wu_ref[:PACK_W, :], preferred_element_type=F32)
             + jnp.dot(hi, wu_ref[PACK_W:, :], preferred_element_type=F32))
        y = jnp.dot((_silu(g) * u).astype(BF16), wd_ref[...], preferred_element_type=F32)
        _store_packed(ys_ref, _pack_bf16_pairs(y))

    @pl.when(tv_ref[i] == 0)
    def _unused_tile():
        ys_ref[...] = jnp.zeros_like(ys_ref)


def _experts(xs, tile_expert, tile_valid, w_gu, w_down):
    _, n_rows, _ = xs.shape
    E, F, D = w_down.shape
    tile = EXPERT_TILE
    rows = pl.BlockSpec((PACK_CHUNKS, tile, LANES), lambda i, te, tv: (0, i, 0))
    return pl.pallas_call(
        _expert_kernel,
        grid_spec=pltpu.PrefetchScalarGridSpec(
            num_scalar_prefetch=2,
            grid=(n_rows // tile,),
            in_specs=[rows,
                      pl.BlockSpec((None, D, F), lambda i, te, tv: (te[i], 0, 0)),
                      pl.BlockSpec((None, D, F), lambda i, te, tv: (te[i], 0, 1)),
                      pl.BlockSpec((None, F, D), lambda i, te, tv: (te[i], 0, 0))],
            out_specs=rows),
        out_shape=jax.ShapeDtypeStruct(xs.shape, jnp.int32),
        compiler_params=_params(("arbitrary",)),
    )(tile_expert, tile_valid, xs, w_gu, w_gu, w_down)


def _combine_kernel(*refs, tm, ctx_len, skip, final):
    if final:
        x_ref, g2_ref, cg2_ref, route_ref, ya_ref, yb_ref, fg_ref, o_ref = refs
    else:
        x_ref, g2_ref, cg2_ref, route_ref, ya_ref, yb_ref, o_ref = refs
    is_ctx = _row_is_ctx(pl.program_id(1) + skip, tm, ctx_len)
    route = route_ref[...]
    lane = lax.broadcasted_iota(jnp.int32, route.shape, 1)
    gate_a = jnp.sum(jnp.where(lane == 2, route, 0.0), axis=-1, keepdims=True)
    gate_b = jnp.sum(jnp.where(lane == 3, route, 0.0), axis=-1, keepdims=True)
    unpack = lambda ref: jnp.concatenate(_unpack_bf16_pairs(_load_packed(ref)), axis=1)
    f = gate_a * unpack(ya_ref) + gate_b * unpack(yb_ref)
    x = x_ref[...] + jnp.where(is_ctx, cg2_ref[...], g2_ref[...]) * f
    if final:
        x = x * lax.rsqrt(jnp.mean(x * x, axis=-1, keepdims=True) + EPS) * fg_ref[...]
    o_ref[...] = x


def _combine(xt, mod_l, mod_c, route, yg, ctx_len, final_g=None):
    B, T, D = xt.shape
    final = final_g is not None
    if final:
        tm = SEQ_TILE
        assert ctx_len % tm == 0 and T % tm == 0
        skip = ctx_len // tm
    else:
        tm = _pick_tile(T, 640)
        skip = 0
    nt = T // tm
    row = lambda w: pl.BlockSpec((None, tm, w), lambda b, j: (b, j + skip, 0))
    slot = lambda s: pl.BlockSpec((PACK_CHUNKS, None, tm, LANES), lambda b, j: (0, s, b * nt + j + skip, 0))
    in_specs = [row(D), *_mod_specs(5, D), row(LANES), slot(0), slot(1)]
    args = [xt, mod_l, mod_c, route, yg, yg]
    if final:
        in_specs.append(pl.BlockSpec((1, D), lambda b, j: (0, 0)))
        args.append(final_g)
    return pl.pallas_call(
        functools.partial(_combine_kernel, tm=tm, ctx_len=ctx_len, skip=skip, final=final),
        grid=(B, nt - skip),
        in_specs=in_specs,
        out_specs=pl.BlockSpec((None, tm, D), lambda b, j: (b, j, 0)),
        out_shape=jax.ShapeDtypeStruct((B, T - skip * tm, D), F32),
        compiler_params=_params(("parallel", "parallel")),
    )(*args)


def _moe(hp, route, xt, mod_l, mod_c, w_gu, w_down, ctx_len, final_g=None):
    B, T, _ = xt.shape
    n = B * T
    dest, row_token, tile_expert, tile_valid = _routing(route, EXPERT_TILE)
    n_rows = row_token.shape[0]
    chunk = jnp.arange(PACK_CHUNKS, dtype=jnp.int32)[:, None]
    xs = _sc_gather(hp.reshape(PACK_CHUNKS * n, LANES), (chunk * n + row_token[None, :]).reshape(-1))
    ys = _experts(xs.reshape(PACK_CHUNKS, n_rows, LANES), tile_expert, tile_valid, w_gu, w_down)
    yg = _sc_gather(ys.reshape(PACK_CHUNKS * n_rows, LANES), (chunk * n_rows + dest[None, :]).reshape(-1))
    return _combine(xt, mod_l, mod_c, route, yg.reshape(PACK_CHUNKS, 2, n, LANES), ctx_len, final_g)


def _final_norm_kernel(x_ref, g_ref, o_ref):
    x = x_ref[...]
    o_ref[...] = x * lax.rsqrt(jnp.mean(x * x, axis=-1, keepdims=True) + EPS) * g_ref[...]


def _final_norm(xt, g, ctx_len):
    B, T, D = xt.shape
    S = T - ctx_len
    tm = SEQ_TILE
    assert ctx_len % tm == 0 and S % tm == 0
    skip = ctx_len // tm
    return pl.pallas_call(
        _final_norm_kernel,
        grid=(B, S // tm),
        in_specs=[pl.BlockSpec((None, tm, D), lambda b, j: (b, j + skip, 0)),
                  pl.BlockSpec((1, D), lambda b, j: (0, 0))],
        out_specs=pl.BlockSpec((None, tm, D), lambda b, j: (b, j, 0)),
        out_shape=jax.ShapeDtypeStruct((B, S, D), F32),
        compiler_params=_params(("parallel", "parallel")),
    )(xt, g)


def _rope_tables(seq, ctx_len):
    pos = jnp.arange(seq, dtype=jnp.int32)
    nf = HEAD_DIM // 4
    inv = ROPE_THETA ** (-jnp.arange(nf, dtype=F32) / nf)
    ang = jnp.concatenate([(pos // GRID_W).astype(F32)[:, None] * inv, (pos % GRID_W).astype(F32)[:, None] * inv], axis=-1)
    cos, sin = jnp.cos(ang), jnp.sin(ang)
    reps = LANES // HEAD_DIM
    cos_t = jnp.tile(jnp.concatenate([cos, cos], axis=-1), (1, reps))
    sin_t = jnp.tile(jnp.concatenate([-sin, sin], axis=-1), (1, reps))
    cos_t = jnp.concatenate([jnp.ones((ctx_len, LANES), F32), cos_t], axis=0)
    sin_t = jnp.concatenate([jnp.zeros((ctx_len, LANES), F32), sin_t], axis=0)
    return cos_t, sin_t


def _relayout_w_in(w_in):
    o_lr = ATT_W + 2 * KV_W + 5 * GLA_W
    q_scale = HEAD_DIM ** -0.5
    gq_scale = GLA_DK ** -0.5
    parts = [w_in[..., :ATT_W] * q_scale, w_in[..., ATT_W:COL_GQ], w_in[..., COL_GQ:COL_GK] * gq_scale,
             w_in[..., COL_GK:o_lr], w_in[..., o_lr + 2 * GLA_LOWRANK:], w_in[..., o_lr:o_lr + 2 * GLA_LOWRANK],
             jnp.zeros(w_in.shape[:-1] + (LANES - 2 * GLA_LOWRANK,), w_in.dtype)]
    return jnp.concatenate(parts, axis=-1).astype(BF16)


def kernel(x, c, ctx, c_ctx, w_mod, b_mod, norm1_g, norm2_g, w_in, w_out, attn_sink, gla_w2_f, gla_b_f, gla_w2_b, gla_b_b, gla_norm_g, conv_w, ffn_w_gu, ffn_w_down, router_w, expert_w_gu, expert_w_down, final_norm_g):
    B, S, D = x.shape
    L = ctx.shape[1]
    depth = w_in.shape[0]
    assert D == D_MODEL and S % GRID_W == 0

    xt = jnp.concatenate([ctx, x], axis=1)
    rows = 16
    cc = jnp.concatenate([c, c_ctx[None, :], jnp.zeros((rows - B - 1, D), F32)], axis=0)
    mod = _adaln(cc, w_mod, b_mod)
    cos_t, sin_t = _rope_tables(S, L)

    w_in_p = _relayout_w_in(w_in)
    w_out_b = _to_bf16(w_out)
    zpad = jnp.zeros((depth, LANES - 2 * GLA_LOWRANK, GLA_W), F32)
    w2_f = jnp.concatenate([gla_w2_f, jnp.zeros_like(gla_w2_b), zpad], axis=1).astype(BF16)
    w2_b = jnp.concatenate([jnp.zeros_like(gla_w2_f), gla_w2_b, zpad], axis=1).astype(BF16)
    gla_ng = jnp.tile(gla_norm_g, (1, GLA_HEADS))
    ffn_gu_b = _to_bf16(ffn_w_gu)
    ffn_down_b = _to_bf16(ffn_w_down)
    exp_gu_b = _to_bf16(expert_w_gu)
    exp_down_b = _to_bf16(expert_w_down)
    router_b = jnp.pad(router_w, ((0, 0), (0, 0), (0, LANES - N_EXPERTS))).astype(BF16)

    for l in range(depth):
        mod_l = mod[l, :B].reshape(B, 1, 6 * D)
        mod_c = mod[l, B:B + 1]
        proj = _inproj(xt, norm1_g[l][None, :], mod_l, mod_c, cos_t, sin_t, w_in_p[l], L)
        att = _attention(proj, attn_sink[l], L)
        cv = _short_conv(proj, conv_w[l], L)
        gf = _gla(proj, w2_f[l], gla_b_f[l][None, :], gla_ng[l][None, :], L, reverse=False)
        gb = _gla(proj, w2_b[l], gla_b_b[l][None, :], gla_ng[l][None, :], L, reverse=True)
        if l % 2 == 0:
            xt, h2 = _outproj(xt, att, gf, gb, cv, w_out_b[l], norm2_g[l][None, :], mod_l, mod_c, L)
            xt = _ffn(h2, xt, mod_l, mod_c, ffn_gu_b[l // 2], ffn_down_b[l // 2], L)
        else:
            xt, hp, route = _outproj(xt, att, gf, gb, cv, w_out_b[l], norm2_g[l][None, :], mod_l, mod_c, L,
                                     router=router_b[l // 2])
            xt = _moe(hp, route, xt, mod_l, mod_c, exp_gu_b[l // 2], exp_down_b[l // 2], L,
                      final_g=final_norm_g[None, :] if l == depth - 1 else None)
    return xt if depth % 2 == 0 else _final_norm(xt, final_norm_g[None, :], L)
```

```python
import functools

import jax
import jax.numpy as jnp
from jax import lax
from jax.experimental import pallas as pl
from jax.experimental.pallas import tpu as pltpu
from jax.experimental.pallas import tpu_sc as plsc

F32 = jnp.float32
BF16 = jnp.bfloat16

D_MODEL = 1024
EPS = 1e-6
GRID_W = 64
ROPE_THETA = 10000.0
HEAD_DIM = 64
ATT_HEADS = 8
ATT_KV_HEADS = 2
ATT_GROUP = ATT_HEADS // ATT_KV_HEADS
ATT_W = ATT_HEADS * HEAD_DIM
KV_W = ATT_KV_HEADS * HEAD_DIM
WINDOW = 128
GLA_HEADS = 4
GLA_DK = 64
GLA_W = GLA_HEADS * GLA_DK
GLA_LOWRANK = 16
GLA_TAU = 16.0
GLA_CHUNK = 64
CONV_CH = 256
N_EXPERTS = 8

LANES = 128
SUBLANES_BF16 = 16

COL_Q = 0
COL_K = ATT_W
COL_V = COL_K + KV_W
ROPE_W = COL_V
COL_GQ = COL_V + KV_W
COL_GK = COL_GQ + GLA_W
COL_GV = COL_GK + GLA_W
COL_GGF = COL_GV + GLA_W
COL_GGB = COL_GGF + GLA_W
COL_CB = COL_GGB + GLA_W
COL_CC = COL_CB + CONV_CH
COL_CX = COL_CC + CONV_CH
COL_LR = COL_CX + CONV_CH
PROJ_W = COL_LR + LANES

SEQ_TILE = 256
VMEM_LIMIT = 56 * 1024 * 1024
CAST_BLOCK_BYTES = 4 * 1024 * 1024

PACK_W = D_MODEL // 2
PACK_CHUNKS = PACK_W // LANES
EXPERT_TILE = 512
SC_CORES = 2
SC_SUBCORES = 16
SC_INDEX_CHUNK = 128


def _silu(v):
    return v / (1.0 + jnp.exp(-v))


def _pick_tile(total, cap):
    best = None
    for t in range(SUBLANES_BF16, cap + 1, SUBLANES_BF16):
        if total % t == 0:
            best = t
    assert best is not None
    return best


def _params(sem):
    return pltpu.CompilerParams(dimension_semantics=sem, vmem_limit_bytes=VMEM_LIMIT)


def _pack_bf16_pairs(v):
    bits = lambda t: lax.bitcast_convert_type(t.astype(BF16).astype(F32), jnp.int32)
    return ((bits(v[:, :PACK_W]) >> 16) & 0xFFFF) | (bits(v[:, PACK_W:]) & -65536)


def _unpack_bf16_pairs(w):
    return lax.bitcast_convert_type(w << 16, F32), lax.bitcast_convert_type(w & -65536, F32)


def _cast_kernel(x_ref, o_ref):
    o_ref[...] = x_ref[...].astype(BF16)


def _to_bf16(w):
    cols = w.shape[-1]
    rows = w.size // cols
    tr = _pick_tile(rows, max(SUBLANES_BF16, CAST_BLOCK_BYTES // (4 * cols)))
    out = pl.pallas_call(
        _cast_kernel,
        grid=(rows // tr,),
        in_specs=[pl.BlockSpec((tr, cols), lambda i: (i, 0))],
        out_specs=pl.BlockSpec((tr, cols), lambda i: (i, 0)),
        out_shape=jax.ShapeDtypeStruct((rows, cols), BF16),
        compiler_params=_params(("parallel",)),
    )(w.reshape(rows, cols))
    return out.reshape(w.shape)


def _load_packed(ref):
    return jnp.concatenate([ref[k] for k in range(PACK_CHUNKS)], axis=1)


def _store_packed(ref, packed):
    for k in range(PACK_CHUNKS):
        ref[k] = packed[:, k * LANES:(k + 1) * LANES]


def _adaln_kernel(c_ref, w_ref, b_ref, o_ref):
    s = _silu(c_ref[...]).astype(BF16)
    o_ref[...] = jnp.dot(s, w_ref[...].astype(BF16), preferred_element_type=F32) + b_ref[...]


def _adaln(cc, w_mod, b_mod):
    depth, d, n = w_mod.shape
    tn = 1024
    rows = cc.shape[0]
    return pl.pallas_call(
        _adaln_kernel,
        grid=(depth, n // tn),
        in_specs=[pl.BlockSpec((rows, d), lambda l, j: (0, 0)),
                  pl.BlockSpec((None, d, tn), lambda l, j: (l, 0, j)),
                  pl.BlockSpec((None, 1, tn), lambda l, j: (l, 0, j))],
        out_specs=pl.BlockSpec((None, rows, tn), lambda l, j: (l, 0, j)),
        out_shape=jax.ShapeDtypeStruct((depth, rows, n), F32),
        compiler_params=_params(("parallel", "parallel")),
    )(cc, w_mod, b_mod.reshape(depth, 1, n))


def _mod_specs(k, d):
    return [pl.BlockSpec((None, 1, d), lambda b, j, *_: (b, 0, k)),
            pl.BlockSpec((1, d), lambda b, j, *_: (0, k))]


def _row_is_ctx(j, tm, ctx_len):
    return (j * tm + lax.broadcasted_iota(jnp.int32, (tm, 1), 0)) < ctx_len


def _inproj_kernel(x_ref, g_ref, sc_ref, csc_ref, sh_ref, csh_ref, cos_ref, sin_ref, w_ref, o_ref, *, tm, ctx_len):
    x = x_ref[...]
    n = x * lax.rsqrt(jnp.mean(x * x, axis=-1, keepdims=True) + EPS) * g_ref[...]
    is_ctx = _row_is_ctx(pl.program_id(1), tm, ctx_len)
    scale = jnp.where(is_ctx, csc_ref[...], sc_ref[...])
    shift = jnp.where(is_ctx, csh_ref[...], sh_ref[...])
    h = (n * (1.0 + scale) + shift).astype(BF16)
    qk = jnp.dot(h, w_ref[:, :ROPE_W], preferred_element_type=F32)
    lower_half = (lax.broadcasted_iota(jnp.int32, (tm, LANES), 1) % HEAD_DIM) < (HEAD_DIM // 2)
    cos = cos_ref[...]
    sin = sin_ref[...]
    for i in range(ROPE_W // LANES):
        t = qk[:, i * LANES:(i + 1) * LANES]
        partner = jnp.where(lower_half, pltpu.roll(t, LANES - HEAD_DIM // 2, axis=1), pltpu.roll(t, HEAD_DIM // 2, axis=1))
        o_ref[:, i * LANES:(i + 1) * LANES] = (t * cos + partner * sin).astype(BF16)
    o_ref[:, ROPE_W:] = jnp.dot(h, w_ref[:, ROPE_W:], preferred_element_type=F32).astype(BF16)


def _inproj(xt, g, mod_l, mod_c, cos_t, sin_t, w, layer, ctx_len):
    B, T, D = xt.shape
    tm = _pick_tile(T, 640)
    kern = functools.partial(_inproj_kernel, tm=tm, ctx_len=ctx_len)
    return pl.pallas_call(
        kern,
        grid=(B, T // tm),
        in_specs=[pl.BlockSpec((None, tm, D), lambda b, j: (b, j, 0)),
                  pl.BlockSpec((1, D), lambda b, j: (0, 0)),
                  *_mod_specs(1, D), *_mod_specs(0, D),
                  pl.BlockSpec((tm, LANES), lambda b, j: (j, 0)),
                  pl.BlockSpec((tm, LANES), lambda b, j: (j, 0)),
                  pl.BlockSpec((None, D, PROJ_W), lambda b, j: (layer, 0, 0))],
        out_specs=pl.BlockSpec((None, tm, PROJ_W), lambda b, j: (b, j, 0)),
        out_shape=jax.ShapeDtypeStruct((B, T, PROJ_W), BF16),
        compiler_params=_params(("parallel", "parallel")),
    )(xt, g, mod_l, mod_c, mod_l, mod_c, cos_t, sin_t, w)


def _softmax_pv(s, sink, v):
    m = jnp.maximum(jnp.max(s, axis=-1, keepdims=True), sink)
    p = jnp.exp(s - m)
    den = jnp.sum(p, axis=-1, keepdims=True) + jnp.exp(sink - m)
    return jnp.dot(p.astype(BF16), v, preferred_element_type=F32) / den


def _attn_kernel(sink_ref, q_ref, kc_ref, vc_ref, km_ref, vm_ref, kp_ref, vp_ref, kn_ref, vn_ref, o_ref, *, n_tiles):
    j = pl.program_id(1)
    nt_dims = (((1,), (1,)), ((), ()))

    def group_attention(rows, n_rows, g, k, v, mask):
        heads = [g * ATT_GROUP + hh for hh in range(ATT_GROUP)]
        q = jnp.concatenate([q_ref[rows, h * HEAD_DIM:(h + 1) * HEAD_DIM] for h in heads], axis=0)
        s = lax.dot_general(q, k, nt_dims, preferred_element_type=F32)
        if mask is not None:
            s = jnp.where(mask, s, -1e30)
        head_row = lax.broadcasted_iota(jnp.int32, (ATT_GROUP * n_rows, 1), 0) // n_rows
        sink = jnp.full((ATT_GROUP * n_rows, 1), sink_ref[heads[0]], F32)
        for hh in range(1, ATT_GROUP):
            sink = jnp.where(head_row == hh, sink_ref[heads[hh]], sink)
        o = _softmax_pv(s, sink, v)
        for hh, h in enumerate(heads):
            o_ref[rows, h * HEAD_DIM:(h + 1) * HEAD_DIM] = o[hh * n_rows:(hh + 1) * n_rows].astype(BF16)

    @pl.when(j == 0)
    def _context_queries():
        for g in range(ATT_KV_HEADS):
            gs = slice(g * HEAD_DIM, (g + 1) * HEAD_DIM)
            group_attention(slice(0, SEQ_TILE), SEQ_TILE, g, kc_ref[:, gs], vc_ref[:, gs], None)

    @pl.when(j > 0)
    def _latent_queries():
        n_ctx = kc_ref.shape[0]
        n_keys = n_ctx + 3 * WINDOW
        r = lax.broadcasted_iota(jnp.int32, (ATT_GROUP * WINDOW, n_keys), 0) % WINDOW
        kcol = lax.broadcasted_iota(jnp.int32, (ATT_GROUP * WINDOW, n_keys), 1)
        w = kcol - n_ctx
        band = (w >= r) & (w <= r + 2 * WINDOW)
        for sub in range(SEQ_TILE // WINDOW):
            rows = slice(sub * WINDOW, (sub + 1) * WINDOW)
            if sub == 0:
                kprev, vprev = kp_ref[...], vp_ref[...]
                kmid, vmid = km_ref[:WINDOW], vm_ref[:WINDOW]
                knext, vnext = km_ref[WINDOW:], vm_ref[WINDOW:]
                w_lo = jnp.where(j > 1, 0, WINDOW)
                w_hi = 3 * WINDOW
            else:
                kprev, vprev = km_ref[:WINDOW], vm_ref[:WINDOW]
                kmid, vmid = km_ref[WINDOW:], vm_ref[WINDOW:]
                knext, vnext = kn_ref[...], vn_ref[...]
                w_lo = 0
                w_hi = jnp.where(j < n_tiles - 1, 3 * WINDOW, 2 * WINDOW)
            mask = (kcol < n_ctx) | (band & (w >= w_lo) & (w < w_hi))
            for g in range(ATT_KV_HEADS):
                gs = slice(g * HEAD_DIM, (g + 1) * HEAD_DIM)
                kcat = jnp.concatenate([kc_ref[:, gs], kprev[:, gs], kmid[:, gs], knext[:, gs]], axis=0)
                vcat = jnp.concatenate([vc_ref[:, gs], vprev[:, gs], vmid[:, gs], vnext[:, gs]], axis=0)
                group_attention(rows, WINDOW, g, kcat, vcat, mask)


def _attention(proj, sink, ctx_len):
    B, T, _ = proj.shape
    assert ctx_len == SEQ_TILE and T % SEQ_TILE == 0
    n_tiles = T // SEQ_TILE
    n_win = T // WINDOW
    kcol, vcol = COL_K // KV_W, COL_V // KV_W
    per_tile = SEQ_TILE // WINDOW
    tile = lambda col: pl.BlockSpec((None, SEQ_TILE, KV_W), lambda b, j: (b, j, col))
    ctx = lambda col: pl.BlockSpec((None, SEQ_TILE, KV_W), lambda b, j: (b, 0, col))
    prev = lambda col: pl.BlockSpec((None, WINDOW, KV_W), lambda b, j: (b, jnp.maximum(j * per_tile - 1, 0), col))
    nxt = lambda col: pl.BlockSpec((None, WINDOW, KV_W), lambda b, j: (b, jnp.minimum((j + 1) * per_tile, n_win - 1), col))
    return pl.pallas_call(
        functools.partial(_attn_kernel, n_tiles=n_tiles),
        grid=(B, n_tiles),
        in_specs=[pl.BlockSpec(memory_space=pltpu.SMEM),
                  pl.BlockSpec((None, SEQ_TILE, ATT_W), lambda b, j: (b, j, 0)),
                  ctx(kcol), ctx(vcol), tile(kcol), tile(vcol), prev(kcol), prev(vcol), nxt(kcol), nxt(vcol)],
        out_specs=pl.BlockSpec((None, SEQ_TILE, ATT_W), lambda b, j: (b, j, 0)),
        out_shape=jax.ShapeDtypeStruct((B, T, ATT_W), BF16),
        compiler_params=_params(("parallel", "parallel")),
    )(sink, *([proj] * 9))


def _conv_kernel(b_ref, c_ref, x_ref, w_ref, o_ref, u_scr, *, ctx_len):
    T = c_ref.shape[0]
    pad = 8
    u = c_ref[...].astype(F32) * x_ref[...].astype(F32)
    zero_row = jnp.zeros((1, CONV_CH), F32)
    u_scr[pad - 1:pad, :] = zero_row
    u_scr[pad:pad + T, :] = u
    u_scr[pad + T:pad + T + 1, :] = zero_row
    row = lax.broadcasted_iota(jnp.int32, (T, 1), 0)
    u_prev = jnp.where(row == ctx_len, 0.0, u_scr[pad - 1:pad - 1 + T, :])
    u_next = jnp.where(row == ctx_len - 1, 0.0, u_scr[pad + 1:pad + 1 + T, :])
    y = w_ref[0:1, :] * u_prev + w_ref[1:2, :] * u + w_ref[2:3, :] * u_next
    o_ref[...] = (b_ref[...].astype(F32) * y).astype(BF16)


def _short_conv(proj, conv_w, ctx_len):
    B, T, _ = proj.shape
    col = lambda c0: pl.BlockSpec((None, T, CONV_CH), lambda b: (b, 0, c0 // CONV_CH))
    return pl.pallas_call(
        functools.partial(_conv_kernel, ctx_len=ctx_len),
        grid=(B,),
        in_specs=[col(COL_CB), col(COL_CC), col(COL_CX), pl.BlockSpec((3, CONV_CH), lambda b: (0, 0))],
        out_specs=pl.BlockSpec((None, T, CONV_CH), lambda b: (b, 0, 0)),
        out_shape=jax.ShapeDtypeStruct((B, T, CONV_CH), BF16),
        scratch_shapes=[pltpu.VMEM((T + 16, CONV_CH), F32)],
        compiler_params=_params(("parallel",)),
    )(proj, proj, proj, conv_w)


def _split2(v):
    hi = v.astype(BF16)
    lo = (v - hi.astype(F32)).astype(BF16)
    return hi, lo


def _gla_kernel(q_ref, k_ref, v_ref, gate_ref, lr_ref, w2_ref, bias_ref, ng_ref, o_ref,
                oacc_scr, qs_scr, u_scr, dec_scr, sin_scr, *, reverse, n_ctx_chunks):
    T = q_ref.shape[0]
    C = GLA_CHUNK
    n_tiles = T // SEQ_TILE
    n_chunks = T // C
    per_tile = SEQ_TILE // C
    nt_dims = (((1,), (1,)), ((), ()))
    tn_dims = (((0,), (0,)), ((), ()))

    r = lax.broadcasted_iota(jnp.int32, (SEQ_TILE, SEQ_TILE), 0)
    c = lax.broadcasted_iota(jnp.int32, (SEQ_TILE, SEQ_TILE), 1)
    same_chunk = (r // C) == (c // C)
    causal = same_chunk & ((c >= r) if reverse else (c <= r))
    cum_mat = jnp.where(causal, 1.0, 0.0).astype(BF16)
    same_head = (r // GLA_DK) == (c // GLA_DK)
    head_mean = jnp.where(same_head, 1.0 / GLA_DK, 0.0).astype(BF16)
    lane_head = lax.broadcasted_iota(jnp.int32, (C, GLA_W), 1) // GLA_DK

    def phase1(i, carry):
        rows = pl.ds(pl.multiple_of(i * SEQ_TILE, SEQ_TILE), SEQ_TILE)
        z = jnp.dot(lr_ref[rows, :], w2_ref[...], preferred_element_type=F32) + bias_ref[...]
        la = (jnp.minimum(z, 0.0) - jnp.log(1.0 + jnp.exp(-jnp.abs(z)))) / GLA_TAU
        hi, lo = _split2(la)
        b = jnp.dot(cum_mat, hi, preferred_element_type=F32) + jnp.dot(cum_mat, lo, preferred_element_type=F32)
        b3 = b.reshape(per_tile, C, GLA_W)
        total = b3[:, 0:1, :] if reverse else b3[:, C - 1:C, :]
        b_last = jnp.broadcast_to(total, (per_tile, C, GLA_W)).reshape(SEQ_TILE, GLA_W)
        b_ref = 0.5 * b_last
        q = q_ref[rows, :].astype(F32)
        k = k_ref[rows, :].astype(F32)
        v = v_ref[rows, :]
        e_fwd = jnp.exp(b - b_ref)
        e_bwd = jnp.exp(b_ref - b)
        e_half = jnp.exp(b_ref)
        qe = (q * e_fwd).astype(BF16)
        ke = (k * e_bwd).astype(BF16)
        ku = (k * (e_bwd * e_half)).astype(BF16)
        qs_scr[rows, :] = (q * (e_fwd * e_half)).astype(BF16)
        dec = e_half * e_half
        for h in range(GLA_HEADS):
            hs = slice(h * GLA_DK, (h + 1) * GLA_DK)
            s = lax.dot_general(qe[:, hs], ke[:, hs], nt_dims, preferred_element_type=F32)
            a = jnp.where(causal, s, 0.0).astype(BF16)
            oacc_scr[rows, hs] = jnp.dot(a, v[:, hs], preferred_element_type=F32)
        for ci in range(per_tile):
            cs = slice(ci * C, (ci + 1) * C)
            full = lax.dot_general(v[cs, :], ku[cs, :], tn_dims, preferred_element_type=F32)
            ut = full[0:C, :]
            for h in range(1, GLA_HEADS):
                ut = jnp.where(lane_head == h, full[h * C:(h + 1) * C, :], ut)
            u_scr[i * per_tile + ci] = ut
            dec_scr[i * per_tile + ci] = dec[ci * C:ci * C + 1, :]
        return carry

    lax.fori_loop(0, n_tiles, phase1, 0, unroll=2)

    def phase2(i, st):
        if reverse:
            ci = jnp.where(i < n_ctx_chunks, n_ctx_chunks - 1 - i, n_chunks - 1 + n_ctx_chunks - i)
        else:
            ci = i
        sin_scr[ci] = st.astype(BF16)
        return st * dec_scr[ci] + u_scr[ci]

    lax.fori_loop(0, n_chunks, phase2, jnp.zeros((C, GLA_W), F32))

    def phase3(i, carry):
        rows = pl.ds(pl.multiple_of(i * SEQ_TILE, SEQ_TILE), SEQ_TILE)
        qs = qs_scr[rows, :]
        parts = []
        for ci in range(per_tile):
            st = sin_scr[i * per_tile + ci]
            st_heads = jnp.where(same_head, jnp.concatenate([st] * GLA_HEADS, axis=0), jnp.zeros((), BF16))
            parts.append(lax.dot_general(qs[ci * C:(ci + 1) * C, :], st_heads, nt_dims, preferred_element_type=F32))
        o = oacc_scr[rows, :] + jnp.concatenate(parts, axis=0)
        ms = jnp.dot((o * o).astype(BF16), head_mean, preferred_element_type=F32)
        y = o * lax.rsqrt(ms + EPS) * ng_ref[...]
        o_ref[rows, :] = (y * _silu(gate_ref[rows, :].astype(F32))).astype(BF16)
        return carry

    lax.fori_loop(0, n_tiles, phase3, 0, unroll=2)


def _gla(proj, w2pad, bias, ng, ctx_len, reverse):
    B, T, _ = proj.shape
    assert T % SEQ_TILE == 0 and ctx_len % GLA_CHUNK == 0
    n_chunks = T // GLA_CHUNK
    col = lambda c0: pl.BlockSpec((None, T, GLA_W), lambda b: (b, 0, c0 // GLA_W))
    gate_col = COL_GGB if reverse else COL_GGF
    kern = functools.partial(_gla_kernel, reverse=reverse, n_ctx_chunks=ctx_len // GLA_CHUNK)
    return pl.pallas_call(
        kern,
        grid=(B,),
        in_specs=[col(COL_GQ), col(COL_GK), col(COL_GV), col(gate_col),
                  pl.BlockSpec((None, T, LANES), lambda b: (b, 0, COL_LR // LANES)),
                  pl.BlockSpec((LANES, GLA_W), lambda b: (0, 0)),
                  pl.BlockSpec((1, GLA_W), lambda b: (0, 0)),
                  pl.BlockSpec((1, GLA_W), lambda b: (0, 0))],
        out_specs=pl.BlockSpec((None, T, GLA_W), lambda b: (b, 0, 0)),
        out_shape=jax.ShapeDtypeStruct((B, T, GLA_W), BF16),
        scratch_shapes=[pltpu.VMEM((T, GLA_W), F32),
                        pltpu.VMEM((T, GLA_W), BF16),
                        pltpu.VMEM((n_chunks, GLA_CHUNK, GLA_W), F32),
                        pltpu.VMEM((n_chunks, 1, GLA_W), F32),
                        pltpu.VMEM((n_chunks, GLA_CHUNK, GLA_W), BF16)],
        compiler_params=_params(("parallel",)),
    )(proj, proj, proj, proj, proj, w2pad, bias, ng)


def _outproj_kernel(*refs, tm, ctx_len, moe):
    if moe:
        (x_ref, att_ref, gf_ref, gb_ref, cv_ref, w_ref, g1_ref, cg1_ref, ng_ref, sc_ref, csc_ref, sh_ref, csh_ref,
         router_ref, xo_ref, hp_ref, route_ref) = refs
    else:
        (x_ref, att_ref, gf_ref, gb_ref, cv_ref, w_ref, g1_ref, cg1_ref, ng_ref, sc_ref, csc_ref, sh_ref, csh_ref,
         xo_ref, h_ref) = refs
    is_ctx = _row_is_ctx(pl.program_id(1), tm, ctx_len)
    gla = (gf_ref[...].astype(F32) + gb_ref[...].astype(F32)).astype(BF16)
    y = jnp.dot(att_ref[...], w_ref[:ATT_W, :], preferred_element_type=F32)
    y += jnp.dot(gla, w_ref[ATT_W:ATT_W + GLA_W, :], preferred_element_type=F32)
    y += jnp.dot(cv_ref[...], w_ref[ATT_W + GLA_W:, :], preferred_element_type=F32)
    x = x_ref[...] + jnp.where(is_ctx, cg1_ref[...], g1_ref[...]) * y
    xo_ref[...] = x
    n = x * lax.rsqrt(jnp.mean(x * x, axis=-1, keepdims=True) + EPS) * ng_ref[...]
    scale = jnp.where(is_ctx, csc_ref[...], sc_ref[...])
    shift = jnp.where(is_ctx, csh_ref[...], sh_ref[...])
    h = n * (1.0 + scale) + shift
    hi = h.astype(BF16)
    if not moe:
        h_ref[...] = hi
    else:
        _store_packed(hp_ref, _pack_bf16_pairs(h))
        logits = jnp.dot(hi, router_ref[...], preferred_element_type=F32)
        lane = lax.broadcasted_iota(jnp.int32, logits.shape, 1)
        neg = -jnp.inf
        lg = jnp.where(lane < N_EXPERTS, logits, neg)
        m1 = jnp.max(lg, axis=-1, keepdims=True)
        i1 = jnp.min(jnp.where(lg == m1, lane, LANES), axis=-1, keepdims=True)
        lg2 = jnp.where(lane == i1, neg, lg)
        m2 = jnp.max(lg2, axis=-1, keepdims=True)
        i2 = jnp.min(jnp.where(lg2 == m2, lane, LANES), axis=-1, keepdims=True)
        e2 = jnp.exp(m2 - m1)
        gate1 = 1.0 / (1.0 + e2)
        gate2 = e2 / (1.0 + e2)
        route_ref[...] = jnp.where(lane == 0, i1.astype(F32), jnp.where(lane == 1, i2.astype(F32),
                                   jnp.where(lane == 2, gate1, jnp.where(lane == 3, gate2, 0.0))))


def _outproj(xt, att, gf, gb, cv, w_out, layer, ng, mod_l, mod_c, ctx_len, router=None):
    B, T, D = xt.shape
    tm = _pick_tile(T, 640)
    moe = router is not None
    row = lambda w: pl.BlockSpec((None, tm, w), lambda b, j: (b, j, 0))
    const = lambda shape: pl.BlockSpec(shape, lambda b, j: (0,) * len(shape))
    in_specs = [row(D), row(ATT_W), row(GLA_W), row(GLA_W), row(CONV_CH),
                pl.BlockSpec((None, D, D), lambda b, j: (layer, 0, 0)),
                *_mod_specs(2, D), const((1, D)), *_mod_specs(4, D), *_mod_specs(3, D)]
    args = [xt, att, gf, gb, cv, w_out, mod_l, mod_c, ng, mod_l, mod_c, mod_l, mod_c]
    nt = T // tm
    if moe:
        in_specs.append(pl.BlockSpec((None, D, LANES), lambda b, j: (router[1], 0, 0)))
        args.append(router[0])
        out_specs = [row(D), pl.BlockSpec((PACK_CHUNKS, tm, LANES), lambda b, j: (0, b * nt + j, 0)), row(LANES)]
        out_shape = [jax.ShapeDtypeStruct((B, T, D), F32), jax.ShapeDtypeStruct((PACK_CHUNKS, B * T, LANES), jnp.int32),
                     jax.ShapeDtypeStruct((B, T, LANES), F32)]
    else:
        out_specs = [row(D), row(D)]
        out_shape = [jax.ShapeDtypeStruct((B, T, D), F32), jax.ShapeDtypeStruct((B, T, D), BF16)]
    return pl.pallas_call(
        functools.partial(_outproj_kernel, tm=tm, ctx_len=ctx_len, moe=moe),
        grid=(B, T // tm),
        in_specs=in_specs, out_specs=out_specs, out_shape=out_shape,
        compiler_params=_params(("parallel", "parallel")),
    )(*args)


def _ffn_kernel(h_ref, x_ref, g2_ref, cg2_ref, wg_ref, wu_ref, wd_ref, o_ref, acc_ref, *, tm, ctx_len):
    j = pl.program_id(1)
    f = pl.program_id(2)

    @pl.when(f == 0)
    def _init():
        acc_ref[...] = jnp.zeros_like(acc_ref)

    h = h_ref[...]
    g = jnp.dot(h, wg_ref[...], preferred_element_type=F32)
    u = jnp.dot(h, wu_ref[...], preferred_element_type=F32)
    acc_ref[...] += jnp.dot((_silu(g) * u).astype(BF16), wd_ref[...], preferred_element_type=F32)

    @pl.when(f == pl.num_programs(2) - 1)
    def _finish():
        is_ctx = _row_is_ctx(j, tm, ctx_len)
        o_ref[...] = x_ref[...] + jnp.where(is_ctx, cg2_ref[...], g2_ref[...]) * acc_ref[...]


def _ffn(h2, xt, mod_l, mod_c, w_gu, w_down, layer, ctx_len):
    B, T, D = xt.shape
    F = w_down.shape[1]
    fc = 1408
    assert F % fc == 0
    nf = F // fc
    tm = _pick_tile(T, 640)
    row = lambda w: pl.BlockSpec((None, tm, w), lambda b, j, f: (b, j, 0))
    return pl.pallas_call(
        functools.partial(_ffn_kernel, tm=tm, ctx_len=ctx_len),
        grid=(B, T // tm, nf),
        in_specs=[row(D), row(D), *_mod_specs(5, D),
                  pl.BlockSpec((None, D, fc), lambda b, j, f: (layer, 0, f)),
                  pl.BlockSpec((None, D, fc), lambda b, j, f: (layer, 0, nf + f)),
                  pl.BlockSpec((None, fc, D), lambda b, j, f: (layer, f, 0))],
        out_specs=row(D),
        out_shape=jax.ShapeDtypeStruct((B, T, D), F32),
        scratch_shapes=[pltpu.VMEM((tm, D), F32)],
        compiler_params=_params(("parallel", "parallel", "arbitrary")),
    )(h2, xt, mod_l, mod_c, w_gu, w_gu, w_down)


def _sc_gather(table, idx):
    _, width = table.shape
    m = idx.shape[0]
    workers = SC_CORES * SC_SUBCORES
    ch = SC_INDEX_CHUNK
    assert width == LANES and m % (workers * ch) == 0
    n_chunks = m // (workers * ch)
    nb = max(b for b in (4, 3, 2, 1) if n_chunks % b == 0)
    mesh = plsc.VectorSubcoreMesh(core_axis_name="c", subcore_axis_name="s",
                                  num_cores=SC_CORES, num_subcores=SC_SUBCORES)

    @functools.partial(
        pl.kernel, mesh=mesh, out_type=jax.ShapeDtypeStruct((m, width), table.dtype),
        scratch_types=[pltpu.VMEM((n_chunks, ch), jnp.int32), pltpu.VMEM((nb, ch, width), table.dtype)]
        + [pltpu.SemaphoreType.DMA] * (2 * nb))
    def gather(table_hbm, idx_hbm, out_hbm, idx_v, rows_v, *sems):
        wid = lax.axis_index("s") * SC_CORES + lax.axis_index("c")
        base = wid * (n_chunks * ch)
        pltpu.sync_copy(idx_hbm.at[wid], idx_v)

        def fetch(c, b):
            return pltpu.make_async_copy(table_hbm.at[idx_v.at[c]], rows_v.at[b], sems[b])

        def flush(c, b):
            return pltpu.make_async_copy(rows_v.at[b], out_hbm.at[pl.ds(base + c * ch, ch)], sems[nb + b])

        @pl.loop(0, n_chunks, step=nb)
        def _(c0):
            for b in range(nb):
                @pl.when(c0 > 0)
                def _():
                    flush(c0 - nb + b, b).wait()
                fetch(c0 + b, b).start()
            for b in range(nb):
                fetch(c0 + b, b).wait()
                flush(c0 + b, b).start()

        for b in range(nb):
            flush(n_chunks - nb + b, b).wait()

    return gather(table, idx.reshape(workers, n_chunks, ch))


def _routing(route, tile):
    B, T, _ = route.shape
    n = B * T
    assert (2 * n) % tile == 0
    ids = route[..., :2].astype(jnp.int32).reshape(n, 2)
    e_cat = jnp.concatenate([ids[:, 0], ids[:, 1]])
    onehot = (e_cat[:, None] == jnp.arange(N_EXPERTS, dtype=jnp.int32)[None, :]).astype(jnp.int32)
    csum = jnp.cumsum(onehot, axis=0)
    pos = jnp.sum(onehot * csum, axis=1) - 1
    padded = (csum[-1] + tile - 1) // tile * tile
    ends = jnp.cumsum(padded)
    dest = ((ends - padded)[e_cat] + pos).astype(jnp.int32)
    n_rows = 2 * n + N_EXPERTS * tile
    id_bits = (2 * n - 1).bit_length()
    assert N_EXPERTS << id_bits < 2 ** 31
    order = jnp.sort((e_cat << id_bits) | jnp.arange(2 * n, dtype=jnp.int32)) & ((1 << id_bits) - 1)
    max_shift = N_EXPERTS * tile
    base = jnp.concatenate([jnp.zeros((max_shift,), jnp.int32), order, jnp.zeros((max_shift,), jnp.int32)])
    shift = (ends - padded) - (jnp.cumsum(csum[-1]) - csum[-1])
    row = jnp.arange(n_rows, dtype=jnp.int32)
    row_src = jnp.zeros((n_rows,), jnp.int32)
    for e in range(N_EXPERTS):
        shifted = lax.dynamic_slice(base, (max_shift - shift[e],), (n_rows,))
        row_src = jnp.where(row >= (ends - padded)[e], shifted, row_src)
    row_token = row_src % n
    tile_start = jnp.arange(n_rows // tile, dtype=jnp.int32) * tile
    tile_expert = jnp.minimum(jnp.searchsorted(ends, tile_start, side="right"), N_EXPERTS - 1).astype(jnp.int32)
    tile_valid = (tile_start < ends[-1]).astype(jnp.int32)
    return dest, row_token, tile_expert, tile_valid


def _expert_kernel(te_ref, tv_ref, xs_ref, wg_ref, wu_ref, wd_ref, ys_ref):
    i = pl.program_id(0)

    @pl.when(tv_ref[i] > 0)
    def _compute():
        lo, hi = _unpack_bf16_pairs(_load_packed(xs_ref))
        lo = lo.astype(BF16)
        hi = hi.astype(BF16)
        g = (jnp.dot(lo, wg_ref[:PACK_W, :], preferred_element_type=F32)
             + jnp.dot(hi, wg_ref[PACK_W:, :], preferred_element_type=F32))
        u = (jnp.dot(lo, wu_ref[:PACK_W, :], preferred_element_type=F32)
             + jnp.dot(hi, wu_ref[PACK_W:, :], preferred_element_type=F32))
        y = jnp.dot((_silu(g) * u).astype(BF16), wd_ref[...], preferred_element_type=F32)
        _store_packed(ys_ref, _pack_bf16_pairs(y))

    @pl.when(tv_ref[i] == 0)
    def _unused_tile():
        ys_ref[...] = jnp.zeros_like(ys_ref)


def _experts(xs, tile_expert, tile_valid, w_gu, w_down, layer):
    _, n_rows, _ = xs.shape
    _, _, F, D = w_down.shape
    tile = EXPERT_TILE
    rows = pl.BlockSpec((PACK_CHUNKS, tile, LANES), lambda i, te, tv: (0, i, 0))
    return pl.pallas_call(
        _expert_kernel,
        grid_spec=pltpu.PrefetchScalarGridSpec(
            num_scalar_prefetch=2,
            grid=(n_rows // tile,),
            in_specs=[rows,
                      pl.BlockSpec((None, None, D, F), lambda i, te, tv: (layer, te[i], 0, 0)),
                      pl.BlockSpec((None, None, D, F), lambda i, te, tv: (layer, te[i], 0, 1)),
                      pl.BlockSpec((None, None, F, D), lambda i, te, tv: (layer, te[i], 0, 0))],
            out_specs=rows),
        out_shape=jax.ShapeDtypeStruct(xs.shape, jnp.int32),
        compiler_params=_params(("arbitrary",)),
    )(tile_expert, tile_valid, xs, w_gu, w_gu, w_down)


def _combine_kernel(*refs, tm, ctx_len, skip, final):
    if final:
        x_ref, g2_ref, cg2_ref, route_ref, ya_ref, yb_ref, fg_ref, o_ref = refs
    else:
        x_ref, g2_ref, cg2_ref, route_ref, ya_ref, yb_ref, o_ref = refs
    is_ctx = _row_is_ctx(pl.program_id(1) + skip, tm, ctx_len)
    route = route_ref[...]
    lane = lax.broadcasted_iota(jnp.int32, route.shape, 1)
    gate_a = jnp.sum(jnp.where(lane == 2, route, 0.0), axis=-1, keepdims=True)
    gate_b = jnp.sum(jnp.where(lane == 3, route, 0.0), axis=-1, keepdims=True)
    unpack = lambda ref: jnp.concatenate(_unpack_bf16_pairs(_load_packed(ref)), axis=1)
    f = gate_a * unpack(ya_ref) + gate_b * unpack(yb_ref)
    x = x_ref[...] + jnp.where(is_ctx, cg2_ref[...], g2_ref[...]) * f
    if final:
        x = x * lax.rsqrt(jnp.mean(x * x, axis=-1, keepdims=True) + EPS) * fg_ref[...]
    o_ref[...] = x


def _combine(xt, mod_l, mod_c, route, yg, ctx_len, final_g=None):
    B, T, D = xt.shape
    final = final_g is not None
    if final:
        tm = SEQ_TILE
        assert ctx_len % tm == 0 and T % tm == 0
        skip = ctx_len // tm
    else:
        tm = _pick_tile(T, 640)
        skip = 0
    nt = T // tm
    row = lambda w: pl.BlockSpec((None, tm, w), lambda b, j: (b, j + skip, 0))
    slot = lambda s: pl.BlockSpec((PACK_CHUNKS, None, tm, LANES), lambda b, j: (0, s, b * nt + j + skip, 0))
    in_specs = [row(D), *_mod_specs(5, D), row(LANES), slot(0), slot(1)]
    args = [xt, mod_l, mod_c, route, yg, yg]
    if final:
        in_specs.append(pl.BlockSpec((1, D), lambda b, j: (0, 0)))
        args.append(final_g)
    return pl.pallas_call(
        functools.partial(_combine_kernel, tm=tm, ctx_len=ctx_len, skip=skip, final=final),
        grid=(B, nt - skip),
        in_specs=in_specs,
        out_specs=pl.BlockSpec((None, tm, D), lambda b, j: (b, j, 0)),
        out_shape=jax.ShapeDtypeStruct((B, T - skip * tm, D), F32),
        compiler_params=_params(("parallel", "parallel")),
    )(*args)


def _moe(hp, route, xt, mod_l, mod_c, w_gu, w_down, layer, ctx_len, final_g=None):
    B, T, _ = xt.shape
    n = B * T
    dest, row_token, tile_expert, tile_valid = _routing(route, EXPERT_TILE)
    n_rows = row_token.shape[0]
    chunk = jnp.arange(PACK_CHUNKS, dtype=jnp.int32)[:, None]
    xs = _sc_gather(hp.reshape(PACK_CHUNKS * n, LANES), (chunk * n + row_token[None, :]).reshape(-1))
    ys = _experts(xs.reshape(PACK_CHUNKS, n_rows, LANES), tile_expert, tile_valid, w_gu, w_down, layer)
    yg = _sc_gather(ys.reshape(PACK_CHUNKS * n_rows, LANES), (chunk * n_rows + dest[None, :]).reshape(-1))
    return _combine(xt, mod_l, mod_c, route, yg.reshape(PACK_CHUNKS, 2, n, LANES), ctx_len, final_g)


def _final_norm_kernel(x_ref, g_ref, o_ref):
    x = x_ref[...]
    o_ref[...] = x * lax.rsqrt(jnp.mean(x * x, axis=-1, keepdims=True) + EPS) * g_ref[...]


def _final_norm(xt, g, ctx_len):
    B, T, D = xt.shape
    S = T - ctx_len
    tm = SEQ_TILE
    assert ctx_len % tm == 0 and S % tm == 0
    skip = ctx_len // tm
    return pl.pallas_call(
        _final_norm_kernel,
        grid=(B, S // tm),
        in_specs=[pl.BlockSpec((None, tm, D), lambda b, j: (b, j + skip, 0)),
                  pl.BlockSpec((1, D), lambda b, j: (0, 0))],
        out_specs=pl.BlockSpec((None, tm, D), lambda b, j: (b, j, 0)),
        out_shape=jax.ShapeDtypeStruct((B, S, D), F32),
        compiler_params=_params(("parallel", "parallel")),
    )(xt, g)


def _rope_tables(seq, ctx_len):
    pos = jnp.arange(seq, dtype=jnp.int32)
    nf = HEAD_DIM // 4
    inv = ROPE_THETA ** (-jnp.arange(nf, dtype=F32) / nf)
    ang = jnp.concatenate([(pos // GRID_W).astype(F32)[:, None] * inv, (pos % GRID_W).astype(F32)[:, None] * inv], axis=-1)
    cos, sin = jnp.cos(ang), jnp.sin(ang)
    reps = LANES // HEAD_DIM
    cos_t = jnp.tile(jnp.concatenate([cos, cos], axis=-1), (1, reps))
    sin_t = jnp.tile(jnp.concatenate([-sin, sin], axis=-1), (1, reps))
    cos_t = jnp.concatenate([jnp.ones((ctx_len, LANES), F32), cos_t], axis=0)
    sin_t = jnp.concatenate([jnp.zeros((ctx_len, LANES), F32), sin_t], axis=0)
    return cos_t, sin_t


def _relayout_w_in(w_in):
    o_lr = ATT_W + 2 * KV_W + 5 * GLA_W
    q_scale = HEAD_DIM ** -0.5
    gq_scale = GLA_DK ** -0.5
    parts = [w_in[..., :ATT_W] * q_scale, w_in[..., ATT_W:COL_GQ], w_in[..., COL_GQ:COL_GK] * gq_scale,
             w_in[..., COL_GK:o_lr], w_in[..., o_lr + 2 * GLA_LOWRANK:], w_in[..., o_lr:o_lr + 2 * GLA_LOWRANK],
             jnp.zeros(w_in.shape[:-1] + (LANES - 2 * GLA_LOWRANK,), w_in.dtype)]
    return jnp.concatenate(parts, axis=-1).astype(BF16)


def kernel(x, c, ctx, c_ctx, w_mod, b_mod, norm1_g, norm2_g, w_in, w_out, attn_sink, gla_w2_f, gla_b_f, gla_w2_b, gla_b_b, gla_norm_g, conv_w, ffn_w_gu, ffn_w_down, router_w, expert_w_gu, expert_w_down, final_norm_g):
    B, S, D = x.shape
    L = ctx.shape[1]
    depth = w_in.shape[0]
    assert D == D_MODEL and S % GRID_W == 0

    xt = jnp.concatenate([ctx, x], axis=1)
    rows = 16
    cc = jnp.concatenate([c, c_ctx[None, :], jnp.zeros((rows - B - 1, D), F32)], axis=0)
    mod = _adaln(cc, w_mod, b_mod)
    cos_t, sin_t = _rope_tables(S, L)

    w_in_p = _relayout_w_in(w_in)
    w_out_b = _to_bf16(w_out)
    zpad = jnp.zeros((depth, LANES - 2 * GLA_LOWRANK, GLA_W), F32)
    w2_f = jnp.concatenate([gla_w2_f, jnp.zeros_like(gla_w2_b), zpad], axis=1).astype(BF16)
    w2_b = jnp.concatenate([jnp.zeros_like(gla_w2_f), gla_w2_b, zpad], axis=1).astype(BF16)
    gla_ng = jnp.tile(gla_norm_g, (1, GLA_HEADS))
    ffn_gu_b = _to_bf16(ffn_w_gu)
    ffn_down_b = _to_bf16(ffn_w_down)
    exp_gu_b = _to_bf16(expert_w_gu)
    exp_down_b = _to_bf16(expert_w_down)
    router_b = jnp.pad(router_w, ((0, 0), (0, 0), (0, LANES - N_EXPERTS))).astype(BF16)

    for l in range(depth):
        mod_l = mod[l, :B].reshape(B, 1, 6 * D)
        mod_c = mod[l, B:B + 1]
        proj = _inproj(xt, norm1_g[l][None, :], mod_l, mod_c, cos_t, sin_t, w_in_p, l, L)
        att = _attention(proj, attn_sink[l], L)
        cv = _short_conv(proj, conv_w[l], L)
        gf = _gla(proj, w2_f[l], gla_b_f[l][None, :], gla_ng[l][None, :], L, reverse=False)
        gb = _gla(proj, w2_b[l], gla_b_b[l][None, :], gla_ng[l][None, :], L, reverse=True)
        if l % 2 == 0:
            xt, h2 = _outproj(xt, att, gf, gb, cv, w_out_b, l, norm2_g[l][None, :], mod_l, mod_c, L)
            xt = _ffn(h2, xt, mod_l, mod_c, ffn_gu_b, ffn_down_b, l // 2, L)
        else:
            xt, hp, route = _outproj(xt, att, gf, gb, cv, w_out_b, l, norm2_g[l][None, :], mod_l, mod_c, L,
                                     router=(router_b, l // 2))
            xt = _moe(hp, route, xt, mod_l, mod_c, exp_gu_b, exp_down_b, l // 2, L,
                      final_g=final_norm_g[None, :] if l == depth - 1 else None)
    return xt if depth % 2 == 0 else _final_norm(xt, final_norm_g[None, :], L)
```

```python
import functools

import jax
import jax.numpy as jnp
from jax import lax
from jax.experimental import pallas as pl
from jax.experimental.pallas import tpu as pltpu
from jax.experimental.pallas import tpu_sc as plsc

F32 = jnp.float32
BF16 = jnp.bfloat16

D_MODEL = 1024
EPS = 1e-6
GRID_W = 64
ROPE_THETA = 10000.0
HEAD_DIM = 64
ATT_HEADS = 8
ATT_KV_HEADS = 2
ATT_GROUP = ATT_HEADS // ATT_KV_HEADS
ATT_W = ATT_HEADS * HEAD_DIM
KV_W = ATT_KV_HEADS * HEAD_DIM
WINDOW = 128
GLA_HEADS = 4
GLA_DK = 64
GLA_W = GLA_HEADS * GLA_DK
GLA_LOWRANK = 16
GLA_TAU = 16.0
GLA_CHUNK = 64
CONV_CH = 256
N_EXPERTS = 8

LANES = 128
SUBLANES_BF16 = 16

COL_Q = 0
COL_K = ATT_W
COL_V = COL_K + KV_W
ROPE_W = COL_V
COL_GQ = COL_V + KV_W
COL_GK = COL_GQ + GLA_W
COL_GV = COL_GK + GLA_W
COL_GGF = COL_GV + GLA_W
COL_GGB = COL_GGF + GLA_W
COL_CB = COL_GGB + GLA_W
COL_CC = COL_CB + CONV_CH
COL_CX = COL_CC + CONV_CH
COL_LR = COL_CX + CONV_CH
PROJ_W = COL_LR + LANES

SEQ_TILE = 256
VMEM_LIMIT = 56 * 1024 * 1024
CAST_BLOCK_BYTES = 4 * 1024 * 1024

PACK_W = D_MODEL // 2
PACK_CHUNKS = PACK_W // LANES
EXPERT_TILE = 512
SC_CORES = 2
SC_SUBCORES = 16
SC_INDEX_CHUNK = 128


def _silu(v):
    return v / (1.0 + jnp.exp(-v))


def _pick_tile(total, cap):
    best = None
    for t in range(SUBLANES_BF16, cap + 1, SUBLANES_BF16):
        if total % t == 0:
            best = t
    assert best is not None
    return best


def _params(sem):
    return pltpu.CompilerParams(dimension_semantics=sem, vmem_limit_bytes=VMEM_LIMIT)


def _pack_bf16_pairs(v):
    bits = lambda t: lax.bitcast_convert_type(t.astype(BF16).astype(F32), jnp.int32)
    return ((bits(v[:, :PACK_W]) >> 16) & 0xFFFF) | (bits(v[:, PACK_W:]) & -65536)


def _unpack_bf16_pairs(w):
    return lax.bitcast_convert_type(w << 16, F32), lax.bitcast_convert_type(w & -65536, F32)


def _cast_kernel(x_ref, o_ref):
    o_ref[...] = x_ref[...].astype(BF16)


def _to_bf16(w):
    cols = w.shape[-1]
    rows = w.size // cols
    tr = _pick_tile(rows, max(SUBLANES_BF16, CAST_BLOCK_BYTES // (4 * cols)))
    out = pl.pallas_call(
        _cast_kernel,
        grid=(rows // tr,),
        in_specs=[pl.BlockSpec((tr, cols), lambda i: (i, 0))],
        out_specs=pl.BlockSpec((tr, cols), lambda i: (i, 0)),
        out_shape=jax.ShapeDtypeStruct((rows, cols), BF16),
        compiler_params=_params(("parallel",)),
    )(w.reshape(rows, cols))
    return out.reshape(w.shape)


def _load_packed(ref):
    return jnp.concatenate([ref[k] for k in range(PACK_CHUNKS)], axis=1)


def _store_packed(ref, packed):
    for k in range(PACK_CHUNKS):
        ref[k] = packed[:, k * LANES:(k + 1) * LANES]


def _adaln_kernel(c_ref, w_ref, b_ref, o_ref):
    s = _silu(c_ref[...]).astype(BF16)
    o_ref[...] = jnp.dot(s, w_ref[...].astype(BF16), preferred_element_type=F32) + b_ref[...]


def _adaln(cc, w_mod, b_mod):
    depth, d, n = w_mod.shape
    tn = 1024
    rows = cc.shape[0]
    return pl.pallas_call(
        _adaln_kernel,
        grid=(depth, n // tn),
        in_specs=[pl.BlockSpec((rows, d), lambda l, j: (0, 0)),
                  pl.BlockSpec((None, d, tn), lambda l, j: (l, 0, j)),
                  pl.BlockSpec((None, 1, tn), lambda l, j: (l, 0, j))],
        out_specs=pl.BlockSpec((None, rows, tn), lambda l, j: (l, 0, j)),
        out_shape=jax.ShapeDtypeStruct((depth, rows, n), F32),
        compiler_params=_params(("parallel", "parallel")),
    )(cc, w_mod, b_mod.reshape(depth, 1, n))


def _mod_specs(k, d):
    return [pl.BlockSpec((None, 1, d), lambda b, j, *_: (b, 0, k)),
            pl.BlockSpec((1, d), lambda b, j, *_: (0, k))]


def _row_is_ctx(j, tm, ctx_len):
    return (j * tm + lax.broadcasted_iota(jnp.int32, (tm, 1), 0)) < ctx_len


def _inproj_kernel(x_ref, g_ref, sc_ref, csc_ref, sh_ref, csh_ref, cos_ref, sin_ref, w_ref, o_ref, *, tm, ctx_len):
    x = x_ref[...]
    n = x * lax.rsqrt(jnp.mean(x * x, axis=-1, keepdims=True) + EPS) * g_ref[...]
    is_ctx = _row_is_ctx(pl.program_id(1), tm, ctx_len)
    scale = jnp.where(is_ctx, csc_ref[...], sc_ref[...])
    shift = jnp.where(is_ctx, csh_ref[...], sh_ref[...])
    h = (n * (1.0 + scale) + shift).astype(BF16)
    qk = jnp.dot(h, w_ref[:, :ROPE_W], preferred_element_type=F32)
    lower_half = (lax.broadcasted_iota(jnp.int32, (tm, LANES), 1) % HEAD_DIM) < (HEAD_DIM // 2)
    cos = cos_ref[...]
    sin = sin_ref[...]
    for i in range(ROPE_W // LANES):
        t = qk[:, i * LANES:(i + 1) * LANES]
        partner = jnp.where(lower_half, pltpu.roll(t, LANES - HEAD_DIM // 2, axis=1), pltpu.roll(t, HEAD_DIM // 2, axis=1))
        o_ref[:, i * LANES:(i + 1) * LANES] = (t * cos + partner * sin).astype(BF16)
    o_ref[:, ROPE_W:] = jnp.dot(h, w_ref[:, ROPE_W:], preferred_element_type=F32).astype(BF16)


def _inproj(xt, g, mod_l, mod_c, cos_t, sin_t, w, layer, ctx_len):
    B, T, D = xt.shape
    tm = _pick_tile(T, 640)
    kern = functools.partial(_inproj_kernel, tm=tm, ctx_len=ctx_len)
    return pl.pallas_call(
        kern,
        grid=(B, T // tm),
        in_specs=[pl.BlockSpec((None, tm, D), lambda b, j: (b, j, 0)),
                  pl.BlockSpec((1, D), lambda b, j: (0, 0)),
                  *_mod_specs(1, D), *_mod_specs(0, D),
                  pl.BlockSpec((tm, LANES), lambda b, j: (j, 0)),
                  pl.BlockSpec((tm, LANES), lambda b, j: (j, 0)),
                  pl.BlockSpec((None, D, PROJ_W), lambda b, j: (layer, 0, 0))],
        out_specs=pl.BlockSpec((None, tm, PROJ_W), lambda b, j: (b, j, 0)),
        out_shape=jax.ShapeDtypeStruct((B, T, PROJ_W), BF16),
        compiler_params=_params(("parallel", "parallel")),
    )(xt, g, mod_l, mod_c, mod_l, mod_c, cos_t, sin_t, w)


def _attn_kernel(sink_ref, q_ref, kc_ref, vc_ref, km_ref, vm_ref, kp_ref, vp_ref, kn_ref, vn_ref, o_ref, *, n_tiles):
    j = pl.program_id(1)
    nt_dims = (((1,), (1,)), ((), ()))

    def group_attention(rows, n_rows, g, k_ctx, v_ctx, k_win=None, v_win=None, mask_win=None):
        heads = [g * ATT_GROUP + hh for hh in range(ATT_GROUP)]
        q = jnp.concatenate([q_ref[rows, h * HEAD_DIM:(h + 1) * HEAD_DIM] for h in heads], axis=0)
        head_row = lax.broadcasted_iota(jnp.int32, (ATT_GROUP * n_rows, 1), 0) // n_rows
        sink = jnp.full((ATT_GROUP * n_rows, 1), sink_ref[heads[0]], F32)
        for hh in range(1, ATT_GROUP):
            sink = jnp.where(head_row == hh, sink_ref[heads[hh]], sink)
        s_c = lax.dot_general(q, k_ctx, nt_dims, preferred_element_type=F32)
        m = jnp.maximum(jnp.max(s_c, axis=-1, keepdims=True), sink)
        if k_win is not None:
            s_w = jnp.where(mask_win, lax.dot_general(q, k_win, nt_dims, preferred_element_type=F32), -1e30)
            m = jnp.maximum(m, jnp.max(s_w, axis=-1, keepdims=True))
        p_c = jnp.exp(s_c - m)
        den = jnp.sum(p_c, axis=-1, keepdims=True) + jnp.exp(sink - m)
        o = jnp.dot(p_c.astype(BF16), v_ctx, preferred_element_type=F32)
        if k_win is not None:
            p_w = jnp.exp(s_w - m)
            den += jnp.sum(p_w, axis=-1, keepdims=True)
            o += jnp.dot(p_w.astype(BF16), v_win, preferred_element_type=F32)
        o = o / den
        for hh, h in enumerate(heads):
            o_ref[rows, h * HEAD_DIM:(h + 1) * HEAD_DIM] = o[hh * n_rows:(hh + 1) * n_rows].astype(BF16)

    @pl.when(j == 0)
    def _context_queries():
        for g in range(ATT_KV_HEADS):
            gs = slice(g * HEAD_DIM, (g + 1) * HEAD_DIM)
            group_attention(slice(0, SEQ_TILE), SEQ_TILE, g, kc_ref[:, gs], vc_ref[:, gs])

    @pl.when(j > 0)
    def _latent_queries():
        r = lax.broadcasted_iota(jnp.int32, (ATT_GROUP * WINDOW, 3 * WINDOW), 0) % WINDOW
        w = lax.broadcasted_iota(jnp.int32, (ATT_GROUP * WINDOW, 3 * WINDOW), 1)
        band = (w >= r) & (w <= r + 2 * WINDOW)
        for sub in range(SEQ_TILE // WINDOW):
            rows = slice(sub * WINDOW, (sub + 1) * WINDOW)
            if sub == 0:
                kprev, vprev = kp_ref[...], vp_ref[...]
                kmid, vmid = km_ref[:WINDOW], vm_ref[:WINDOW]
                knext, vnext = km_ref[WINDOW:], vm_ref[WINDOW:]
                w_lo = jnp.where(j > 1, 0, WINDOW)
                w_hi = 3 * WINDOW
            else:
                kprev, vprev = km_ref[:WINDOW], vm_ref[:WINDOW]
                kmid, vmid = km_ref[WINDOW:], vm_ref[WINDOW:]
                knext, vnext = kn_ref[...], vn_ref[...]
                w_lo = 0
                w_hi = jnp.where(j < n_tiles - 1, 3 * WINDOW, 2 * WINDOW)
            mask = band & (w >= w_lo) & (w < w_hi)
            for g in range(ATT_KV_HEADS):
                gs = slice(g * HEAD_DIM, (g + 1) * HEAD_DIM)
                k_win = jnp.concatenate([kprev[:, gs], kmid[:, gs], knext[:, gs]], axis=0)
                v_win = jnp.concatenate([vprev[:, gs], vmid[:, gs], vnext[:, gs]], axis=0)
                group_attention(rows, WINDOW, g, kc_ref[:, gs], vc_ref[:, gs], k_win, v_win, mask)


def _attention(proj, sink, ctx_len):
    B, T, _ = proj.shape
    assert ctx_len == SEQ_TILE and T % SEQ_TILE == 0
    n_tiles = T // SEQ_TILE
    n_win = T // WINDOW
    kcol, vcol = COL_K // KV_W, COL_V // KV_W
    per_tile = SEQ_TILE // WINDOW
    tile = lambda col: pl.BlockSpec((None, SEQ_TILE, KV_W), lambda b, j: (b, j, col))
    ctx = lambda col: pl.BlockSpec((None, SEQ_TILE, KV_W), lambda b, j: (b, 0, col))
    prev = lambda col: pl.BlockSpec((None, WINDOW, KV_W), lambda b, j: (b, jnp.maximum(j * per_tile - 1, 0), col))
    nxt = lambda col: pl.BlockSpec((None, WINDOW, KV_W), lambda b, j: (b, jnp.minimum((j + 1) * per_tile, n_win - 1), col))
    return pl.pallas_call(
        functools.partial(_attn_kernel, n_tiles=n_tiles),
        grid=(B, n_tiles),
        in_specs=[pl.BlockSpec(memory_space=pltpu.SMEM),
                  pl.BlockSpec((None, SEQ_TILE, ATT_W), lambda b, j: (b, j, 0)),
                  ctx(kcol), ctx(vcol), tile(kcol), tile(vcol), prev(kcol), prev(vcol), nxt(kcol), nxt(vcol)],
        out_specs=pl.BlockSpec((None, SEQ_TILE, ATT_W), lambda b, j: (b, j, 0)),
        out_shape=jax.ShapeDtypeStruct((B, T, ATT_W), BF16),
        compiler_params=_params(("parallel", "parallel")),
    )(sink, *([proj] * 9))


def _conv_kernel(b_ref, c_ref, x_ref, w_ref, o_ref, u_scr, *, ctx_len):
    T = c_ref.shape[0]
    pad = 8
    u = c_ref[...].astype(F32) * x_ref[...].astype(F32)
    zero_row = jnp.zeros((1, CONV_CH), F32)
    u_scr[pad - 1:pad, :] = zero_row
    u_scr[pad:pad + T, :] = u
    u_scr[pad + T:pad + T + 1, :] = zero_row
    row = lax.broadcasted_iota(jnp.int32, (T, 1), 0)
    u_prev = jnp.where(row == ctx_len, 0.0, u_scr[pad - 1:pad - 1 + T, :])
    u_next = jnp.where(row == ctx_len - 1, 0.0, u_scr[pad + 1:pad + 1 + T, :])
    y = w_ref[0:1, :] * u_prev + w_ref[1:2, :] * u + w_ref[2:3, :] * u_next
    o_ref[...] = (b_ref[...].astype(F32) * y).astype(BF16)


def _short_conv(proj, conv_w, ctx_len):
    B, T, _ = proj.shape
    col = lambda c0: pl.BlockSpec((None, T, CONV_CH), lambda b: (b, 0, c0 // CONV_CH))
    return pl.pallas_call(
        functools.partial(_conv_kernel, ctx_len=ctx_len),
        grid=(B,),
        in_specs=[col(COL_CB), col(COL_CC), col(COL_CX), pl.BlockSpec((3, CONV_CH), lambda b: (0, 0))],
        out_specs=pl.BlockSpec((None, T, CONV_CH), lambda b: (b, 0, 0)),
        out_shape=jax.ShapeDtypeStruct((B, T, CONV_CH), BF16),
        scratch_shapes=[pltpu.VMEM((T + 16, CONV_CH), F32)],
        compiler_params=_params(("parallel",)),
    )(proj, proj, proj, conv_w)


def _split2(v):
    hi = v.astype(BF16)
    lo = (v - hi.astype(F32)).astype(BF16)
    return hi, lo


def _gla_kernel(q_ref, k_ref, v_ref, gate_ref, lr_ref, w2_ref, bias_ref, ng_ref, o_ref,
                oacc_scr, qs_scr, u_scr, dec_scr, sin_scr, *, reverse, n_ctx_chunks):
    T = q_ref.shape[0]
    C = GLA_CHUNK
    n_tiles = T // SEQ_TILE
    n_chunks = T // C
    per_tile = SEQ_TILE // C
    nt_dims = (((1,), (1,)), ((), ()))
    tn_dims = (((0,), (0,)), ((), ()))

    r = lax.broadcasted_iota(jnp.int32, (SEQ_TILE, SEQ_TILE), 0)
    c = lax.broadcasted_iota(jnp.int32, (SEQ_TILE, SEQ_TILE), 1)
    same_chunk = (r // C) == (c // C)
    causal = same_chunk & ((c >= r) if reverse else (c <= r))
    cum_mat = jnp.where(causal, 1.0, 0.0).astype(BF16)
    same_head = (r // GLA_DK) == (c // GLA_DK)
    head_mean = jnp.where(same_head, 1.0 / GLA_DK, 0.0).astype(BF16)
    lane_head = lax.broadcasted_iota(jnp.int32, (C, GLA_W), 1) // GLA_DK

    def phase1(i, carry):
        rows = pl.ds(pl.multiple_of(i * SEQ_TILE, SEQ_TILE), SEQ_TILE)
        z = jnp.dot(lr_ref[rows, :], w2_ref[...], preferred_element_type=F32) + bias_ref[...]
        la = (jnp.minimum(z, 0.0) - jnp.log(1.0 + jnp.exp(-jnp.abs(z)))) / GLA_TAU
        hi, lo = _split2(la)
        b = jnp.dot(cum_mat, hi, preferred_element_type=F32) + jnp.dot(cum_mat, lo, preferred_element_type=F32)
        b3 = b.reshape(per_tile, C, GLA_W)
        total = b3[:, 0:1, :] if reverse else b3[:, C - 1:C, :]
        b_last = jnp.broadcast_to(total, (per_tile, C, GLA_W)).reshape(SEQ_TILE, GLA_W)
        b_ref = 0.5 * b_last
        q = q_ref[rows, :].astype(F32)
        k = k_ref[rows, :].astype(F32)
        v = v_ref[rows, :]
        e_fwd = jnp.exp(b - b_ref)
        e_bwd = jnp.exp(b_ref - b)
        e_half = jnp.exp(b_ref)
        qe = (q * e_fwd).astype(BF16)
        ke = (k * e_bwd).astype(BF16)
        ku = (k * (e_bwd * e_half)).astype(BF16)
        qs_scr[rows, :] = (q * (e_fwd * e_half)).astype(BF16)
        dec = e_half * e_half
        for h in range(GLA_HEADS):
            hs = slice(h * GLA_DK, (h + 1) * GLA_DK)
            s = lax.dot_general(qe[:, hs], ke[:, hs], nt_dims, preferred_element_type=F32)
            a = jnp.where(causal, s, 0.0).astype(BF16)
            oacc_scr[rows, hs] = jnp.dot(a, v[:, hs], preferred_element_type=F32)
        for ci in range(per_tile):
            cs = slice(ci * C, (ci + 1) * C)
            full = lax.dot_general(v[cs, :], ku[cs, :], tn_dims, preferred_element_type=F32)
            ut = full[0:C, :]
            for h in range(1, GLA_HEADS):
                ut = jnp.where(lane_head == h, full[h * C:(h + 1) * C, :], ut)
            u_scr[i * per_tile + ci] = ut
            dec_scr[i * per_tile + ci] = dec[ci * C:ci * C + 1, :]
        return carry

    lax.fori_loop(0, n_tiles, phase1, 0, unroll=2)

    def phase2(i, st):
        if reverse:
            ci = jnp.where(i < n_ctx_chunks, n_ctx_chunks - 1 - i, n_chunks - 1 + n_ctx_chunks - i)
        else:
            ci = i
        sin_scr[ci] = st.astype(BF16)
        return st * dec_scr[ci] + u_scr[ci]

    lax.fori_loop(0, n_chunks, phase2, jnp.zeros((C, GLA_W), F32))

    def phase3(i, carry):
        rows = pl.ds(pl.multiple_of(i * SEQ_TILE, SEQ_TILE), SEQ_TILE)
        qs = qs_scr[rows, :]
        parts = []
        for ci in range(per_tile):
            st = sin_scr[i * per_tile + ci]
            st_heads = jnp.where(same_head, jnp.concatenate([st] * GLA_HEADS, axis=0), jnp.zeros((), BF16))
            parts.append(lax.dot_general(qs[ci * C:(ci + 1) * C, :], st_heads, nt_dims, preferred_element_type=F32))
        o = oacc_scr[rows, :] + jnp.concatenate(parts, axis=0)
        ms = jnp.dot((o * o).astype(BF16), head_mean, preferred_element_type=F32)
        y = o * lax.rsqrt(ms + EPS) * ng_ref[...]
        o_ref[rows, :] = (y * _silu(gate_ref[rows, :].astype(F32))).astype(BF16)
        return carry

    lax.fori_loop(0, n_tiles, phase3, 0, unroll=2)


def _gla(proj, w2pad, bias, ng, ctx_len, reverse):
    B, T, _ = proj.shape
    assert T % SEQ_TILE == 0 and ctx_len % GLA_CHUNK == 0
    n_chunks = T // GLA_CHUNK
    col = lambda c0: pl.BlockSpec((None, T, GLA_W), lambda b: (b, 0, c0 // GLA_W))
    gate_col = COL_GGB if reverse else COL_GGF
    kern = functools.partial(_gla_kernel, reverse=reverse, n_ctx_chunks=ctx_len // GLA_CHUNK)
    return pl.pallas_call(
        kern,
        grid=(B,),
        in_specs=[col(COL_GQ), col(COL_GK), col(COL_GV), col(gate_col),
                  pl.BlockSpec((None, T, LANES), lambda b: (b, 0, COL_LR // LANES)),
                  pl.BlockSpec((LANES, GLA_W), lambda b: (0, 0)),
                  pl.BlockSpec((1, GLA_W), lambda b: (0, 0)),
                  pl.BlockSpec((1, GLA_W), lambda b: (0, 0))],
        out_specs=pl.BlockSpec((None, T, GLA_W), lambda b: (b, 0, 0)),
        out_shape=jax.ShapeDtypeStruct((B, T, GLA_W), BF16),
        scratch_shapes=[pltpu.VMEM((T, GLA_W), F32),
                        pltpu.VMEM((T, GLA_W), BF16),
                        pltpu.VMEM((n_chunks, GLA_CHUNK, GLA_W), F32),
                        pltpu.VMEM((n_chunks, 1, GLA_W), F32),
                        pltpu.VMEM((n_chunks, GLA_CHUNK, GLA_W), BF16)],
        compiler_params=_params(("parallel",)),
    )(proj, proj, proj, proj, proj, w2pad, bias, ng)


def _outproj_kernel(*refs, tm, ctx_len, moe):
    if moe:
        (x_ref, att_ref, gf_ref, gb_ref, cv_ref, w_ref, g1_ref, cg1_ref, ng_ref, sc_ref, csc_ref, sh_ref, csh_ref,
         router_ref, xo_ref, hp_ref, route_ref) = refs
    else:
        (x_ref, att_ref, gf_ref, gb_ref, cv_ref, w_ref, g1_ref, cg1_ref, ng_ref, sc_ref, csc_ref, sh_ref, csh_ref,
         xo_ref, h_ref) = refs
    is_ctx = _row_is_ctx(pl.program_id(1), tm, ctx_len)
    gla = (gf_ref[...].astype(F32) + gb_ref[...].astype(F32)).astype(BF16)
    y = jnp.dot(att_ref[...], w_ref[:ATT_W, :], preferred_element_type=F32)
    y += jnp.dot(gla, w_ref[ATT_W:ATT_W + GLA_W, :], preferred_element_type=F32)
    y += jnp.dot(cv_ref[...], w_ref[ATT_W + GLA_W:, :], preferred_element_type=F32)
    x = x_ref[...] + jnp.where(is_ctx, cg1_ref[...], g1_ref[...]) * y
    xo_ref[...] = x
    n = x * lax.rsqrt(jnp.mean(x * x, axis=-1, keepdims=True) + EPS) * ng_ref[...]
    scale = jnp.where(is_ctx, csc_ref[...], sc_ref[...])
    shift = jnp.where(is_ctx, csh_ref[...], sh_ref[...])
    h = n * (1.0 + scale) + shift
    hi = h.astype(BF16)
    if not moe:
        h_ref[...] = hi
    else:
        _store_packed(hp_ref, _pack_bf16_pairs(h))
        logits = jnp.dot(hi, router_ref[...], preferred_element_type=F32)
        lane = lax.broadcasted_iota(jnp.int32, logits.shape, 1)
        neg = -jnp.inf
        lg = jnp.where(lane < N_EXPERTS, logits, neg)
        m1 = jnp.max(lg, axis=-1, keepdims=True)
        i1 = jnp.min(jnp.where(lg == m1, lane, LANES), axis=-1, keepdims=True)
        lg2 = jnp.where(lane == i1, neg, lg)
        m2 = jnp.max(lg2, axis=-1, keepdims=True)
        i2 = jnp.min(jnp.where(lg2 == m2, lane, LANES), axis=-1, keepdims=True)
        e2 = jnp.exp(m2 - m1)
        gate1 = 1.0 / (1.0 + e2)
        gate2 = e2 / (1.0 + e2)
        route_ref[...] = jnp.where(lane == 0, i1.astype(F32), jnp.where(lane == 1, i2.astype(F32),
                                   jnp.where(lane == 2, gate1, jnp.where(lane == 3, gate2, 0.0))))


def _outproj(xt, att, gf, gb, cv, w_out, layer, ng, mod_l, mod_c, ctx_len, router=None):
    B, T, D = xt.shape
    tm = _pick_tile(T, 640)
    moe = router is not None
    row = lambda w: pl.BlockSpec((None, tm, w), lambda b, j: (b, j, 0))
    const = lambda shape: pl.BlockSpec(shape, lambda b, j: (0,) * len(shape))
    in_specs = [row(D), row(ATT_W), row(GLA_W), row(GLA_W), row(CONV_CH),
                pl.BlockSpec((None, D, D), lambda b, j: (layer, 0, 0)),
                *_mod_specs(2, D), const((1, D)), *_mod_specs(4, D), *_mod_specs(3, D)]
    args = [xt, att, gf, gb, cv, w_out, mod_l, mod_c, ng, mod_l, mod_c, mod_l, mod_c]
    nt = T // tm
    if moe:
        in_specs.append(pl.BlockSpec((None, D, LANES), lambda b, j: (router[1], 0, 0)))
        args.append(router[0])
        out_specs = [row(D), pl.BlockSpec((PACK_CHUNKS, tm, LANES), lambda b, j: (0, b * nt + j, 0)), row(LANES)]
        out_shape = [jax.ShapeDtypeStruct((B, T, D), F32), jax.ShapeDtypeStruct((PACK_CHUNKS, B * T, LANES), jnp.int32),
                     jax.ShapeDtypeStruct((B, T, LANES), F32)]
    else:
        out_specs = [row(D), row(D)]
        out_shape = [jax.ShapeDtypeStruct((B, T, D), F32), jax.ShapeDtypeStruct((B, T, D), BF16)]
    return pl.pallas_call(
        functools.partial(_outproj_kernel, tm=tm, ctx_len=ctx_len, moe=moe),
        grid=(B, T // tm),
        in_specs=in_specs, out_specs=out_specs, out_shape=out_shape,
        compiler_params=_params(("parallel", "parallel")),
    )(*args)


MXU_TILE = 256
FFN_CHUNK = 3 * MXU_TILE


def _ffn_kernel(h_ref, x_ref, g2_ref, cg2_ref, wgu_ref, wd_ref, o_ref, *, tm, ctx_len):
    F = wd_ref.shape[0]
    h = h_ref[...]
    y = None
    for c0 in range(0, F, FFN_CHUNK):
        c1 = min(c0 + FFN_CHUNK, F)
        g = jnp.dot(h, wgu_ref[:, c0:c1], preferred_element_type=F32)
        u = jnp.dot(h, wgu_ref[:, F + c0:F + c1], preferred_element_type=F32)
        part = jnp.dot((_silu(g) * u).astype(BF16), wd_ref[c0:c1, :], preferred_element_type=F32)
        y = part if y is None else y + part
    is_ctx = _row_is_ctx(pl.program_id(1), tm, ctx_len)
    o_ref[...] = x_ref[...] + jnp.where(is_ctx, cg2_ref[...], g2_ref[...]) * y


def _ffn(h2, xt, mod_l, mod_c, w_gu, w_down, layer, ctx_len):
    B, T, D = xt.shape
    F = w_down.shape[1]
    assert F % MXU_TILE == 0
    tm = _pick_tile(T, 640)
    row = lambda w: pl.BlockSpec((None, tm, w), lambda b, j: (b, j, 0))
    resident = lambda shape: pl.BlockSpec((None,) + shape, lambda b, j: (layer, 0, 0), pipeline_mode=pl.Buffered(1))
    return pl.pallas_call(
        functools.partial(_ffn_kernel, tm=tm, ctx_len=ctx_len),
        grid=(B, T // tm),
        in_specs=[row(D), row(D), *_mod_specs(5, D), resident((D, 2 * F)), resident((F, D))],
        out_specs=row(D),
        out_shape=jax.ShapeDtypeStruct((B, T, D), F32),
        compiler_params=_params(("parallel", "parallel")),
    )(h2, xt, mod_l, mod_c, w_gu, w_down)


def _sc_gather(table, idx):
    _, width = table.shape
    m = idx.shape[0]
    workers = SC_CORES * SC_SUBCORES
    ch = SC_INDEX_CHUNK
    assert width == LANES and m % (workers * ch) == 0
    n_chunks = m // (workers * ch)
    nb = max(b for b in (4, 3, 2, 1) if n_chunks % b == 0)
    mesh = plsc.VectorSubcoreMesh(core_axis_name="c", subcore_axis_name="s",
                                  num_cores=SC_CORES, num_subcores=SC_SUBCORES)

    @functools.partial(
        pl.kernel, mesh=mesh, out_type=jax.ShapeDtypeStruct((m, width), table.dtype),
        scratch_types=[pltpu.VMEM((n_chunks, ch), jnp.int32), pltpu.VMEM((nb, ch, width), table.dtype)]
        + [pltpu.SemaphoreType.DMA] * (2 * nb))
    def gather(table_hbm, idx_hbm, out_hbm, idx_v, rows_v, *sems):
        wid = lax.axis_index("s") * SC_CORES + lax.axis_index("c")
        base = wid * (n_chunks * ch)
        pltpu.sync_copy(idx_hbm.at[wid], idx_v)

        def fetch(c, b):
            return pltpu.make_async_copy(table_hbm.at[idx_v.at[c]], rows_v.at[b], sems[b])

        def flush(c, b):
            return pltpu.make_async_copy(rows_v.at[b], out_hbm.at[pl.ds(base + c * ch, ch)], sems[nb + b])

        @pl.loop(0, n_chunks, step=nb)
        def _(c0):
            for b in range(nb):
                @pl.when(c0 > 0)
                def _():
                    flush(c0 - nb + b, b).wait()
                fetch(c0 + b, b).start()
            for b in range(nb):
                fetch(c0 + b, b).wait()
                flush(c0 + b, b).start()

        for b in range(nb):
            flush(n_chunks - nb + b, b).wait()

    return gather(table, idx.reshape(workers, n_chunks, ch))


def _routing(route, tile):
    B, T, _ = route.shape
    n = B * T
    assert (2 * n) % tile == 0
    ids = route[..., :2].astype(jnp.int32).reshape(n, 2)
    e_cat = jnp.concatenate([ids[:, 0], ids[:, 1]])
    onehot = (e_cat[:, None] == jnp.arange(N_EXPERTS, dtype=jnp.int32)[None, :]).astype(jnp.int32)
    csum = jnp.cumsum(onehot, axis=0)
    pos = jnp.sum(onehot * csum, axis=1) - 1
    padded = (csum[-1] + tile - 1) // tile * tile
    ends = jnp.cumsum(padded)
    dest = ((ends - padded)[e_cat] + pos).astype(jnp.int32)
    n_rows = 2 * n + N_EXPERTS * tile
    id_bits = (2 * n - 1).bit_length()
    assert N_EXPERTS << id_bits < 2 ** 31
    order = jnp.sort((e_cat << id_bits) | jnp.arange(2 * n, dtype=jnp.int32)) & ((1 << id_bits) - 1)
    max_shift = N_EXPERTS * tile
    base = jnp.concatenate([jnp.zeros((max_shift,), jnp.int32), order, jnp.zeros((max_shift,), jnp.int32)])
    shift = (ends - padded) - (jnp.cumsum(csum[-1]) - csum[-1])
    row = jnp.arange(n_rows, dtype=jnp.int32)
    row_src = jnp.zeros((n_rows,), jnp.int32)
    for e in range(N_EXPERTS):
        shifted = lax.dynamic_slice(base, (max_shift - shift[e],), (n_rows,))
        row_src = jnp.where(row >= (ends - padded)[e], shifted, row_src)
    row_token = row_src % n
    tile_start = jnp.arange(n_rows // tile, dtype=jnp.int32) * tile
    tile_expert = jnp.minimum(jnp.searchsorted(ends, tile_start, side="right"), N_EXPERTS - 1).astype(jnp.int32)
    tile_valid = (tile_start < ends[-1]).astype(jnp.int32)
    return dest, row_token, tile_expert, tile_valid


def _expert_kernel(te_ref, tv_ref, xs_ref, wgu_ref, wd_ref, ys_ref, wgu_scr):
    i = pl.program_id(0)
    F = wd_ref.shape[0]

    @pl.when((i == 0) | (te_ref[i] != te_ref[jnp.maximum(i - 1, 0)]))
    def _new_expert():
        wgu_scr[...] = wgu_ref[...].astype(BF16)

    @pl.when(tv_ref[i] > 0)
    def _compute():
        lo, hi = _unpack_bf16_pairs(_load_packed(xs_ref))
        gu = (jnp.dot(lo.astype(BF16), wgu_scr[:PACK_W, :], preferred_element_type=F32)
              + jnp.dot(hi.astype(BF16), wgu_scr[PACK_W:, :], preferred_element_type=F32))
        act = (_silu(gu[:, :F]) * gu[:, F:]).astype(BF16)
        y = jnp.dot(act, wd_ref[...], preferred_element_type=F32)
        _store_packed(ys_ref, _pack_bf16_pairs(y))

    @pl.when(tv_ref[i] == 0)
    def _unused_tile():
        ys_ref[...] = jnp.zeros_like(ys_ref)


def _experts(xs, tile_expert, tile_valid, w_gu, w_down, layer):
    _, n_rows, _ = xs.shape
    _, _, F, D = w_down.shape
    tile = EXPERT_TILE
    rows = pl.BlockSpec((PACK_CHUNKS, tile, LANES), lambda i, te, tv: (0, i, 0))
    return pl.pallas_call(
        _expert_kernel,
        grid_spec=pltpu.PrefetchScalarGridSpec(
            num_scalar_prefetch=2,
            grid=(n_rows // tile,),
            in_specs=[rows,
                      pl.BlockSpec((None, None, D, 2 * F), lambda i, te, tv: (layer, te[i], 0, 0)),
                      pl.BlockSpec((None, None, F, D), lambda i, te, tv: (layer, te[i], 0, 0))],
            out_specs=rows,
            scratch_shapes=[pltpu.VMEM((D, 2 * F), BF16)]),
        out_shape=jax.ShapeDtypeStruct(xs.shape, jnp.int32),
        compiler_params=_params(("arbitrary",)),
    )(tile_expert, tile_valid, xs, w_gu, w_down)


def _combine_kernel(*refs, tm, ctx_len, skip, final):
    if final:
        x_ref, g2_ref, cg2_ref, route_ref, ya_ref, yb_ref, fg_ref, o_ref = refs
    else:
        x_ref, g2_ref, cg2_ref, route_ref, ya_ref, yb_ref, o_ref = refs
    is_ctx = _row_is_ctx(pl.program_id(1) + skip, tm, ctx_len)
    route = route_ref[...]
    lane = lax.broadcasted_iota(jnp.int32, route.shape, 1)
    gate_a = jnp.sum(jnp.where(lane == 2, route, 0.0), axis=-1, keepdims=True)
    gate_b = jnp.sum(jnp.where(lane == 3, route, 0.0), axis=-1, keepdims=True)
    unpack = lambda ref: jnp.concatenate(_unpack_bf16_pairs(_load_packed(ref)), axis=1)
    f = gate_a * unpack(ya_ref) + gate_b * unpack(yb_ref)
    x = x_ref[...] + jnp.where(is_ctx, cg2_ref[...], g2_ref[...]) * f
    if final:
        x = x * lax.rsqrt(jnp.mean(x * x, axis=-1, keepdims=True) + EPS) * fg_ref[...]
    o_ref[...] = x


def _combine(xt, mod_l, mod_c, route, yg, ctx_len, final_g=None):
    B, T, D = xt.shape
    final = final_g is not None
    if final:
        tm = SEQ_TILE
        assert ctx_len % tm == 0 and T % tm == 0
        skip = ctx_len // tm
    else:
        tm = _pick_tile(T, 640)
        skip = 0
    nt = T // tm
    row = lambda w: pl.BlockSpec((None, tm, w), lambda b, j: (b, j + skip, 0))
    slot = lambda s: pl.BlockSpec((PACK_CHUNKS, None, tm, LANES), lambda b, j: (0, s, b * nt + j + skip, 0))
    in_specs = [row(D), *_mod_specs(5, D), row(LANES), slot(0), slot(1)]
    args = [xt, mod_l, mod_c, route, yg, yg]
    if final:
        in_specs.append(pl.BlockSpec((1, D), lambda b, j: (0, 0)))
        args.append(final_g)
    return pl.pallas_call(
        functools.partial(_combine_kernel, tm=tm, ctx_len=ctx_len, skip=skip, final=final),
        grid=(B, nt - skip),
        in_specs=in_specs,
        out_specs=pl.BlockSpec((None, tm, D), lambda b, j: (b, j, 0)),
        out_shape=jax.ShapeDtypeStruct((B, T - skip * tm, D), F32),
        compiler_params=_params(("parallel", "parallel")),
    )(*args)


def _moe(hp, route, xt, mod_l, mod_c, w_gu, w_down, layer, ctx_len, final_g=None):
    B, T, _ = xt.shape
    n = B * T
    dest, row_token, tile_expert, tile_valid = _routing(route, EXPERT_TILE)
    n_rows = row_token.shape[0]
    chunk = jnp.arange(PACK_CHUNKS, dtype=jnp.int32)[:, None]
    xs = _sc_gather(hp.reshape(PACK_CHUNKS * n, LANES), (chunk * n + row_token[None, :]).reshape(-1))
    ys = _experts(xs.reshape(PACK_CHUNKS, n_rows, LANES), tile_expert, tile_valid, w_gu, w_down, layer)
    yg = _sc_gather(ys.reshape(PACK_CHUNKS * n_rows, LANES), (chunk * n_rows + dest[None, :]).reshape(-1))
    return _combine(xt, mod_l, mod_c, route, yg.reshape(PACK_CHUNKS, 2, n, LANES), ctx_len, final_g)


def _final_norm_kernel(x_ref, g_ref, o_ref):
    x = x_ref[...]
    o_ref[...] = x * lax.rsqrt(jnp.mean(x * x, axis=-1, keepdims=True) + EPS) * g_ref[...]


def _final_norm(xt, g, ctx_len):
    B, T, D = xt.shape
    S = T - ctx_len
    tm = SEQ_TILE
    assert ctx_len % tm == 0 and S % tm == 0
    skip = ctx_len // tm
    return pl.pallas_call(
        _final_norm_kernel,
        grid=(B, S // tm),
        in_specs=[pl.BlockSpec((None, tm, D), lambda b, j: (b, j + skip, 0)),
                  pl.BlockSpec((1, D), lambda b, j: (0, 0))],
        out_specs=pl.BlockSpec((None, tm, D), lambda b, j: (b, j, 0)),
        out_shape=jax.ShapeDtypeStruct((B, S, D), F32),
        compiler_params=_params(("parallel", "parallel")),
    )(xt, g)


def _rope_tables(seq, ctx_len):
    pos = jnp.arange(seq, dtype=jnp.int32)
    nf = HEAD_DIM // 4
    inv = ROPE_THETA ** (-jnp.arange(nf, dtype=F32) / nf)
    ang = jnp.concatenate([(pos // GRID_W).astype(F32)[:, None] * inv, (pos % GRID_W).astype(F32)[:, None] * inv], axis=-1)
    cos, sin = jnp.cos(ang), jnp.sin(ang)
    reps = LANES // HEAD_DIM
    cos_t = jnp.tile(jnp.concatenate([cos, cos], axis=-1), (1, reps))
    sin_t = jnp.tile(jnp.concatenate([-sin, sin], axis=-1), (1, reps))
    cos_t = jnp.concatenate([jnp.ones((ctx_len, LANES), F32), cos_t], axis=0)
    sin_t = jnp.concatenate([jnp.zeros((ctx_len, LANES), F32), sin_t], axis=0)
    return cos_t, sin_t


def _relayout_w_in(w_in):
    o_lr = ATT_W + 2 * KV_W + 5 * GLA_W
    q_scale = HEAD_DIM ** -0.5
    gq_scale = GLA_DK ** -0.5
    parts = [w_in[..., :ATT_W] * q_scale, w_in[..., ATT_W:COL_GQ], w_in[..., COL_GQ:COL_GK] * gq_scale,
             w_in[..., COL_GK:o_lr], w_in[..., o_lr + 2 * GLA_LOWRANK:], w_in[..., o_lr:o_lr + 2 * GLA_LOWRANK],
             jnp.zeros(w_in.shape[:-1] + (LANES - 2 * GLA_LOWRANK,), w_in.dtype)]
    return jnp.concatenate(parts, axis=-1).astype(BF16)


def kernel(x, c, ctx, c_ctx, w_mod, b_mod, norm1_g, norm2_g, w_in, w_out, attn_sink, gla_w2_f, gla_b_f, gla_w2_b, gla_b_b, gla_norm_g, conv_w, ffn_w_gu, ffn_w_down, router_w, expert_w_gu, expert_w_down, final_norm_g):
    B, S, D = x.shape
    L = ctx.shape[1]
    depth = w_in.shape[0]
    assert D == D_MODEL and S % GRID_W == 0

    xt = jnp.concatenate([ctx, x], axis=1)
    rows = 16
    cc = jnp.concatenate([c, c_ctx[None, :], jnp.zeros((rows - B - 1, D), F32)], axis=0)
    mod = _adaln(cc, w_mod, b_mod)
    cos_t, sin_t = _rope_tables(S, L)

    w_in_p = _relayout_w_in(w_in)
    w_out_b = _to_bf16(w_out)
    zpad = jnp.zeros((depth, LANES - 2 * GLA_LOWRANK, GLA_W), F32)
    w2_f = jnp.concatenate([gla_w2_f, jnp.zeros_like(gla_w2_b), zpad], axis=1).astype(BF16)
    w2_b = jnp.concatenate([jnp.zeros_like(gla_w2_f), gla_w2_b, zpad], axis=1).astype(BF16)
    gla_ng = jnp.tile(gla_norm_g, (1, GLA_HEADS))
    ffn_gu_b = _to_bf16(ffn_w_gu)
    ffn_down_b = _to_bf16(ffn_w_down)
    exp_down_b = _to_bf16(expert_w_down)
    router_b = jnp.pad(router_w, ((0, 0), (0, 0), (0, LANES - N_EXPERTS))).astype(BF16)

    for l in range(depth):
        mod_l = mod[l, :B].reshape(B, 1, 6 * D)
        mod_c = mod[l, B:B + 1]
        proj = _inproj(xt, norm1_g[l][None, :], mod_l, mod_c, cos_t, sin_t, w_in_p, l, L)
        att = _attention(proj, attn_sink[l], L)
        cv = _short_conv(proj, conv_w[l], L)
        gf = _gla(proj, w2_f[l], gla_b_f[l][None, :], gla_ng[l][None, :], L, reverse=False)
        gb = _gla(proj, w2_b[l], gla_b_b[l][None, :], gla_ng[l][None, :], L, reverse=True)
        if l % 2 == 0:
            xt, h2 = _outproj(xt, att, gf, gb, cv, w_out_b, l, norm2_g[l][None, :], mod_l, mod_c, L)
            xt = _ffn(h2, xt, mod_l, mod_c, ffn_gu_b, ffn_down_b, l // 2, L)
        else:
            xt, hp, route = _outproj(xt, att, gf, gb, cv, w_out_b, l, norm2_g[l][None, :], mod_l, mod_c, L,
                                     router=(router_b, l // 2))
            xt = _moe(hp, route, xt, mod_l, mod_c, expert_w_gu, exp_down_b, l // 2, L,
                      final_g=final_norm_g[None, :] if l == depth - 1 else None)
    return xt if depth % 2 == 0 else _final_norm(xt, final_norm_g[None, :], L)
```

```python
import functools

import jax
import jax.numpy as jnp
from jax import lax
from jax.experimental import pallas as pl
from jax.experimental.pallas import tpu as pltpu
from jax.experimental.pallas import tpu_sc as plsc

F32 = jnp.float32
BF16 = jnp.bfloat16

D_MODEL = 1024
EPS = 1e-6
GRID_W = 64
ROPE_THETA = 10000.0
HEAD_DIM = 64
ATT_HEADS = 8
ATT_KV_HEADS = 2
ATT_GROUP = ATT_HEADS // ATT_KV_HEADS
ATT_W = ATT_HEADS * HEAD_DIM
KV_W = ATT_KV_HEADS * HEAD_DIM
WINDOW = 128
GLA_HEADS = 4
GLA_DK = 64
GLA_W = GLA_HEADS * GLA_DK
GLA_LOWRANK = 16
GLA_TAU = 16.0
GLA_CHUNK = 64
CONV_CH = 256
N_EXPERTS = 8

LANES = 128
SUBLANES_BF16 = 16

COL_Q = 0
COL_K = ATT_W
COL_V = COL_K + KV_W
ROPE_W = COL_V
COL_GQ = COL_V + KV_W
COL_GK = COL_GQ + GLA_W
COL_GV = COL_GK + GLA_W
COL_GGF = COL_GV + GLA_W
COL_GGB = COL_GGF + GLA_W
COL_CB = COL_GGB + GLA_W
COL_CC = COL_CB + CONV_CH
COL_CX = COL_CC + CONV_CH
COL_LR = COL_CX + CONV_CH
PROJ_W = COL_LR + LANES

SEQ_TILE = 256
VMEM_LIMIT = 56 * 1024 * 1024
CAST_BLOCK_BYTES = 4 * 1024 * 1024

PACK_W = D_MODEL // 2
PACK_CHUNKS = PACK_W // LANES
EXPERT_TILE = 512
SC_CORES = 2
SC_SUBCORES = 16
SC_INDEX_CHUNK = 128


def _silu(v):
    return v / (1.0 + jnp.exp(-v))


def _pick_tile(total, cap):
    best = None
    for t in range(SUBLANES_BF16, cap + 1, SUBLANES_BF16):
        if total % t == 0:
            best = t
    assert best is not None
    return best


def _params(sem):
    return pltpu.CompilerParams(dimension_semantics=sem, vmem_limit_bytes=VMEM_LIMIT)


def _pack_bf16_pairs(v):
    bits = lambda t: lax.bitcast_convert_type(t.astype(BF16).astype(F32), jnp.int32)
    return ((bits(v[:, :PACK_W]) >> 16) & 0xFFFF) | (bits(v[:, PACK_W:]) & -65536)


def _unpack_bf16_pairs(w):
    return lax.bitcast_convert_type(w << 16, F32), lax.bitcast_convert_type(w & -65536, F32)


def _cast_kernel(x_ref, o_ref):
    o_ref[...] = x_ref[...].astype(BF16)


def _to_bf16(w):
    cols = w.shape[-1]
    rows = w.size // cols
    tr = _pick_tile(rows, max(SUBLANES_BF16, CAST_BLOCK_BYTES // (4 * cols)))
    out = pl.pallas_call(
        _cast_kernel,
        grid=(rows // tr,),
        in_specs=[pl.BlockSpec((tr, cols), lambda i: (i, 0))],
        out_specs=pl.BlockSpec((tr, cols), lambda i: (i, 0)),
        out_shape=jax.ShapeDtypeStruct((rows, cols), BF16),
        compiler_params=_params(("parallel",)),
    )(w.reshape(rows, cols))
    return out.reshape(w.shape)


def _load_packed(ref):
    return jnp.concatenate([ref[k] for k in range(PACK_CHUNKS)], axis=1)


def _store_packed(ref, packed):
    for k in range(PACK_CHUNKS):
        ref[k] = packed[:, k * LANES:(k + 1) * LANES]


def _adaln_kernel(c_ref, w_ref, b_ref, o_ref):
    s = _silu(c_ref[...]).astype(BF16)
    o_ref[...] = jnp.dot(s, w_ref[...].astype(BF16), preferred_element_type=F32) + b_ref[...]


def _adaln(cc, w_mod, b_mod):
    depth, d, n = w_mod.shape
    tn = 1024
    rows = cc.shape[0]
    return pl.pallas_call(
        _adaln_kernel,
        grid=(depth, n // tn),
        in_specs=[pl.BlockSpec((rows, d), lambda l, j: (0, 0)),
                  pl.BlockSpec((None, d, tn), lambda l, j: (l, 0, j)),
                  pl.BlockSpec((None, 1, tn), lambda l, j: (l, 0, j))],
        out_specs=pl.BlockSpec((None, rows, tn), lambda l, j: (l, 0, j)),
        out_shape=jax.ShapeDtypeStruct((depth, rows, n), F32),
        compiler_params=_params(("parallel", "parallel")),
    )(cc, w_mod, b_mod.reshape(depth, 1, n))


def _mod_specs(k, d):
    return [pl.BlockSpec((None, 1, d), lambda b, j, *_: (b, 0, k)),
            pl.BlockSpec((1, d), lambda b, j, *_: (0, k))]


def _row_is_ctx(j, tm, ctx_len):
    return (j * tm + lax.broadcasted_iota(jnp.int32, (tm, 1), 0)) < ctx_len


def _inproj_kernel(x_ref, g_ref, sc_ref, csc_ref, sh_ref, csh_ref, cos_ref, sin_ref, w_ref, o_ref, *, tm, ctx_len):
    x = x_ref[...]
    n = x * lax.rsqrt(jnp.mean(x * x, axis=-1, keepdims=True) + EPS) * g_ref[...]
    is_ctx = _row_is_ctx(pl.program_id(1), tm, ctx_len)
    scale = jnp.where(is_ctx, csc_ref[...], sc_ref[...])
    shift = jnp.where(is_ctx, csh_ref[...], sh_ref[...])
    h = (n * (1.0 + scale) + shift).astype(BF16)
    qk = jnp.dot(h, w_ref[:, :ROPE_W], preferred_element_type=F32)
    lower_half = (lax.broadcasted_iota(jnp.int32, (tm, LANES), 1) % HEAD_DIM) < (HEAD_DIM // 2)
    cos = cos_ref[...]
    sin = sin_ref[...]
    for i in range(ROPE_W // LANES):
        t = qk[:, i * LANES:(i + 1) * LANES]
        partner = jnp.where(lower_half, pltpu.roll(t, LANES - HEAD_DIM // 2, axis=1), pltpu.roll(t, HEAD_DIM // 2, axis=1))
        o_ref[:, i * LANES:(i + 1) * LANES] = (t * cos + partner * sin).astype(BF16)
    o_ref[:, ROPE_W:] = jnp.dot(h, w_ref[:, ROPE_W:], preferred_element_type=F32).astype(BF16)


def _inproj(xt, g, mod_l, mod_c, cos_t, sin_t, w, layer, ctx_len):
    B, T, D = xt.shape
    tm = _pick_tile(T, 640)
    kern = functools.partial(_inproj_kernel, tm=tm, ctx_len=ctx_len)
    return pl.pallas_call(
        kern,
        grid=(B, T // tm),
        in_specs=[pl.BlockSpec((None, tm, D), lambda b, j: (b, j, 0)),
                  pl.BlockSpec((1, D), lambda b, j: (0, 0)),
                  *_mod_specs(1, D), *_mod_specs(0, D),
                  pl.BlockSpec((tm, LANES), lambda b, j: (j, 0)),
                  pl.BlockSpec((tm, LANES), lambda b, j: (j, 0)),
                  pl.BlockSpec((None, D, PROJ_W), lambda b, j: (layer, 0, 0))],
        out_specs=pl.BlockSpec((None, tm, PROJ_W), lambda b, j: (b, j, 0)),
        out_shape=jax.ShapeDtypeStruct((B, T, PROJ_W), BF16),
        compiler_params=_params(("parallel", "parallel")),
    )(xt, g, mod_l, mod_c, mod_l, mod_c, cos_t, sin_t, w)


def _attn_kernel(sink_ref, q_ref, kc_ref, vc_ref, km_ref, vm_ref, kp_ref, vp_ref, kn_ref, vn_ref, o_ref, *, n_tiles):
    j = pl.program_id(1)
    nt_dims = (((1,), (1,)), ((), ()))

    def group_attention(rows, n_rows, g, k_ctx, v_ctx, k_win=None, v_win=None, mask_win=None):
        heads = [g * ATT_GROUP + hh for hh in range(ATT_GROUP)]
        q = jnp.concatenate([q_ref[rows, h * HEAD_DIM:(h + 1) * HEAD_DIM] for h in heads], axis=0)
        head_row = lax.broadcasted_iota(jnp.int32, (ATT_GROUP * n_rows, 1), 0) // n_rows
        sink = jnp.full((ATT_GROUP * n_rows, 1), sink_ref[heads[0]], F32)
        for hh in range(1, ATT_GROUP):
            sink = jnp.where(head_row == hh, sink_ref[heads[hh]], sink)
        s_c = lax.dot_general(q, k_ctx, nt_dims, preferred_element_type=F32)
        m = jnp.maximum(jnp.max(s_c, axis=-1, keepdims=True), sink)
        if k_win is not None:
            s_w = jnp.where(mask_win, lax.dot_general(q, k_win, nt_dims, preferred_element_type=F32), -1e30)
            m = jnp.maximum(m, jnp.max(s_w, axis=-1, keepdims=True))
        p_c = jnp.exp(s_c - m)
        den = jnp.sum(p_c, axis=-1, keepdims=True) + jnp.exp(sink - m)
        o = jnp.dot(p_c.astype(BF16), v_ctx, preferred_element_type=F32)
        if k_win is not None:
            p_w = jnp.exp(s_w - m)
            den += jnp.sum(p_w, axis=-1, keepdims=True)
            o += jnp.dot(p_w.astype(BF16), v_win, preferred_element_type=F32)
        o = o / den
        for hh, h in enumerate(heads):
            o_ref[rows, h * HEAD_DIM:(h + 1) * HEAD_DIM] = o[hh * n_rows:(hh + 1) * n_rows].astype(BF16)

    @pl.when(j == 0)
    def _context_queries():
        for g in range(ATT_KV_HEADS):
            gs = slice(g * HEAD_DIM, (g + 1) * HEAD_DIM)
            group_attention(slice(0, SEQ_TILE), SEQ_TILE, g, kc_ref[:, gs], vc_ref[:, gs])

    @pl.when(j > 0)
    def _latent_queries():
        r = lax.broadcasted_iota(jnp.int32, (ATT_GROUP * WINDOW, 3 * WINDOW), 0) % WINDOW
        w = lax.broadcasted_iota(jnp.int32, (ATT_GROUP * WINDOW, 3 * WINDOW), 1)
        band = (w >= r) & (w <= r + 2 * WINDOW)
        for sub in range(SEQ_TILE // WINDOW):
            rows = slice(sub * WINDOW, (sub + 1) * WINDOW)
            if sub == 0:
                kprev, vprev = kp_ref[...], vp_ref[...]
                kmid, vmid = km_ref[:WINDOW], vm_ref[:WINDOW]
                knext, vnext = km_ref[WINDOW:], vm_ref[WINDOW:]
                w_lo = jnp.where(j > 1, 0, WINDOW)
                w_hi = 3 * WINDOW
            else:
                kprev, vprev = km_ref[:WINDOW], vm_ref[:WINDOW]
                kmid, vmid = km_ref[WINDOW:], vm_ref[WINDOW:]
                knext, vnext = kn_ref[...], vn_ref[...]
                w_lo = 0
                w_hi = jnp.where(j < n_tiles - 1, 3 * WINDOW, 2 * WINDOW)
            mask = band & (w >= w_lo) & (w < w_hi)
            for g in range(ATT_KV_HEADS):
                gs = slice(g * HEAD_DIM, (g + 1) * HEAD_DIM)
                k_win = jnp.concatenate([kprev[:, gs], kmid[:, gs], knext[:, gs]], axis=0)
                v_win = jnp.concatenate([vprev[:, gs], vmid[:, gs], vnext[:, gs]], axis=0)
                group_attention(rows, WINDOW, g, kc_ref[:, gs], vc_ref[:, gs], k_win, v_win, mask)


def _attention(proj, sink, ctx_len):
    B, T, _ = proj.shape
    assert ctx_len == SEQ_TILE and T % SEQ_TILE == 0
    n_tiles = T // SEQ_TILE
    n_win = T // WINDOW
    kcol, vcol = COL_K // KV_W, COL_V // KV_W
    per_tile = SEQ_TILE // WINDOW
    tile = lambda col: pl.BlockSpec((None, SEQ_TILE, KV_W), lambda b, j: (b, j, col))
    ctx = lambda col: pl.BlockSpec((None, SEQ_TILE, KV_W), lambda b, j: (b, 0, col))
    prev = lambda col: pl.BlockSpec((None, WINDOW, KV_W), lambda b, j: (b, jnp.maximum(j * per_tile - 1, 0), col))
    nxt = lambda col: pl.BlockSpec((None, WINDOW, KV_W), lambda b, j: (b, jnp.minimum((j + 1) * per_tile, n_win - 1), col))
    return pl.pallas_call(
        functools.partial(_attn_kernel, n_tiles=n_tiles),
        grid=(B, n_tiles),
        in_specs=[pl.BlockSpec(memory_space=pltpu.SMEM),
                  pl.BlockSpec((None, SEQ_TILE, ATT_W), lambda b, j: (b, j, 0)),
                  ctx(kcol), ctx(vcol), tile(kcol), tile(vcol), prev(kcol), prev(vcol), nxt(kcol), nxt(vcol)],
        out_specs=pl.BlockSpec((None, SEQ_TILE, ATT_W), lambda b, j: (b, j, 0)),
        out_shape=jax.ShapeDtypeStruct((B, T, ATT_W), BF16),
        compiler_params=_params(("parallel", "parallel")),
    )(sink, *([proj] * 9))


def _conv_kernel(b_ref, c_ref, x_ref, w_ref, o_ref, u_scr, *, ctx_len):
    T = c_ref.shape[0]
    pad = 8
    u = c_ref[...].astype(F32) * x_ref[...].astype(F32)
    zero_row = jnp.zeros((1, CONV_CH), F32)
    u_scr[pad - 1:pad, :] = zero_row
    u_scr[pad:pad + T, :] = u
    u_scr[pad + T:pad + T + 1, :] = zero_row
    row = lax.broadcasted_iota(jnp.int32, (T, 1), 0)
    u_prev = jnp.where(row == ctx_len, 0.0, u_scr[pad - 1:pad - 1 + T, :])
    u_next = jnp.where(row == ctx_len - 1, 0.0, u_scr[pad + 1:pad + 1 + T, :])
    y = w_ref[0:1, :] * u_prev + w_ref[1:2, :] * u + w_ref[2:3, :] * u_next
    o_ref[...] = (b_ref[...].astype(F32) * y).astype(BF16)


def _short_conv(proj, conv_w, ctx_len):
    B, T, _ = proj.shape
    col = lambda c0: pl.BlockSpec((None, T, CONV_CH), lambda b: (b, 0, c0 // CONV_CH))
    return pl.pallas_call(
        functools.partial(_conv_kernel, ctx_len=ctx_len),
        grid=(B,),
        in_specs=[col(COL_CB), col(COL_CC), col(COL_CX), pl.BlockSpec((3, CONV_CH), lambda b: (0, 0))],
        out_specs=pl.BlockSpec((None, T, CONV_CH), lambda b: (b, 0, 0)),
        out_shape=jax.ShapeDtypeStruct((B, T, CONV_CH), BF16),
        scratch_shapes=[pltpu.VMEM((T + 16, CONV_CH), F32)],
        compiler_params=_params(("parallel",)),
    )(proj, proj, proj, conv_w)


def _split2(v):
    hi = v.astype(BF16)
    lo = (v - hi.astype(F32)).astype(BF16)
    return hi, lo


def _gla_kernel(q_ref, k_ref, v_ref, gate_ref, lr_ref, w2_ref, bias_ref, ng_ref, o_ref,
                oacc_scr, qs_scr, u_scr, dec_scr, sin_scr, *, reverse, n_ctx_chunks):
    T = q_ref.shape[0]
    C = GLA_CHUNK
    n_tiles = T // SEQ_TILE
    n_chunks = T // C
    per_tile = SEQ_TILE // C
    nt_dims = (((1,), (1,)), ((), ()))
    tn_dims = (((0,), (0,)), ((), ()))

    r = lax.broadcasted_iota(jnp.int32, (SEQ_TILE, SEQ_TILE), 0)
    c = lax.broadcasted_iota(jnp.int32, (SEQ_TILE, SEQ_TILE), 1)
    same_chunk = (r // C) == (c // C)
    causal = same_chunk & ((c >= r) if reverse else (c <= r))
    cum_mat = jnp.where(causal, 1.0, 0.0).astype(BF16)
    same_head = (r // GLA_DK) == (c // GLA_DK)
    head_mean = jnp.where(same_head, 1.0 / GLA_DK, 0.0).astype(BF16)
    lane_head = lax.broadcasted_iota(jnp.int32, (C, GLA_W), 1) // GLA_DK

    def phase1(i, carry):
        rows = pl.ds(pl.multiple_of(i * SEQ_TILE, SEQ_TILE), SEQ_TILE)
        z = jnp.dot(lr_ref[rows, :], w2_ref[...], preferred_element_type=F32) + bias_ref[...]
        la = (jnp.minimum(z, 0.0) - jnp.log(1.0 + jnp.exp(-jnp.abs(z)))) / GLA_TAU
        hi, lo = _split2(la)
        b = jnp.dot(cum_mat, hi, preferred_element_type=F32) + jnp.dot(cum_mat, lo, preferred_element_type=F32)
        b3 = b.reshape(per_tile, C, GLA_W)
        total = b3[:, 0:1, :] if reverse else b3[:, C - 1:C, :]
        b_last = jnp.broadcast_to(total, (per_tile, C, GLA_W)).reshape(SEQ_TILE, GLA_W)
        b_ref = 0.5 * b_last
        q = q_ref[rows, :].astype(F32)
        k = k_ref[rows, :].astype(F32)
        v = v_ref[rows, :]
        e_fwd = jnp.exp(b - b_ref)
        e_bwd = jnp.exp(b_ref - b)
        e_half = jnp.exp(b_ref)
        qe = (q * e_fwd).astype(BF16)
        ke = (k * e_bwd).astype(BF16)
        ku = (k * (e_bwd * e_half)).astype(BF16)
        qs_scr[rows, :] = (q * (e_fwd * e_half)).astype(BF16)
        dec = e_half * e_half
        for h in range(GLA_HEADS):
            hs = slice(h * GLA_DK, (h + 1) * GLA_DK)
            s = lax.dot_general(qe[:, hs], ke[:, hs], nt_dims, preferred_element_type=F32)
            a = jnp.where(causal, s, 0.0).astype(BF16)
            oacc_scr[rows, hs] = jnp.dot(a, v[:, hs], preferred_element_type=F32)
        for ci in range(per_tile):
            cs = slice(ci * C, (ci + 1) * C)
            full = lax.dot_general(v[cs, :], ku[cs, :], tn_dims, preferred_element_type=F32)
            ut = full[0:C, :]
            for h in range(1, GLA_HEADS):
                ut = jnp.where(lane_head == h, full[h * C:(h + 1) * C, :], ut)
            u_scr[i * per_tile + ci] = ut
            dec_scr[i * per_tile + ci] = dec[ci * C:ci * C + 1, :]
        return carry

    lax.fori_loop(0, n_tiles, phase1, 0, unroll=2)

    def phase2(i, st):
        if reverse:
            ci = jnp.where(i < n_ctx_chunks, n_ctx_chunks - 1 - i, n_chunks - 1 + n_ctx_chunks - i)
        else:
            ci = i
        sin_scr[ci] = st.astype(BF16)
        return st * dec_scr[ci] + u_scr[ci]

    lax.fori_loop(0, n_chunks, phase2, jnp.zeros((C, GLA_W), F32))

    def phase3(i, carry):
        rows = pl.ds(pl.multiple_of(i * SEQ_TILE, SEQ_TILE), SEQ_TILE)
        qs = qs_scr[rows, :]
        parts = []
        for ci in range(per_tile):
            st = sin_scr[i * per_tile + ci]
            st_heads = jnp.where(same_head, jnp.concatenate([st] * GLA_HEADS, axis=0), jnp.zeros((), BF16))
            parts.append(lax.dot_general(qs[ci * C:(ci + 1) * C, :], st_heads, nt_dims, preferred_element_type=F32))
        o = oacc_scr[rows, :] + jnp.concatenate(parts, axis=0)
        ms = jnp.dot((o * o).astype(BF16), head_mean, preferred_element_type=F32)
        y = o * lax.rsqrt(ms + EPS) * ng_ref[...]
        o_ref[rows, :] = (y * _silu(gate_ref[rows, :].astype(F32))).astype(BF16)
        return carry

    lax.fori_loop(0, n_tiles, phase3, 0, unroll=2)


def _gla(proj, w2pad, bias, ng, ctx_len, reverse):
    B, T, _ = proj.shape
    assert T % SEQ_TILE == 0 and ctx_len % GLA_CHUNK == 0
    n_chunks = T // GLA_CHUNK
    col = lambda c0: pl.BlockSpec((None, T, GLA_W), lambda b: (b, 0, c0 // GLA_W))
    gate_col = COL_GGB if reverse else COL_GGF
    kern = functools.partial(_gla_kernel, reverse=reverse, n_ctx_chunks=ctx_len // GLA_CHUNK)
    return pl.pallas_call(
        kern,
        grid=(B,),
        in_specs=[col(COL_GQ), col(COL_GK), col(COL_GV), col(gate_col),
                  pl.BlockSpec((None, T, LANES), lambda b: (b, 0, COL_LR // LANES)),
                  pl.BlockSpec((LANES, GLA_W), lambda b: (0, 0)),
                  pl.BlockSpec((1, GLA_W), lambda b: (0, 0)),
                  pl.BlockSpec((1, GLA_W), lambda b: (0, 0))],
        out_specs=pl.BlockSpec((None, T, GLA_W), lambda b: (b, 0, 0)),
        out_shape=jax.ShapeDtypeStruct((B, T, GLA_W), BF16),
        scratch_shapes=[pltpu.VMEM((T, GLA_W), F32),
                        pltpu.VMEM((T, GLA_W), BF16),
                        pltpu.VMEM((n_chunks, GLA_CHUNK, GLA_W), F32),
                        pltpu.VMEM((n_chunks, 1, GLA_W), F32),
                        pltpu.VMEM((n_chunks, GLA_CHUNK, GLA_W), BF16)],
        compiler_params=_params(("parallel",)),
    )(proj, proj, proj, proj, proj, w2pad, bias, ng)


def _outproj_kernel(*refs, tm, ctx_len, moe):
    if moe:
        (x_ref, att_ref, gf_ref, gb_ref, cv_ref, w_ref, g1_ref, cg1_ref, ng_ref, sc_ref, csc_ref, sh_ref, csh_ref,
         router_ref, xo_ref, hp_ref, route_ref) = refs
    else:
        (x_ref, att_ref, gf_ref, gb_ref, cv_ref, w_ref, g1_ref, cg1_ref, ng_ref, sc_ref, csc_ref, sh_ref, csh_ref,
         xo_ref, h_ref) = refs
    is_ctx = _row_is_ctx(pl.program_id(1), tm, ctx_len)
    gla = (gf_ref[...].astype(F32) + gb_ref[...].astype(F32)).astype(BF16)
    y = jnp.dot(att_ref[...], w_ref[:ATT_W, :], preferred_element_type=F32)
    y += jnp.dot(gla, w_ref[ATT_W:ATT_W + GLA_W, :], preferred_element_type=F32)
    y += jnp.dot(cv_ref[...], w_ref[ATT_W + GLA_W:, :], preferred_element_type=F32)
    x = x_ref[...] + jnp.where(is_ctx, cg1_ref[...], g1_ref[...]) * y
    xo_ref[...] = x
    n = x * lax.rsqrt(jnp.mean(x * x, axis=-1, keepdims=True) + EPS) * ng_ref[...]
    scale = jnp.where(is_ctx, csc_ref[...], sc_ref[...])
    shift = jnp.where(is_ctx, csh_ref[...], sh_ref[...])
    h = n * (1.0 + scale) + shift
    hi = h.astype(BF16)
    if not moe:
        h_ref[...] = hi
    else:
        _store_packed(hp_ref, _pack_bf16_pairs(h))
        logits = jnp.dot(hi, router_ref[...], preferred_element_type=F32)
        lane = lax.broadcasted_iota(jnp.int32, logits.shape, 1)
        neg = -jnp.inf
        lg = jnp.where(lane < N_EXPERTS, logits, neg)
        m1 = jnp.max(lg, axis=-1, keepdims=True)
        i1 = jnp.min(jnp.where(lg == m1, lane, LANES), axis=-1, keepdims=True)
        lg2 = jnp.where(lane == i1, neg, lg)
        m2 = jnp.max(lg2, axis=-1, keepdims=True)
        i2 = jnp.min(jnp.where(lg2 == m2, lane, LANES), axis=-1, keepdims=True)
        e2 = jnp.exp(m2 - m1)
        gate1 = 1.0 / (1.0 + e2)
        gate2 = e2 / (1.0 + e2)
        route_ref[...] = jnp.where(lane == 0, i1.astype(F32), jnp.where(lane == 1, i2.astype(F32),
                                   jnp.where(lane == 2, gate1, jnp.where(lane == 3, gate2, 0.0))))


def _outproj(xt, att, gf, gb, cv, w_out, layer, ng, mod_l, mod_c, ctx_len, router=None):
    B, T, D = xt.shape
    tm = _pick_tile(T, 640)
    moe = router is not None
    row = lambda w: pl.BlockSpec((None, tm, w), lambda b, j: (b, j, 0))
    const = lambda shape: pl.BlockSpec(shape, lambda b, j: (0,) * len(shape))
    in_specs = [row(D), row(ATT_W), row(GLA_W), row(GLA_W), row(CONV_CH),
                pl.BlockSpec((None, D, D), lambda b, j: (layer, 0, 0)),
                *_mod_specs(2, D), const((1, D)), *_mod_specs(4, D), *_mod_specs(3, D)]
    args = [xt, att, gf, gb, cv, w_out, mod_l, mod_c, ng, mod_l, mod_c, mod_l, mod_c]
    nt = T // tm
    if moe:
        in_specs.append(pl.BlockSpec((None, D, LANES), lambda b, j: (router[1], 0, 0)))
        args.append(router[0])
        out_specs = [row(D), pl.BlockSpec((PACK_CHUNKS, tm, LANES), lambda b, j: (0, b * nt + j, 0)), row(LANES)]
        out_shape = [jax.ShapeDtypeStruct((B, T, D), F32), jax.ShapeDtypeStruct((PACK_CHUNKS, B * T, LANES), jnp.int32),
                     jax.ShapeDtypeStruct((B, T, LANES), F32)]
    else:
        out_specs = [row(D), row(D)]
        out_shape = [jax.ShapeDtypeStruct((B, T, D), F32), jax.ShapeDtypeStruct((B, T, D), BF16)]
    return pl.pallas_call(
        functools.partial(_outproj_kernel, tm=tm, ctx_len=ctx_len, moe=moe),
        grid=(B, T // tm),
        in_specs=in_specs, out_specs=out_specs, out_shape=out_shape,
        compiler_params=_params(("parallel", "parallel")),
    )(*args)


MXU_TILE = 256
FFN_CHUNK = 3 * MXU_TILE


def _ffn_kernel(h_ref, x_ref, g2_ref, cg2_ref, wgu_ref, wd_ref, o_ref, *, tm, ctx_len):
    F = wd_ref.shape[0]
    h = h_ref[...]
    y = None
    for c0 in range(0, F, FFN_CHUNK):
        c1 = min(c0 + FFN_CHUNK, F)
        g = jnp.dot(h, wgu_ref[:, c0:c1], preferred_element_type=F32)
        u = jnp.dot(h, wgu_ref[:, F + c0:F + c1], preferred_element_type=F32)
        part = jnp.dot((_silu(g) * u).astype(BF16), wd_ref[c0:c1, :], preferred_element_type=F32)
        y = part if y is None else y + part
    is_ctx = _row_is_ctx(pl.program_id(1), tm, ctx_len)
    o_ref[...] = x_ref[...] + jnp.where(is_ctx, cg2_ref[...], g2_ref[...]) * y


def _ffn(h2, xt, mod_l, mod_c, w_gu, w_down, layer, ctx_len):
    B, T, D = xt.shape
    F = w_down.shape[1]
    assert F % MXU_TILE == 0
    tm = _pick_tile(T, 640)
    row = lambda w: pl.BlockSpec((None, tm, w), lambda b, j: (b, j, 0))
    resident = lambda shape: pl.BlockSpec((None,) + shape, lambda b, j: (layer, 0, 0), pipeline_mode=pl.Buffered(1))
    return pl.pallas_call(
        functools.partial(_ffn_kernel, tm=tm, ctx_len=ctx_len),
        grid=(B, T // tm),
        in_specs=[row(D), row(D), *_mod_specs(5, D), resident((D, 2 * F)), resident((F, D))],
        out_specs=row(D),
        out_shape=jax.ShapeDtypeStruct((B, T, D), F32),
        compiler_params=_params(("parallel", "parallel")),
    )(h2, xt, mod_l, mod_c, w_gu, w_down)


def _sc_gather(table, idx):
    _, width = table.shape
    m = idx.shape[0]
    workers = SC_CORES * SC_SUBCORES
    ch = SC_INDEX_CHUNK
    assert width == LANES and m % (workers * ch) == 0
    n_chunks = m // (workers * ch)
    nb = max(b for b in (4, 3, 2, 1) if n_chunks % b == 0)
    mesh = plsc.VectorSubcoreMesh(core_axis_name="c", subcore_axis_name="s",
                                  num_cores=SC_CORES, num_subcores=SC_SUBCORES)

    @functools.partial(
        pl.kernel, mesh=mesh, out_type=jax.ShapeDtypeStruct((m, width), table.dtype),
        scratch_types=[pltpu.VMEM((n_chunks, ch), jnp.int32), pltpu.VMEM((nb, ch, width), table.dtype)]
        + [pltpu.SemaphoreType.DMA] * (2 * nb))
    def gather(table_hbm, idx_hbm, out_hbm, idx_v, rows_v, *sems):
        wid = lax.axis_index("s") * SC_CORES + lax.axis_index("c")
        base = wid * (n_chunks * ch)
        pltpu.sync_copy(idx_hbm.at[wid], idx_v)

        def fetch(c, b):
            return pltpu.make_async_copy(table_hbm.at[idx_v.at[c]], rows_v.at[b], sems[b])

        def flush(c, b):
            return pltpu.make_async_copy(rows_v.at[b], out_hbm.at[pl.ds(base + c * ch, ch)], sems[nb + b])

        @pl.loop(0, n_chunks, step=nb)
        def _(c0):
            for b in range(nb):
                @pl.when(c0 > 0)
                def _():
                    flush(c0 - nb + b, b).wait()
                fetch(c0 + b, b).start()
            for b in range(nb):
                fetch(c0 + b, b).wait()
                flush(c0 + b, b).start()

        for b in range(nb):
            flush(n_chunks - nb + b, b).wait()

    return gather(table, idx.reshape(workers, n_chunks, ch))


def _sc_dispatch(table, dest_a, dest_b, dest_pad, n_out):
    m, width = table.shape
    n_pad = dest_pad.shape[0]
    workers = SC_CORES * SC_SUBCORES
    ch = SC_INDEX_CHUNK
    assert width == LANES and m % (workers * ch) == 0 and n_pad % (workers * ch) == 0
    n_chunks = m // (workers * ch)
    pad_chunks = n_pad // (workers * ch)
    nb = max(b for b in (4, 3, 2, 1) if n_chunks % b == 0)
    mesh = plsc.VectorSubcoreMesh(core_axis_name="c", subcore_axis_name="s",
                                  num_cores=SC_CORES, num_subcores=SC_SUBCORES)

    @functools.partial(
        pl.kernel, mesh=mesh, out_type=jax.ShapeDtypeStruct((n_out, width), table.dtype),
        scratch_types=[pltpu.VMEM((n_chunks, ch), jnp.int32), pltpu.VMEM((n_chunks, ch), jnp.int32),
                       pltpu.VMEM((pad_chunks, ch), jnp.int32), pltpu.VMEM((nb, ch, width), table.dtype),
                       pltpu.VMEM((ch, width), table.dtype)]
        + [pltpu.SemaphoreType.DMA] * (3 * nb + 1))
    def dispatch(table_hbm, zeros_hbm, da_hbm, db_hbm, dp_hbm, out_hbm, da_v, db_v, dp_v, rows_v, zero_v, *sems):
        wid = lax.axis_index("s") * SC_CORES + lax.axis_index("c")
        base = wid * (n_chunks * ch)
        pltpu.sync_copy(da_hbm.at[wid], da_v)
        pltpu.sync_copy(db_hbm.at[wid], db_v)
        pltpu.sync_copy(dp_hbm.at[wid], dp_v)
        pltpu.sync_copy(zeros_hbm, zero_v)

        def fetch(c, b):
            return pltpu.make_async_copy(table_hbm.at[pl.ds(base + c * ch, ch)], rows_v.at[b], sems[b])

        def put_a(c, b):
            return pltpu.make_async_copy(rows_v.at[b], out_hbm.at[da_v.at[c]], sems[nb + b])

        def put_b(c, b):
            return pltpu.make_async_copy(rows_v.at[b], out_hbm.at[db_v.at[c]], sems[2 * nb + b])

        def put_zero(p):
            return pltpu.make_async_copy(zero_v, out_hbm.at[dp_v.at[p]], sems[3 * nb])

        for p in range(pad_chunks):
            put_zero(p).start()

        @pl.loop(0, n_chunks, step=nb)
        def _(c0):
            for b in range(nb):
                @pl.when(c0 > 0)
                def _():
                    put_a(c0 - nb + b, b).wait()
                    put_b(c0 - nb + b, b).wait()
                fetch(c0 + b, b).start()
            for b in range(nb):
                fetch(c0 + b, b).wait()
                put_a(c0 + b, b).start()
                put_b(c0 + b, b).start()

        for b in range(nb):
            put_a(n_chunks - nb + b, b).wait()
            put_b(n_chunks - nb + b, b).wait()
        for p in range(pad_chunks):
            put_zero(p).wait()

    shape3 = lambda v: v.reshape(workers, -1, ch)
    return dispatch(table, jnp.zeros((ch, width), table.dtype), shape3(dest_a), shape3(dest_b), shape3(dest_pad))


def _routing(route, tile):
    B, T, _ = route.shape
    n = B * T
    assert (2 * n) % tile == 0
    ids = route[..., :2].astype(jnp.int32).reshape(n, 2)
    e_cat = jnp.concatenate([ids[:, 0], ids[:, 1]])
    onehot = (e_cat[:, None] == jnp.arange(N_EXPERTS, dtype=jnp.int32)[None, :]).astype(jnp.int32)
    csum = jnp.cumsum(onehot, axis=0)
    pos = jnp.sum(onehot * csum, axis=1) - 1
    counts = csum[-1]
    padded = (counts + tile - 1) // tile * tile
    ends = jnp.cumsum(padded)
    starts = ends - padded
    dest = (starts[e_cat] + pos).astype(jnp.int32)
    n_pad = N_EXPERTS * tile
    n_rows = 2 * n + n_pad
    seg_first = jnp.concatenate([starts + counts, ends[-1:]])
    seg_size = jnp.concatenate([padded - counts, (n_rows - ends[-1])[None]])
    seg_end = jnp.cumsum(seg_size)
    j = jnp.arange(n_pad, dtype=jnp.int32)
    seg = jnp.sum((j[:, None] >= seg_end[None, :]).astype(jnp.int32), axis=1)
    pad_rows = (seg_first[seg] + j - (seg_end - seg_size)[seg]).astype(jnp.int32)
    tile_start = jnp.arange(n_rows // tile, dtype=jnp.int32) * tile
    tile_expert = jnp.minimum(jnp.searchsorted(ends, tile_start, side="right"), N_EXPERTS - 1).astype(jnp.int32)
    tile_valid = (tile_start < ends[-1]).astype(jnp.int32)
    return dest, pad_rows, n_rows, tile_expert, tile_valid


def _expert_kernel(te_ref, tv_ref, xs_ref, wgu_ref, wd_ref, ys_ref, wgu_scr):
    i = pl.program_id(0)
    F = wd_ref.shape[0]

    @pl.when((i == 0) | (te_ref[i] != te_ref[jnp.maximum(i - 1, 0)]))
    def _new_expert():
        wgu_scr[...] = wgu_ref[...].astype(BF16)

    @pl.when(tv_ref[i] > 0)
    def _compute():
        lo, hi = _unpack_bf16_pairs(_load_packed(xs_ref))
        gu = (jnp.dot(lo.astype(BF16), wgu_scr[:PACK_W, :], preferred_element_type=F32)
              + jnp.dot(hi.astype(BF16), wgu_scr[PACK_W:, :], preferred_element_type=F32))
        act = (_silu(gu[:, :F]) * gu[:, F:]).astype(BF16)
        y = jnp.dot(act, wd_ref[...], preferred_element_type=F32)
        _store_packed(ys_ref, _pack_bf16_pairs(y))

    @pl.when(tv_ref[i] == 0)
    def _unused_tile():
        ys_ref[...] = jnp.zeros_like(ys_ref)


def _experts(xs, tile_expert, tile_valid, w_gu, w_down, layer):
    _, n_rows, _ = xs.shape
    _, _, F, D = w_down.shape
    tile = EXPERT_TILE
    rows = pl.BlockSpec((PACK_CHUNKS, tile, LANES), lambda i, te, tv: (0, i, 0))
    return pl.pallas_call(
        _expert_kernel,
        grid_spec=pltpu.PrefetchScalarGridSpec(
            num_scalar_prefetch=2,
            grid=(n_rows // tile,),
            in_specs=[rows,
                      pl.BlockSpec((None, None, D, 2 * F), lambda i, te, tv: (layer, te[i], 0, 0)),
                      pl.BlockSpec((None, None, F, D), lambda i, te, tv: (layer, te[i], 0, 0))],
            out_specs=rows,
            scratch_shapes=[pltpu.VMEM((D, 2 * F), BF16)]),
        out_shape=jax.ShapeDtypeStruct(xs.shape, jnp.int32),
        compiler_params=_params(("arbitrary",)),
    )(tile_expert, tile_valid, xs, w_gu, w_down)


def _combine_kernel(*refs, tm, ctx_len, skip, final):
    if final:
        x_ref, g2_ref, cg2_ref, route_ref, ya_ref, yb_ref, fg_ref, o_ref = refs
    else:
        x_ref, g2_ref, cg2_ref, route_ref, ya_ref, yb_ref, o_ref = refs
    is_ctx = _row_is_ctx(pl.program_id(1) + skip, tm, ctx_len)
    route = route_ref[...]
    lane = lax.broadcasted_iota(jnp.int32, route.shape, 1)
    gate_a = jnp.sum(jnp.where(lane == 2, route, 0.0), axis=-1, keepdims=True)
    gate_b = jnp.sum(jnp.where(lane == 3, route, 0.0), axis=-1, keepdims=True)
    unpack = lambda ref: jnp.concatenate(_unpack_bf16_pairs(_load_packed(ref)), axis=1)
    f = gate_a * unpack(ya_ref) + gate_b * unpack(yb_ref)
    x = x_ref[...] + jnp.where(is_ctx, cg2_ref[...], g2_ref[...]) * f
    if final:
        x = x * lax.rsqrt(jnp.mean(x * x, axis=-1, keepdims=True) + EPS) * fg_ref[...]
    o_ref[...] = x


def _combine(xt, mod_l, mod_c, route, yg, ctx_len, final_g=None):
    B, T, D = xt.shape
    final = final_g is not None
    if final:
        tm = SEQ_TILE
        assert ctx_len % tm == 0 and T % tm == 0
        skip = ctx_len // tm
    else:
        tm = _pick_tile(T, 640)
        skip = 0
    nt = T // tm
    row = lambda w: pl.BlockSpec((None, tm, w), lambda b, j: (b, j + skip, 0))
    slot = lambda s: pl.BlockSpec((PACK_CHUNKS, None, tm, LANES), lambda b, j: (0, s, b * nt + j + skip, 0))
    in_specs = [row(D), *_mod_specs(5, D), row(LANES), slot(0), slot(1)]
    args = [xt, mod_l, mod_c, route, yg, yg]
    if final:
        in_specs.append(pl.BlockSpec((1, D), lambda b, j: (0, 0)))
        args.append(final_g)
    return pl.pallas_call(
        functools.partial(_combine_kernel, tm=tm, ctx_len=ctx_len, skip=skip, final=final),
        grid=(B, nt - skip),
        in_specs=in_specs,
        out_specs=pl.BlockSpec((None, tm, D), lambda b, j: (b, j, 0)),
        out_shape=jax.ShapeDtypeStruct((B, T - skip * tm, D), F32),
        compiler_params=_params(("parallel", "parallel")),
    )(*args)


def _moe(hp, route, xt, mod_l, mod_c, w_gu, w_down, layer, ctx_len, final_g=None):
    B, T, _ = xt.shape
    n = B * T
    dest, pad_rows, n_rows, tile_expert, tile_valid = _routing(route, EXPERT_TILE)
    chunk = jnp.arange(PACK_CHUNKS, dtype=jnp.int32)[:, None] * n_rows
    xs = _sc_dispatch(hp.reshape(PACK_CHUNKS * n, LANES), (chunk + dest[None, :n]).reshape(-1),
                      (chunk + dest[None, n:]).reshape(-1), (chunk + pad_rows[None, :]).reshape(-1),
                      PACK_CHUNKS * n_rows)
    ys = _experts(xs.reshape(PACK_CHUNKS, n_rows, LANES), tile_expert, tile_valid, w_gu, w_down, layer)
    yg = _sc_gather(ys.reshape(PACK_CHUNKS * n_rows, LANES), (chunk + dest[None, :]).reshape(-1))
    return _combine(xt, mod_l, mod_c, route, yg.reshape(PACK_CHUNKS, 2, n, LANES), ctx_len, final_g)


def _final_norm_kernel(x_ref, g_ref, o_ref):
    x = x_ref[...]
    o_ref[...] = x * lax.rsqrt(jnp.mean(x * x, axis=-1, keepdims=True) + EPS) * g_ref[...]


def _final_norm(xt, g, ctx_len):
    B, T, D = xt.shape
    S = T - ctx_len
    tm = SEQ_TILE
    assert ctx_len % tm == 0 and S % tm == 0
    skip = ctx_len // tm
    return pl.pallas_call(
        _final_norm_kernel,
        grid=(B, S // tm),
        in_specs=[pl.BlockSpec((None, tm, D), lambda b, j: (b, j + skip, 0)),
                  pl.BlockSpec((1, D), lambda b, j: (0, 0))],
        out_specs=pl.BlockSpec((None, tm, D), lambda b, j: (b, j, 0)),
        out_shape=jax.ShapeDtypeStruct((B, S, D), F32),
        compiler_params=_params(("parallel", "parallel")),
    )(xt, g)


def _rope_tables(seq, ctx_len):
    pos = jnp.arange(seq, dtype=jnp.int32)
    nf = HEAD_DIM // 4
    inv = ROPE_THETA ** (-jnp.arange(nf, dtype=F32) / nf)
    ang = jnp.concatenate([(pos // GRID_W).astype(F32)[:, None] * inv, (pos % GRID_W).astype(F32)[:, None] * inv], axis=-1)
    cos, sin = jnp.cos(ang), jnp.sin(ang)
    reps = LANES // HEAD_DIM
    cos_t = jnp.tile(jnp.concatenate([cos, cos], axis=-1), (1, reps))
    sin_t = jnp.tile(jnp.concatenate([-sin, sin], axis=-1), (1, reps))
    cos_t = jnp.concatenate([jnp.ones((ctx_len, LANES), F32), cos_t], axis=0)
    sin_t = jnp.concatenate([jnp.zeros((ctx_len, LANES), F32), sin_t], axis=0)
    return cos_t, sin_t


def _relayout_w_in(w_in):
    o_lr = ATT_W + 2 * KV_W + 5 * GLA_W
    q_scale = HEAD_DIM ** -0.5
    gq_scale = GLA_DK ** -0.5
    parts = [w_in[..., :ATT_W] * q_scale, w_in[..., ATT_W:COL_GQ], w_in[..., COL_GQ:COL_GK] * gq_scale,
             w_in[..., COL_GK:o_lr], w_in[..., o_lr + 2 * GLA_LOWRANK:], w_in[..., o_lr:o_lr + 2 * GLA_LOWRANK],
             jnp.zeros(w_in.shape[:-1] + (LANES - 2 * GLA_LOWRANK,), w_in.dtype)]
    return jnp.concatenate(parts, axis=-1).astype(BF16)


def kernel(x, c, ctx, c_ctx, w_mod, b_mod, norm1_g, norm2_g, w_in, w_out, attn_sink, gla_w2_f, gla_b_f, gla_w2_b, gla_b_b, gla_norm_g, conv_w, ffn_w_gu, ffn_w_down, router_w, expert_w_gu, expert_w_down, final_norm_g):
    B, S, D = x.shape
    L = ctx.shape[1]
    depth = w_in.shape[0]
    assert D == D_MODEL and S % GRID_W == 0

    xt = jnp.concatenate([ctx, x], axis=1)
    rows = 16
    cc = jnp.concatenate([c, c_ctx[None, :], jnp.zeros((rows - B - 1, D), F32)], axis=0)
    mod = _adaln(cc, w_mod, b_mod)
    cos_t, sin_t = _rope_tables(S, L)

    w_in_p = _relayout_w_in(w_in)
    w_out_b = _to_bf16(w_out)
    zpad = jnp.zeros((depth, LANES - 2 * GLA_LOWRANK, GLA_W), F32)
    w2_f = jnp.concatenate([gla_w2_f, jnp.zeros_like(gla_w2_b), zpad], axis=1).astype(BF16)
    w2_b = jnp.concatenate([jnp.zeros_like(gla_w2_f), gla_w2_b, zpad], axis=1).astype(BF16)
    gla_ng = jnp.tile(gla_norm_g, (1, GLA_HEADS))
    ffn_gu_b = _to_bf16(ffn_w_gu)
    ffn_down_b = _to_bf16(ffn_w_down)
    exp_down_b = _to_bf16(expert_w_down)
    router_b = jnp.pad(router_w, ((0, 0), (0, 0), (0, LANES - N_EXPERTS))).astype(BF16)

    for l in range(depth):
        mod_l = mod[l, :B].reshape(B, 1, 6 * D)
        mod_c = mod[l, B:B + 1]
        proj = _inproj(xt, norm1_g[l][None, :], mod_l, mod_c, cos_t, sin_t, w_in_p, l, L)
        att = _attention(proj, attn_sink[l], L)
        cv = _short_conv(proj, conv_w[l], L)
        gf = _gla(proj, w2_f[l], gla_b_f[l][None, :], gla_ng[l][None, :], L, reverse=False)
        gb = _gla(proj, w2_b[l], gla_b_b[l][None, :], gla_ng[l][None, :], L, reverse=True)
        if l % 2 == 0:
            xt, h2 = _outproj(xt, att, gf, gb, cv, w_out_b, l, norm2_g[l][None, :], mod_l, mod_c, L)
            xt = _ffn(h2, xt, mod_l, mod_c, ffn_gu_b, ffn_down_b, l // 2, L)
        else:
            xt, hp, route = _outproj(xt, att, gf, gb, cv, w_out_b, l, norm2_g[l][None, :], mod_l, mod_c, L,
                                     router=(router_b, l // 2))
            xt = _moe(hp, route, xt, mod_l, mod_c, expert_w_gu, exp_down_b, l // 2, L,
                      final_g=final_norm_g[None, :] if l == depth - 1 else None)
    return xt if depth % 2 == 0 else _final_norm(xt, final_norm_g[None, :], L)
```

```python
import functools

import jax
import jax.numpy as jnp
from jax import lax
from jax.experimental import pallas as pl
from jax.experimental.pallas import tpu as pltpu
from jax.experimental.pallas import tpu_sc as plsc

F32 = jnp.float32
BF16 = jnp.bfloat16

D_MODEL = 1024
EPS = 1e-6
GRID_W = 64
ROPE_THETA = 10000.0
HEAD_DIM = 64
ATT_HEADS = 8
ATT_KV_HEADS = 2
ATT_GROUP = ATT_HEADS // ATT_KV_HEADS
ATT_W = ATT_HEADS * HEAD_DIM
KV_W = ATT_KV_HEADS * HEAD_DIM
WINDOW = 128
GLA_HEADS = 4
GLA_DK = 64
GLA_W = GLA_HEADS * GLA_DK
GLA_LOWRANK = 16
GLA_TAU = 16.0
GLA_CHUNK = 64
CONV_CH = 256
N_EXPERTS = 8

LANES = 128
SUBLANES_BF16 = 16

COL_Q = 0
COL_K = ATT_W
COL_V = COL_K + KV_W
ROPE_W = COL_V
COL_GQ = COL_V + KV_W
COL_GK = COL_GQ + GLA_W
COL_GV = COL_GK + GLA_W
COL_GGF = COL_GV + GLA_W
COL_GGB = COL_GGF + GLA_W
COL_CB = COL_GGB + GLA_W
COL_CC = COL_CB + CONV_CH
COL_CX = COL_CC + CONV_CH
COL_LR = COL_CX + CONV_CH
PROJ_W = COL_LR + LANES

SEQ_TILE = 256
VMEM_LIMIT = 56 * 1024 * 1024
CAST_BLOCK_BYTES = 4 * 1024 * 1024

PACK_W = D_MODEL // 2
PACK_CHUNKS = PACK_W // LANES
EXPERT_TILE = 512
SC_CORES = 2
SC_SUBCORES = 16
SC_INDEX_CHUNK = 128


def _silu(v):
    return v / (1.0 + jnp.exp(-v))


def _pick_tile(total, cap):
    best = None
    for t in range(SUBLANES_BF16, cap + 1, SUBLANES_BF16):
        if total % t == 0:
            best = t
    assert best is not None
    return best


def _params(sem):
    return pltpu.CompilerParams(dimension_semantics=sem, vmem_limit_bytes=VMEM_LIMIT)


def _pack_bf16_pairs(v):
    bits = lambda t: lax.bitcast_convert_type(t.astype(BF16).astype(F32), jnp.int32)
    return ((bits(v[:, :PACK_W]) >> 16) & 0xFFFF) | (bits(v[:, PACK_W:]) & -65536)


def _unpack_bf16_pairs(w):
    return lax.bitcast_convert_type(w << 16, F32), lax.bitcast_convert_type(w & -65536, F32)


def _cast_kernel(x_ref, o_ref):
    o_ref[...] = x_ref[...].astype(BF16)


def _to_bf16(w):
    cols = w.shape[-1]
    rows = w.size // cols
    tr = _pick_tile(rows, max(SUBLANES_BF16, CAST_BLOCK_BYTES // (4 * cols)))
    out = pl.pallas_call(
        _cast_kernel,
        grid=(rows // tr,),
        in_specs=[pl.BlockSpec((tr, cols), lambda i: (i, 0))],
        out_specs=pl.BlockSpec((tr, cols), lambda i: (i, 0)),
        out_shape=jax.ShapeDtypeStruct((rows, cols), BF16),
        compiler_params=_params(("parallel",)),
    )(w.reshape(rows, cols))
    return out.reshape(w.shape)


def _load_packed(ref):
    return jnp.concatenate([ref[k] for k in range(PACK_CHUNKS)], axis=1)


def _store_packed(ref, packed):
    for k in range(PACK_CHUNKS):
        ref[k] = packed[:, k * LANES:(k + 1) * LANES]


def _adaln_kernel(c_ref, w_ref, b_ref, o_ref):
    s = _silu(c_ref[...]).astype(BF16)
    o_ref[...] = jnp.dot(s, w_ref[...].astype(BF16), preferred_element_type=F32) + b_ref[...]


def _adaln(cc, w_mod, b_mod):
    depth, d, n = w_mod.shape
    tn = 1024
    rows = cc.shape[0]
    return pl.pallas_call(
        _adaln_kernel,
        grid=(depth, n // tn),
        in_specs=[pl.BlockSpec((rows, d), lambda l, j: (0, 0)),
                  pl.BlockSpec((None, d, tn), lambda l, j: (l, 0, j)),
                  pl.BlockSpec((None, 1, tn), lambda l, j: (l, 0, j))],
        out_specs=pl.BlockSpec((None, rows, tn), lambda l, j: (l, 0, j)),
        out_shape=jax.ShapeDtypeStruct((depth, rows, n), F32),
        compiler_params=_params(("parallel", "parallel")),
    )(cc, w_mod, b_mod.reshape(depth, 1, n))


def _mod_specs(k, d):
    return [pl.BlockSpec((None, 1, d), lambda b, j, *_: (b, 0, k)),
            pl.BlockSpec((1, d), lambda b, j, *_: (0, k))]


def _row_is_ctx(j, tm, ctx_len):
    return (j * tm + lax.broadcasted_iota(jnp.int32, (tm, 1), 0)) < ctx_len


def _inproj_kernel(x_ref, g_ref, sc_ref, csc_ref, sh_ref, csh_ref, cos_ref, sin_ref, w_ref, o_ref, *, tm, ctx_len):
    x = x_ref[...]
    n = x * lax.rsqrt(jnp.mean(x * x, axis=-1, keepdims=True) + EPS) * g_ref[...]
    is_ctx = _row_is_ctx(pl.program_id(1), tm, ctx_len)
    scale = jnp.where(is_ctx, csc_ref[...], sc_ref[...])
    shift = jnp.where(is_ctx, csh_ref[...], sh_ref[...])
    h = (n * (1.0 + scale) + shift).astype(BF16)
    qk = jnp.dot(h, w_ref[:, :ROPE_W], preferred_element_type=F32)
    lower_half = (lax.broadcasted_iota(jnp.int32, (tm, LANES), 1) % HEAD_DIM) < (HEAD_DIM // 2)
    cos = cos_ref[...]
    sin = sin_ref[...]
    for i in range(ROPE_W // LANES):
        t = qk[:, i * LANES:(i + 1) * LANES]
        partner = jnp.where(lower_half, pltpu.roll(t, LANES - HEAD_DIM // 2, axis=1), pltpu.roll(t, HEAD_DIM // 2, axis=1))
        o_ref[:, i * LANES:(i + 1) * LANES] = (t * cos + partner * sin).astype(BF16)
    o_ref[:, ROPE_W:] = jnp.dot(h, w_ref[:, ROPE_W:], preferred_element_type=F32).astype(BF16)


def _inproj(xt, g, mod_l, mod_c, cos_t, sin_t, w, layer, ctx_len):
    B, T, D = xt.shape
    tm = _pick_tile(T, 640)
    kern = functools.partial(_inproj_kernel, tm=tm, ctx_len=ctx_len)
    return pl.pallas_call(
        kern,
        grid=(B, T // tm),
        in_specs=[pl.BlockSpec((None, tm, D), lambda b, j: (b, j, 0)),
                  pl.BlockSpec((1, D), lambda b, j: (0, 0)),
                  *_mod_specs(1, D), *_mod_specs(0, D),
                  pl.BlockSpec((tm, LANES), lambda b, j: (j, 0)),
                  pl.BlockSpec((tm, LANES), lambda b, j: (j, 0)),
                  pl.BlockSpec((None, D, PROJ_W), lambda b, j: (layer, 0, 0))],
        out_specs=pl.BlockSpec((None, tm, PROJ_W), lambda b, j: (b, j, 0)),
        out_shape=jax.ShapeDtypeStruct((B, T, PROJ_W), BF16),
        compiler_params=_params(("parallel", "parallel")),
    )(xt, g, mod_l, mod_c, mod_l, mod_c, cos_t, sin_t, w)


def _attn_kernel(sink_ref, q_ref, kc_ref, vc_ref, km_ref, vm_ref, kp_ref, vp_ref, kn_ref, vn_ref, o_ref, *, n_tiles):
    j = pl.program_id(1)
    nt_dims = (((1,), (1,)), ((), ()))

    def scores(rows, n_rows, g, k_ctx, k_win=None, mask_win=None):
        heads = [g * ATT_GROUP + hh for hh in range(ATT_GROUP)]
        q = jnp.concatenate([q_ref[rows, h * HEAD_DIM:(h + 1) * HEAD_DIM] for h in heads], axis=0)
        head_row = lax.broadcasted_iota(jnp.int32, (ATT_GROUP * n_rows, 1), 0) // n_rows
        sink = jnp.full((ATT_GROUP * n_rows, 1), sink_ref[heads[0]], F32)
        for hh in range(1, ATT_GROUP):
            sink = jnp.where(head_row == hh, sink_ref[heads[hh]], sink)
        s_c = lax.dot_general(q, k_ctx, nt_dims, preferred_element_type=F32)
        s_w = None
        if k_win is not None:
            s_w = jnp.where(mask_win, lax.dot_general(q, k_win, nt_dims, preferred_element_type=F32), -1e30)
        return rows, n_rows, heads, sink, s_c, s_w

    def finish(item, v_ctx, v_win=None):
        rows, n_rows, heads, sink, s_c, s_w = item
        m = jnp.maximum(jnp.max(s_c, axis=-1, keepdims=True), sink)
        if s_w is not None:
            m = jnp.maximum(m, jnp.max(s_w, axis=-1, keepdims=True))
        p_c = jnp.exp(s_c - m)
        den = jnp.sum(p_c, axis=-1, keepdims=True) + jnp.exp(sink - m)
        o = jnp.dot(p_c.astype(BF16), v_ctx, preferred_element_type=F32)
        if s_w is not None:
            p_w = jnp.exp(s_w - m)
            den += jnp.sum(p_w, axis=-1, keepdims=True)
            o += jnp.dot(p_w.astype(BF16), v_win, preferred_element_type=F32)
        out = o / den
        for hh, h in enumerate(heads):
            o_ref[rows, h * HEAD_DIM:(h + 1) * HEAD_DIM] = out[hh * n_rows:(hh + 1) * n_rows].astype(BF16)

    def pipelined(work):
        pending = None
        for score_fn, finish_fn in work:
            item = score_fn()
            if pending is not None:
                pending[1](pending[0])
            pending = (item, finish_fn)
        pending[1](pending[0])

    @pl.when(j == 0)
    def _context_queries():
        work = []
        for g in range(ATT_KV_HEADS):
            gs = slice(g * HEAD_DIM, (g + 1) * HEAD_DIM)
            work.append((functools.partial(scores, slice(0, SEQ_TILE), SEQ_TILE, g, kc_ref[:, gs]),
                         functools.partial(finish, v_ctx=vc_ref[:, gs])))
        pipelined(work)

    @pl.when(j > 0)
    def _latent_queries():
        r = lax.broadcasted_iota(jnp.int32, (ATT_GROUP * WINDOW, 3 * WINDOW), 0) % WINDOW
        w = lax.broadcasted_iota(jnp.int32, (ATT_GROUP * WINDOW, 3 * WINDOW), 1)
        band = (w >= r) & (w <= r + 2 * WINDOW)
        work = []
        for sub in range(SEQ_TILE // WINDOW):
            rows = slice(sub * WINDOW, (sub + 1) * WINDOW)
            if sub == 0:
                kprev, vprev = kp_ref[...], vp_ref[...]
                kmid, vmid = km_ref[:WINDOW], vm_ref[:WINDOW]
                knext, vnext = km_ref[WINDOW:], vm_ref[WINDOW:]
                w_lo = jnp.where(j > 1, 0, WINDOW)
                w_hi = 3 * WINDOW
            else:
                kprev, vprev = km_ref[:WINDOW], vm_ref[:WINDOW]
                kmid, vmid = km_ref[WINDOW:], vm_ref[WINDOW:]
                knext, vnext = kn_ref[...], vn_ref[...]
                w_lo = 0
                w_hi = jnp.where(j < n_tiles - 1, 3 * WINDOW, 2 * WINDOW)
            mask = band & (w >= w_lo) & (w < w_hi)
            for g in range(ATT_KV_HEADS):
                gs = slice(g * HEAD_DIM, (g + 1) * HEAD_DIM)
                k_win = jnp.concatenate([kprev[:, gs], kmid[:, gs], knext[:, gs]], axis=0)
                v_win = jnp.concatenate([vprev[:, gs], vmid[:, gs], vnext[:, gs]], axis=0)
                work.append((functools.partial(scores, rows, WINDOW, g, kc_ref[:, gs], k_win, mask),
                             functools.partial(finish, v_ctx=vc_ref[:, gs], v_win=v_win)))
        pipelined(work)


def _attention(proj, sink, ctx_len):
    B, T, _ = proj.shape
    assert ctx_len == SEQ_TILE and T % SEQ_TILE == 0
    n_tiles = T // SEQ_TILE
    n_win = T // WINDOW
    kcol, vcol = COL_K // KV_W, COL_V // KV_W
    per_tile = SEQ_TILE // WINDOW
    tile = lambda col: pl.BlockSpec((None, SEQ_TILE, KV_W), lambda b, j: (b, j, col))
    ctx = lambda col: pl.BlockSpec((None, SEQ_TILE, KV_W), lambda b, j: (b, 0, col))
    prev = lambda col: pl.BlockSpec((None, WINDOW, KV_W), lambda b, j: (b, jnp.maximum(j * per_tile - 1, 0), col))
    nxt = lambda col: pl.BlockSpec((None, WINDOW, KV_W), lambda b, j: (b, jnp.minimum((j + 1) * per_tile, n_win - 1), col))
    return pl.pallas_call(
        functools.partial(_attn_kernel, n_tiles=n_tiles),
        grid=(B, n_tiles),
        in_specs=[pl.BlockSpec(memory_space=pltpu.SMEM),
                  pl.BlockSpec((None, SEQ_TILE, ATT_W), lambda b, j: (b, j, 0)),
                  ctx(kcol), ctx(vcol), tile(kcol), tile(vcol), prev(kcol), prev(vcol), nxt(kcol), nxt(vcol)],
        out_specs=pl.BlockSpec((None, SEQ_TILE, ATT_W), lambda b, j: (b, j, 0)),
        out_shape=jax.ShapeDtypeStruct((B, T, ATT_W), BF16),
        compiler_params=_params(("parallel", "parallel")),
    )(sink, *([proj] * 9))


def _conv_kernel(b_ref, c_ref, x_ref, w_ref, o_ref, u_scr, *, ctx_len):
    T = c_ref.shape[0]
    pad = 8
    u = c_ref[...].astype(F32) * x_ref[...].astype(F32)
    zero_row = jnp.zeros((1, CONV_CH), F32)
    u_scr[pad - 1:pad, :] = zero_row
    u_scr[pad:pad + T, :] = u
    u_scr[pad + T:pad + T + 1, :] = zero_row
    row = lax.broadcasted_iota(jnp.int32, (T, 1), 0)
    u_prev = jnp.where(row == ctx_len, 0.0, u_scr[pad - 1:pad - 1 + T, :])
    u_next = jnp.where(row == ctx_len - 1, 0.0, u_scr[pad + 1:pad + 1 + T, :])
    y = w_ref[0:1, :] * u_prev + w_ref[1:2, :] * u + w_ref[2:3, :] * u_next
    o_ref[...] = (b_ref[...].astype(F32) * y).astype(BF16)


def _short_conv(proj, conv_w, ctx_len):
    B, T, _ = proj.shape
    col = lambda c0: pl.BlockSpec((None, T, CONV_CH), lambda b: (b, 0, c0 // CONV_CH))
    return pl.pallas_call(
        functools.partial(_conv_kernel, ctx_len=ctx_len),
        grid=(B,),
        in_specs=[col(COL_CB), col(COL_CC), col(COL_CX), pl.BlockSpec((3, CONV_CH), lambda b: (0, 0))],
        out_specs=pl.BlockSpec((None, T, CONV_CH), lambda b: (b, 0, 0)),
        out_shape=jax.ShapeDtypeStruct((B, T, CONV_CH), BF16),
        scratch_shapes=[pltpu.VMEM((T + 16, CONV_CH), F32)],
        compiler_params=_params(("parallel",)),
    )(proj, proj, proj, conv_w)


def _split2(v):
    hi = v.astype(BF16)
    lo = (v - hi.astype(F32)).astype(BF16)
    return hi, lo


def _gla_kernel(q_ref, k_ref, v_ref, gate_ref, lr_ref, w2_ref, bias_ref, ng_ref, o_ref,
                oacc_scr, qs_scr, u_scr, dec_scr, sin_scr, *, reverse, n_ctx_chunks):
    T = q_ref.shape[0]
    C = GLA_CHUNK
    n_tiles = T // SEQ_TILE
    n_chunks = T // C
    per_tile = SEQ_TILE // C
    nt_dims = (((1,), (1,)), ((), ()))
    tn_dims = (((0,), (0,)), ((), ()))

    r = lax.broadcasted_iota(jnp.int32, (SEQ_TILE, SEQ_TILE), 0)
    c = lax.broadcasted_iota(jnp.int32, (SEQ_TILE, SEQ_TILE), 1)
    same_chunk = (r // C) == (c // C)
    causal = same_chunk & ((c >= r) if reverse else (c <= r))
    cum_mat = jnp.where(causal, 1.0, 0.0).astype(BF16)
    same_head = (r // GLA_DK) == (c // GLA_DK)
    head_mean = jnp.where(same_head, 1.0 / GLA_DK, 0.0).astype(BF16)
    lane_head = lax.broadcasted_iota(jnp.int32, (C, GLA_W), 1) // GLA_DK

    def phase1(i, carry):
        rows = pl.ds(pl.multiple_of(i * SEQ_TILE, SEQ_TILE), SEQ_TILE)
        z = jnp.dot(lr_ref[rows, :], w2_ref[...], preferred_element_type=F32) + bias_ref[...]
        la = (jnp.minimum(z, 0.0) - jnp.log(1.0 + jnp.exp(-jnp.abs(z)))) / GLA_TAU
        hi, lo = _split2(la)
        b = jnp.dot(cum_mat, hi, preferred_element_type=F32) + jnp.dot(cum_mat, lo, preferred_element_type=F32)
        b3 = b.reshape(per_tile, C, GLA_W)
        total = b3[:, 0:1, :] if reverse else b3[:, C - 1:C, :]
        b_last = jnp.broadcast_to(total, (per_tile, C, GLA_W)).reshape(SEQ_TILE, GLA_W)
        b_ref = 0.5 * b_last
        q = q_ref[rows, :].astype(F32)
        k = k_ref[rows, :].astype(F32)
        v = v_ref[rows, :]
        e_fwd = jnp.exp(b - b_ref)
        e_bwd = jnp.exp(b_ref - b)
        e_half = jnp.exp(b_ref)
        qe = (q * e_fwd).astype(BF16)
        ke = (k * e_bwd).astype(BF16)
        ku = (k * (e_bwd * e_half)).astype(BF16)
        qs_scr[rows, :] = (q * (e_fwd * e_half)).astype(BF16)
        dec = e_half * e_half
        for h in range(GLA_HEADS):
            hs = slice(h * GLA_DK, (h + 1) * GLA_DK)
            s = lax.dot_general(qe[:, hs], ke[:, hs], nt_dims, preferred_element_type=F32)
            a = jnp.where(causal, s, 0.0).astype(BF16)
            oacc_scr[rows, hs] = jnp.dot(a, v[:, hs], preferred_element_type=F32)
        for ci in range(per_tile):
            cs = slice(ci * C, (ci + 1) * C)
            full = lax.dot_general(v[cs, :], ku[cs, :], tn_dims, preferred_element_type=F32)
            ut = full[0:C, :]
            for h in range(1, GLA_HEADS):
                ut = jnp.where(lane_head == h, full[h * C:(h + 1) * C, :], ut)
            u_scr[i * per_tile + ci] = ut
            dec_scr[i * per_tile + ci] = dec[ci * C:ci * C + 1, :]
        return carry

    lax.fori_loop(0, n_tiles, phase1, 0, unroll=2)

    def phase2(i, st):
        if reverse:
            ci = jnp.where(i < n_ctx_chunks, n_ctx_chunks - 1 - i, n_chunks - 1 + n_ctx_chunks - i)
        else:
            ci = i
        sin_scr[ci] = st.astype(BF16)
        return st * dec_scr[ci] + u_scr[ci]

    lax.fori_loop(0, n_chunks, phase2, jnp.zeros((C, GLA_W), F32))

    def phase3(i, carry):
        rows = pl.ds(pl.multiple_of(i * SEQ_TILE, SEQ_TILE), SEQ_TILE)
        qs = qs_scr[rows, :]
        parts = []
        for ci in range(per_tile):
            st = sin_scr[i * per_tile + ci]
            st_heads = jnp.where(same_head, jnp.concatenate([st] * GLA_HEADS, axis=0), jnp.zeros((), BF16))
            parts.append(lax.dot_general(qs[ci * C:(ci + 1) * C, :], st_heads, nt_dims, preferred_element_type=F32))
        o = oacc_scr[rows, :] + jnp.concatenate(parts, axis=0)
        ms = jnp.dot((o * o).astype(BF16), head_mean, preferred_element_type=F32)
        y = o * lax.rsqrt(ms + EPS) * ng_ref[...]
        o_ref[rows, :] = (y * _silu(gate_ref[rows, :].astype(F32))).astype(BF16)
        return carry

    lax.fori_loop(0, n_tiles, phase3, 0, unroll=2)


def _gla(proj, w2pad, bias, ng, ctx_len, reverse):
    B, T, _ = proj.shape
    assert T % SEQ_TILE == 0 and ctx_len % GLA_CHUNK == 0
    n_chunks = T // GLA_CHUNK
    col = lambda c0: pl.BlockSpec((None, T, GLA_W), lambda b: (b, 0, c0 // GLA_W))
    gate_col = COL_GGB if reverse else COL_GGF
    kern = functools.partial(_gla_kernel, reverse=reverse, n_ctx_chunks=ctx_len // GLA_CHUNK)
    return pl.pallas_call(
        kern,
        grid=(B,),
        in_specs=[col(COL_GQ), col(COL_GK), col(COL_GV), col(gate_col),
                  pl.BlockSpec((None, T, LANES), lambda b: (b, 0, COL_LR // LANES)),
                  pl.BlockSpec((LANES, GLA_W), lambda b: (0, 0)),
                  pl.BlockSpec((1, GLA_W), lambda b: (0, 0)),
                  pl.BlockSpec((1, GLA_W), lambda b: (0, 0))],
        out_specs=pl.BlockSpec((None, T, GLA_W), lambda b: (b, 0, 0)),
        out_shape=jax.ShapeDtypeStruct((B, T, GLA_W), BF16),
        scratch_shapes=[pltpu.VMEM((T, GLA_W), F32),
                        pltpu.VMEM((T, GLA_W), BF16),
                        pltpu.VMEM((n_chunks, GLA_CHUNK, GLA_W), F32),
                        pltpu.VMEM((n_chunks, 1, GLA_W), F32),
                        pltpu.VMEM((n_chunks, GLA_CHUNK, GLA_W), BF16)],
        compiler_params=_params(("parallel",)),
    )(proj, proj, proj, proj, proj, w2pad, bias, ng)


def _outproj_kernel(*refs, tm, ctx_len, moe):
    if moe:
        (x_ref, att_ref, gf_ref, gb_ref, cv_ref, w_ref, g1_ref, cg1_ref, ng_ref, sc_ref, csc_ref, sh_ref, csh_ref,
         router_ref, xo_ref, hp_ref, route_ref) = refs
    else:
        (x_ref, att_ref, gf_ref, gb_ref, cv_ref, w_ref, g1_ref, cg1_ref, ng_ref, sc_ref, csc_ref, sh_ref, csh_ref,
         xo_ref, h_ref) = refs
    is_ctx = _row_is_ctx(pl.program_id(1), tm, ctx_len)
    gla = (gf_ref[...].astype(F32) + gb_ref[...].astype(F32)).astype(BF16)
    y = jnp.dot(att_ref[...], w_ref[:ATT_W, :], preferred_element_type=F32)
    y += jnp.dot(gla, w_ref[ATT_W:ATT_W + GLA_W, :], preferred_element_type=F32)
    y += jnp.dot(cv_ref[...], w_ref[ATT_W + GLA_W:, :], preferred_element_type=F32)
    x = x_ref[...] + jnp.where(is_ctx, cg1_ref[...], g1_ref[...]) * y
    xo_ref[...] = x
    n = x * lax.rsqrt(jnp.mean(x * x, axis=-1, keepdims=True) + EPS) * ng_ref[...]
    scale = jnp.where(is_ctx, csc_ref[...], sc_ref[...])
    shift = jnp.where(is_ctx, csh_ref[...], sh_ref[...])
    h = n * (1.0 + scale) + shift
    hi = h.astype(BF16)
    if not moe:
        h_ref[...] = hi
    else:
        _store_packed(hp_ref, _pack_bf16_pairs(h))
        logits = jnp.dot(hi, router_ref[...], preferred_element_type=F32)
        lane = lax.broadcasted_iota(jnp.int32, logits.shape, 1)
        neg = -jnp.inf
        lg = jnp.where(lane < N_EXPERTS, logits, neg)
        m1 = jnp.max(lg, axis=-1, keepdims=True)
        i1 = jnp.min(jnp.where(lg == m1, lane, LANES), axis=-1, keepdims=True)
        lg2 = jnp.where(lane == i1, neg, lg)
        m2 = jnp.max(lg2, axis=-1, keepdims=True)
        i2 = jnp.min(jnp.where(lg2 == m2, lane, LANES), axis=-1, keepdims=True)
        e2 = jnp.exp(m2 - m1)
        gate1 = 1.0 / (1.0 + e2)
        gate2 = e2 / (1.0 + e2)
        route_ref[...] = jnp.where(lane == 0, i1.astype(F32), jnp.where(lane == 1, i2.astype(F32),
                                   jnp.where(lane == 2, gate1, jnp.where(lane == 3, gate2, 0.0))))


def _outproj(xt, att, gf, gb, cv, w_out, layer, ng, mod_l, mod_c, ctx_len, router=None):
    B, T, D = xt.shape
    tm = _pick_tile(T, 640)
    moe = router is not None
    row = lambda w: pl.BlockSpec((None, tm, w), lambda b, j: (b, j, 0))
    const = lambda shape: pl.BlockSpec(shape, lambda b, j: (0,) * len(shape))
    in_specs = [row(D), row(ATT_W), row(GLA_W), row(GLA_W), row(CONV_CH),
                pl.BlockSpec((None, D, D), lambda b, j: (layer, 0, 0)),
                *_mod_specs(2, D), const((1, D)), *_mod_specs(4, D), *_mod_specs(3, D)]
    args = [xt, att, gf, gb, cv, w_out, mod_l, mod_c, ng, mod_l, mod_c, mod_l, mod_c]
    nt = T // tm
    if moe:
        in_specs.append(pl.BlockSpec((None, D, LANES), lambda b, j: (router[1], 0, 0)))
        args.append(router[0])
        out_specs = [row(D), pl.BlockSpec((PACK_CHUNKS, tm, LANES), lambda b, j: (0, b * nt + j, 0)), row(LANES)]
        out_shape = [jax.ShapeDtypeStruct((B, T, D), F32), jax.ShapeDtypeStruct((PACK_CHUNKS, B * T, LANES), jnp.int32),
                     jax.ShapeDtypeStruct((B, T, LANES), F32)]
    else:
        out_specs = [row(D), row(D)]
        out_shape = [jax.ShapeDtypeStruct((B, T, D), F32), jax.ShapeDtypeStruct((B, T, D), BF16)]
    return pl.pallas_call(
        functools.partial(_outproj_kernel, tm=tm, ctx_len=ctx_len, moe=moe),
        grid=(B, T // tm),
        in_specs=in_specs, out_specs=out_specs, out_shape=out_shape,
        compiler_params=_params(("parallel", "parallel")),
    )(*args)


MXU_TILE = 256
FFN_CHUNK = 3 * MXU_TILE


def _ffn_kernel(h_ref, x_ref, g2_ref, cg2_ref, wgu_ref, wd_ref, o_ref, *, tm, ctx_len):
    F = wd_ref.shape[0]
    h = h_ref[...]
    y = None
    for c0 in range(0, F, FFN_CHUNK):
        c1 = min(c0 + FFN_CHUNK, F)
        g = jnp.dot(h, wgu_ref[:, c0:c1], preferred_element_type=F32)
        u = jnp.dot(h, wgu_ref[:, F + c0:F + c1], preferred_element_type=F32)
        part = jnp.dot((_silu(g) * u).astype(BF16), wd_ref[c0:c1, :], preferred_element_type=F32)
        y = part if y is None else y + part
    is_ctx = _row_is_ctx(pl.program_id(1), tm, ctx_len)
    o_ref[...] = x_ref[...] + jnp.where(is_ctx, cg2_ref[...], g2_ref[...]) * y


def _ffn(h2, xt, mod_l, mod_c, w_gu, w_down, layer, ctx_len):
    B, T, D = xt.shape
    F = w_down.shape[1]
    assert F % MXU_TILE == 0
    tm = _pick_tile(T, 640)
    row = lambda w: pl.BlockSpec((None, tm, w), lambda b, j: (b, j, 0))
    resident = lambda shape: pl.BlockSpec((None,) + shape, lambda b, j: (layer, 0, 0), pipeline_mode=pl.Buffered(1))
    return pl.pallas_call(
        functools.partial(_ffn_kernel, tm=tm, ctx_len=ctx_len),
        grid=(B, T // tm),
        in_specs=[row(D), row(D), *_mod_specs(5, D), resident((D, 2 * F)), resident((F, D))],
        out_specs=row(D),
        out_shape=jax.ShapeDtypeStruct((B, T, D), F32),
        compiler_params=_params(("parallel", "parallel")),
    )(h2, xt, mod_l, mod_c, w_gu, w_down)


def _sc_gather(table, idx):
    _, width = table.shape
    m = idx.shape[0]
    workers = SC_CORES * SC_SUBCORES
    ch = SC_INDEX_CHUNK
    assert width == LANES and m % (workers * ch) == 0
    n_chunks = m // (workers * ch)
    nb = max(b for b in (4, 3, 2, 1) if n_chunks % b == 0)
    mesh = plsc.VectorSubcoreMesh(core_axis_name="c", subcore_axis_name="s",
                                  num_cores=SC_CORES, num_subcores=SC_SUBCORES)

    @functools.partial(
        pl.kernel, mesh=mesh, out_type=jax.ShapeDtypeStruct((m, width), table.dtype),
        scratch_types=[pltpu.VMEM((n_chunks, ch), jnp.int32), pltpu.VMEM((nb, ch, width), table.dtype)]
        + [pltpu.SemaphoreType.DMA] * (2 * nb))
    def gather(table_hbm, idx_hbm, out_hbm, idx_v, rows_v, *sems):
        wid = lax.axis_index("s") * SC_CORES + lax.axis_index("c")
        base = wid * (n_chunks * ch)
        pltpu.sync_copy(idx_hbm.at[wid], idx_v)

        def fetch(c, b):
            return pltpu.make_async_copy(table_hbm.at[idx_v.at[c]], rows_v.at[b], sems[b])

        def flush(c, b):
            return pltpu.make_async_copy(rows_v.at[b], out_hbm.at[pl.ds(base + c * ch, ch)], sems[nb + b])

        @pl.loop(0, n_chunks, step=nb)
        def _(c0):
            for b in range(nb):
                @pl.when(c0 > 0)
                def _():
                    flush(c0 - nb + b, b).wait()
                fetch(c0 + b, b).start()
            for b in range(nb):
                fetch(c0 + b, b).wait()
                flush(c0 + b, b).start()

        for b in range(nb):
            flush(n_chunks - nb + b, b).wait()

    return gather(table, idx.reshape(workers, n_chunks, ch))


def _sc_dispatch(table, dest_a, dest_b, dest_pad, n_out):
    m, width = table.shape
    n_pad = dest_pad.shape[0]
    workers = SC_CORES * SC_SUBCORES
    ch = SC_INDEX_CHUNK
    assert width == LANES and m % (workers * ch) == 0 and n_pad % (workers * ch) == 0
    n_chunks = m // (workers * ch)
    pad_chunks = n_pad // (workers * ch)
    nb = max(b for b in (4, 3, 2, 1) if n_chunks % b == 0)
    mesh = plsc.VectorSubcoreMesh(core_axis_name="c", subcore_axis_name="s",
                                  num_cores=SC_CORES, num_subcores=SC_SUBCORES)

    @functools.partial(
        pl.kernel, mesh=mesh, out_type=jax.ShapeDtypeStruct((n_out, width), table.dtype),
        scratch_types=[pltpu.VMEM((n_chunks, ch), jnp.int32), pltpu.VMEM((n_chunks, ch), jnp.int32),
                       pltpu.VMEM((pad_chunks, ch), jnp.int32), pltpu.VMEM((nb, ch, width), table.dtype),
                       pltpu.VMEM((ch, width), table.dtype)]
        + [pltpu.SemaphoreType.DMA] * (3 * nb + 1))
    def dispatch(table_hbm, zeros_hbm, da_hbm, db_hbm, dp_hbm, out_hbm, da_v, db_v, dp_v, rows_v, zero_v, *sems):
        wid = lax.axis_index("s") * SC_CORES + lax.axis_index("c")
        base = wid * (n_chunks * ch)
        pltpu.sync_copy(da_hbm.at[wid], da_v)
        pltpu.sync_copy(db_hbm.at[wid], db_v)
        pltpu.sync_copy(dp_hbm.at[wid], dp_v)
        pltpu.sync_copy(zeros_hbm, zero_v)

        def fetch(c, b):
            return pltpu.make_async_copy(table_hbm.at[pl.ds(base + c * ch, ch)], rows_v.at[b], sems[b])

        def put_a(c, b):
            return pltpu.make_async_copy(rows_v.at[b], out_hbm.at[da_v.at[c]], sems[nb + b])

        def put_b(c, b):
            return pltpu.make_async_copy(rows_v.at[b], out_hbm.at[db_v.at[c]], sems[2 * nb + b])

        def put_zero(p):
            return pltpu.make_async_copy(zero_v, out_hbm.at[dp_v.at[p]], sems[3 * nb])

        for p in range(pad_chunks):
            put_zero(p).start()

        @pl.loop(0, n_chunks, step=nb)
        def _(c0):
            for b in range(nb):
                @pl.when(c0 > 0)
                def _():
                    put_a(c0 - nb + b, b).wait()
                    put_b(c0 - nb + b, b).wait()
                fetch(c0 + b, b).start()
            for b in range(nb):
                fetch(c0 + b, b).wait()
                put_a(c0 + b, b).start()
                put_b(c0 + b, b).start()

        for b in range(nb):
            put_a(n_chunks - nb + b, b).wait()
            put_b(n_chunks - nb + b, b).wait()
        for p in range(pad_chunks):
            put_zero(p).wait()

    shape3 = lambda v: v.reshape(workers, -1, ch)
    return dispatch(table, jnp.zeros((ch, width), table.dtype), shape3(dest_a), shape3(dest_b), shape3(dest_pad))


def _routing(route, tile):
    B, T, _ = route.shape
    n = B * T
    assert (2 * n) % tile == 0
    ids = route[..., :2].astype(jnp.int32).reshape(n, 2)
    e_cat = jnp.concatenate([ids[:, 0], ids[:, 1]])
    onehot = (e_cat[:, None] == jnp.arange(N_EXPERTS, dtype=jnp.int32)[None, :]).astype(jnp.int32)
    csum = jnp.cumsum(onehot, axis=0)
    pos = jnp.sum(onehot * csum, axis=1) - 1
    counts = csum[-1]
    padded = (counts + tile - 1) // tile * tile
    ends = jnp.cumsum(padded)
    starts = ends - padded
    dest = (starts[e_cat] + pos).astype(jnp.int32)
    n_pad = N_EXPERTS * tile
    n_rows = 2 * n + n_pad
    seg_first = jnp.concatenate([starts + counts, ends[-1:]])
    seg_size = jnp.concatenate([padded - counts, (n_rows - ends[-1])[None]])
    seg_end = jnp.cumsum(seg_size)
    j = jnp.arange(n_pad, dtype=jnp.int32)
    seg = jnp.sum((j[:, None] >= seg_end[None, :]).astype(jnp.int32), axis=1)
    pad_rows = (seg_first[seg] + j - (seg_end - seg_size)[seg]).astype(jnp.int32)
    tile_start = jnp.arange(n_rows // tile, dtype=jnp.int32) * tile
    tile_expert = jnp.minimum(jnp.searchsorted(ends, tile_start, side="right"), N_EXPERTS - 1).astype(jnp.int32)
    tile_valid = (tile_start < ends[-1]).astype(jnp.int32)
    return dest, pad_rows, n_rows, tile_expert, tile_valid


def _expert_kernel(te_ref, tv_ref, xs_ref, wgu_ref, wd_ref, ys_ref, wgu_scr):
    i = pl.program_id(0)
    F = wd_ref.shape[0]

    @pl.when((i == 0) | (te_ref[i] != te_ref[jnp.maximum(i - 1, 0)]))
    def _new_expert():
        wgu_scr[...] = wgu_ref[...].astype(BF16)

    @pl.when(tv_ref[i] > 0)
    def _compute():
        lo, hi = _unpack_bf16_pairs(_load_packed(xs_ref))
        gu = (jnp.dot(lo.astype(BF16), wgu_scr[:PACK_W, :], preferred_element_type=F32)
              + jnp.dot(hi.astype(BF16), wgu_scr[PACK_W:, :], preferred_element_type=F32))
        act = (_silu(gu[:, :F]) * gu[:, F:]).astype(BF16)
        y = jnp.dot(act, wd_ref[...], preferred_element_type=F32)
        _store_packed(ys_ref, _pack_bf16_pairs(y))

    @pl.when(tv_ref[i] == 0)
    def _unused_tile():
        ys_ref[...] = jnp.zeros_like(ys_ref)


def _experts(xs, tile_expert, tile_valid, w_gu, w_down, layer):
    _, n_rows, _ = xs.shape
    _, _, F, D = w_down.shape
    tile = EXPERT_TILE
    rows = pl.BlockSpec((PACK_CHUNKS, tile, LANES), lambda i, te, tv: (0, i, 0))
    return pl.pallas_call(
        _expert_kernel,
        grid_spec=pltpu.PrefetchScalarGridSpec(
            num_scalar_prefetch=2,
            grid=(n_rows // tile,),
            in_specs=[rows,
                      pl.BlockSpec((None, None, D, 2 * F), lambda i, te, tv: (layer, te[i], 0, 0)),
                      pl.BlockSpec((None, None, F, D), lambda i, te, tv: (layer, te[i], 0, 0))],
            out_specs=rows,
            scratch_shapes=[pltpu.VMEM((D, 2 * F), BF16)]),
        out_shape=jax.ShapeDtypeStruct(xs.shape, jnp.int32),
        compiler_params=_params(("arbitrary",)),
    )(tile_expert, tile_valid, xs, w_gu, w_down)


def _combine_kernel(*refs, tm, ctx_len, skip, final):
    if final:
        x_ref, g2_ref, cg2_ref, route_ref, ya_ref, yb_ref, fg_ref, o_ref = refs
    else:
        x_ref, g2_ref, cg2_ref, route_ref, ya_ref, yb_ref, o_ref = refs
    is_ctx = _row_is_ctx(pl.program_id(1) + skip, tm, ctx_len)
    route = route_ref[...]
    lane = lax.broadcasted_iota(jnp.int32, route.shape, 1)
    gate_a = jnp.sum(jnp.where(lane == 2, route, 0.0), axis=-1, keepdims=True)
    gate_b = jnp.sum(jnp.where(lane == 3, route, 0.0), axis=-1, keepdims=True)
    unpack = lambda ref: jnp.concatenate(_unpack_bf16_pairs(_load_packed(ref)), axis=1)
    f = gate_a * unpack(ya_ref) + gate_b * unpack(yb_ref)
    x = x_ref[...] + jnp.where(is_ctx, cg2_ref[...], g2_ref[...]) * f
    if final:
        x = x * lax.rsqrt(jnp.mean(x * x, axis=-1, keepdims=True) + EPS) * fg_ref[...]
    o_ref[...] = x


def _combine(xt, mod_l, mod_c, route, yg, ctx_len, final_g=None):
    B, T, D = xt.shape
    final = final_g is not None
    if final:
        tm = SEQ_TILE
        assert ctx_len % tm == 0 and T % tm == 0
        skip = ctx_len // tm
    else:
        tm = _pick_tile(T, 640)
        skip = 0
    nt = T // tm
    row = lambda w: pl.BlockSpec((None, tm, w), lambda b, j: (b, j + skip, 0))
    slot = lambda s: pl.BlockSpec((PACK_CHUNKS, None, tm, LANES), lambda b, j: (0, s, b * nt + j + skip, 0))
    in_specs = [row(D), *_mod_specs(5, D), row(LANES), slot(0), slot(1)]
    args = [xt, mod_l, mod_c, route, yg, yg]
    if final:
        in_specs.append(pl.BlockSpec((1, D), lambda b, j: (0, 0)))
        args.append(final_g)
    return pl.pallas_call(
        functools.partial(_combine_kernel, tm=tm, ctx_len=ctx_len, skip=skip, final=final),
        grid=(B, nt - skip),
        in_specs=in_specs,
        out_specs=pl.BlockSpec((None, tm, D), lambda b, j: (b, j, 0)),
        out_shape=jax.ShapeDtypeStruct((B, T - skip * tm, D), F32),
        compiler_params=_params(("parallel", "parallel")),
    )(*args)


def _moe(hp, route, xt, mod_l, mod_c, w_gu, w_down, layer, ctx_len, final_g=None):
    B, T, _ = xt.shape
    n = B * T
    dest, pad_rows, n_rows, tile_expert, tile_valid = _routing(route, EXPERT_TILE)
    chunk = jnp.arange(PACK_CHUNKS, dtype=jnp.int32)[:, None] * n_rows
    xs = _sc_dispatch(hp.reshape(PACK_CHUNKS * n, LANES), (chunk + dest[None, :n]).reshape(-1),
                      (chunk + dest[None, n:]).reshape(-1), (chunk + pad_rows[None, :]).reshape(-1),
                      PACK_CHUNKS * n_rows)
    ys = _experts(xs.reshape(PACK_CHUNKS, n_rows, LANES), tile_expert, tile_valid, w_gu, w_down, layer)
    yg = _sc_gather(ys.reshape(PACK_CHUNKS * n_rows, LANES), (chunk + dest[None, :]).reshape(-1))
    return _combine(xt, mod_l, mod_c, route, yg.reshape(PACK_CHUNKS, 2, n, LANES), ctx_len, final_g)


def _final_norm_kernel(x_ref, g_ref, o_ref):
    x = x_ref[...]
    o_ref[...] = x * lax.rsqrt(jnp.mean(x * x, axis=-1, keepdims=True) + EPS) * g_ref[...]


def _final_norm(xt, g, ctx_len):
    B, T, D = xt.shape
    S = T - ctx_len
    tm = SEQ_TILE
    assert ctx_len % tm == 0 and S % tm == 0
    skip = ctx_len // tm
    return pl.pallas_call(
        _final_norm_kernel,
        grid=(B, S // tm),
        in_specs=[pl.BlockSpec((None, tm, D), lambda b, j: (b, j + skip, 0)),
                  pl.BlockSpec((1, D), lambda b, j: (0, 0))],
        out_specs=pl.BlockSpec((None, tm, D), lambda b, j: (b, j, 0)),
        out_shape=jax.ShapeDtypeStruct((B, S, D), F32),
        compiler_params=_params(("parallel", "parallel")),
    )(xt, g)


def _rope_tables(seq, ctx_len):
    pos = jnp.arange(seq, dtype=jnp.int32)
    nf = HEAD_DIM // 4
    inv = ROPE_THETA ** (-jnp.arange(nf, dtype=F32) / nf)
    ang = jnp.concatenate([(pos // GRID_W).astype(F32)[:, None] * inv, (pos % GRID_W).astype(F32)[:, None] * inv], axis=-1)
    cos, sin = jnp.cos(ang), jnp.sin(ang)
    reps = LANES // HEAD_DIM
    cos_t = jnp.tile(jnp.concatenate([cos, cos], axis=-1), (1, reps))
    sin_t = jnp.tile(jnp.concatenate([-sin, sin], axis=-1), (1, reps))
    cos_t = jnp.concatenate([jnp.ones((ctx_len, LANES), F32), cos_t], axis=0)
    sin_t = jnp.concatenate([jnp.zeros((ctx_len, LANES), F32), sin_t], axis=0)
    return cos_t, sin_t


def _relayout_w_in(w_in):
    o_lr = ATT_W + 2 * KV_W + 5 * GLA_W
    q_scale = HEAD_DIM ** -0.5
    gq_scale = GLA_DK ** -0.5
    parts = [w_in[..., :ATT_W] * q_scale, w_in[..., ATT_W:COL_GQ], w_in[..., COL_GQ:COL_GK] * gq_scale,
             w_in[..., COL_GK:o_lr], w_in[..., o_lr + 2 * GLA_LOWRANK:], w_in[..., o_lr:o_lr + 2 * GLA_LOWRANK],
             jnp.zeros(w_in.shape[:-1] + (LANES - 2 * GLA_LOWRANK,), w_in.dtype)]
    return jnp.concatenate(parts, axis=-1).astype(BF16)


def kernel(x, c, ctx, c_ctx, w_mod, b_mod, norm1_g, norm2_g, w_in, w_out, attn_sink, gla_w2_f, gla_b_f, gla_w2_b, gla_b_b, gla_norm_g, conv_w, ffn_w_gu, ffn_w_down, router_w, expert_w_gu, expert_w_down, final_norm_g):
    B, S, D = x.shape
    L = ctx.shape[1]
    depth = w_in.shape[0]
    assert D == D_MODEL and S % GRID_W == 0

    xt = jnp.concatenate([ctx, x], axis=1)
    rows = 16
    cc = jnp.concatenate([c, c_ctx[None, :], jnp.zeros((rows - B - 1, D), F32)], axis=0)
    mod = _adaln(cc, w_mod, b_mod)
    cos_t, sin_t = _rope_tables(S, L)

    w_in_p = _relayout_w_in(w_in)
    w_out_b = _to_bf16(w_out)
    zpad = jnp.zeros((depth, LANES - 2 * GLA_LOWRANK, GLA_W), F32)
    w2_f = jnp.concatenate([gla_w2_f, jnp.zeros_like(gla_w2_b), zpad], axis=1).astype(BF16)
    w2_b = jnp.concatenate([jnp.zeros_like(gla_w2_f), gla_w2_b, zpad], axis=1).astype(BF16)
    gla_ng = jnp.tile(gla_norm_g, (1, GLA_HEADS))
    ffn_gu_b = _to_bf16(ffn_w_gu)
    ffn_down_b = _to_bf16(ffn_w_down)
    exp_down_b = _to_bf16(expert_w_down)
    router_b = jnp.pad(router_w, ((0, 0), (0, 0), (0, LANES - N_EXPERTS))).astype(BF16)

    for l in range(depth):
        mod_l = mod[l, :B].reshape(B, 1, 6 * D)
        mod_c = mod[l, B:B + 1]
        proj = _inproj(xt, norm1_g[l][None, :], mod_l, mod_c, cos_t, sin_t, w_in_p, l, L)
        att = _attention(proj, attn_sink[l], L)
        cv = _short_conv(proj, conv_w[l], L)
        gf = _gla(proj, w2_f[l], gla_b_f[l][None, :], gla_ng[l][None, :], L, reverse=False)
        gb = _gla(proj, w2_b[l], gla_b_b[l][None, :], gla_ng[l][None, :], L, reverse=True)
        if l % 2 == 0:
            xt, h2 = _outproj(xt, att, gf, gb, cv, w_out_b, l, norm2_g[l][None, :], mod_l, mod_c, L)
            xt = _ffn(h2, xt, mod_l, mod_c, ffn_gu_b, ffn_down_b, l // 2, L)
        else:
            xt, hp, route = _outproj(xt, att, gf, gb, cv, w_out_b, l, norm2_g[l][None, :], mod_l, mod_c, L,
                                     router=(router_b, l // 2))
            xt = _moe(hp, route, xt, mod_l, mod_c, expert_w_gu, exp_down_b, l // 2, L,
                      final_g=final_norm_g[None, :] if l == depth - 1 else None)
    return xt if depth % 2 == 0 else _final_norm(xt, final_norm_g[None, :], L)
```

```python
import functools

import jax
import jax.numpy as jnp
from jax import lax
from jax.experimental import pallas as pl
from jax.experimental.pallas import tpu as pltpu
from jax.experimental.pallas import tpu_sc as plsc

F32 = jnp.float32
BF16 = jnp.bfloat16

D_MODEL = 1024
EPS = 1e-6
GRID_W = 64
ROPE_THETA = 10000.0
HEAD_DIM = 64
ATT_HEADS = 8
ATT_KV_HEADS = 2
ATT_GROUP = ATT_HEADS // ATT_KV_HEADS
ATT_W = ATT_HEADS * HEAD_DIM
KV_W = ATT_KV_HEADS * HEAD_DIM
WINDOW = 128
GLA_HEADS = 4
GLA_DK = 64
GLA_W = GLA_HEADS * GLA_DK
GLA_LOWRANK = 16
GLA_TAU = 16.0
GLA_CHUNK = 64
CONV_CH = 256
N_EXPERTS = 8

LANES = 128
SUBLANES_BF16 = 16

COL_Q = 0
COL_K = ATT_W
COL_V = COL_K + KV_W
ROPE_W = COL_V
COL_GQ = COL_V + KV_W
COL_GK = COL_GQ + GLA_W
COL_GV = COL_GK + GLA_W
COL_GGF = COL_GV + GLA_W
COL_GGB = COL_GGF + GLA_W
COL_CB = COL_GGB + GLA_W
COL_CC = COL_CB + CONV_CH
COL_CX = COL_CC + CONV_CH
COL_LR = COL_CX + CONV_CH
PROJ_W = COL_LR + LANES

SEQ_TILE = 256
VMEM_LIMIT = 56 * 1024 * 1024
CAST_BLOCK_BYTES = 4 * 1024 * 1024

PACK_W = D_MODEL // 2
PACK_CHUNKS = PACK_W // LANES
EXPERT_TILE = 512
SC_CORES = 2
SC_SUBCORES = 16
SC_INDEX_CHUNK = 128


def _silu(v):
    return v / (1.0 + jnp.exp(-v))


def _pick_tile(total, cap):
    best = None
    for t in range(SUBLANES_BF16, cap + 1, SUBLANES_BF16):
        if total % t == 0:
            best = t
    assert best is not None
    return best


def _params(sem):
    return pltpu.CompilerParams(dimension_semantics=sem, vmem_limit_bytes=VMEM_LIMIT)


def _pack_bf16_pairs(v):
    bits = lambda t: lax.bitcast_convert_type(t.astype(BF16).astype(F32), jnp.int32)
    return ((bits(v[:, :PACK_W]) >> 16) & 0xFFFF) | (bits(v[:, PACK_W:]) & -65536)


def _unpack_bf16_pairs(w):
    return lax.bitcast_convert_type(w << 16, F32), lax.bitcast_convert_type(w & -65536, F32)


def _cast_kernel(x_ref, o_ref):
    o_ref[...] = x_ref[...].astype(BF16)


def _to_bf16(w):
    cols = w.shape[-1]
    rows = w.size // cols
    tr = _pick_tile(rows, max(SUBLANES_BF16, CAST_BLOCK_BYTES // (4 * cols)))
    out = pl.pallas_call(
        _cast_kernel,
        grid=(rows // tr,),
        in_specs=[pl.BlockSpec((tr, cols), lambda i: (i, 0))],
        out_specs=pl.BlockSpec((tr, cols), lambda i: (i, 0)),
        out_shape=jax.ShapeDtypeStruct((rows, cols), BF16),
        compiler_params=_params(("parallel",)),
    )(w.reshape(rows, cols))
    return out.reshape(w.shape)


def _load_packed(ref):
    return jnp.concatenate([ref[k] for k in range(PACK_CHUNKS)], axis=1)


def _store_packed(ref, packed):
    for k in range(PACK_CHUNKS):
        ref[k] = packed[:, k * LANES:(k + 1) * LANES]


def _adaln_kernel(c_ref, w_ref, b_ref, o_ref):
    s = _silu(c_ref[...]).astype(BF16)
    o_ref[...] = jnp.dot(s, w_ref[...].astype(BF16), preferred_element_type=F32) + b_ref[...]


def _adaln(cc, w_mod, b_mod):
    depth, d, n = w_mod.shape
    tn = 1024
    rows = cc.shape[0]
    return pl.pallas_call(
        _adaln_kernel,
        grid=(depth, n // tn),
        in_specs=[pl.BlockSpec((rows, d), lambda l, j: (0, 0)),
                  pl.BlockSpec((None, d, tn), lambda l, j: (l, 0, j)),
                  pl.BlockSpec((None, 1, tn), lambda l, j: (l, 0, j))],
        out_specs=pl.BlockSpec((None, rows, tn), lambda l, j: (l, 0, j)),
        out_shape=jax.ShapeDtypeStruct((depth, rows, n), F32),
        compiler_params=_params(("parallel", "parallel")),
    )(cc, w_mod, b_mod.reshape(depth, 1, n))


def _mod_specs(k, d):
    return [pl.BlockSpec((None, 1, d), lambda b, j, *_: (b, 0, k)),
            pl.BlockSpec((1, d), lambda b, j, *_: (0, k))]


def _row_is_ctx(j, tm, ctx_len):
    return (j * tm + lax.broadcasted_iota(jnp.int32, (tm, 1), 0)) < ctx_len


def _inproj_kernel(x_ref, g_ref, sc_ref, csc_ref, sh_ref, csh_ref, cos_ref, sin_ref, w_ref, o_ref, *, tm, ctx_len):
    x = x_ref[...]
    n = x * lax.rsqrt(jnp.mean(x * x, axis=-1, keepdims=True) + EPS) * g_ref[...]
    is_ctx = _row_is_ctx(pl.program_id(1), tm, ctx_len)
    scale = jnp.where(is_ctx, csc_ref[...], sc_ref[...])
    shift = jnp.where(is_ctx, csh_ref[...], sh_ref[...])
    h = (n * (1.0 + scale) + shift).astype(BF16)
    qk = jnp.dot(h, w_ref[:, :ROPE_W], preferred_element_type=F32)
    lower_half = (lax.broadcasted_iota(jnp.int32, (tm, LANES), 1) % HEAD_DIM) < (HEAD_DIM // 2)
    cos = cos_ref[...]
    sin = sin_ref[...]
    for i in range(ROPE_W // LANES):
        t = qk[:, i * LANES:(i + 1) * LANES]
        partner = jnp.where(lower_half, pltpu.roll(t, LANES - HEAD_DIM // 2, axis=1), pltpu.roll(t, HEAD_DIM // 2, axis=1))
        o_ref[:, i * LANES:(i + 1) * LANES] = (t * cos + partner * sin).astype(BF16)
    o_ref[:, ROPE_W:] = jnp.dot(h, w_ref[:, ROPE_W:], preferred_element_type=F32).astype(BF16)


def _inproj(xt, g, mod_l, mod_c, cos_t, sin_t, w, layer, ctx_len):
    B, T, D = xt.shape
    tm = _pick_tile(T, 640)
    kern = functools.partial(_inproj_kernel, tm=tm, ctx_len=ctx_len)
    return pl.pallas_call(
        kern,
        grid=(B, T // tm),
        in_specs=[pl.BlockSpec((None, tm, D), lambda b, j: (b, j, 0)),
                  pl.BlockSpec((1, D), lambda b, j: (0, 0)),
                  *_mod_specs(1, D), *_mod_specs(0, D),
                  pl.BlockSpec((tm, LANES), lambda b, j: (j, 0)),
                  pl.BlockSpec((tm, LANES), lambda b, j: (j, 0)),
                  pl.BlockSpec((None, D, PROJ_W), lambda b, j: (layer, 0, 0))],
        out_specs=pl.BlockSpec((None, tm, PROJ_W), lambda b, j: (b, j, 0)),
        out_shape=jax.ShapeDtypeStruct((B, T, PROJ_W), BF16),
        compiler_params=_params(("parallel", "parallel")),
    )(xt, g, mod_l, mod_c, mod_l, mod_c, cos_t, sin_t, w)


def _attn_kernel(sink_ref, q_ref, kc_ref, vc_ref, km_ref, vm_ref, kp_ref, vp_ref, kn_ref, vn_ref, o_ref, *, n_tiles):
    j = pl.program_id(1)
    nt_dims = (((1,), (1,)), ((), ()))

    def scores(rows, n_rows, g, k_ctx, k_win=None, mask_win=None):
        heads = [g * ATT_GROUP + hh for hh in range(ATT_GROUP)]
        q = jnp.concatenate([q_ref[rows, h * HEAD_DIM:(h + 1) * HEAD_DIM] for h in heads], axis=0)
        head_row = lax.broadcasted_iota(jnp.int32, (ATT_GROUP * n_rows, 1), 0) // n_rows
        sink = jnp.full((ATT_GROUP * n_rows, 1), sink_ref[heads[0]], F32)
        for hh in range(1, ATT_GROUP):
            sink = jnp.where(head_row == hh, sink_ref[heads[hh]], sink)
        s_c = lax.dot_general(q, k_ctx, nt_dims, preferred_element_type=F32)
        s_w = None
        if k_win is not None:
            s_w = jnp.where(mask_win, lax.dot_general(q, k_win, nt_dims, preferred_element_type=F32), -1e30)
        return rows, n_rows, heads, sink, s_c, s_w

    def finish(item, v_ctx, v_win=None):
        rows, n_rows, heads, sink, s_c, s_w = item
        m = jnp.maximum(jnp.max(s_c, axis=-1, keepdims=True), sink)
        if s_w is not None:
            m = jnp.maximum(m, jnp.max(s_w, axis=-1, keepdims=True))
        p_c = jnp.exp(s_c - m)
        den = jnp.sum(p_c, axis=-1, keepdims=True) + jnp.exp(sink - m)
        o = jnp.dot(p_c.astype(BF16), v_ctx, preferred_element_type=F32)
        if s_w is not None:
            p_w = jnp.exp(s_w - m)
            den += jnp.sum(p_w, axis=-1, keepdims=True)
            o += jnp.dot(p_w.astype(BF16), v_win, preferred_element_type=F32)
        out = o / den
        for hh, h in enumerate(heads):
            o_ref[rows, h * HEAD_DIM:(h + 1) * HEAD_DIM] = out[hh * n_rows:(hh + 1) * n_rows].astype(BF16)

    def pipelined(work):
        pending = None
        for score_fn, finish_fn in work:
            item = score_fn()
            if pending is not None:
                pending[1](pending[0])
            pending = (item, finish_fn)
        pending[1](pending[0])

    @pl.when(j == 0)
    def _context_queries():
        work = []
        for g in range(ATT_KV_HEADS):
            gs = slice(g * HEAD_DIM, (g + 1) * HEAD_DIM)
            work.append((functools.partial(scores, slice(0, SEQ_TILE), SEQ_TILE, g, kc_ref[:, gs]),
                         functools.partial(finish, v_ctx=vc_ref[:, gs])))
        pipelined(work)

    @pl.when(j > 0)
    def _latent_queries():
        r = lax.broadcasted_iota(jnp.int32, (ATT_GROUP * WINDOW, 3 * WINDOW), 0) % WINDOW
        w = lax.broadcasted_iota(jnp.int32, (ATT_GROUP * WINDOW, 3 * WINDOW), 1)
        band = (w >= r) & (w <= r + 2 * WINDOW)
        work = []
        for sub in range(SEQ_TILE // WINDOW):
            rows = slice(sub * WINDOW, (sub + 1) * WINDOW)
            if sub == 0:
                kprev, vprev = kp_ref[...], vp_ref[...]
                kmid, vmid = km_ref[:WINDOW], vm_ref[:WINDOW]
                knext, vnext = km_ref[WINDOW:], vm_ref[WINDOW:]
                w_lo = jnp.where(j > 1, 0, WINDOW)
                w_hi = 3 * WINDOW
            else:
                kprev, vprev = km_ref[:WINDOW], vm_ref[:WINDOW]
                kmid, vmid = km_ref[WINDOW:], vm_ref[WINDOW:]
                knext, vnext = kn_ref[...], vn_ref[...]
                w_lo = 0
                w_hi = jnp.where(j < n_tiles - 1, 3 * WINDOW, 2 * WINDOW)
            mask = band & (w >= w_lo) & (w < w_hi)
            for g in range(ATT_KV_HEADS):
                gs = slice(g * HEAD_DIM, (g + 1) * HEAD_DIM)
                k_win = jnp.concatenate([kprev[:, gs], kmid[:, gs], knext[:, gs]], axis=0)
                v_win = jnp.concatenate([vprev[:, gs], vmid[:, gs], vnext[:, gs]], axis=0)
                work.append((functools.partial(scores, rows, WINDOW, g, kc_ref[:, gs], k_win, mask),
                             functools.partial(finish, v_ctx=vc_ref[:, gs], v_win=v_win)))
        pipelined(work)


def _attention(proj, sink, ctx_len):
    B, T, _ = proj.shape
    assert ctx_len == SEQ_TILE and T % SEQ_TILE == 0
    n_tiles = T // SEQ_TILE
    n_win = T // WINDOW
    kcol, vcol = COL_K // KV_W, COL_V // KV_W
    per_tile = SEQ_TILE // WINDOW
    tile = lambda col: pl.BlockSpec((None, SEQ_TILE, KV_W), lambda b, j: (b, j, col))
    ctx = lambda col: pl.BlockSpec((None, SEQ_TILE, KV_W), lambda b, j: (b, 0, col))
    prev = lambda col: pl.BlockSpec((None, WINDOW, KV_W), lambda b, j: (b, jnp.maximum(j * per_tile - 1, 0), col))
    nxt = lambda col: pl.BlockSpec((None, WINDOW, KV_W), lambda b, j: (b, jnp.minimum((j + 1) * per_tile, n_win - 1), col))
    return pl.pallas_call(
        functools.partial(_attn_kernel, n_tiles=n_tiles),
        grid=(B, n_tiles),
        in_specs=[pl.BlockSpec(memory_space=pltpu.SMEM),
                  pl.BlockSpec((None, SEQ_TILE, ATT_W), lambda b, j: (b, j, 0)),
                  ctx(kcol), ctx(vcol), tile(kcol), tile(vcol), prev(kcol), prev(vcol), nxt(kcol), nxt(vcol)],
        out_specs=pl.BlockSpec((None, SEQ_TILE, ATT_W), lambda b, j: (b, j, 0)),
        out_shape=jax.ShapeDtypeStruct((B, T, ATT_W), BF16),
        compiler_params=_params(("parallel", "parallel")),
    )(sink, *([proj] * 9))


def _conv_kernel(b_ref, c_ref, x_ref, w_ref, o_ref, u_scr, *, ctx_len):
    T = c_ref.shape[0]
    pad = 8
    u = c_ref[...].astype(F32) * x_ref[...].astype(F32)
    zero_row = jnp.zeros((1, CONV_CH), F32)
    u_scr[pad - 1:pad, :] = zero_row
    u_scr[pad:pad + T, :] = u
    u_scr[pad + T:pad + T + 1, :] = zero_row
    row = lax.broadcasted_iota(jnp.int32, (T, 1), 0)
    u_prev = jnp.where(row == ctx_len, 0.0, u_scr[pad - 1:pad - 1 + T, :])
    u_next = jnp.where(row == ctx_len - 1, 0.0, u_scr[pad + 1:pad + 1 + T, :])
    y = w_ref[0:1, :] * u_prev + w_ref[1:2, :] * u + w_ref[2:3, :] * u_next
    o_ref[...] = (b_ref[...].astype(F32) * y).astype(BF16)


def _short_conv(proj, conv_w, ctx_len):
    B, T, _ = proj.shape
    col = lambda c0: pl.BlockSpec((None, T, CONV_CH), lambda b: (b, 0, c0 // CONV_CH))
    return pl.pallas_call(
        functools.partial(_conv_kernel, ctx_len=ctx_len),
        grid=(B,),
        in_specs=[col(COL_CB), col(COL_CC), col(COL_CX), pl.BlockSpec((3, CONV_CH), lambda b: (0, 0))],
        out_specs=pl.BlockSpec((None, T, CONV_CH), lambda b: (b, 0, 0)),
        out_shape=jax.ShapeDtypeStruct((B, T, CONV_CH), BF16),
        scratch_shapes=[pltpu.VMEM((T + 16, CONV_CH), F32)],
        compiler_params=_params(("parallel",)),
    )(proj, proj, proj, conv_w)


def _split2(v):
    hi = v.astype(BF16)
    lo = (v - hi.astype(F32)).astype(BF16)
    return hi, lo


def _gla_kernel(q_ref, k_ref, v_ref, gate_ref, lr_ref, w2_ref, bias_ref, ng_ref, o_ref,
                oacc_scr, qs_scr, u_scr, dec_scr, sin_scr, *, reverse, n_ctx_chunks):
    T = q_ref.shape[0]
    C = GLA_CHUNK
    n_tiles = T // SEQ_TILE
    n_chunks = T // C
    per_tile = SEQ_TILE // C
    nt_dims = (((1,), (1,)), ((), ()))
    tn_dims = (((0,), (0,)), ((), ()))

    r = lax.broadcasted_iota(jnp.int32, (SEQ_TILE, SEQ_TILE), 0)
    c = lax.broadcasted_iota(jnp.int32, (SEQ_TILE, SEQ_TILE), 1)
    same_chunk = (r // C) == (c // C)
    causal = same_chunk & ((c >= r) if reverse else (c <= r))
    cum_mat = jnp.where(causal, 1.0, 0.0).astype(BF16)
    same_head = (r // GLA_DK) == (c // GLA_DK)
    head_mean = jnp.where(same_head, 1.0 / GLA_DK, 0.0).astype(BF16)
    lane_head = lax.broadcasted_iota(jnp.int32, (C, GLA_W), 1) // GLA_DK

    def phase1(i, carry):
        rows = pl.ds(pl.multiple_of(i * SEQ_TILE, SEQ_TILE), SEQ_TILE)
        z = jnp.dot(lr_ref[rows, :], w2_ref[...], preferred_element_type=F32) + bias_ref[...]
        la = (jnp.minimum(z, 0.0) - jnp.log(1.0 + jnp.exp(-jnp.abs(z)))) / GLA_TAU
        hi, lo = _split2(la)
        b = jnp.dot(cum_mat, hi, preferred_element_type=F32) + jnp.dot(cum_mat, lo, preferred_element_type=F32)
        b3 = b.reshape(per_tile, C, GLA_W)
        total = b3[:, 0:1, :] if reverse else b3[:, C - 1:C, :]
        b_last = jnp.broadcast_to(total, (per_tile, C, GLA_W)).reshape(SEQ_TILE, GLA_W)
        b_ref = 0.5 * b_last
        q = q_ref[rows, :].astype(F32)
        k = k_ref[rows, :].astype(F32)
        v = v_ref[rows, :]
        e_fwd = jnp.exp(b - b_ref)
        e_bwd = jnp.exp(b_ref - b)
        e_half = jnp.exp(b_ref)
        qe = (q * e_fwd).astype(BF16)
        ke = (k * e_bwd).astype(BF16)
        ku = (k * (e_bwd * e_half)).astype(BF16)
        qs_scr[rows, :] = (q * (e_fwd * e_half)).astype(BF16)
        dec = e_half * e_half
        for h in range(GLA_HEADS):
            hs = slice(h * GLA_DK, (h + 1) * GLA_DK)
            s = lax.dot_general(qe[:, hs], ke[:, hs], nt_dims, preferred_element_type=F32)
            a = jnp.where(causal, s, 0.0).astype(BF16)
            oacc_scr[rows, hs] = jnp.dot(a, v[:, hs], preferred_element_type=F32)
        for ci in range(per_tile):
            cs = slice(ci * C, (ci + 1) * C)
            full = lax.dot_general(v[cs, :], ku[cs, :], tn_dims, preferred_element_type=F32)
            ut = full[0:C, :]
            for h in range(1, GLA_HEADS):
                ut = jnp.where(lane_head == h, full[h * C:(h + 1) * C, :], ut)
            u_scr[i * per_tile + ci] = ut
            dec_scr[i * per_tile + ci] = dec[ci * C:ci * C + 1, :]
        return carry

    lax.fori_loop(0, n_tiles, phase1, 0, unroll=2)

    def phase2(i, st):
        if reverse:
            ci = jnp.where(i < n_ctx_chunks, n_ctx_chunks - 1 - i, n_chunks - 1 + n_ctx_chunks - i)
        else:
            ci = i
        sin_scr[ci] = st.astype(BF16)
        return st * dec_scr[ci] + u_scr[ci]

    lax.fori_loop(0, n_chunks, phase2, jnp.zeros((C, GLA_W), F32))

    def phase3(i, carry):
        rows = pl.ds(pl.multiple_of(i * SEQ_TILE, SEQ_TILE), SEQ_TILE)
        qs = qs_scr[rows, :]
        parts = []
        for ci in range(per_tile):
            st = sin_scr[i * per_tile + ci]
            st_heads = jnp.where(same_head, jnp.concatenate([st] * GLA_HEADS, axis=0), jnp.zeros((), BF16))
            parts.append(lax.dot_general(qs[ci * C:(ci + 1) * C, :], st_heads, nt_dims, preferred_element_type=F32))
        o = oacc_scr[rows, :] + jnp.concatenate(parts, axis=0)
        ms = jnp.dot((o * o).astype(BF16), head_mean, preferred_element_type=F32)
        y = o * lax.rsqrt(ms + EPS) * ng_ref[...]
        o_ref[rows, :] = (y * _silu(gate_ref[rows, :].astype(F32))).astype(BF16)
        return carry

    lax.fori_loop(0, n_tiles, phase3, 0, unroll=2)


def _gla(proj, w2pad, bias, ng, ctx_len, reverse):
    B, T, _ = proj.shape
    assert T % SEQ_TILE == 0 and ctx_len % GLA_CHUNK == 0
    n_chunks = T // GLA_CHUNK
    col = lambda c0: pl.BlockSpec((None, T, GLA_W), lambda b: (b, 0, c0 // GLA_W))
    gate_col = COL_GGB if reverse else COL_GGF
    kern = functools.partial(_gla_kernel, reverse=reverse, n_ctx_chunks=ctx_len // GLA_CHUNK)
    return pl.pallas_call(
        kern,
        grid=(B,),
        in_specs=[col(COL_GQ), col(COL_GK), col(COL_GV), col(gate_col),
                  pl.BlockSpec((None, T, LANES), lambda b: (b, 0, COL_LR // LANES)),
                  pl.BlockSpec((LANES, GLA_W), lambda b: (0, 0)),
                  pl.BlockSpec((1, GLA_W), lambda b: (0, 0)),
                  pl.BlockSpec((1, GLA_W), lambda b: (0, 0))],
        out_specs=pl.BlockSpec((None, T, GLA_W), lambda b: (b, 0, 0)),
        out_shape=jax.ShapeDtypeStruct((B, T, GLA_W), BF16),
        scratch_shapes=[pltpu.VMEM((T, GLA_W), F32),
                        pltpu.VMEM((T, GLA_W), BF16),
                        pltpu.VMEM((n_chunks, GLA_CHUNK, GLA_W), F32),
                        pltpu.VMEM((n_chunks, 1, GLA_W), F32),
                        pltpu.VMEM((n_chunks, GLA_CHUNK, GLA_W), BF16)],
        compiler_params=_params(("parallel",)),
    )(proj, proj, proj, proj, proj, w2pad, bias, ng)


MXU_TILE = 256
FFN_CHUNK = 3 * MXU_TILE


def _swiglu_resident(h, wgu_ref, wd_ref):
    F = wd_ref.shape[0]
    y = None
    for c0 in range(0, F, FFN_CHUNK):
        c1 = min(c0 + FFN_CHUNK, F)
        g = jnp.dot(h, wgu_ref[:, c0:c1], preferred_element_type=F32)
        u = jnp.dot(h, wgu_ref[:, F + c0:F + c1], preferred_element_type=F32)
        part = jnp.dot((_silu(g) * u).astype(BF16), wd_ref[c0:c1, :], preferred_element_type=F32)
        y = part if y is None else y + part
    return y


def _outproj_kernel(*refs, tm, ctx_len, moe):
    if moe:
        (x_ref, att_ref, gf_ref, gb_ref, cv_ref, w_ref, g1_ref, cg1_ref, ng_ref, sc_ref, csc_ref, sh_ref, csh_ref,
         router_ref, xo_ref, hp_ref, route_ref) = refs
    else:
        (x_ref, att_ref, gf_ref, gb_ref, cv_ref, w_ref, g1_ref, cg1_ref, ng_ref, sc_ref, csc_ref, sh_ref, csh_ref,
         g2_ref, cg2_ref, wgu_ref, wd_ref, xo_ref) = refs
    is_ctx = _row_is_ctx(pl.program_id(1), tm, ctx_len)
    gla = (gf_ref[...].astype(F32) + gb_ref[...].astype(F32)).astype(BF16)
    y = jnp.dot(att_ref[...], w_ref[:ATT_W, :], preferred_element_type=F32)
    y += jnp.dot(gla, w_ref[ATT_W:ATT_W + GLA_W, :], preferred_element_type=F32)
    y += jnp.dot(cv_ref[...], w_ref[ATT_W + GLA_W:, :], preferred_element_type=F32)
    x = x_ref[...] + jnp.where(is_ctx, cg1_ref[...], g1_ref[...]) * y
    if moe:
        xo_ref[...] = x
    n = x * lax.rsqrt(jnp.mean(x * x, axis=-1, keepdims=True) + EPS) * ng_ref[...]
    scale = jnp.where(is_ctx, csc_ref[...], sc_ref[...])
    shift = jnp.where(is_ctx, csh_ref[...], sh_ref[...])
    h = n * (1.0 + scale) + shift
    hi = h.astype(BF16)
    if not moe:
        xo_ref[...] = x + jnp.where(is_ctx, cg2_ref[...], g2_ref[...]) * _swiglu_resident(hi, wgu_ref, wd_ref)
    else:
        _store_packed(hp_ref, _pack_bf16_pairs(h))
        logits = jnp.dot(hi, router_ref[...], preferred_element_type=F32)
        lane = lax.broadcasted_iota(jnp.int32, logits.shape, 1)
        neg = -jnp.inf
        lg = jnp.where(lane < N_EXPERTS, logits, neg)
        m1 = jnp.max(lg, axis=-1, keepdims=True)
        i1 = jnp.min(jnp.where(lg == m1, lane, LANES), axis=-1, keepdims=True)
        lg2 = jnp.where(lane == i1, neg, lg)
        m2 = jnp.max(lg2, axis=-1, keepdims=True)
        i2 = jnp.min(jnp.where(lg2 == m2, lane, LANES), axis=-1, keepdims=True)
        e2 = jnp.exp(m2 - m1)
        gate1 = 1.0 / (1.0 + e2)
        gate2 = e2 / (1.0 + e2)
        route_ref[...] = jnp.where(lane == 0, i1.astype(F32), jnp.where(lane == 1, i2.astype(F32),
                                   jnp.where(lane == 2, gate1, jnp.where(lane == 3, gate2, 0.0))))


def _outproj(xt, att, gf, gb, cv, w_out, layer, ng, mod_l, mod_c, ctx_len, router=None, ffn=None):
    B, T, D = xt.shape
    tm = _pick_tile(T, 640)
    moe = router is not None
    assert moe != (ffn is not None)
    row = lambda w: pl.BlockSpec((None, tm, w), lambda b, j: (b, j, 0))
    const = lambda shape: pl.BlockSpec(shape, lambda b, j: (0,) * len(shape))
    in_specs = [row(D), row(ATT_W), row(GLA_W), row(GLA_W), row(CONV_CH),
                pl.BlockSpec((None, D, D), lambda b, j: (layer, 0, 0)),
                *_mod_specs(2, D), const((1, D)), *_mod_specs(4, D), *_mod_specs(3, D)]
    args = [xt, att, gf, gb, cv, w_out, mod_l, mod_c, ng, mod_l, mod_c, mod_l, mod_c]
    nt = T // tm
    if moe:
        in_specs.append(pl.BlockSpec((None, D, LANES), lambda b, j: (router[1], 0, 0)))
        args.append(router[0])
        out_specs = [row(D), pl.BlockSpec((PACK_CHUNKS, tm, LANES), lambda b, j: (0, b * nt + j, 0)), row(LANES)]
        out_shape = [jax.ShapeDtypeStruct((B, T, D), F32), jax.ShapeDtypeStruct((PACK_CHUNKS, B * T, LANES), jnp.int32),
                     jax.ShapeDtypeStruct((B, T, LANES), F32)]
    else:
        w_gu, w_down, idx = ffn
        F = w_down.shape[1]
        assert F % MXU_TILE == 0
        resident = lambda shape: pl.BlockSpec((None,) + shape, lambda b, j: (idx, 0, 0), pipeline_mode=pl.Buffered(1))
        in_specs += [*_mod_specs(5, D), resident((D, 2 * F)), resident((F, D))]
        args += [mod_l, mod_c, w_gu, w_down]
        out_specs = row(D)
        out_shape = jax.ShapeDtypeStruct((B, T, D), F32)
    return pl.pallas_call(
        functools.partial(_outproj_kernel, tm=tm, ctx_len=ctx_len, moe=moe),
        grid=(B, T // tm),
        in_specs=in_specs, out_specs=out_specs, out_shape=out_shape,
        compiler_params=_params(("parallel", "parallel")),
    )(*args)


def _sc_gather(table, idx):
    _, width = table.shape
    m = idx.shape[0]
    workers = SC_CORES * SC_SUBCORES
    ch = SC_INDEX_CHUNK
    assert width == LANES and m % (workers * ch) == 0
    n_chunks = m // (workers * ch)
    nb = max(b for b in (4, 3, 2, 1) if n_chunks % b == 0)
    mesh = plsc.VectorSubcoreMesh(core_axis_name="c", subcore_axis_name="s",
                                  num_cores=SC_CORES, num_subcores=SC_SUBCORES)

    @functools.partial(
        pl.kernel, mesh=mesh, out_type=jax.ShapeDtypeStruct((m, width), table.dtype),
        scratch_types=[pltpu.VMEM((n_chunks, ch), jnp.int32), pltpu.VMEM((nb, ch, width), table.dtype)]
        + [pltpu.SemaphoreType.DMA] * (2 * nb))
    def gather(table_hbm, idx_hbm, out_hbm, idx_v, rows_v, *sems):
        wid = lax.axis_index("s") * SC_CORES + lax.axis_index("c")
        base = wid * (n_chunks * ch)
        pltpu.sync_copy(idx_hbm.at[wid], idx_v)

        def fetch(c, b):
            return pltpu.make_async_copy(table_hbm.at[idx_v.at[c]], rows_v.at[b], sems[b])

        def flush(c, b):
            return pltpu.make_async_copy(rows_v.at[b], out_hbm.at[pl.ds(base + c * ch, ch)], sems[nb + b])

        @pl.loop(0, n_chunks, step=nb)
        def _(c0):
            for b in range(nb):
                @pl.when(c0 > 0)
                def _():
                    flush(c0 - nb + b, b).wait()
                fetch(c0 + b, b).start()
            for b in range(nb):
                fetch(c0 + b, b).wait()
                flush(c0 + b, b).start()

        for b in range(nb):
            flush(n_chunks - nb + b, b).wait()

    return gather(table, idx.reshape(workers, n_chunks, ch))


def _sc_dispatch(table, dest_a, dest_b, dest_pad, n_out):
    m, width = table.shape
    n_pad = dest_pad.shape[0]
    workers = SC_CORES * SC_SUBCORES
    ch = SC_INDEX_CHUNK
    assert width == LANES and m % (workers * ch) == 0 and n_pad % (workers * ch) == 0
    n_chunks = m // (workers * ch)
    pad_chunks = n_pad // (workers * ch)
    nb = max(b for b in (4, 3, 2, 1) if n_chunks % b == 0)
    mesh = plsc.VectorSubcoreMesh(core_axis_name="c", subcore_axis_name="s",
                                  num_cores=SC_CORES, num_subcores=SC_SUBCORES)

    @functools.partial(
        pl.kernel, mesh=mesh, out_type=jax.ShapeDtypeStruct((n_out, width), table.dtype),
        scratch_types=[pltpu.VMEM((n_chunks, ch), jnp.int32), pltpu.VMEM((n_chunks, ch), jnp.int32),
                       pltpu.VMEM((pad_chunks, ch), jnp.int32), pltpu.VMEM((nb, ch, width), table.dtype),
                       pltpu.VMEM((ch, width), table.dtype)]
        + [pltpu.SemaphoreType.DMA] * (3 * nb + 1))
    def dispatch(table_hbm, zeros_hbm, da_hbm, db_hbm, dp_hbm, out_hbm, da_v, db_v, dp_v, rows_v, zero_v, *sems):
        wid = lax.axis_index("s") * SC_CORES + lax.axis_index("c")
        base = wid * (n_chunks * ch)
        pltpu.sync_copy(da_hbm.at[wid], da_v)
        pltpu.sync_copy(db_hbm.at[wid], db_v)
        pltpu.sync_copy(dp_hbm.at[wid], dp_v)
        pltpu.sync_copy(zeros_hbm, zero_v)

        def fetch(c, b):
            return pltpu.make_async_copy(table_hbm.at[pl.ds(base + c * ch, ch)], rows_v.at[b], sems[b])

        def put_a(c, b):
            return pltpu.make_async_copy(rows_v.at[b], out_hbm.at[da_v.at[c]], sems[nb + b])

        def put_b(c, b):
            return pltpu.make_async_copy(rows_v.at[b], out_hbm.at[db_v.at[c]], sems[2 * nb + b])

        def put_zero(p):
            return pltpu.make_async_copy(zero_v, out_hbm.at[dp_v.at[p]], sems[3 * nb])

        for p in range(pad_chunks):
            put_zero(p).start()

        @pl.loop(0, n_chunks, step=nb)
        def _(c0):
            for b in range(nb):
                @pl.when(c0 > 0)
                def _():
                    put_a(c0 - nb + b, b).wait()
                    put_b(c0 - nb + b, b).wait()
                fetch(c0 + b, b).start()
            for b in range(nb):
                fetch(c0 + b, b).wait()
                put_a(c0 + b, b).start()
                put_b(c0 + b, b).start()

        for b in range(nb):
            put_a(n_chunks - nb + b, b).wait()
            put_b(n_chunks - nb + b, b).wait()
        for p in range(pad_chunks):
            put_zero(p).wait()

    shape3 = lambda v: v.reshape(workers, -1, ch)
    return dispatch(table, jnp.zeros((ch, width), table.dtype), shape3(dest_a), shape3(dest_b), shape3(dest_pad))


def _routing(route, tile):
    B, T, _ = route.shape
    n = B * T
    assert (2 * n) % tile == 0
    ids = route[..., :2].astype(jnp.int32).reshape(n, 2)
    e_cat = jnp.concatenate([ids[:, 0], ids[:, 1]])
    onehot = (e_cat[:, None] == jnp.arange(N_EXPERTS, dtype=jnp.int32)[None, :]).astype(jnp.int32)
    csum = jnp.cumsum(onehot, axis=0)
    pos = jnp.sum(onehot * csum, axis=1) - 1
    counts = csum[-1]
    padded = (counts + tile - 1) // tile * tile
    ends = jnp.cumsum(padded)
    starts = ends - padded
    dest = (starts[e_cat] + pos).astype(jnp.int32)
    n_pad = N_EXPERTS * tile
    n_rows = 2 * n + n_pad
    seg_first = jnp.concatenate([starts + counts, ends[-1:]])
    seg_size = jnp.concatenate([padded - counts, (n_rows - ends[-1])[None]])
    seg_end = jnp.cumsum(seg_size)
    j = jnp.arange(n_pad, dtype=jnp.int32)
    seg = jnp.sum((j[:, None] >= seg_end[None, :]).astype(jnp.int32), axis=1)
    pad_rows = (seg_first[seg] + j - (seg_end - seg_size)[seg]).astype(jnp.int32)
    tile_start = jnp.arange(n_rows // tile, dtype=jnp.int32) * tile
    tile_expert = jnp.minimum(jnp.searchsorted(ends, tile_start, side="right"), N_EXPERTS - 1).astype(jnp.int32)
    tile_valid = (tile_start < ends[-1]).astype(jnp.int32)
    return dest, pad_rows, n_rows, tile_expert, tile_valid


def _expert_kernel(te_ref, tv_ref, xs_ref, wgu_ref, wd_ref, ys_ref, wgu_scr):
    i = pl.program_id(0)
    F = wd_ref.shape[0]

    @pl.when((i == 0) | (te_ref[i] != te_ref[jnp.maximum(i - 1, 0)]))
    def _new_expert():
        wgu_scr[...] = wgu_ref[...].astype(BF16)

    @pl.when(tv_ref[i] > 0)
    def _compute():
        lo, hi = _unpack_bf16_pairs(_load_packed(xs_ref))
        gu = (jnp.dot(lo.astype(BF16), wgu_scr[:PACK_W, :], preferred_element_type=F32)
              + jnp.dot(hi.astype(BF16), wgu_scr[PACK_W:, :], preferred_element_type=F32))
        act = (_silu(gu[:, :F]) * gu[:, F:]).astype(BF16)
        y = jnp.dot(act, wd_ref[...], preferred_element_type=F32)
        _store_packed(ys_ref, _pack_bf16_pairs(y))

    @pl.when(tv_ref[i] == 0)
    def _unused_tile():
        ys_ref[...] = jnp.zeros_like(ys_ref)


def _experts(xs, tile_expert, tile_valid, w_gu, w_down, layer):
    _, n_rows, _ = xs.shape
    _, _, F, D = w_down.shape
    tile = EXPERT_TILE
    rows = pl.BlockSpec((PACK_CHUNKS, tile, LANES), lambda i, te, tv: (0, i, 0))
    return pl.pallas_call(
        _expert_kernel,
        grid_spec=pltpu.PrefetchScalarGridSpec(
            num_scalar_prefetch=2,
            grid=(n_rows // tile,),
            in_specs=[rows,
                      pl.BlockSpec((None, None, D, 2 * F), lambda i, te, tv: (layer, te[i], 0, 0)),
                      pl.BlockSpec((None, None, F, D), lambda i, te, tv: (layer, te[i], 0, 0))],
            out_specs=rows,
            scratch_shapes=[pltpu.VMEM((D, 2 * F), BF16)]),
        out_shape=jax.ShapeDtypeStruct(xs.shape, jnp.int32),
        compiler_params=_params(("arbitrary",)),
    )(tile_expert, tile_valid, xs, w_gu, w_down)


def _combine_kernel(*refs, tm, ctx_len, skip, final):
    x_ref, g2_ref, cg2_ref, route_ref, ya_ref, yb_ref = refs[:6]
    fg_ref = refs[6] if final else None
    o_ref = refs[-1]
    is_ctx = _row_is_ctx(pl.program_id(1) + skip, tm, ctx_len)
    route = route_ref[...]
    lane = lax.broadcasted_iota(jnp.int32, route.shape, 1)
    gate_a = jnp.sum(jnp.where(lane == 2, route, 0.0), axis=-1, keepdims=True)
    gate_b = jnp.sum(jnp.where(lane == 3, route, 0.0), axis=-1, keepdims=True)
    unpack = lambda ref: jnp.concatenate(_unpack_bf16_pairs(_load_packed(ref)), axis=1)
    f = gate_a * unpack(ya_ref) + gate_b * unpack(yb_ref)
    x = x_ref[...] + jnp.where(is_ctx, cg2_ref[...], g2_ref[...]) * f
    if final:
        x = x * lax.rsqrt(jnp.mean(x * x, axis=-1, keepdims=True) + EPS) * fg_ref[...]
    o_ref[...] = x


def _combine(xt, mod_l, mod_c, route, yg, ctx_len, final_g=None, part=(0, 1), prev=None):
    B, T, D = xt.shape
    bh = B // part[1]
    b0 = part[0] * bh
    final = final_g is not None
    if final:
        tm = SEQ_TILE
        assert ctx_len % tm == 0 and T % tm == 0
        skip = ctx_len // tm
    else:
        tm = _pick_tile(T, 640)
        skip = 0
    nt = T // tm
    row = lambda w: pl.BlockSpec((None, tm, w), lambda b, j: (b + b0, j + skip, 0))
    slot = lambda s: pl.BlockSpec((PACK_CHUNKS, None, tm, LANES), lambda b, j: (0, s, b * nt + j + skip, 0))
    in_specs = [row(D), *_mod_specs(5, D), row(LANES), slot(0), slot(1)]
    args = [xt, mod_l[b0:b0 + bh], mod_c, route, yg, yg]
    if final:
        in_specs.append(pl.BlockSpec((1, D), lambda b, j: (0, 0)))
        args.append(final_g)
    aliases = {}
    if prev is not None:
        in_specs.append(pl.BlockSpec(memory_space=pl.ANY))
        args.append(prev)
        aliases = {len(args) - 1: 0}
    return pl.pallas_call(
        functools.partial(_combine_kernel, tm=tm, ctx_len=ctx_len, skip=skip, final=final),
        grid=(bh, nt - skip),
        in_specs=in_specs,
        out_specs=pl.BlockSpec((None, tm, D), lambda b, j: (b + b0, j, 0)),
        out_shape=jax.ShapeDtypeStruct((B, T - skip * tm, D), F32),
        input_output_aliases=aliases,
        compiler_params=_params(("parallel", "parallel")),
    )(*args)


def _moe(hp, route, xt, mod_l, mod_c, w_gu, w_down, layer, ctx_len, final_g=None):
    B, T, _ = xt.shape
    n = B * T
    dest, pad_rows, n_rows, tile_expert, tile_valid = _routing(route, EXPERT_TILE)
    chunk = jnp.arange(PACK_CHUNKS, dtype=jnp.int32)[:, None] * n_rows
    xs = _sc_dispatch(hp.reshape(PACK_CHUNKS * n, LANES), (chunk + dest[None, :n]).reshape(-1),
                      (chunk + dest[None, n:]).reshape(-1), (chunk + pad_rows[None, :]).reshape(-1),
                      PACK_CHUNKS * n_rows)
    ys = _experts(xs.reshape(PACK_CHUNKS, n_rows, LANES), tile_expert, tile_valid, w_gu, w_down, layer)
    ys = ys.reshape(PACK_CHUNKS * n_rows, LANES)
    parts = 2 if B % 2 == 0 else 1
    nh = n // parts
    out = None
    for h in range(parts):
        rows = jnp.concatenate([dest[h * nh:(h + 1) * nh], dest[n + h * nh:n + (h + 1) * nh]])
        yg = _sc_gather(ys, (chunk + rows[None, :]).reshape(-1)).reshape(PACK_CHUNKS, 2, nh, LANES)
        out = _combine(xt, mod_l, mod_c, route, yg, ctx_len, final_g, part=(h, parts), prev=out)
    return out


def _final_norm_kernel(x_ref, g_ref, o_ref):
    x = x_ref[...]
    o_ref[...] = x * lax.rsqrt(jnp.mean(x * x, axis=-1, keepdims=True) + EPS) * g_ref[...]


def _final_norm(xt, g, ctx_len):
    B, T, D = xt.shape
    S = T - ctx_len
    tm = SEQ_TILE
    assert ctx_len % tm == 0 and S % tm == 0
    skip = ctx_len // tm
    return pl.pallas_call(
        _final_norm_kernel,
        grid=(B, S // tm),
        in_specs=[pl.BlockSpec((None, tm, D), lambda b, j: (b, j + skip, 0)),
                  pl.BlockSpec((1, D), lambda b, j: (0, 0))],
        out_specs=pl.BlockSpec((None, tm, D), lambda b, j: (b, j, 0)),
        out_shape=jax.ShapeDtypeStruct((B, S, D), F32),
        compiler_params=_params(("parallel", "parallel")),
    )(xt, g)


def _rope_tables(seq, ctx_len):
    pos = jnp.arange(seq, dtype=jnp.int32)
    nf = HEAD_DIM // 4
    inv = ROPE_THETA ** (-jnp.arange(nf, dtype=F32) / nf)
    ang = jnp.concatenate([(pos // GRID_W).astype(F32)[:, None] * inv, (pos % GRID_W).astype(F32)[:, None] * inv], axis=-1)
    cos, sin = jnp.cos(ang), jnp.sin(ang)
    reps = LANES // HEAD_DIM
    cos_t = jnp.tile(jnp.concatenate([cos, cos], axis=-1), (1, reps))
    sin_t = jnp.tile(jnp.concatenate([-sin, sin], axis=-1), (1, reps))
    cos_t = jnp.concatenate([jnp.ones((ctx_len, LANES), F32), cos_t], axis=0)
    sin_t = jnp.concatenate([jnp.zeros((ctx_len, LANES), F32), sin_t], axis=0)
    return cos_t, sin_t


def _relayout_w_in(w_in):
    o_lr = ATT_W + 2 * KV_W + 5 * GLA_W
    q_scale = HEAD_DIM ** -0.5
    gq_scale = GLA_DK ** -0.5
    parts = [w_in[..., :ATT_W] * q_scale, w_in[..., ATT_W:COL_GQ], w_in[..., COL_GQ:COL_GK] * gq_scale,
             w_in[..., COL_GK:o_lr], w_in[..., o_lr + 2 * GLA_LOWRANK:], w_in[..., o_lr:o_lr + 2 * GLA_LOWRANK],
             jnp.zeros(w_in.shape[:-1] + (LANES - 2 * GLA_LOWRANK,), w_in.dtype)]
    return jnp.concatenate(parts, axis=-1).astype(BF16)


def kernel(x, c, ctx, c_ctx, w_mod, b_mod, norm1_g, norm2_g, w_in, w_out, attn_sink, gla_w2_f, gla_b_f, gla_w2_b, gla_b_b, gla_norm_g, conv_w, ffn_w_gu, ffn_w_down, router_w, expert_w_gu, expert_w_down, final_norm_g):
    B, S, D = x.shape
    L = ctx.shape[1]
    depth = w_in.shape[0]
    assert D == D_MODEL and S % GRID_W == 0

    xt = jnp.concatenate([ctx, x], axis=1)
    rows = 16
    cc = jnp.concatenate([c, c_ctx[None, :], jnp.zeros((rows - B - 1, D), F32)], axis=0)
    mod = _adaln(cc, w_mod, b_mod)
    cos_t, sin_t = _rope_tables(S, L)

    w_in_p = _relayout_w_in(w_in)
    w_out_b = _to_bf16(w_out)
    zpad = jnp.zeros((depth, LANES - 2 * GLA_LOWRANK, GLA_W), F32)
    w2_f = jnp.concatenate([gla_w2_f, jnp.zeros_like(gla_w2_b), zpad], axis=1).astype(BF16)
    w2_b = jnp.concatenate([jnp.zeros_like(gla_w2_f), gla_w2_b, zpad], axis=1).astype(BF16)
    gla_ng = jnp.tile(gla_norm_g, (1, GLA_HEADS))
    ffn_gu_b = _to_bf16(ffn_w_gu)
    ffn_down_b = _to_bf16(ffn_w_down)
    exp_down_b = _to_bf16(expert_w_down)
    router_b = jnp.pad(router_w, ((0, 0), (0, 0), (0, LANES - N_EXPERTS))).astype(BF16)

    for l in range(depth):
        mod_l = mod[l, :B].reshape(B, 1, 6 * D)
        mod_c = mod[l, B:B + 1]
        proj = _inproj(xt, norm1_g[l][None, :], mod_l, mod_c, cos_t, sin_t, w_in_p, l, L)
        att = _attention(proj, attn_sink[l], L)
        cv = _short_conv(proj, conv_w[l], L)
        gf = _gla(proj, w2_f[l], gla_b_f[l][None, :], gla_ng[l][None, :], L, reverse=False)
        gb = _gla(proj, w2_b[l], gla_b_b[l][None, :], gla_ng[l][None, :], L, reverse=True)
        if l % 2 == 0:
            xt = _outproj(xt, att, gf, gb, cv, w_out_b, l, norm2_g[l][None, :], mod_l, mod_c, L,
                          ffn=(ffn_gu_b, ffn_down_b, l // 2))
        else:
            xt, hp, route = _outproj(xt, att, gf, gb, cv, w_out_b, l, norm2_g[l][None, :], mod_l, mod_c, L,
                                     router=(router_b, l // 2))
            xt = _moe(hp, route, xt, mod_l, mod_c, expert_w_gu, exp_down_b, l // 2, L,
                      final_g=final_norm_g[None, :] if l == depth - 1 else None)
    return xt if depth % 2 == 0 else _final_norm(xt, final_norm_g[None, :], L)
```

```python
import functools

import jax
import jax.numpy as jnp
from jax import lax
from jax.experimental import pallas as pl
from jax.experimental.pallas import tpu as pltpu
from jax.experimental.pallas import tpu_sc as plsc

F32 = jnp.float32
BF16 = jnp.bfloat16

D_MODEL = 1024
EPS = 1e-6
GRID_W = 64
ROPE_THETA = 10000.0
LOG2E = 1.4426950408889634
HEAD_DIM = 64
ATT_HEADS = 8
ATT_KV_HEADS = 2
ATT_GROUP = ATT_HEADS // ATT_KV_HEADS
ATT_W = ATT_HEADS * HEAD_DIM
KV_W = ATT_KV_HEADS * HEAD_DIM
WINDOW = 128
GLA_HEADS = 4
GLA_DK = 64
GLA_W = GLA_HEADS * GLA_DK
GLA_LOWRANK = 16
GLA_TAU = 16.0
GLA_CHUNK = 64
CONV_CH = 256
N_EXPERTS = 8

LANES = 128
SUBLANES_BF16 = 16

COL_Q = 0
COL_K = ATT_W
COL_V = COL_K + KV_W
ROPE_W = COL_V
COL_GQ = COL_V + KV_W
COL_GK = COL_GQ + GLA_W
COL_GV = COL_GK + GLA_W
COL_GGF = COL_GV + GLA_W
COL_GGB = COL_GGF + GLA_W
COL_CB = COL_GGB + GLA_W
COL_CC = COL_CB + CONV_CH
COL_CX = COL_CC + CONV_CH
COL_LR = COL_CX + CONV_CH
PROJ_W = COL_LR + LANES

SEQ_TILE = 256
VMEM_LIMIT = 56 * 1024 * 1024
CAST_BLOCK_BYTES = 4 * 1024 * 1024

PACK_W = D_MODEL // 2
PACK_CHUNKS = PACK_W // LANES
EXPERT_TILE = 512
SC_CORES = 2
SC_SUBCORES = 16
SC_INDEX_CHUNK = 128


def _silu(v):
    return v / (1.0 + jnp.exp(-v))


def _pick_tile(total, cap):
    best = None
    for t in range(SUBLANES_BF16, cap + 1, SUBLANES_BF16):
        if total % t == 0:
            best = t
    assert best is not None
    return best


def _params(sem):
    return pltpu.CompilerParams(dimension_semantics=sem, vmem_limit_bytes=VMEM_LIMIT)


def _pack_bf16_pairs(v):
    bits = lambda t: lax.bitcast_convert_type(t.astype(BF16).astype(F32), jnp.int32)
    return ((bits(v[:, :PACK_W]) >> 16) & 0xFFFF) | (bits(v[:, PACK_W:]) & -65536)


def _unpack_bf16_pairs(w):
    return lax.bitcast_convert_type(w << 16, F32), lax.bitcast_convert_type(w & -65536, F32)


def _cast_kernel(x_ref, o_ref):
    o_ref[...] = x_ref[...].astype(BF16)


def _to_bf16(w):
    cols = w.shape[-1]
    rows = w.size // cols
    tr = _pick_tile(rows, max(SUBLANES_BF16, CAST_BLOCK_BYTES // (4 * cols)))
    out = pl.pallas_call(
        _cast_kernel,
        grid=(rows // tr,),
        in_specs=[pl.BlockSpec((tr, cols), lambda i: (i, 0))],
        out_specs=pl.BlockSpec((tr, cols), lambda i: (i, 0)),
        out_shape=jax.ShapeDtypeStruct((rows, cols), BF16),
        compiler_params=_params(("parallel",)),
    )(w.reshape(rows, cols))
    return out.reshape(w.shape)


def _load_packed(ref):
    return jnp.concatenate([ref[k] for k in range(PACK_CHUNKS)], axis=1)


def _store_packed(ref, packed):
    for k in range(PACK_CHUNKS):
        ref[k] = packed[:, k * LANES:(k + 1) * LANES]


def _adaln_kernel(c_ref, w_ref, b_ref, o_ref):
    s = _silu(c_ref[...]).astype(BF16)
    o_ref[...] = jnp.dot(s, w_ref[...].astype(BF16), preferred_element_type=F32) + b_ref[...]


def _adaln(cc, w_mod, b_mod):
    depth, d, n = w_mod.shape
    tn = 1024
    rows = cc.shape[0]
    return pl.pallas_call(
        _adaln_kernel,
        grid=(depth, n // tn),
        in_specs=[pl.BlockSpec((rows, d), lambda l, j: (0, 0)),
                  pl.BlockSpec((None, d, tn), lambda l, j: (l, 0, j)),
                  pl.BlockSpec((None, 1, tn), lambda l, j: (l, 0, j))],
        out_specs=pl.BlockSpec((None, rows, tn), lambda l, j: (l, 0, j)),
        out_shape=jax.ShapeDtypeStruct((depth, rows, n), F32),
        compiler_params=_params(("parallel", "parallel")),
    )(cc, w_mod, b_mod.reshape(depth, 1, n))


def _mod_specs(k, d):
    return [pl.BlockSpec((None, 1, d), lambda b, j, *_: (b, 0, k)),
            pl.BlockSpec((1, d), lambda b, j, *_: (0, k))]


def _row_is_ctx(j, tm, ctx_len):
    return (j * tm + lax.broadcasted_iota(jnp.int32, (tm, 1), 0)) < ctx_len


def _inproj_kernel(x_ref, g_ref, sc_ref, csc_ref, sh_ref, csh_ref, cos_ref, sin_ref, w_ref, o_ref, *, tm, ctx_len):
    x = x_ref[...]
    n = x * lax.rsqrt(jnp.mean(x * x, axis=-1, keepdims=True) + EPS) * g_ref[...]
    is_ctx = _row_is_ctx(pl.program_id(1), tm, ctx_len)
    scale = jnp.where(is_ctx, csc_ref[...], sc_ref[...])
    shift = jnp.where(is_ctx, csh_ref[...], sh_ref[...])
    h = (n * (1.0 + scale) + shift).astype(BF16)
    qk = jnp.dot(h, w_ref[:, :ROPE_W], preferred_element_type=F32)
    lower_half = (lax.broadcasted_iota(jnp.int32, (tm, LANES), 1) % HEAD_DIM) < (HEAD_DIM // 2)
    cos = cos_ref[...]
    sin = sin_ref[...]
    for i in range(ROPE_W // LANES):
        t = qk[:, i * LANES:(i + 1) * LANES]
        partner = jnp.where(lower_half, pltpu.roll(t, LANES - HEAD_DIM // 2, axis=1), pltpu.roll(t, HEAD_DIM // 2, axis=1))
        o_ref[:, i * LANES:(i + 1) * LANES] = (t * cos + partner * sin).astype(BF16)
    o_ref[:, ROPE_W:] = jnp.dot(h, w_ref[:, ROPE_W:], preferred_element_type=F32).astype(BF16)


def _inproj(xt, g, mod_l, mod_c, cos_t, sin_t, w, layer, ctx_len):
    B, T, D = xt.shape
    tm = _pick_tile(T, 640)
    kern = functools.partial(_inproj_kernel, tm=tm, ctx_len=ctx_len)
    return pl.pallas_call(
        kern,
        grid=(B, T // tm),
        in_specs=[pl.BlockSpec((None, tm, D), lambda b, j: (b, j, 0)),
                  pl.BlockSpec((1, D), lambda b, j: (0, 0)),
                  *_mod_specs(1, D), *_mod_specs(0, D),
                  pl.BlockSpec((tm, LANES), lambda b, j: (j, 0)),
                  pl.BlockSpec((tm, LANES), lambda b, j: (j, 0)),
                  pl.BlockSpec((None, D, PROJ_W), lambda b, j: (layer, 0, 0))],
        out_specs=pl.BlockSpec((None, tm, PROJ_W), lambda b, j: (b, j, 0)),
        out_shape=jax.ShapeDtypeStruct((B, T, PROJ_W), BF16),
        compiler_params=_params(("parallel", "parallel")),
    )(xt, g, mod_l, mod_c, mod_l, mod_c, cos_t, sin_t, w)


def _attn_kernel(sink_ref, q_ref, kc_ref, vc_ref, km_ref, vm_ref, kp_ref, vp_ref, kn_ref, vn_ref, o_ref, *, n_tiles):
    j = pl.program_id(1)
    nt_dims = (((1,), (1,)), ((), ()))
    tn_dims = (((0,), (0,)), ((), ()))

    def scores(rows, n_rows, g, k_ctx, k_win=None, mask_win=None):
        heads = [g * ATT_GROUP + hh for hh in range(ATT_GROUP)]
        q = jnp.concatenate([q_ref[rows, h * HEAD_DIM:(h + 1) * HEAD_DIM] for h in heads], axis=0)
        head_col = lax.broadcasted_iota(jnp.int32, (1, ATT_GROUP * n_rows), 1) // n_rows
        sink = jnp.full((1, ATT_GROUP * n_rows), sink_ref[heads[0]] * LOG2E, F32)
        for hh in range(1, ATT_GROUP):
            sink = jnp.where(head_col == hh, sink_ref[heads[hh]] * LOG2E, sink)
        s_c = lax.dot_general(k_ctx, q, nt_dims, preferred_element_type=F32)
        s_w = None
        if k_win is not None:
            s_w = jnp.where(mask_win, lax.dot_general(k_win, q, nt_dims, preferred_element_type=F32), -1e30)
        return sink, s_c, s_w

    def finish(item, v_ctx, v_win=None):
        sink, s_c, s_w = item
        m = jnp.maximum(jnp.max(s_c, axis=0, keepdims=True), sink)
        if s_w is not None:
            m = jnp.maximum(m, jnp.max(s_w, axis=0, keepdims=True))
        p_c = jnp.exp2(s_c - m)
        den = jnp.sum(p_c, axis=0, keepdims=True) + jnp.exp2(sink - m)
        o = lax.dot_general(v_ctx, p_c.astype(BF16), tn_dims, preferred_element_type=F32)
        if s_w is not None:
            p_w = jnp.exp2(s_w - m)
            den += jnp.sum(p_w, axis=0, keepdims=True)
            o += lax.dot_general(v_win, p_w.astype(BF16), tn_dims, preferred_element_type=F32)
        return o / den

    def store(rows, n_rows, outs):
        t = jnp.transpose(jnp.concatenate(outs, axis=0))
        for g in range(ATT_KV_HEADS):
            for hh in range(ATT_GROUP):
                h = g * ATT_GROUP + hh
                o_ref[rows, h * HEAD_DIM:(h + 1) * HEAD_DIM] = (
                    t[hh * n_rows:(hh + 1) * n_rows, g * HEAD_DIM:(g + 1) * HEAD_DIM].astype(BF16))

    def run(blocks):
        flat = [(bi, s, f) for bi, (_, _, work) in enumerate(blocks) for s, f in work]
        outs = [[] for _ in blocks]
        pending = None
        for bi, score_fn, finish_fn in flat + [(None, None, None)]:
            item = None if score_fn is None else score_fn()
            if pending is not None:
                pbi, pitem, pfinish = pending
                outs[pbi].append(pfinish(pitem))
                if len(outs[pbi]) == ATT_KV_HEADS:
                    store(blocks[pbi][0], blocks[pbi][1], outs[pbi])
            pending = None if score_fn is None else (bi, item, finish_fn)

    @pl.when(j == 0)
    def _context_queries():
        work = []
        for g in range(ATT_KV_HEADS):
            gs = slice(g * HEAD_DIM, (g + 1) * HEAD_DIM)
            work.append((functools.partial(scores, slice(0, SEQ_TILE), SEQ_TILE, g, kc_ref[:, gs]),
                         functools.partial(finish, v_ctx=vc_ref[:, gs])))
        run([(slice(0, SEQ_TILE), SEQ_TILE, work)])

    @pl.when(j > 0)
    def _latent_queries():
        w = lax.broadcasted_iota(jnp.int32, (3 * WINDOW, ATT_GROUP * WINDOW), 0)
        r = lax.broadcasted_iota(jnp.int32, (3 * WINDOW, ATT_GROUP * WINDOW), 1) % WINDOW
        band = (w >= r) & (w <= r + 2 * WINDOW)
        blocks = []
        for sub in range(SEQ_TILE // WINDOW):
            rows = slice(sub * WINDOW, (sub + 1) * WINDOW)
            if sub == 0:
                kprev, vprev = kp_ref[...], vp_ref[...]
                kmid, vmid = km_ref[:WINDOW], vm_ref[:WINDOW]
                knext, vnext = km_ref[WINDOW:], vm_ref[WINDOW:]
                w_lo = jnp.where(j > 1, 0, WINDOW)
                w_hi = 3 * WINDOW
            else:
                kprev, vprev = km_ref[:WINDOW], vm_ref[:WINDOW]
                kmid, vmid = km_ref[WINDOW:], vm_ref[WINDOW:]
                knext, vnext = kn_ref[...], vn_ref[...]
                w_lo = 0
                w_hi = jnp.where(j < n_tiles - 1, 3 * WINDOW, 2 * WINDOW)
            mask = band & (w >= w_lo) & (w < w_hi)
            work = []
            for g in range(ATT_KV_HEADS):
                gs = slice(g * HEAD_DIM, (g + 1) * HEAD_DIM)
                k_win = jnp.concatenate([kprev[:, gs], kmid[:, gs], knext[:, gs]], axis=0)
                v_win = jnp.concatenate([vprev[:, gs], vmid[:, gs], vnext[:, gs]], axis=0)
                work.append((functools.partial(scores, rows, WINDOW, g, kc_ref[:, gs], k_win, mask),
                             functools.partial(finish, v_ctx=vc_ref[:, gs], v_win=v_win)))
            blocks.append((rows, WINDOW, work))
        run(blocks)


def _attention(proj, sink, ctx_len):
    B, T, _ = proj.shape
    assert ctx_len == SEQ_TILE and T % SEQ_TILE == 0
    n_tiles = T // SEQ_TILE
    n_win = T // WINDOW
    kcol, vcol = COL_K // KV_W, COL_V // KV_W
    per_tile = SEQ_TILE // WINDOW
    tile = lambda col: pl.BlockSpec((None, SEQ_TILE, KV_W), lambda b, j: (b, j, col))
    ctx = lambda col: pl.BlockSpec((None, SEQ_TILE, KV_W), lambda b, j: (b, 0, col))
    prev = lambda col: pl.BlockSpec((None, WINDOW, KV_W), lambda b, j: (b, jnp.maximum(j * per_tile - 1, 0), col))
    nxt = lambda col: pl.BlockSpec((None, WINDOW, KV_W), lambda b, j: (b, jnp.minimum((j + 1) * per_tile, n_win - 1), col))
    return pl.pallas_call(
        functools.partial(_attn_kernel, n_tiles=n_tiles),
        grid=(B, n_tiles),
        in_specs=[pl.BlockSpec(memory_space=pltpu.SMEM),
                  pl.BlockSpec((None, SEQ_TILE, ATT_W), lambda b, j: (b, j, 0)),
                  ctx(kcol), ctx(vcol), tile(kcol), tile(vcol), prev(kcol), prev(vcol), nxt(kcol), nxt(vcol)],
        out_specs=pl.BlockSpec((None, SEQ_TILE, ATT_W), lambda b, j: (b, j, 0)),
        out_shape=jax.ShapeDtypeStruct((B, T, ATT_W), BF16),
        compiler_params=_params(("parallel", "parallel")),
    )(sink, *([proj] * 9))


def _conv_kernel(b_ref, c_ref, x_ref, w_ref, o_ref, u_scr, *, ctx_len):
    T = c_ref.shape[0]
    pad = 8
    u = c_ref[...].astype(F32) * x_ref[...].astype(F32)
    zero_row = jnp.zeros((1, CONV_CH), F32)
    u_scr[pad - 1:pad, :] = zero_row
    u_scr[pad:pad + T, :] = u
    u_scr[pad + T:pad + T + 1, :] = zero_row
    row = lax.broadcasted_iota(jnp.int32, (T, 1), 0)
    u_prev = jnp.where(row == ctx_len, 0.0, u_scr[pad - 1:pad - 1 + T, :])
    u_next = jnp.where(row == ctx_len - 1, 0.0, u_scr[pad + 1:pad + 1 + T, :])
    y = w_ref[0:1, :] * u_prev + w_ref[1:2, :] * u + w_ref[2:3, :] * u_next
    o_ref[...] = (b_ref[...].astype(F32) * y).astype(BF16)


def _short_conv(proj, conv_w, ctx_len):
    B, T, _ = proj.shape
    col = lambda c0: pl.BlockSpec((None, T, CONV_CH), lambda b: (b, 0, c0 // CONV_CH))
    return pl.pallas_call(
        functools.partial(_conv_kernel, ctx_len=ctx_len),
        grid=(B,),
        in_specs=[col(COL_CB), col(COL_CC), col(COL_CX), pl.BlockSpec((3, CONV_CH), lambda b: (0, 0))],
        out_specs=pl.BlockSpec((None, T, CONV_CH), lambda b: (b, 0, 0)),
        out_shape=jax.ShapeDtypeStruct((B, T, CONV_CH), BF16),
        scratch_shapes=[pltpu.VMEM((T + 16, CONV_CH), F32)],
        compiler_params=_params(("parallel",)),
    )(proj, proj, proj, conv_w)


def _split2(v):
    hi = v.astype(BF16)
    lo = (v - hi.astype(F32)).astype(BF16)
    return hi, lo


def _gla_kernel(q_ref, k_ref, v_ref, gate_ref, lr_ref, w2_ref, bias_ref, ng_ref, o_ref,
                oacc_scr, qs_scr, u_scr, dec_scr, sin_scr, *, reverse, n_ctx_chunks):
    T = q_ref.shape[0]
    C = GLA_CHUNK
    n_tiles = T // SEQ_TILE
    n_chunks = T // C
    per_tile = SEQ_TILE // C
    nt_dims = (((1,), (1,)), ((), ()))
    tn_dims = (((0,), (0,)), ((), ()))

    r = lax.broadcasted_iota(jnp.int32, (SEQ_TILE, SEQ_TILE), 0)
    c = lax.broadcasted_iota(jnp.int32, (SEQ_TILE, SEQ_TILE), 1)
    same_chunk = (r // C) == (c // C)
    causal = same_chunk & ((c >= r) if reverse else (c <= r))
    cum_mat = jnp.where(causal, 1.0, 0.0).astype(BF16)
    same_head = (r // GLA_DK) == (c // GLA_DK)
    head_mean = jnp.where(same_head, 1.0 / GLA_DK, 0.0).astype(BF16)
    lane_head = lax.broadcasted_iota(jnp.int32, (C, GLA_W), 1) // GLA_DK

    def phase1(i, carry):
        rows = pl.ds(pl.multiple_of(i * SEQ_TILE, SEQ_TILE), SEQ_TILE)
        z = jnp.dot(lr_ref[rows, :], w2_ref[...], preferred_element_type=F32) + bias_ref[...]
        la = (jnp.minimum(z, 0.0) - jnp.log(1.0 + jnp.exp(-jnp.abs(z)))) / GLA_TAU
        hi, lo = _split2(la)
        b = jnp.dot(cum_mat, hi, preferred_element_type=F32) + jnp.dot(cum_mat, lo, preferred_element_type=F32)
        b3 = b.reshape(per_tile, C, GLA_W)
        total = b3[:, 0:1, :] if reverse else b3[:, C - 1:C, :]
        b_last = jnp.broadcast_to(total, (per_tile, C, GLA_W)).reshape(SEQ_TILE, GLA_W)
        b_ref = 0.5 * b_last
        q = q_ref[rows, :].astype(F32)
        k = k_ref[rows, :].astype(F32)
        v = v_ref[rows, :]
        e_fwd = jnp.exp(b - b_ref)
        e_bwd = jnp.exp(b_ref - b)
        e_half = jnp.exp(b_ref)
        qe = (q * e_fwd).astype(BF16)
        ke = (k * e_bwd).astype(BF16)
        ku = (k * (e_bwd * e_half)).astype(BF16)
        qs_scr[rows, :] = (q * (e_fwd * e_half)).astype(BF16)
        dec = e_half * e_half
        for h in range(GLA_HEADS):
            hs = slice(h * GLA_DK, (h + 1) * GLA_DK)
            s = lax.dot_general(qe[:, hs], ke[:, hs], nt_dims, preferred_element_type=F32)
            a = jnp.where(causal, s, 0.0).astype(BF16)
            oacc_scr[rows, hs] = jnp.dot(a, v[:, hs], preferred_element_type=F32)
        for ci in range(per_tile):
            cs = slice(ci * C, (ci + 1) * C)
            full = lax.dot_general(v[cs, :], ku[cs, :], tn_dims, preferred_element_type=F32)
            ut = full[0:C, :]
            for h in range(1, GLA_HEADS):
                ut = jnp.where(lane_head == h, full[h * C:(h + 1) * C, :], ut)
            u_scr[i * per_tile + ci] = ut
            dec_scr[i * per_tile + ci] = dec[ci * C:ci * C + 1, :]
        return carry

    lax.fori_loop(0, n_tiles, phase1, 0, unroll=2)

    def phase2(i, st):
        if reverse:
            ci = jnp.where(i < n_ctx_chunks, n_ctx_chunks - 1 - i, n_chunks - 1 + n_ctx_chunks - i)
        else:
            ci = i
        sin_scr[ci] = st.astype(BF16)
        return st * dec_scr[ci] + u_scr[ci]

    lax.fori_loop(0, n_chunks, phase2, jnp.zeros((C, GLA_W), F32))

    def phase3(i, carry):
        rows = pl.ds(pl.multiple_of(i * SEQ_TILE, SEQ_TILE), SEQ_TILE)
        qs = qs_scr[rows, :]
        parts = []
        for ci in range(per_tile):
            st = sin_scr[i * per_tile + ci]
            st_heads = jnp.where(same_head, jnp.concatenate([st] * GLA_HEADS, axis=0), jnp.zeros((), BF16))
            parts.append(lax.dot_general(qs[ci * C:(ci + 1) * C, :], st_heads, nt_dims, preferred_element_type=F32))
        o = oacc_scr[rows, :] + jnp.concatenate(parts, axis=0)
        ms = jnp.dot((o * o).astype(BF16), head_mean, preferred_element_type=F32)
        y = o * lax.rsqrt(ms + EPS) * ng_ref[...]
        o_ref[rows, :] = (y * _silu(gate_ref[rows, :].astype(F32))).astype(BF16)
        return carry

    lax.fori_loop(0, n_tiles, phase3, 0, unroll=2)


def _gla(proj, w2pad, bias, ng, ctx_len, reverse):
    B, T, _ = proj.shape
    assert T % SEQ_TILE == 0 and ctx_len % GLA_CHUNK == 0
    n_chunks = T // GLA_CHUNK
    col = lambda c0: pl.BlockSpec((None, T, GLA_W), lambda b: (b, 0, c0 // GLA_W))
    gate_col = COL_GGB if reverse else COL_GGF
    kern = functools.partial(_gla_kernel, reverse=reverse, n_ctx_chunks=ctx_len // GLA_CHUNK)
    return pl.pallas_call(
        kern,
        grid=(B,),
        in_specs=[col(COL_GQ), col(COL_GK), col(COL_GV), col(gate_col),
                  pl.BlockSpec((None, T, LANES), lambda b: (b, 0, COL_LR // LANES)),
                  pl.BlockSpec((LANES, GLA_W), lambda b: (0, 0)),
                  pl.BlockSpec((1, GLA_W), lambda b: (0, 0)),
                  pl.BlockSpec((1, GLA_W), lambda b: (0, 0))],
        out_specs=pl.BlockSpec((None, T, GLA_W), lambda b: (b, 0, 0)),
        out_shape=jax.ShapeDtypeStruct((B, T, GLA_W), BF16),
        scratch_shapes=[pltpu.VMEM((T, GLA_W), F32),
                        pltpu.VMEM((T, GLA_W), BF16),
                        pltpu.VMEM((n_chunks, GLA_CHUNK, GLA_W), F32),
                        pltpu.VMEM((n_chunks, 1, GLA_W), F32),
                        pltpu.VMEM((n_chunks, GLA_CHUNK, GLA_W), BF16)],
        compiler_params=_params(("parallel",)),
    )(proj, proj, proj, proj, proj, w2pad, bias, ng)


MXU_TILE = 256
FFN_CHUNK = 3 * MXU_TILE


def _swiglu_resident(h, wgu_ref, wd_ref):
    F = wd_ref.shape[0]
    y = None
    for c0 in range(0, F, FFN_CHUNK):
        c1 = min(c0 + FFN_CHUNK, F)
        g = jnp.dot(h, wgu_ref[:, c0:c1], preferred_element_type=F32)
        u = jnp.dot(h, wgu_ref[:, F + c0:F + c1], preferred_element_type=F32)
        part = jnp.dot((_silu(g) * u).astype(BF16), wd_ref[c0:c1, :], preferred_element_type=F32)
        y = part if y is None else y + part
    return y


def _outproj_kernel(*refs, tm, ctx_len, moe):
    if moe:
        (x_ref, att_ref, gf_ref, gb_ref, cv_ref, w_ref, g1_ref, cg1_ref, ng_ref, sc_ref, csc_ref, sh_ref, csh_ref,
         router_ref, xo_ref, hp_ref, route_ref) = refs
    else:
        (x_ref, att_ref, gf_ref, gb_ref, cv_ref, w_ref, g1_ref, cg1_ref, ng_ref, sc_ref, csc_ref, sh_ref, csh_ref,
         g2_ref, cg2_ref, wgu_ref, wd_ref, xo_ref) = refs
    is_ctx = _row_is_ctx(pl.program_id(1), tm, ctx_len)
    gla = (gf_ref[...].astype(F32) + gb_ref[...].astype(F32)).astype(BF16)
    y = jnp.dot(att_ref[...], w_ref[:ATT_W, :], preferred_element_type=F32)
    y += jnp.dot(gla, w_ref[ATT_W:ATT_W + GLA_W, :], preferred_element_type=F32)
    y += jnp.dot(cv_ref[...], w_ref[ATT_W + GLA_W:, :], preferred_element_type=F32)
    x = x_ref[...] + jnp.where(is_ctx, cg1_ref[...], g1_ref[...]) * y
    if moe:
        xo_ref[...] = x
    n = x * lax.rsqrt(jnp.mean(x * x, axis=-1, keepdims=True) + EPS) * ng_ref[...]
    scale = jnp.where(is_ctx, csc_ref[...], sc_ref[...])
    shift = jnp.where(is_ctx, csh_ref[...], sh_ref[...])
    h = n * (1.0 + scale) + shift
    hi = h.astype(BF16)
    if not moe:
        xo_ref[...] = x + jnp.where(is_ctx, cg2_ref[...], g2_ref[...]) * _swiglu_resident(hi, wgu_ref, wd_ref)
    else:
        _store_packed(hp_ref, _pack_bf16_pairs(h))
        logits = jnp.dot(hi, router_ref[...], preferred_element_type=F32)
        lane = lax.broadcasted_iota(jnp.int32, logits.shape, 1)
        neg = -jnp.inf
        lg = jnp.where(lane < N_EXPERTS, logits, neg)
        m1 = jnp.max(lg, axis=-1, keepdims=True)
        i1 = jnp.min(jnp.where(lg == m1, lane, LANES), axis=-1, keepdims=True)
        lg2 = jnp.where(lane == i1, neg, lg)
        m2 = jnp.max(lg2, axis=-1, keepdims=True)
        i2 = jnp.min(jnp.where(lg2 == m2, lane, LANES), axis=-1, keepdims=True)
        e2 = jnp.exp(m2 - m1)
        gate1 = 1.0 / (1.0 + e2)
        gate2 = e2 / (1.0 + e2)
        route_ref[...] = jnp.where(lane == 0, i1.astype(F32), jnp.where(lane == 1, i2.astype(F32),
                                   jnp.where(lane == 2, gate1, jnp.where(lane == 3, gate2, 0.0))))


def _outproj(xt, att, gf, gb, cv, w_out, layer, ng, mod_l, mod_c, ctx_len, router=None, ffn=None):
    B, T, D = xt.shape
    tm = _pick_tile(T, 640)
    moe = router is not None
    assert moe != (ffn is not None)
    row = lambda w: pl.BlockSpec((None, tm, w), lambda b, j: (b, j, 0))
    const = lambda shape: pl.BlockSpec(shape, lambda b, j: (0,) * len(shape))
    in_specs = [row(D), row(ATT_W), row(GLA_W), row(GLA_W), row(CONV_CH),
                pl.BlockSpec((None, D, D), lambda b, j: (layer, 0, 0)),
                *_mod_specs(2, D), const((1, D)), *_mod_specs(4, D), *_mod_specs(3, D)]
    args = [xt, att, gf, gb, cv, w_out, mod_l, mod_c, ng, mod_l, mod_c, mod_l, mod_c]
    nt = T // tm
    if moe:
        in_specs.append(pl.BlockSpec((None, D, LANES), lambda b, j: (router[1], 0, 0)))
        args.append(router[0])
        out_specs = [row(D), pl.BlockSpec((PACK_CHUNKS, tm, LANES), lambda b, j: (0, b * nt + j, 0)), row(LANES)]
        out_shape = [jax.ShapeDtypeStruct((B, T, D), F32), jax.ShapeDtypeStruct((PACK_CHUNKS, B * T, LANES), jnp.int32),
                     jax.ShapeDtypeStruct((B, T, LANES), F32)]
    else:
        w_gu, w_down, idx = ffn
        F = w_down.shape[1]
        assert F % MXU_TILE == 0
        resident = lambda shape: pl.BlockSpec((None,) + shape, lambda b, j: (idx, 0, 0), pipeline_mode=pl.Buffered(1))
        in_specs += [*_mod_specs(5, D), resident((D, 2 * F)), resident((F, D))]
        args += [mod_l, mod_c, w_gu, w_down]
        out_specs = row(D)
        out_shape = jax.ShapeDtypeStruct((B, T, D), F32)
    return pl.pallas_call(
        functools.partial(_outproj_kernel, tm=tm, ctx_len=ctx_len, moe=moe),
        grid=(B, T // tm),
        in_specs=in_specs, out_specs=out_specs, out_shape=out_shape,
        compiler_params=_params(("parallel", "parallel")),
    )(*args)


def _sc_gather(table, idx):
    _, width = table.shape
    m = idx.shape[0]
    workers = SC_CORES * SC_SUBCORES
    ch = SC_INDEX_CHUNK
    assert width == LANES and m % (workers * ch) == 0
    n_chunks = m // (workers * ch)
    nb = max(b for b in (4, 3, 2, 1) if n_chunks % b == 0)
    mesh = plsc.VectorSubcoreMesh(core_axis_name="c", subcore_axis_name="s",
                                  num_cores=SC_CORES, num_subcores=SC_SUBCORES)

    @functools.partial(
        pl.kernel, mesh=mesh, out_type=jax.ShapeDtypeStruct((m, width), table.dtype),
        scratch_types=[pltpu.VMEM((n_chunks, ch), jnp.int32), pltpu.VMEM((nb, ch, width), table.dtype)]
        + [pltpu.SemaphoreType.DMA] * (2 * nb))
    def gather(table_hbm, idx_hbm, out_hbm, idx_v, rows_v, *sems):
        wid = lax.axis_index("s") * SC_CORES + lax.axis_index("c")
        base = wid * (n_chunks * ch)
        pltpu.sync_copy(idx_hbm.at[wid], idx_v)

        def fetch(c, b):
            return pltpu.make_async_copy(table_hbm.at[idx_v.at[c]], rows_v.at[b], sems[b])

        def flush(c, b):
            return pltpu.make_async_copy(rows_v.at[b], out_hbm.at[pl.ds(base + c * ch, ch)], sems[nb + b])

        @pl.loop(0, n_chunks, step=nb)
        def _(c0):
            for b in range(nb):
                @pl.when(c0 > 0)
                def _():
                    flush(c0 - nb + b, b).wait()
                fetch(c0 + b, b).start()
            for b in range(nb):
                fetch(c0 + b, b).wait()
                flush(c0 + b, b).start()

        for b in range(nb):
            flush(n_chunks - nb + b, b).wait()

    return gather(table, idx.reshape(workers, n_chunks, ch))


def _sc_dispatch(table, dest_a, dest_b, dest_pad, n_out):
    m, width = table.shape
    n_pad = dest_pad.shape[0]
    workers = SC_CORES * SC_SUBCORES
    ch = SC_INDEX_CHUNK
    assert width == LANES and m % (workers * ch) == 0 and n_pad % (workers * ch) == 0
    n_chunks = m // (workers * ch)
    pad_chunks = n_pad // (workers * ch)
    nb = max(b for b in (4, 3, 2, 1) if n_chunks % b == 0)
    mesh = plsc.VectorSubcoreMesh(core_axis_name="c", subcore_axis_name="s",
                                  num_cores=SC_CORES, num_subcores=SC_SUBCORES)

    @functools.partial(
        pl.kernel, mesh=mesh, out_type=jax.ShapeDtypeStruct((n_out, width), table.dtype),
        scratch_types=[pltpu.VMEM((n_chunks, ch), jnp.int32), pltpu.VMEM((n_chunks, ch), jnp.int32),
                       pltpu.VMEM((pad_chunks, ch), jnp.int32), pltpu.VMEM((nb, ch, width), table.dtype),
                       pltpu.VMEM((ch, width), table.dtype)]
        + [pltpu.SemaphoreType.DMA] * (3 * nb + 1))
    def dispatch(table_hbm, zeros_hbm, da_hbm, db_hbm, dp_hbm, out_hbm, da_v, db_v, dp_v, rows_v, zero_v, *sems):
        wid = lax.axis_index("s") * SC_CORES + lax.axis_index("c")
        base = wid * (n_chunks * ch)
        pltpu.sync_copy(da_hbm.at[wid], da_v)
        pltpu.sync_copy(db_hbm.at[wid], db_v)
        pltpu.sync_copy(dp_hbm.at[wid], dp_v)
        pltpu.sync_copy(zeros_hbm, zero_v)

        def fetch(c, b):
            return pltpu.make_async_copy(table_hbm.at[pl.ds(base + c * ch, ch)], rows_v.at[b], sems[b])

        def put_a(c, b):
            return pltpu.make_async_copy(rows_v.at[b], out_hbm.at[da_v.at[c]], sems[nb + b])

        def put_b(c, b):
            return pltpu.make_async_copy(rows_v.at[b], out_hbm.at[db_v.at[c]], sems[2 * nb + b])

        def put_zero(p):
            return pltpu.make_async_copy(zero_v, out_hbm.at[dp_v.at[p]], sems[3 * nb])

        for p in range(pad_chunks):
            put_zero(p).start()

        @pl.loop(0, n_chunks, step=nb)
        def _(c0):
            for b in range(nb):
                @pl.when(c0 > 0)
                def _():
                    put_a(c0 - nb + b, b).wait()
                    put_b(c0 - nb + b, b).wait()
                fetch(c0 + b, b).start()
            for b in range(nb):
                fetch(c0 + b, b).wait()
                put_a(c0 + b, b).start()
                put_b(c0 + b, b).start()

        for b in range(nb):
            put_a(n_chunks - nb + b, b).wait()
            put_b(n_chunks - nb + b, b).wait()
        for p in range(pad_chunks):
            put_zero(p).wait()

    shape3 = lambda v: v.reshape(workers, -1, ch)
    return dispatch(table, jnp.zeros((ch, width), table.dtype), shape3(dest_a), shape3(dest_b), shape3(dest_pad))


def _routing(route, tile):
    B, T, _ = route.shape
    n = B * T
    assert (2 * n) % tile == 0
    ids = route[..., :2].astype(jnp.int32).reshape(n, 2)
    e_cat = jnp.concatenate([ids[:, 0], ids[:, 1]])
    onehot = (e_cat[:, None] == jnp.arange(N_EXPERTS, dtype=jnp.int32)[None, :]).astype(jnp.int32)
    csum = jnp.cumsum(onehot, axis=0)
    pos = jnp.sum(onehot * csum, axis=1) - 1
    counts = csum[-1]
    padded = (counts + tile - 1) // tile * tile
    ends = jnp.cumsum(padded)
    starts = ends - padded
    dest = (starts[e_cat] + pos).astype(jnp.int32)
    n_pad = N_EXPERTS * tile
    n_rows = 2 * n + n_pad
    seg_first = jnp.concatenate([starts + counts, ends[-1:]])
    seg_size = jnp.concatenate([padded - counts, (n_rows - ends[-1])[None]])
    seg_end = jnp.cumsum(seg_size)
    j = jnp.arange(n_pad, dtype=jnp.int32)
    seg = jnp.sum((j[:, None] >= seg_end[None, :]).astype(jnp.int32), axis=1)
    pad_rows = (seg_first[seg] + j - (seg_end - seg_size)[seg]).astype(jnp.int32)
    tile_start = jnp.arange(n_rows // tile, dtype=jnp.int32) * tile
    tile_expert = jnp.minimum(jnp.searchsorted(ends, tile_start, side="right"), N_EXPERTS - 1).astype(jnp.int32)
    tile_valid = (tile_start < ends[-1]).astype(jnp.int32)
    return dest, pad_rows, n_rows, tile_expert, tile_valid


def _expert_kernel(te_ref, tv_ref, xs_ref, wgu_ref, wd_ref, ys_ref, wgu_scr):
    i = pl.program_id(0)
    F = wd_ref.shape[0]

    @pl.when((i == 0) | (te_ref[i] != te_ref[jnp.maximum(i - 1, 0)]))
    def _new_expert():
        wgu_scr[...] = wgu_ref[...].astype(BF16)

    @pl.when(tv_ref[i] > 0)
    def _compute():
        lo, hi = _unpack_bf16_pairs(_load_packed(xs_ref))
        gu = (jnp.dot(lo.astype(BF16), wgu_scr[:PACK_W, :], preferred_element_type=F32)
              + jnp.dot(hi.astype(BF16), wgu_scr[PACK_W:, :], preferred_element_type=F32))
        act = (_silu(gu[:, :F]) * gu[:, F:]).astype(BF16)
        y = jnp.dot(act, wd_ref[...], preferred_element_type=F32)
        _store_packed(ys_ref, _pack_bf16_pairs(y))

    @pl.when(tv_ref[i] == 0)
    def _unused_tile():
        ys_ref[...] = jnp.zeros_like(ys_ref)


def _experts(xs, tile_expert, tile_valid, w_gu, w_down, layer):
    _, n_rows, _ = xs.shape
    _, _, F, D = w_down.shape
    tile = EXPERT_TILE
    rows = pl.BlockSpec((PACK_CHUNKS, tile, LANES), lambda i, te, tv: (0, i, 0))
    return pl.pallas_call(
        _expert_kernel,
        grid_spec=pltpu.PrefetchScalarGridSpec(
            num_scalar_prefetch=2,
            grid=(n_rows // tile,),
            in_specs=[rows,
                      pl.BlockSpec((None, None, D, 2 * F), lambda i, te, tv: (layer, te[i], 0, 0)),
                      pl.BlockSpec((None, None, F, D), lambda i, te, tv: (layer, te[i], 0, 0))],
            out_specs=rows,
            scratch_shapes=[pltpu.VMEM((D, 2 * F), BF16)]),
        out_shape=jax.ShapeDtypeStruct(xs.shape, jnp.int32),
        compiler_params=_params(("arbitrary",)),
    )(tile_expert, tile_valid, xs, w_gu, w_down)


def _combine_kernel(*refs, tm, ctx_len, skip, final):
    x_ref, g2_ref, cg2_ref, route_ref, ya_ref, yb_ref = refs[:6]
    fg_ref = refs[6] if final else None
    o_ref = refs[-1]
    is_ctx = _row_is_ctx(pl.program_id(1) + skip, tm, ctx_len)
    route = route_ref[...]
    lane = lax.broadcasted_iota(jnp.int32, route.shape, 1)
    gate_a = jnp.sum(jnp.where(lane == 2, route, 0.0), axis=-1, keepdims=True)
    gate_b = jnp.sum(jnp.where(lane == 3, route, 0.0), axis=-1, keepdims=True)
    unpack = lambda ref: jnp.concatenate(_unpack_bf16_pairs(_load_packed(ref)), axis=1)
    f = gate_a * unpack(ya_ref) + gate_b * unpack(yb_ref)
    x = x_ref[...] + jnp.where(is_ctx, cg2_ref[...], g2_ref[...]) * f
    if final:
        x = x * lax.rsqrt(jnp.mean(x * x, axis=-1, keepdims=True) + EPS) * fg_ref[...]
    o_ref[...] = x


def _combine(xt, mod_l, mod_c, route, yg, ctx_len, final_g=None, part=(0, 1), prev=None):
    B, T, D = xt.shape
    bh = B // part[1]
    b0 = part[0] * bh
    final = final_g is not None
    if final:
        tm = SEQ_TILE
        assert ctx_len % tm == 0 and T % tm == 0
        skip = ctx_len // tm
    else:
        tm = _pick_tile(T, 640)
        skip = 0
    nt = T // tm
    row = lambda w: pl.BlockSpec((None, tm, w), lambda b, j: (b + b0, j + skip, 0))
    slot = lambda s: pl.BlockSpec((PACK_CHUNKS, None, tm, LANES), lambda b, j: (0, s, b * nt + j + skip, 0))
    in_specs = [row(D), *_mod_specs(5, D), row(LANES), slot(0), slot(1)]
    args = [xt, mod_l[b0:b0 + bh], mod_c, route, yg, yg]
    if final:
        in_specs.append(pl.BlockSpec((1, D), lambda b, j: (0, 0)))
        args.append(final_g)
    aliases = {}
    if prev is not None:
        in_specs.append(pl.BlockSpec(memory_space=pl.ANY))
        args.append(prev)
        aliases = {len(args) - 1: 0}
    return pl.pallas_call(
        functools.partial(_combine_kernel, tm=tm, ctx_len=ctx_len, skip=skip, final=final),
        grid=(bh, nt - skip),
        in_specs=in_specs,
        out_specs=pl.BlockSpec((None, tm, D), lambda b, j: (b + b0, j, 0)),
        out_shape=jax.ShapeDtypeStruct((B, T - skip * tm, D), F32),
        input_output_aliases=aliases,
        compiler_params=_params(("parallel", "parallel")),
    )(*args)


def _moe(hp, route, xt, mod_l, mod_c, w_gu, w_down, layer, ctx_len, final_g=None):
    B, T, _ = xt.shape
    n = B * T
    dest, pad_rows, n_rows, tile_expert, tile_valid = _routing(route, EXPERT_TILE)
    chunk = jnp.arange(PACK_CHUNKS, dtype=jnp.int32)[:, None] * n_rows
    xs = _sc_dispatch(hp.reshape(PACK_CHUNKS * n, LANES), (chunk + dest[None, :n]).reshape(-1),
                      (chunk + dest[None, n:]).reshape(-1), (chunk + pad_rows[None, :]).reshape(-1),
                      PACK_CHUNKS * n_rows)
    ys = _experts(xs.reshape(PACK_CHUNKS, n_rows, LANES), tile_expert, tile_valid, w_gu, w_down, layer)
    ys = ys.reshape(PACK_CHUNKS * n_rows, LANES)
    parts = 2 if B % 2 == 0 else 1
    nh = n // parts
    out = None
    for h in range(parts):
        rows = jnp.concatenate([dest[h * nh:(h + 1) * nh], dest[n + h * nh:n + (h + 1) * nh]])
        yg = _sc_gather(ys, (chunk + rows[None, :]).reshape(-1)).reshape(PACK_CHUNKS, 2, nh, LANES)
        out = _combine(xt, mod_l, mod_c, route, yg, ctx_len, final_g, part=(h, parts), prev=out)
    return out


def _final_norm_kernel(x_ref, g_ref, o_ref):
    x = x_ref[...]
    o_ref[...] = x * lax.rsqrt(jnp.mean(x * x, axis=-1, keepdims=True) + EPS) * g_ref[...]


def _final_norm(xt, g, ctx_len):
    B, T, D = xt.shape
    S = T - ctx_len
    tm = SEQ_TILE
    assert ctx_len % tm == 0 and S % tm == 0
    skip = ctx_len // tm
    return pl.pallas_call(
        _final_norm_kernel,
        grid=(B, S // tm),
        in_specs=[pl.BlockSpec((None, tm, D), lambda b, j: (b, j + skip, 0)),
                  pl.BlockSpec((1, D), lambda b, j: (0, 0))],
        out_specs=pl.BlockSpec((None, tm, D), lambda b, j: (b, j, 0)),
        out_shape=jax.ShapeDtypeStruct((B, S, D), F32),
        compiler_params=_params(("parallel", "parallel")),
    )(xt, g)


def _rope_tables(seq, ctx_len):
    pos = jnp.arange(seq, dtype=jnp.int32)
    nf = HEAD_DIM // 4
    inv = ROPE_THETA ** (-jnp.arange(nf, dtype=F32) / nf)
    ang = jnp.concatenate([(pos // GRID_W).astype(F32)[:, None] * inv, (pos % GRID_W).astype(F32)[:, None] * inv], axis=-1)
    cos, sin = jnp.cos(ang), jnp.sin(ang)
    reps = LANES // HEAD_DIM
    cos_t = jnp.tile(jnp.concatenate([cos, cos], axis=-1), (1, reps))
    sin_t = jnp.tile(jnp.concatenate([-sin, sin], axis=-1), (1, reps))
    cos_t = jnp.concatenate([jnp.ones((ctx_len, LANES), F32), cos_t], axis=0)
    sin_t = jnp.concatenate([jnp.zeros((ctx_len, LANES), F32), sin_t], axis=0)
    return cos_t, sin_t


def _relayout_w_in(w_in):
    o_lr = ATT_W + 2 * KV_W + 5 * GLA_W
    q_scale = HEAD_DIM ** -0.5 * LOG2E
    gq_scale = GLA_DK ** -0.5
    parts = [w_in[..., :ATT_W] * q_scale, w_in[..., ATT_W:COL_GQ], w_in[..., COL_GQ:COL_GK] * gq_scale,
             w_in[..., COL_GK:o_lr], w_in[..., o_lr + 2 * GLA_LOWRANK:], w_in[..., o_lr:o_lr + 2 * GLA_LOWRANK],
             jnp.zeros(w_in.shape[:-1] + (LANES - 2 * GLA_LOWRANK,), w_in.dtype)]
    return jnp.concatenate(parts, axis=-1).astype(BF16)


def kernel(x, c, ctx, c_ctx, w_mod, b_mod, norm1_g, norm2_g, w_in, w_out, attn_sink, gla_w2_f, gla_b_f, gla_w2_b, gla_b_b, gla_norm_g, conv_w, ffn_w_gu, ffn_w_down, router_w, expert_w_gu, expert_w_down, final_norm_g):
    B, S, D = x.shape
    L = ctx.shape[1]
    depth = w_in.shape[0]
    assert D == D_MODEL and S % GRID_W == 0

    xt = jnp.concatenate([ctx, x], axis=1)
    rows = 16
    cc = jnp.concatenate([c, c_ctx[None, :], jnp.zeros((rows - B - 1, D), F32)], axis=0)
    mod = _adaln(cc, w_mod, b_mod)
    cos_t, sin_t = _rope_tables(S, L)

    w_in_p = _relayout_w_in(w_in)
    w_out_b = _to_bf16(w_out)
    zpad = jnp.zeros((depth, LANES - 2 * GLA_LOWRANK, GLA_W), F32)
    w2_f = jnp.concatenate([gla_w2_f, jnp.zeros_like(gla_w2_b), zpad], axis=1).astype(BF16)
    w2_b = jnp.concatenate([jnp.zeros_like(gla_w2_f), gla_w2_b, zpad], axis=1).astype(BF16)
    gla_ng = jnp.tile(gla_norm_g, (1, GLA_HEADS))
    ffn_gu_b = _to_bf16(ffn_w_gu)
    ffn_down_b = _to_bf16(ffn_w_down)
    exp_down_b = _to_bf16(expert_w_down)
    router_b = jnp.pad(router_w, ((0, 0), (0, 0), (0, LANES - N_EXPERTS))).astype(BF16)

    for l in range(depth):
        mod_l = mod[l, :B].reshape(B, 1, 6 * D)
        mod_c = mod[l, B:B + 1]
        proj = _inproj(xt, norm1_g[l][None, :], mod_l, mod_c, cos_t, sin_t, w_in_p, l, L)
        att = _attention(proj, attn_sink[l], L)
        cv = _short_conv(proj, conv_w[l], L)
        gf = _gla(proj, w2_f[l], gla_b_f[l][None, :], gla_ng[l][None, :], L, reverse=False)
        gb = _gla(proj, w2_b[l], gla_b_b[l][None, :], gla_ng[l][None, :], L, reverse=True)
        if l % 2 == 0:
            xt = _outproj(xt, att, gf, gb, cv, w_out_b, l, norm2_g[l][None, :], mod_l, mod_c, L,
                          ffn=(ffn_gu_b, ffn_down_b, l // 2))
        else:
            xt, hp, route = _outproj(xt, att, gf, gb, cv, w_out_b, l, norm2_g[l][None, :], mod_l, mod_c, L,
                                     router=(router_b, l // 2))
            xt = _moe(hp, route, xt, mod_l, mod_c, expert_w_gu, exp_down_b, l // 2, L,
                      final_g=final_norm_g[None, :] if l == depth - 1 else None)
    return xt if depth % 2 == 0 else _final_norm(xt, final_norm_g[None, :], L)
```

```python
import functools

import jax
import jax.numpy as jnp
from jax import lax
from jax.experimental import pallas as pl
from jax.experimental.pallas import tpu as pltpu
from jax.experimental.pallas import tpu_sc as plsc

F32 = jnp.float32
BF16 = jnp.bfloat16

D_MODEL = 1024
EPS = 1e-6
GRID_W = 64
ROPE_THETA = 10000.0
LOG2E = 1.4426950408889634
HEAD_DIM = 64
ATT_HEADS = 8
ATT_KV_HEADS = 2
ATT_GROUP = ATT_HEADS // ATT_KV_HEADS
ATT_W = ATT_HEADS * HEAD_DIM
KV_W = ATT_KV_HEADS * HEAD_DIM
WINDOW = 128
GLA_HEADS = 4
GLA_DK = 64
GLA_W = GLA_HEADS * GLA_DK
GLA_LOWRANK = 16
GLA_TAU = 16.0
GLA_CHUNK = 64
CONV_CH = 256
N_EXPERTS = 8

LANES = 128
SUBLANES_BF16 = 16

COL_Q = 0
COL_K = ATT_W
COL_V = COL_K + KV_W
ROPE_W = COL_V
COL_GQ = COL_V + KV_W
COL_GK = COL_GQ + GLA_W
COL_GV = COL_GK + GLA_W
COL_GGF = COL_GV + GLA_W
COL_GGB = COL_GGF + GLA_W
COL_CB = COL_GGB + GLA_W
COL_CC = COL_CB + CONV_CH
COL_CX = COL_CC + CONV_CH
COL_LR = COL_CX + CONV_CH
PROJ_W = COL_LR + LANES

SEQ_TILE = 256
VMEM_LIMIT = 56 * 1024 * 1024
CAST_BLOCK_BYTES = 4 * 1024 * 1024

PACK_W = D_MODEL // 2
PACK_CHUNKS = PACK_W // LANES
EXPERT_TILE = 512
SC_CORES = 2
SC_SUBCORES = 16
SC_INDEX_CHUNK = 128


def _silu(v):
    return v / (1.0 + jnp.exp(-v))


def _pick_tile(total, cap):
    best = None
    for t in range(SUBLANES_BF16, cap + 1, SUBLANES_BF16):
        if total % t == 0:
            best = t
    assert best is not None
    return best


def _params(sem):
    return pltpu.CompilerParams(dimension_semantics=sem, vmem_limit_bytes=VMEM_LIMIT)


def _pack_bf16_pairs(v):
    bits = lambda t: lax.bitcast_convert_type(t.astype(BF16).astype(F32), jnp.int32)
    return ((bits(v[:, :PACK_W]) >> 16) & 0xFFFF) | (bits(v[:, PACK_W:]) & -65536)


def _unpack_bf16_pairs(w):
    return lax.bitcast_convert_type(w << 16, F32), lax.bitcast_convert_type(w & -65536, F32)


CAST_STREAMS = 4


def _cast_kernel(*refs):
    o_ref = refs[-1]
    slab = refs[0].shape[0]
    for s, x_ref in enumerate(refs[:-1]):
        o_ref[s * slab:(s + 1) * slab, :] = x_ref[...].astype(BF16)


def _to_bf16(w):
    cols = w.shape[-1]
    rows = w.size // cols
    tr = _pick_tile(rows, max(SUBLANES_BF16, CAST_BLOCK_BYTES // (4 * cols)))
    streams = CAST_STREAMS if tr % (CAST_STREAMS * SUBLANES_BF16) == 0 else 1
    slab = tr // streams
    w2 = w.reshape(rows, cols)
    out = pl.pallas_call(
        _cast_kernel,
        grid=(rows // tr,),
        in_specs=[pl.BlockSpec((slab, cols), lambda i, s=s: (i * streams + s, 0)) for s in range(streams)],
        out_specs=pl.BlockSpec((tr, cols), lambda i: (i, 0)),
        out_shape=jax.ShapeDtypeStruct((rows, cols), BF16),
        compiler_params=_params(("parallel",)),
    )(*([w2] * streams))
    return out.reshape(w.shape)


def _load_packed(ref):
    return jnp.concatenate([ref[k] for k in range(PACK_CHUNKS)], axis=1)


def _store_packed(ref, packed):
    for k in range(PACK_CHUNKS):
        ref[k] = packed[:, k * LANES:(k + 1) * LANES]


def _adaln_kernel(c_ref, w_ref, b_ref, o_ref):
    s = _silu(c_ref[...]).astype(BF16)
    o_ref[...] = jnp.dot(s, w_ref[...].astype(BF16), preferred_element_type=F32) + b_ref[...]


def _adaln(cc, w_mod, b_mod):
    depth, d, n = w_mod.shape
    tn = 1024
    rows = cc.shape[0]
    return pl.pallas_call(
        _adaln_kernel,
        grid=(depth, n // tn),
        in_specs=[pl.BlockSpec((rows, d), lambda l, j: (0, 0)),
                  pl.BlockSpec((None, d, tn), lambda l, j: (l, 0, j)),
                  pl.BlockSpec((None, 1, tn), lambda l, j: (l, 0, j))],
        out_specs=pl.BlockSpec((None, rows, tn), lambda l, j: (l, 0, j)),
        out_shape=jax.ShapeDtypeStruct((depth, rows, n), F32),
        compiler_params=_params(("parallel", "parallel")),
    )(cc, w_mod, b_mod.reshape(depth, 1, n))


def _mod_specs(k, d):
    return [pl.BlockSpec((None, 1, d), lambda b, j, *_: (b, 0, k)),
            pl.BlockSpec((1, d), lambda b, j, *_: (0, k))]


def _row_is_ctx(j, tm, ctx_len):
    return (j * tm + lax.broadcasted_iota(jnp.int32, (tm, 1), 0)) < ctx_len


def _inproj_kernel(x_ref, g_ref, sc_ref, csc_ref, sh_ref, csh_ref, cos_ref, sin_ref, w_ref, o_ref, *, tm, ctx_len):
    x = x_ref[...]
    n = x * lax.rsqrt(jnp.mean(x * x, axis=-1, keepdims=True) + EPS) * g_ref[...]
    is_ctx = _row_is_ctx(pl.program_id(1), tm, ctx_len)
    scale = jnp.where(is_ctx, csc_ref[...], sc_ref[...])
    shift = jnp.where(is_ctx, csh_ref[...], sh_ref[...])
    h = (n * (1.0 + scale) + shift).astype(BF16)
    qk = jnp.dot(h, w_ref[:, :ROPE_W], preferred_element_type=F32)
    lower_half = (lax.broadcasted_iota(jnp.int32, (tm, LANES), 1) % HEAD_DIM) < (HEAD_DIM // 2)
    cos = cos_ref[...]
    sin = sin_ref[...]
    for i in range(ROPE_W // LANES):
        t = qk[:, i * LANES:(i + 1) * LANES]
        partner = jnp.where(lower_half, pltpu.roll(t, LANES - HEAD_DIM // 2, axis=1), pltpu.roll(t, HEAD_DIM // 2, axis=1))
        o_ref[:, i * LANES:(i + 1) * LANES] = (t * cos + partner * sin).astype(BF16)
    o_ref[:, ROPE_W:] = jnp.dot(h, w_ref[:, ROPE_W:], preferred_element_type=F32).astype(BF16)


def _inproj(xt, g, mod_l, mod_c, cos_t, sin_t, w, layer, ctx_len):
    B, T, D = xt.shape
    tm = _pick_tile(T, 640)
    kern = functools.partial(_inproj_kernel, tm=tm, ctx_len=ctx_len)
    return pl.pallas_call(
        kern,
        grid=(B, T // tm),
        in_specs=[pl.BlockSpec((None, tm, D), lambda b, j: (b, j, 0)),
                  pl.BlockSpec((1, D), lambda b, j: (0, 0)),
                  *_mod_specs(1, D), *_mod_specs(0, D),
                  pl.BlockSpec((tm, LANES), lambda b, j: (j, 0)),
                  pl.BlockSpec((tm, LANES), lambda b, j: (j, 0)),
                  pl.BlockSpec((None, D, PROJ_W), lambda b, j: (layer, 0, 0))],
        out_specs=pl.BlockSpec((None, tm, PROJ_W), lambda b, j: (b, j, 0)),
        out_shape=jax.ShapeDtypeStruct((B, T, PROJ_W), BF16),
        compiler_params=_params(("parallel", "parallel")),
    )(xt, g, mod_l, mod_c, mod_l, mod_c, cos_t, sin_t, w)


def _attn_kernel(sink_ref, q_ref, kc_ref, vc_ref, km_ref, vm_ref, kp_ref, vp_ref, kn_ref, vn_ref, o_ref, *, n_tiles):
    j = pl.program_id(1)
    nt_dims = (((1,), (1,)), ((), ()))
    tn_dims = (((0,), (0,)), ((), ()))

    def scores(rows, n_rows, g, k_ctx, k_win=None, mask_win=None):
        heads = [g * ATT_GROUP + hh for hh in range(ATT_GROUP)]
        q = jnp.concatenate([q_ref[rows, h * HEAD_DIM:(h + 1) * HEAD_DIM] for h in heads], axis=0)
        head_col = lax.broadcasted_iota(jnp.int32, (1, ATT_GROUP * n_rows), 1) // n_rows
        sink = jnp.full((1, ATT_GROUP * n_rows), sink_ref[heads[0]] * LOG2E, F32)
        for hh in range(1, ATT_GROUP):
            sink = jnp.where(head_col == hh, sink_ref[heads[hh]] * LOG2E, sink)
        s_c = lax.dot_general(k_ctx, q, nt_dims, preferred_element_type=F32)
        s_w = None
        if k_win is not None:
            s_w = jnp.where(mask_win, lax.dot_general(k_win, q, nt_dims, preferred_element_type=F32), -1e30)
        return sink, s_c, s_w

    def finish(item, v_ctx, v_win=None):
        sink, s_c, s_w = item
        m = jnp.maximum(jnp.max(s_c, axis=0, keepdims=True), sink)
        if s_w is not None:
            m = jnp.maximum(m, jnp.max(s_w, axis=0, keepdims=True))
        p_c = jnp.exp2(s_c - m)
        den = jnp.sum(p_c, axis=0, keepdims=True) + jnp.exp2(sink - m)
        o = lax.dot_general(v_ctx, p_c.astype(BF16), tn_dims, preferred_element_type=F32)
        if s_w is not None:
            p_w = jnp.exp2(s_w - m)
            den += jnp.sum(p_w, axis=0, keepdims=True)
            o += lax.dot_general(v_win, p_w.astype(BF16), tn_dims, preferred_element_type=F32)
        return o / den

    def store(rows, n_rows, outs):
        t = jnp.transpose(jnp.concatenate(outs, axis=0))
        for g in range(ATT_KV_HEADS):
            for hh in range(ATT_GROUP):
                h = g * ATT_GROUP + hh
                o_ref[rows, h * HEAD_DIM:(h + 1) * HEAD_DIM] = (
                    t[hh * n_rows:(hh + 1) * n_rows, g * HEAD_DIM:(g + 1) * HEAD_DIM].astype(BF16))

    def run(blocks):
        flat = [(bi, s, f) for bi, (_, _, work) in enumerate(blocks) for s, f in work]
        outs = [[] for _ in blocks]
        pending = None
        for bi, score_fn, finish_fn in flat + [(None, None, None)]:
            item = None if score_fn is None else score_fn()
            if pending is not None:
                pbi, pitem, pfinish = pending
                outs[pbi].append(pfinish(pitem))
                if len(outs[pbi]) == ATT_KV_HEADS:
                    store(blocks[pbi][0], blocks[pbi][1], outs[pbi])
            pending = None if score_fn is None else (bi, item, finish_fn)

    @pl.when(j == 0)
    def _context_queries():
        work = []
        for g in range(ATT_KV_HEADS):
            gs = slice(g * HEAD_DIM, (g + 1) * HEAD_DIM)
            work.append((functools.partial(scores, slice(0, SEQ_TILE), SEQ_TILE, g, kc_ref[:, gs]),
                         functools.partial(finish, v_ctx=vc_ref[:, gs])))
        run([(slice(0, SEQ_TILE), SEQ_TILE, work)])

    @pl.when(j > 0)
    def _latent_queries():
        w = lax.broadcasted_iota(jnp.int32, (3 * WINDOW, ATT_GROUP * WINDOW), 0)
        r = lax.broadcasted_iota(jnp.int32, (3 * WINDOW, ATT_GROUP * WINDOW), 1) % WINDOW
        band = (w >= r) & (w <= r + 2 * WINDOW)
        blocks = []
        for sub in range(SEQ_TILE // WINDOW):
            rows = slice(sub * WINDOW, (sub + 1) * WINDOW)
            if sub == 0:
                kprev, vprev = kp_ref[...], vp_ref[...]
                kmid, vmid = km_ref[:WINDOW], vm_ref[:WINDOW]
                knext, vnext = km_ref[WINDOW:], vm_ref[WINDOW:]
                w_lo = jnp.where(j > 1, 0, WINDOW)
                w_hi = 3 * WINDOW
            else:
                kprev, vprev = km_ref[:WINDOW], vm_ref[:WINDOW]
                kmid, vmid = km_ref[WINDOW:], vm_ref[WINDOW:]
                knext, vnext = kn_ref[...], vn_ref[...]
                w_lo = 0
                w_hi = jnp.where(j < n_tiles - 1, 3 * WINDOW, 2 * WINDOW)
            mask = band & (w >= w_lo) & (w < w_hi)
            work = []
            for g in range(ATT_KV_HEADS):
                gs = slice(g * HEAD_DIM, (g + 1) * HEAD_DIM)
                k_win = jnp.concatenate([kprev[:, gs], kmid[:, gs], knext[:, gs]], axis=0)
                v_win = jnp.concatenate([vprev[:, gs], vmid[:, gs], vnext[:, gs]], axis=0)
                work.append((functools.partial(scores, rows, WINDOW, g, kc_ref[:, gs], k_win, mask),
                             functools.partial(finish, v_ctx=vc_ref[:, gs], v_win=v_win)))
            blocks.append((rows, WINDOW, work))
        run(blocks)


def _attention(proj, sink, ctx_len):
    B, T, _ = proj.shape
    assert ctx_len == SEQ_TILE and T % SEQ_TILE == 0
    n_tiles = T // SEQ_TILE
    n_win = T // WINDOW
    kcol, vcol = COL_K // KV_W, COL_V // KV_W
    per_tile = SEQ_TILE // WINDOW
    tile = lambda col: pl.BlockSpec((None, SEQ_TILE, KV_W), lambda b, j: (b, j, col))
    ctx = lambda col: pl.BlockSpec((None, SEQ_TILE, KV_W), lambda b, j: (b, 0, col))
    prev = lambda col: pl.BlockSpec((None, WINDOW, KV_W), lambda b, j: (b, jnp.maximum(j * per_tile - 1, 0), col))
    nxt = lambda col: pl.BlockSpec((None, WINDOW, KV_W), lambda b, j: (b, jnp.minimum((j + 1) * per_tile, n_win - 1), col))
    return pl.pallas_call(
        functools.partial(_attn_kernel, n_tiles=n_tiles),
        grid=(B, n_tiles),
        in_specs=[pl.BlockSpec(memory_space=pltpu.SMEM),
                  pl.BlockSpec((None, SEQ_TILE, ATT_W), lambda b, j: (b, j, 0)),
                  ctx(kcol), ctx(vcol), tile(kcol), tile(vcol), prev(kcol), prev(vcol), nxt(kcol), nxt(vcol)],
        out_specs=pl.BlockSpec((None, SEQ_TILE, ATT_W), lambda b, j: (b, j, 0)),
        out_shape=jax.ShapeDtypeStruct((B, T, ATT_W), BF16),
        compiler_params=_params(("parallel", "parallel")),
    )(sink, *([proj] * 9))


def _conv_kernel(b_ref, c_ref, x_ref, w_ref, o_ref, u_scr, *, ctx_len):
    T = c_ref.shape[0]
    pad = 8
    u = c_ref[...].astype(F32) * x_ref[...].astype(F32)
    zero_row = jnp.zeros((1, CONV_CH), F32)
    u_scr[pad - 1:pad, :] = zero_row
    u_scr[pad:pad + T, :] = u
    u_scr[pad + T:pad + T + 1, :] = zero_row
    row = lax.broadcasted_iota(jnp.int32, (T, 1), 0)
    u_prev = jnp.where(row == ctx_len, 0.0, u_scr[pad - 1:pad - 1 + T, :])
    u_next = jnp.where(row == ctx_len - 1, 0.0, u_scr[pad + 1:pad + 1 + T, :])
    y = w_ref[0:1, :] * u_prev + w_ref[1:2, :] * u + w_ref[2:3, :] * u_next
    o_ref[...] = (b_ref[...].astype(F32) * y).astype(BF16)


def _short_conv(proj, conv_w, ctx_len):
    B, T, _ = proj.shape
    col = lambda c0: pl.BlockSpec((None, T, CONV_CH), lambda b: (b, 0, c0 // CONV_CH))
    return pl.pallas_call(
        functools.partial(_conv_kernel, ctx_len=ctx_len),
        grid=(B,),
        in_specs=[col(COL_CB), col(COL_CC), col(COL_CX), pl.BlockSpec((3, CONV_CH), lambda b: (0, 0))],
        out_specs=pl.BlockSpec((None, T, CONV_CH), lambda b: (b, 0, 0)),
        out_shape=jax.ShapeDtypeStruct((B, T, CONV_CH), BF16),
        scratch_shapes=[pltpu.VMEM((T + 16, CONV_CH), F32)],
        compiler_params=_params(("parallel",)),
    )(proj, proj, proj, conv_w)


def _split2(v):
    hi = v.astype(BF16)
    lo = (v - hi.astype(F32)).astype(BF16)
    return hi, lo


def _gla_kernel(q_ref, k_ref, v_ref, gate_ref, lr_ref, w2_ref, bias_ref, ng_ref, o_ref,
                oacc_scr, qs_scr, u_scr, dec_scr, sin_scr, *, reverse, n_ctx_chunks):
    T = q_ref.shape[0]
    C = GLA_CHUNK
    n_tiles = T // SEQ_TILE
    n_chunks = T // C
    per_tile = SEQ_TILE // C
    nt_dims = (((1,), (1,)), ((), ()))
    tn_dims = (((0,), (0,)), ((), ()))

    r = lax.broadcasted_iota(jnp.int32, (SEQ_TILE, SEQ_TILE), 0)
    c = lax.broadcasted_iota(jnp.int32, (SEQ_TILE, SEQ_TILE), 1)
    same_chunk = (r // C) == (c // C)
    causal = same_chunk & ((c >= r) if reverse else (c <= r))
    cum_mat = jnp.where(causal, 1.0, 0.0).astype(BF16)
    same_head = (r // GLA_DK) == (c // GLA_DK)
    head_mean = jnp.where(same_head, 1.0 / GLA_DK, 0.0).astype(BF16)
    lane_head = lax.broadcasted_iota(jnp.int32, (C, GLA_W), 1) // GLA_DK

    def phase1(i, carry):
        rows = pl.ds(pl.multiple_of(i * SEQ_TILE, SEQ_TILE), SEQ_TILE)
        z = jnp.dot(lr_ref[rows, :], w2_ref[...], preferred_element_type=F32) + bias_ref[...]
        la = (jnp.minimum(z, 0.0) - jnp.log(1.0 + jnp.exp(-jnp.abs(z)))) / GLA_TAU
        hi, lo = _split2(la)
        b = jnp.dot(cum_mat, hi, preferred_element_type=F32) + jnp.dot(cum_mat, lo, preferred_element_type=F32)
        b3 = b.reshape(per_tile, C, GLA_W)
        total = b3[:, 0:1, :] if reverse else b3[:, C - 1:C, :]
        b_last = jnp.broadcast_to(total, (per_tile, C, GLA_W)).reshape(SEQ_TILE, GLA_W)
        b_ref = 0.5 * b_last
        q = q_ref[rows, :].astype(F32)
        k = k_ref[rows, :].astype(F32)
        v = v_ref[rows, :]
        e_fwd = jnp.exp(b - b_ref)
        e_bwd = jnp.exp(b_ref - b)
        e_half = jnp.exp(b_ref)
        qe = (q * e_fwd).astype(BF16)
        ke = (k * e_bwd).astype(BF16)
        ku = (k * (e_bwd * e_half)).astype(BF16)
        qs_scr[rows, :] = (q * (e_fwd * e_half)).astype(BF16)
        dec = e_half * e_half
        for h in range(GLA_HEADS):
            hs = slice(h * GLA_DK, (h + 1) * GLA_DK)
            s = lax.dot_general(qe[:, hs], ke[:, hs], nt_dims, preferred_element_type=F32)
            a = jnp.where(causal, s, 0.0).astype(BF16)
            oacc_scr[rows, hs] = jnp.dot(a, v[:, hs], preferred_element_type=F32)
        for ci in range(per_tile):
            cs = slice(ci * C, (ci + 1) * C)
            full = lax.dot_general(v[cs, :], ku[cs, :], tn_dims, preferred_element_type=F32)
            ut = full[0:C, :]
            for h in range(1, GLA_HEADS):
                ut = jnp.where(lane_head == h, full[h * C:(h + 1) * C, :], ut)
            u_scr[i * per_tile + ci] = ut
            dec_scr[i * per_tile + ci] = dec[ci * C:ci * C + 1, :]
        return carry

    lax.fori_loop(0, n_tiles, phase1, 0, unroll=4)

    def phase2(i, st):
        if reverse:
            ci = jnp.where(i < n_ctx_chunks, n_ctx_chunks - 1 - i, n_chunks - 1 + n_ctx_chunks - i)
        else:
            ci = i
        sin_scr[ci] = st.astype(BF16)
        return st * dec_scr[ci] + u_scr[ci]

    lax.fori_loop(0, n_chunks, phase2, jnp.zeros((C, GLA_W), F32))

    def phase3(i, carry):
        rows = pl.ds(pl.multiple_of(i * SEQ_TILE, SEQ_TILE), SEQ_TILE)
        qs = qs_scr[rows, :]
        parts = []
        for ci in range(per_tile):
            st = sin_scr[i * per_tile + ci]
            st_heads = jnp.where(same_head, jnp.concatenate([st] * GLA_HEADS, axis=0), jnp.zeros((), BF16))
            parts.append(lax.dot_general(qs[ci * C:(ci + 1) * C, :], st_heads, nt_dims, preferred_element_type=F32))
        o = oacc_scr[rows, :] + jnp.concatenate(parts, axis=0)
        ms = jnp.dot((o * o).astype(BF16), head_mean, preferred_element_type=F32)
        y = o * lax.rsqrt(ms + EPS) * ng_ref[...]
        o_ref[rows, :] = (y * _silu(gate_ref[rows, :].astype(F32))).astype(BF16)
        return carry

    lax.fori_loop(0, n_tiles, phase3, 0, unroll=4)


def _gla(proj, w2pad, bias, ng, ctx_len, reverse):
    B, T, _ = proj.shape
    assert T % SEQ_TILE == 0 and ctx_len % GLA_CHUNK == 0
    n_chunks = T // GLA_CHUNK
    col = lambda c0: pl.BlockSpec((None, T, GLA_W), lambda b: (b, 0, c0 // GLA_W))
    gate_col = COL_GGB if reverse else COL_GGF
    kern = functools.partial(_gla_kernel, reverse=reverse, n_ctx_chunks=ctx_len // GLA_CHUNK)
    return pl.pallas_call(
        kern,
        grid=(B,),
        in_specs=[col(COL_GQ), col(COL_GK), col(COL_GV), col(gate_col),
                  pl.BlockSpec((None, T, LANES), lambda b: (b, 0, COL_LR // LANES)),
                  pl.BlockSpec((LANES, GLA_W), lambda b: (0, 0)),
                  pl.BlockSpec((1, GLA_W), lambda b: (0, 0)),
                  pl.BlockSpec((1, GLA_W), lambda b: (0, 0))],
        out_specs=pl.BlockSpec((None, T, GLA_W), lambda b: (b, 0, 0)),
        out_shape=jax.ShapeDtypeStruct((B, T, GLA_W), BF16),
        scratch_shapes=[pltpu.VMEM((T, GLA_W), F32),
                        pltpu.VMEM((T, GLA_W), BF16),
                        pltpu.VMEM((n_chunks, GLA_CHUNK, GLA_W), F32),
                        pltpu.VMEM((n_chunks, 1, GLA_W), F32),
                        pltpu.VMEM((n_chunks, GLA_CHUNK, GLA_W), BF16)],
        compiler_params=_params(("parallel",)),
    )(proj, proj, proj, proj, proj, w2pad, bias, ng)


MXU_TILE = 256
FFN_CHUNK = 3 * MXU_TILE


def _swiglu_resident(h, wgu_ref, wd_ref):
    F = wd_ref.shape[0]
    y = None
    for c0 in range(0, F, FFN_CHUNK):
        c1 = min(c0 + FFN_CHUNK, F)
        g = jnp.dot(h, wgu_ref[:, c0:c1], preferred_element_type=F32)
        u = jnp.dot(h, wgu_ref[:, F + c0:F + c1], preferred_element_type=F32)
        part = jnp.dot((_silu(g) * u).astype(BF16), wd_ref[c0:c1, :], preferred_element_type=F32)
        y = part if y is None else y + part
    return y


def _outproj_kernel(*refs, tm, ctx_len, moe):
    if moe:
        (x_ref, att_ref, gf_ref, gb_ref, cv_ref, w_ref, g1_ref, cg1_ref, ng_ref, sc_ref, csc_ref, sh_ref, csh_ref,
         router_ref, xo_ref, hp_ref, route_ref) = refs
    else:
        (x_ref, att_ref, gf_ref, gb_ref, cv_ref, w_ref, g1_ref, cg1_ref, ng_ref, sc_ref, csc_ref, sh_ref, csh_ref,
         g2_ref, cg2_ref, wgu_ref, wd_ref, xo_ref) = refs
    is_ctx = _row_is_ctx(pl.program_id(1), tm, ctx_len)
    gla = (gf_ref[...].astype(F32) + gb_ref[...].astype(F32)).astype(BF16)
    y = jnp.dot(att_ref[...], w_ref[:ATT_W, :], preferred_element_type=F32)
    y += jnp.dot(gla, w_ref[ATT_W:ATT_W + GLA_W, :], preferred_element_type=F32)
    y += jnp.dot(cv_ref[...], w_ref[ATT_W + GLA_W:, :], preferred_element_type=F32)
    x = x_ref[...] + jnp.where(is_ctx, cg1_ref[...], g1_ref[...]) * y
    if moe:
        xo_ref[...] = x
    n = x * lax.rsqrt(jnp.mean(x * x, axis=-1, keepdims=True) + EPS) * ng_ref[...]
    scale = jnp.where(is_ctx, csc_ref[...], sc_ref[...])
    shift = jnp.where(is_ctx, csh_ref[...], sh_ref[...])
    h = n * (1.0 + scale) + shift
    hi = h.astype(BF16)
    if not moe:
        xo_ref[...] = x + jnp.where(is_ctx, cg2_ref[...], g2_ref[...]) * _swiglu_resident(hi, wgu_ref, wd_ref)
    else:
        _store_packed(hp_ref, _pack_bf16_pairs(h))
        logits = jnp.dot(hi, router_ref[...], preferred_element_type=F32)
        lane = lax.broadcasted_iota(jnp.int32, logits.shape, 1)
        neg = -jnp.inf
        lg = jnp.where(lane < N_EXPERTS, logits, neg)
        m1 = jnp.max(lg, axis=-1, keepdims=True)
        i1 = jnp.min(jnp.where(lg == m1, lane, LANES), axis=-1, keepdims=True)
        lg2 = jnp.where(lane == i1, neg, lg)
        m2 = jnp.max(lg2, axis=-1, keepdims=True)
        i2 = jnp.min(jnp.where(lg2 == m2, lane, LANES), axis=-1, keepdims=True)
        e2 = jnp.exp(m2 - m1)
        gate1 = 1.0 / (1.0 + e2)
        gate2 = e2 / (1.0 + e2)
        route_ref[...] = jnp.where(lane == 0, i1.astype(F32), jnp.where(lane == 1, i2.astype(F32),
                                   jnp.where(lane == 2, gate1, jnp.where(lane == 3, gate2, 0.0))))


def _outproj(xt, att, gf, gb, cv, w_out, layer, ng, mod_l, mod_c, ctx_len, router=None, ffn=None):
    B, T, D = xt.shape
    tm = _pick_tile(T, 640)
    moe = router is not None
    assert moe != (ffn is not None)
    row = lambda w: pl.BlockSpec((None, tm, w), lambda b, j: (b, j, 0))
    const = lambda shape: pl.BlockSpec(shape, lambda b, j: (0,) * len(shape))
    in_specs = [row(D), row(ATT_W), row(GLA_W), row(GLA_W), row(CONV_CH),
                pl.BlockSpec((None, D, D), lambda b, j: (layer, 0, 0)),
                *_mod_specs(2, D), const((1, D)), *_mod_specs(4, D), *_mod_specs(3, D)]
    args = [xt, att, gf, gb, cv, w_out, mod_l, mod_c, ng, mod_l, mod_c, mod_l, mod_c]
    nt = T // tm
    if moe:
        in_specs.append(pl.BlockSpec((None, D, LANES), lambda b, j: (router[1], 0, 0)))
        args.append(router[0])
        out_specs = [row(D), pl.BlockSpec((PACK_CHUNKS, tm, LANES), lambda b, j: (0, b * nt + j, 0)), row(LANES)]
        out_shape = [jax.ShapeDtypeStruct((B, T, D), F32), jax.ShapeDtypeStruct((PACK_CHUNKS, B * T, LANES), jnp.int32),
                     jax.ShapeDtypeStruct((B, T, LANES), F32)]
    else:
        w_gu, w_down, idx = ffn
        F = w_down.shape[1]
        assert F % MXU_TILE == 0
        resident = lambda shape: pl.BlockSpec((None,) + shape, lambda b, j: (idx, 0, 0), pipeline_mode=pl.Buffered(1))
        in_specs += [*_mod_specs(5, D), resident((D, 2 * F)), resident((F, D))]
        args += [mod_l, mod_c, w_gu, w_down]
        out_specs = row(D)
        out_shape = jax.ShapeDtypeStruct((B, T, D), F32)
    return pl.pallas_call(
        functools.partial(_outproj_kernel, tm=tm, ctx_len=ctx_len, moe=moe),
        grid=(B, T // tm),
        in_specs=in_specs, out_specs=out_specs, out_shape=out_shape,
        compiler_params=_params(("parallel", "parallel")),
    )(*args)


def _sc_gather(table, idx):
    _, width = table.shape
    m = idx.shape[0]
    workers = SC_CORES * SC_SUBCORES
    ch = SC_INDEX_CHUNK
    assert width == LANES and m % (workers * ch) == 0
    n_chunks = m // (workers * ch)
    nb = max(b for b in (4, 3, 2, 1) if n_chunks % b == 0)
    mesh = plsc.VectorSubcoreMesh(core_axis_name="c", subcore_axis_name="s",
                                  num_cores=SC_CORES, num_subcores=SC_SUBCORES)

    @functools.partial(
        pl.kernel, mesh=mesh, out_type=jax.ShapeDtypeStruct((m, width), table.dtype),
        scratch_types=[pltpu.VMEM((n_chunks, ch), jnp.int32), pltpu.VMEM((nb, ch, width), table.dtype)]
        + [pltpu.SemaphoreType.DMA] * (2 * nb))
    def gather(table_hbm, idx_hbm, out_hbm, idx_v, rows_v, *sems):
        wid = lax.axis_index("s") * SC_CORES + lax.axis_index("c")
        base = wid * (n_chunks * ch)
        pltpu.sync_copy(idx_hbm.at[wid], idx_v)

        def fetch(c, b):
            return pltpu.make_async_copy(table_hbm.at[idx_v.at[c]], rows_v.at[b], sems[b])

        def flush(c, b):
            return pltpu.make_async_copy(rows_v.at[b], out_hbm.at[pl.ds(base + c * ch, ch)], sems[nb + b])

        @pl.loop(0, n_chunks, step=nb)
        def _(c0):
            for b in range(nb):
                @pl.when(c0 > 0)
                def _():
                    flush(c0 - nb + b, b).wait()
                fetch(c0 + b, b).start()
            for b in range(nb):
                fetch(c0 + b, b).wait()
                flush(c0 + b, b).start()

        for b in range(nb):
            flush(n_chunks - nb + b, b).wait()

    return gather(table, idx.reshape(workers, n_chunks, ch))


def _sc_dispatch(table, dest_a, dest_b, dest_pad, n_out):
    m, width = table.shape
    n_pad = dest_pad.shape[0]
    workers = SC_CORES * SC_SUBCORES
    ch = SC_INDEX_CHUNK
    assert width == LANES and m % (workers * ch) == 0 and n_pad % (workers * ch) == 0
    n_chunks = m // (workers * ch)
    pad_chunks = n_pad // (workers * ch)
    nb = max(b for b in (4, 3, 2, 1) if n_chunks % b == 0)
    mesh = plsc.VectorSubcoreMesh(core_axis_name="c", subcore_axis_name="s",
                                  num_cores=SC_CORES, num_subcores=SC_SUBCORES)

    @functools.partial(
        pl.kernel, mesh=mesh, out_type=jax.ShapeDtypeStruct((n_out, width), table.dtype),
        scratch_types=[pltpu.VMEM((n_chunks, ch), jnp.int32), pltpu.VMEM((n_chunks, ch), jnp.int32),
                       pltpu.VMEM((pad_chunks, ch), jnp.int32), pltpu.VMEM((nb, ch, width), table.dtype),
                       pltpu.VMEM((ch, width), table.dtype)]
        + [pltpu.SemaphoreType.DMA] * (3 * nb + 1))
    def dispatch(table_hbm, zeros_hbm, da_hbm, db_hbm, dp_hbm, out_hbm, da_v, db_v, dp_v, rows_v, zero_v, *sems):
        wid = lax.axis_index("s") * SC_CORES + lax.axis_index("c")
        base = wid * (n_chunks * ch)
        pltpu.sync_copy(da_hbm.at[wid], da_v)
        pltpu.sync_copy(db_hbm.at[wid], db_v)
        pltpu.sync_copy(dp_hbm.at[wid], dp_v)
        pltpu.sync_copy(zeros_hbm, zero_v)

        def fetch(c, b):
            return pltpu.make_async_copy(table_hbm.at[pl.ds(base + c * ch, ch)], rows_v.at[b], sems[b])

        def put_a(c, b):
            return pltpu.make_async_copy(rows_v.at[b], out_hbm.at[da_v.at[c]], sems[nb + b])

        def put_b(c, b):
            return pltpu.make_async_copy(rows_v.at[b], out_hbm.at[db_v.at[c]], sems[2 * nb + b])

        def put_zero(p):
            return pltpu.make_async_copy(zero_v, out_hbm.at[dp_v.at[p]], sems[3 * nb])

        for p in range(pad_chunks):
            put_zero(p).start()

        @pl.loop(0, n_chunks, step=nb)
        def _(c0):
            for b in range(nb):
                @pl.when(c0 > 0)
                def _():
                    put_a(c0 - nb + b, b).wait()
                    put_b(c0 - nb + b, b).wait()
                fetch(c0 + b, b).start()
            for b in range(nb):
                fetch(c0 + b, b).wait()
                put_a(c0 + b, b).start()
                put_b(c0 + b, b).start()

        for b in range(nb):
            put_a(n_chunks - nb + b, b).wait()
            put_b(n_chunks - nb + b, b).wait()
        for p in range(pad_chunks):
            put_zero(p).wait()

    shape3 = lambda v: v.reshape(workers, -1, ch)
    return dispatch(table, jnp.zeros((ch, width), table.dtype), shape3(dest_a), shape3(dest_b), shape3(dest_pad))


def _routing(route, tile):
    B, T, _ = route.shape
    n = B * T
    assert (2 * n) % tile == 0
    ids = route[..., :2].astype(jnp.int32).reshape(n, 2)
    e_cat = jnp.concatenate([ids[:, 0], ids[:, 1]])
    onehot = (e_cat[:, None] == jnp.arange(N_EXPERTS, dtype=jnp.int32)[None, :]).astype(jnp.int32)
    csum = jnp.cumsum(onehot, axis=0)
    pos = jnp.sum(onehot * csum, axis=1) - 1
    counts = csum[-1]
    padded = (counts + tile - 1) // tile * tile
    ends = jnp.cumsum(padded)
    starts = ends - padded
    dest = (starts[e_cat] + pos).astype(jnp.int32)
    n_pad = N_EXPERTS * tile
    n_rows = 2 * n + n_pad
    seg_first = jnp.concatenate([starts + counts, ends[-1:]])
    seg_size = jnp.concatenate([padded - counts, (n_rows - ends[-1])[None]])
    seg_end = jnp.cumsum(seg_size)
    j = jnp.arange(n_pad, dtype=jnp.int32)
    seg = jnp.sum((j[:, None] >= seg_end[None, :]).astype(jnp.int32), axis=1)
    pad_rows = (seg_first[seg] + j - (seg_end - seg_size)[seg]).astype(jnp.int32)
    tile_start = jnp.arange(n_rows // tile, dtype=jnp.int32) * tile
    tile_expert = jnp.minimum(jnp.searchsorted(ends, tile_start, side="right"), N_EXPERTS - 1).astype(jnp.int32)
    tile_valid = (tile_start < ends[-1]).astype(jnp.int32)
    return dest, pad_rows, n_rows, tile_expert, tile_valid


def _expert_kernel(te_ref, tv_ref, xs_ref, wgu_ref, wd_ref, ys_ref, wgu_scr):
    i = pl.program_id(0)
    F = wd_ref.shape[0]

    @pl.when((i == 0) | (te_ref[i] != te_ref[jnp.maximum(i - 1, 0)]))
    def _new_expert():
        wgu_scr[...] = wgu_ref[...].astype(BF16)

    @pl.when(tv_ref[i] > 0)
    def _compute():
        lo, hi = _unpack_bf16_pairs(_load_packed(xs_ref))
        gu = (jnp.dot(lo.astype(BF16), wgu_scr[:PACK_W, :], preferred_element_type=F32)
              + jnp.dot(hi.astype(BF16), wgu_scr[PACK_W:, :], preferred_element_type=F32))
        act = (_silu(gu[:, :F]) * gu[:, F:]).astype(BF16)
        y = jnp.dot(act, wd_ref[...], preferred_element_type=F32)
        _store_packed(ys_ref, _pack_bf16_pairs(y))

    @pl.when(tv_ref[i] == 0)
    def _unused_tile():
        ys_ref[...] = jnp.zeros_like(ys_ref)


def _experts(xs, tile_expert, tile_valid, w_gu, w_down, layer):
    _, n_rows, _ = xs.shape
    _, _, F, D = w_down.shape
    tile = EXPERT_TILE
    rows = pl.BlockSpec((PACK_CHUNKS, tile, LANES), lambda i, te, tv: (0, i, 0))
    return pl.pallas_call(
        _expert_kernel,
        grid_spec=pltpu.PrefetchScalarGridSpec(
            num_scalar_prefetch=2,
            grid=(n_rows // tile,),
            in_specs=[rows,
                      pl.BlockSpec((None, None, D, 2 * F), lambda i, te, tv: (layer, te[i], 0, 0)),
                      pl.BlockSpec((None, None, F, D), lambda i, te, tv: (layer, te[i], 0, 0))],
            out_specs=rows,
            scratch_shapes=[pltpu.VMEM((D, 2 * F), BF16)]),
        out_shape=jax.ShapeDtypeStruct(xs.shape, jnp.int32),
        compiler_params=_params(("arbitrary",)),
    )(tile_expert, tile_valid, xs, w_gu, w_down)


def _combine_kernel(*refs, tm, ctx_len, skip, final):
    x_ref, g2_ref, cg2_ref, route_ref, ya_ref, yb_ref = refs[:6]
    fg_ref = refs[6] if final else None
    o_ref = refs[-1]
    is_ctx = _row_is_ctx(pl.program_id(1) + skip, tm, ctx_len)
    route = route_ref[...]
    lane = lax.broadcasted_iota(jnp.int32, route.shape, 1)
    gate_a = jnp.sum(jnp.where(lane == 2, route, 0.0), axis=-1, keepdims=True)
    gate_b = jnp.sum(jnp.where(lane == 3, route, 0.0), axis=-1, keepdims=True)
    unpack = lambda ref: jnp.concatenate(_unpack_bf16_pairs(_load_packed(ref)), axis=1)
    f = gate_a * unpack(ya_ref) + gate_b * unpack(yb_ref)
    x = x_ref[...] + jnp.where(is_ctx, cg2_ref[...], g2_ref[...]) * f
    if final:
        x = x * lax.rsqrt(jnp.mean(x * x, axis=-1, keepdims=True) + EPS) * fg_ref[...]
    o_ref[...] = x


def _combine(xt, mod_l, mod_c, route, yg, ctx_len, final_g=None):
    B, T, D = xt.shape
    final = final_g is not None
    if final:
        tm = SEQ_TILE
        assert ctx_len % tm == 0 and T % tm == 0
        skip = ctx_len // tm
    else:
        tm = _pick_tile(T, 640)
        skip = 0
    nt = T // tm
    row = lambda w: pl.BlockSpec((None, tm, w), lambda b, j: (b, j + skip, 0))
    slot = lambda s: pl.BlockSpec((PACK_CHUNKS, None, tm, LANES), lambda b, j: (0, s, b * nt + j + skip, 0))
    in_specs = [row(D), *_mod_specs(5, D), row(LANES), slot(0), slot(1)]
    args = [xt, mod_l, mod_c, route, yg, yg]
    if final:
        in_specs.append(pl.BlockSpec((1, D), lambda b, j: (0, 0)))
        args.append(final_g)
    return pl.pallas_call(
        functools.partial(_combine_kernel, tm=tm, ctx_len=ctx_len, skip=skip, final=final),
        grid=(B, nt - skip),
        in_specs=in_specs,
        out_specs=pl.BlockSpec((None, tm, D), lambda b, j: (b, j, 0)),
        out_shape=jax.ShapeDtypeStruct((B, T - skip * tm, D), F32),
        compiler_params=_params(("parallel", "parallel")),
    )(*args)


def _moe(hp, route, xt, mod_l, mod_c, w_gu, w_down, layer, ctx_len, final_g=None):
    B, T, _ = xt.shape
    n = B * T
    dest, pad_rows, n_rows, tile_expert, tile_valid = _routing(route, EXPERT_TILE)
    chunk = jnp.arange(PACK_CHUNKS, dtype=jnp.int32)[:, None] * n_rows
    xs = _sc_dispatch(hp.reshape(PACK_CHUNKS * n, LANES), (chunk + dest[None, :n]).reshape(-1),
                      (chunk + dest[None, n:]).reshape(-1), (chunk + pad_rows[None, :]).reshape(-1),
                      PACK_CHUNKS * n_rows)
    ys = _experts(xs.reshape(PACK_CHUNKS, n_rows, LANES), tile_expert, tile_valid, w_gu, w_down, layer)
    ys = ys.reshape(PACK_CHUNKS * n_rows, LANES)
    yg = _sc_gather(ys, (chunk + dest[None, :]).reshape(-1)).reshape(PACK_CHUNKS, 2, n, LANES)
    return _combine(xt, mod_l, mod_c, route, yg, ctx_len, final_g)


def _final_norm_kernel(x_ref, g_ref, o_ref):
    x = x_ref[...]
    o_ref[...] = x * lax.rsqrt(jnp.mean(x * x, axis=-1, keepdims=True) + EPS) * g_ref[...]


def _final_norm(xt, g, ctx_len):
    B, T, D = xt.shape
    S = T - ctx_len
    tm = SEQ_TILE
    assert ctx_len % tm == 0 and S % tm == 0
    skip = ctx_len // tm
    return pl.pallas_call(
        _final_norm_kernel,
        grid=(B, S // tm),
        in_specs=[pl.BlockSpec((None, tm, D), lambda b, j: (b, j + skip, 0)),
                  pl.BlockSpec((1, D), lambda b, j: (0, 0))],
        out_specs=pl.BlockSpec((None, tm, D), lambda b, j: (b, j, 0)),
        out_shape=jax.ShapeDtypeStruct((B, S, D), F32),
        compiler_params=_params(("parallel", "parallel")),
    )(xt, g)


def _rope_tables(seq, ctx_len):
    pos = jnp.arange(seq, dtype=jnp.int32)
    nf = HEAD_DIM // 4
    inv = ROPE_THETA ** (-jnp.arange(nf, dtype=F32) / nf)
    ang = jnp.concatenate([(pos // GRID_W).astype(F32)[:, None] * inv, (pos % GRID_W).astype(F32)[:, None] * inv], axis=-1)
    cos, sin = jnp.cos(ang), jnp.sin(ang)
    reps = LANES // HEAD_DIM
    cos_t = jnp.tile(jnp.concatenate([cos, cos], axis=-1), (1, reps))
    sin_t = jnp.tile(jnp.concatenate([-sin, sin], axis=-1), (1, reps))
    cos_t = jnp.concatenate([jnp.ones((ctx_len, LANES), F32), cos_t], axis=0)
    sin_t = jnp.concatenate([jnp.zeros((ctx_len, LANES), F32), sin_t], axis=0)
    return cos_t, sin_t


def _relayout_w_in(w_in):
    o_lr = ATT_W + 2 * KV_W + 5 * GLA_W
    q_scale = HEAD_DIM ** -0.5 * LOG2E
    gq_scale = GLA_DK ** -0.5
    parts = [w_in[..., :ATT_W] * q_scale, w_in[..., ATT_W:COL_GQ], w_in[..., COL_GQ:COL_GK] * gq_scale,
             w_in[..., COL_GK:o_lr], w_in[..., o_lr + 2 * GLA_LOWRANK:], w_in[..., o_lr:o_lr + 2 * GLA_LOWRANK],
             jnp.zeros(w_in.shape[:-1] + (LANES - 2 * GLA_LOWRANK,), w_in.dtype)]
    return jnp.concatenate(parts, axis=-1).astype(BF16)


def kernel(x, c, ctx, c_ctx, w_mod, b_mod, norm1_g, norm2_g, w_in, w_out, attn_sink, gla_w2_f, gla_b_f, gla_w2_b, gla_b_b, gla_norm_g, conv_w, ffn_w_gu, ffn_w_down, router_w, expert_w_gu, expert_w_down, final_norm_g):
    B, S, D = x.shape
    L = ctx.shape[1]
    depth = w_in.shape[0]
    assert D == D_MODEL and S % GRID_W == 0

    xt = jnp.concatenate([ctx, x], axis=1)
    rows = 16
    cc = jnp.concatenate([c, c_ctx[None, :], jnp.zeros((rows - B - 1, D), F32)], axis=0)
    mod = _adaln(cc, w_mod, b_mod)
    cos_t, sin_t = _rope_tables(S, L)

    w_in_p = _relayout_w_in(w_in)
    w_out_b = _to_bf16(w_out)
    zpad = jnp.zeros((depth, LANES - 2 * GLA_LOWRANK, GLA_W), F32)
    w2_f = jnp.concatenate([gla_w2_f, jnp.zeros_like(gla_w2_b), zpad], axis=1).astype(BF16)
    w2_b = jnp.concatenate([jnp.zeros_like(gla_w2_f), gla_w2_b, zpad], axis=1).astype(BF16)
    gla_ng = jnp.tile(gla_norm_g, (1, GLA_HEADS))
    ffn_gu_b = _to_bf16(ffn_w_gu)
    ffn_down_b = _to_bf16(ffn_w_down)
    exp_down_b = _to_bf16(expert_w_down)
    router_b = jnp.pad(router_w, ((0, 0), (0, 0), (0, LANES - N_EXPERTS))).astype(BF16)

    for l in range(depth):
        mod_l = mod[l, :B].reshape(B, 1, 6 * D)
        mod_c = mod[l, B:B + 1]
        proj = _inproj(xt, norm1_g[l][None, :], mod_l, mod_c, cos_t, sin_t, w_in_p, l, L)
        att = _attention(proj, attn_sink[l], L)
        cv = _short_conv(proj, conv_w[l], L)
        gf = _gla(proj, w2_f[l], gla_b_f[l][None, :], gla_ng[l][None, :], L, reverse=False)
        gb = _gla(proj, w2_b[l], gla_b_b[l][None, :], gla_ng[l][None, :], L, reverse=True)
        if l % 2 == 0:
            xt = _outproj(xt, att, gf, gb, cv, w_out_b, l, norm2_g[l][None, :], mod_l, mod_c, L,
                          ffn=(ffn_gu_b, ffn_down_b, l // 2))
        else:
            xt, hp, route = _outproj(xt, att, gf, gb, cv, w_out_b, l, norm2_g[l][None, :], mod_l, mod_c, L,
                                     router=(router_b, l // 2))
            xt = _moe(hp, route, xt, mod_l, mod_c, expert_w_gu, exp_down_b, l // 2, L,
                      final_g=final_norm_g[None, :] if l == depth - 1 else None)
    return xt if depth % 2 == 0 else _final_norm(xt, final_norm_g[None, :], L)
```

```python
import functools

import jax
import jax.numpy as jnp
from jax import lax
from jax.experimental import pallas as pl
from jax.experimental.pallas import tpu as pltpu
from jax.experimental.pallas import tpu_sc as plsc

F32 = jnp.float32
BF16 = jnp.bfloat16

D_MODEL = 1024
EPS = 1e-6
GRID_W = 64
ROPE_THETA = 10000.0
LOG2E = 1.4426950408889634
HEAD_DIM = 64
ATT_HEADS = 8
ATT_KV_HEADS = 2
ATT_GROUP = ATT_HEADS // ATT_KV_HEADS
ATT_W = ATT_HEADS * HEAD_DIM
KV_W = ATT_KV_HEADS * HEAD_DIM
WINDOW = 128
GLA_HEADS = 4
GLA_DK = 64
GLA_W = GLA_HEADS * GLA_DK
GLA_LOWRANK = 16
GLA_TAU = 16.0
GLA_CHUNK = 64
CONV_CH = 256
N_EXPERTS = 8

LANES = 128
SUBLANES_BF16 = 16

COL_Q = 0
COL_K = ATT_W
COL_V = COL_K + KV_W
ROPE_W = COL_V
COL_GQ = COL_V + KV_W
COL_GK = COL_GQ + GLA_W
COL_GV = COL_GK + GLA_W
COL_GGF = COL_GV + GLA_W
COL_GGB = COL_GGF + GLA_W
COL_CB = COL_GGB + GLA_W
COL_CC = COL_CB + CONV_CH
COL_CX = COL_CC + CONV_CH
COL_LR = COL_CX + CONV_CH
PROJ_W = COL_LR + LANES

SEQ_TILE = 256
VMEM_LIMIT = 56 * 1024 * 1024
CAST_BLOCK_BYTES = 4 * 1024 * 1024

PACK_W = D_MODEL // 2
PACK_CHUNKS = PACK_W // LANES
EXPERT_TILE = 512
SC_CORES = 2
SC_SUBCORES = 16
SC_INDEX_CHUNK = 128


def _silu(v):
    return v / (1.0 + jnp.exp(-v))


def _pick_tile(total, cap):
    best = None
    for t in range(SUBLANES_BF16, cap + 1, SUBLANES_BF16):
        if total % t == 0:
            best = t
    assert best is not None
    return best


def _params(sem):
    return pltpu.CompilerParams(dimension_semantics=sem, vmem_limit_bytes=VMEM_LIMIT)


def _pack_bf16_pairs(v):
    bits = lambda t: lax.bitcast_convert_type(t.astype(BF16).astype(F32), jnp.int32)
    return ((bits(v[:, :PACK_W]) >> 16) & 0xFFFF) | (bits(v[:, PACK_W:]) & -65536)


def _unpack_bf16_pairs(w):
    return lax.bitcast_convert_type(w << 16, F32), lax.bitcast_convert_type(w & -65536, F32)


CAST_STREAMS = 4


def _cast_kernel(*refs):
    o_ref = refs[-1]
    slab = refs[0].shape[0]
    for s, x_ref in enumerate(refs[:-1]):
        o_ref[s * slab:(s + 1) * slab, :] = x_ref[...].astype(BF16)


def _to_bf16(w):
    cols = w.shape[-1]
    rows = w.size // cols
    tr = _pick_tile(rows, max(SUBLANES_BF16, CAST_BLOCK_BYTES // (4 * cols)))
    streams = CAST_STREAMS if tr % (CAST_STREAMS * SUBLANES_BF16) == 0 else 1
    slab = tr // streams
    w2 = w.reshape(rows, cols)
    out = pl.pallas_call(
        _cast_kernel,
        grid=(rows // tr,),
        in_specs=[pl.BlockSpec((slab, cols), lambda i, s=s: (i * streams + s, 0)) for s in range(streams)],
        out_specs=pl.BlockSpec((tr, cols), lambda i: (i, 0)),
        out_shape=jax.ShapeDtypeStruct((rows, cols), BF16),
        compiler_params=_params(("parallel",)),
    )(*([w2] * streams))
    return out.reshape(w.shape)


def _load_packed(ref):
    return jnp.concatenate([ref[k] for k in range(PACK_CHUNKS)], axis=1)


def _store_packed(ref, packed):
    for k in range(PACK_CHUNKS):
        ref[k] = packed[:, k * LANES:(k + 1) * LANES]


def _adaln_kernel(c_ref, w_ref, b_ref, o_ref):
    s = _silu(c_ref[...]).astype(BF16)
    o_ref[...] = jnp.dot(s, w_ref[...].astype(BF16), preferred_element_type=F32) + b_ref[...]


def _adaln(cc, w_mod, b_mod):
    depth, d, n = w_mod.shape
    tn = 1024
    rows = cc.shape[0]
    return pl.pallas_call(
        _adaln_kernel,
        grid=(depth, n // tn),
        in_specs=[pl.BlockSpec((rows, d), lambda l, j: (0, 0)),
                  pl.BlockSpec((None, d, tn), lambda l, j: (l, 0, j)),
                  pl.BlockSpec((None, 1, tn), lambda l, j: (l, 0, j))],
        out_specs=pl.BlockSpec((None, rows, tn), lambda l, j: (l, 0, j)),
        out_shape=jax.ShapeDtypeStruct((depth, rows, n), F32),
        compiler_params=_params(("parallel", "parallel")),
    )(cc, w_mod, b_mod.reshape(depth, 1, n))


def _mod_specs(k, d):
    return [pl.BlockSpec((None, 1, d), lambda b, j, *_: (b, 0, k)),
            pl.BlockSpec((1, d), lambda b, j, *_: (0, k))]


def _row_is_ctx(j, tm, ctx_len):
    return (j * tm + lax.broadcasted_iota(jnp.int32, (tm, 1), 0)) < ctx_len


def _inproj_kernel(x_ref, g_ref, sc_ref, csc_ref, sh_ref, csh_ref, cos_ref, sin_ref, w_ref, o_ref, *, tm, ctx_len):
    x = x_ref[...]
    n = x * lax.rsqrt(jnp.mean(x * x, axis=-1, keepdims=True) + EPS) * g_ref[...]
    is_ctx = _row_is_ctx(pl.program_id(1), tm, ctx_len)
    scale = jnp.where(is_ctx, csc_ref[...], sc_ref[...])
    shift = jnp.where(is_ctx, csh_ref[...], sh_ref[...])
    h = (n * (1.0 + scale) + shift).astype(BF16)
    qk = jnp.dot(h, w_ref[:, :ROPE_W], preferred_element_type=F32)
    lower_half = (lax.broadcasted_iota(jnp.int32, (tm, LANES), 1) % HEAD_DIM) < (HEAD_DIM // 2)
    cos = cos_ref[...]
    sin = sin_ref[...]
    for i in range(ROPE_W // LANES):
        t = qk[:, i * LANES:(i + 1) * LANES]
        partner = jnp.where(lower_half, pltpu.roll(t, LANES - HEAD_DIM // 2, axis=1), pltpu.roll(t, HEAD_DIM // 2, axis=1))
        o_ref[:, i * LANES:(i + 1) * LANES] = (t * cos + partner * sin).astype(BF16)
    o_ref[:, ROPE_W:] = jnp.dot(h, w_ref[:, ROPE_W:], preferred_element_type=F32).astype(BF16)


def _inproj(xt, g, mod_l, mod_c, cos_t, sin_t, w, layer, ctx_len):
    B, T, D = xt.shape
    tm = _pick_tile(T, 640)
    kern = functools.partial(_inproj_kernel, tm=tm, ctx_len=ctx_len)
    return pl.pallas_call(
        kern,
        grid=(B, T // tm),
        in_specs=[pl.BlockSpec((None, tm, D), lambda b, j: (b, j, 0)),
                  pl.BlockSpec((1, D), lambda b, j: (0, 0)),
                  *_mod_specs(1, D), *_mod_specs(0, D),
                  pl.BlockSpec((tm, LANES), lambda b, j: (j, 0)),
                  pl.BlockSpec((tm, LANES), lambda b, j: (j, 0)),
                  pl.BlockSpec((None, D, PROJ_W), lambda b, j: (layer, 0, 0))],
        out_specs=pl.BlockSpec((None, tm, PROJ_W), lambda b, j: (b, j, 0)),
        out_shape=jax.ShapeDtypeStruct((B, T, PROJ_W), BF16),
        compiler_params=_params(("parallel", "parallel")),
    )(xt, g, mod_l, mod_c, mod_l, mod_c, cos_t, sin_t, w)


def _attn_kernel(sink_ref, q_ref, kc_ref, vc_ref, km_ref, vm_ref, kp_ref, vp_ref, kn_ref, vn_ref, o_ref, *, n_tiles):
    j = pl.program_id(1)
    nt_dims = (((1,), (1,)), ((), ()))
    tn_dims = (((0,), (0,)), ((), ()))

    def scores(rows, n_rows, g, k_ctx, k_win=None, mask_win=None):
        heads = [g * ATT_GROUP + hh for hh in range(ATT_GROUP)]
        q = jnp.concatenate([q_ref[rows, h * HEAD_DIM:(h + 1) * HEAD_DIM] for h in heads], axis=0)
        head_col = lax.broadcasted_iota(jnp.int32, (1, ATT_GROUP * n_rows), 1) // n_rows
        sink = jnp.full((1, ATT_GROUP * n_rows), sink_ref[heads[0]] * LOG2E, F32)
        for hh in range(1, ATT_GROUP):
            sink = jnp.where(head_col == hh, sink_ref[heads[hh]] * LOG2E, sink)
        s_c = lax.dot_general(k_ctx, q, nt_dims, preferred_element_type=F32)
        s_w = None
        if k_win is not None:
            s_w = jnp.where(mask_win, lax.dot_general(k_win, q, nt_dims, preferred_element_type=F32), -1e30)
        return sink, s_c, s_w

    def finish(item, v_ctx, v_win=None):
        sink, s_c, s_w = item
        m = jnp.maximum(jnp.max(s_c, axis=0, keepdims=True), sink)
        if s_w is not None:
            m = jnp.maximum(m, jnp.max(s_w, axis=0, keepdims=True))
        p_c = jnp.exp2(s_c - m)
        den = jnp.sum(p_c, axis=0, keepdims=True) + jnp.exp2(sink - m)
        o = lax.dot_general(v_ctx, p_c.astype(BF16), tn_dims, preferred_element_type=F32)
        if s_w is not None:
            p_w = jnp.exp2(s_w - m)
            den += jnp.sum(p_w, axis=0, keepdims=True)
            o += lax.dot_general(v_win, p_w.astype(BF16), tn_dims, preferred_element_type=F32)
        return o / den

    def store(rows, n_rows, outs):
        t = jnp.transpose(jnp.concatenate(outs, axis=0))
        for g in range(ATT_KV_HEADS):
            for hh in range(ATT_GROUP):
                h = g * ATT_GROUP + hh
                o_ref[rows, h * HEAD_DIM:(h + 1) * HEAD_DIM] = (
                    t[hh * n_rows:(hh + 1) * n_rows, g * HEAD_DIM:(g + 1) * HEAD_DIM].astype(BF16))

    def run(blocks):
        flat = [(bi, s, f) for bi, (_, _, work) in enumerate(blocks) for s, f in work]
        outs = [[] for _ in blocks]
        pending = None
        for bi, score_fn, finish_fn in flat + [(None, None, None)]:
            item = None if score_fn is None else score_fn()
            if pending is not None:
                pbi, pitem, pfinish = pending
                outs[pbi].append(pfinish(pitem))
                if len(outs[pbi]) == ATT_KV_HEADS:
                    store(blocks[pbi][0], blocks[pbi][1], outs[pbi])
            pending = None if score_fn is None else (bi, item, finish_fn)

    @pl.when(j == 0)
    def _context_queries():
        work = []
        for g in range(ATT_KV_HEADS):
            gs = slice(g * HEAD_DIM, (g + 1) * HEAD_DIM)
            work.append((functools.partial(scores, slice(0, SEQ_TILE), SEQ_TILE, g, kc_ref[:, gs]),
                         functools.partial(finish, v_ctx=vc_ref[:, gs])))
        run([(slice(0, SEQ_TILE), SEQ_TILE, work)])

    @pl.when(j > 0)
    def _latent_queries():
        w = lax.broadcasted_iota(jnp.int32, (3 * WINDOW, ATT_GROUP * WINDOW), 0)
        r = lax.broadcasted_iota(jnp.int32, (3 * WINDOW, ATT_GROUP * WINDOW), 1) % WINDOW
        band = (w >= r) & (w <= r + 2 * WINDOW)
        blocks = []
        for sub in range(SEQ_TILE // WINDOW):
            rows = slice(sub * WINDOW, (sub + 1) * WINDOW)
            if sub == 0:
                kprev, vprev = kp_ref[...], vp_ref[...]
                kmid, vmid = km_ref[:WINDOW], vm_ref[:WINDOW]
                knext, vnext = km_ref[WINDOW:], vm_ref[WINDOW:]
                w_lo = jnp.where(j > 1, 0, WINDOW)
                w_hi = 3 * WINDOW
            else:
                kprev, vprev = km_ref[:WINDOW], vm_ref[:WINDOW]
                kmid, vmid = km_ref[WINDOW:], vm_ref[WINDOW:]
                knext, vnext = kn_ref[...], vn_ref[...]
                w_lo = 0
                w_hi = jnp.where(j < n_tiles - 1, 3 * WINDOW, 2 * WINDOW)
            mask = band & (w >= w_lo) & (w < w_hi)
            work = []
            for g in range(ATT_KV_HEADS):
                gs = slice(g * HEAD_DIM, (g + 1) * HEAD_DIM)
                k_win = jnp.concatenate([kprev[:, gs], kmid[:, gs], knext[:, gs]], axis=0)
                v_win = jnp.concatenate([vprev[:, gs], vmid[:, gs], vnext[:, gs]], axis=0)
                work.append((functools.partial(scores, rows, WINDOW, g, kc_ref[:, gs], k_win, mask),
                             functools.partial(finish, v_ctx=vc_ref[:, gs], v_win=v_win)))
            blocks.append((rows, WINDOW, work))
        run(blocks)


def _attention(proj, sink, ctx_len):
    B, T, _ = proj.shape
    assert ctx_len == SEQ_TILE and T % SEQ_TILE == 0
    n_tiles = T // SEQ_TILE
    n_win = T // WINDOW
    kcol, vcol = COL_K // KV_W, COL_V // KV_W
    per_tile = SEQ_TILE // WINDOW
    tile = lambda col: pl.BlockSpec((None, SEQ_TILE, KV_W), lambda b, j: (b, j, col))
    ctx = lambda col: pl.BlockSpec((None, SEQ_TILE, KV_W), lambda b, j: (b, 0, col))
    prev = lambda col: pl.BlockSpec((None, WINDOW, KV_W), lambda b, j: (b, jnp.maximum(j * per_tile - 1, 0), col))
    nxt = lambda col: pl.BlockSpec((None, WINDOW, KV_W), lambda b, j: (b, jnp.minimum((j + 1) * per_tile, n_win - 1), col))
    return pl.pallas_call(
        functools.partial(_attn_kernel, n_tiles=n_tiles),
        grid=(B, n_tiles),
        in_specs=[pl.BlockSpec(memory_space=pltpu.SMEM),
                  pl.BlockSpec((None, SEQ_TILE, ATT_W), lambda b, j: (b, j, 0)),
                  ctx(kcol), ctx(vcol), tile(kcol), tile(vcol), prev(kcol), prev(vcol), nxt(kcol), nxt(vcol)],
        out_specs=pl.BlockSpec((None, SEQ_TILE, ATT_W), lambda b, j: (b, j, 0)),
        out_shape=jax.ShapeDtypeStruct((B, T, ATT_W), BF16),
        compiler_params=_params(("parallel", "parallel")),
    )(sink, *([proj] * 9))


def _conv_kernel(b_ref, c_ref, x_ref, w_ref, o_ref, u_scr, *, ctx_len):
    T = c_ref.shape[0]
    pad = 8
    u = c_ref[...].astype(F32) * x_ref[...].astype(F32)
    zero_row = jnp.zeros((1, CONV_CH), F32)
    u_scr[pad - 1:pad, :] = zero_row
    u_scr[pad:pad + T, :] = u
    u_scr[pad + T:pad + T + 1, :] = zero_row
    row = lax.broadcasted_iota(jnp.int32, (T, 1), 0)
    u_prev = jnp.where(row == ctx_len, 0.0, u_scr[pad - 1:pad - 1 + T, :])
    u_next = jnp.where(row == ctx_len - 1, 0.0, u_scr[pad + 1:pad + 1 + T, :])
    y = w_ref[0:1, :] * u_prev + w_ref[1:2, :] * u + w_ref[2:3, :] * u_next
    o_ref[...] = (b_ref[...].astype(F32) * y).astype(BF16)


def _short_conv(proj, conv_w, ctx_len):
    B, T, _ = proj.shape
    col = lambda c0: pl.BlockSpec((None, T, CONV_CH), lambda b: (b, 0, c0 // CONV_CH))
    return pl.pallas_call(
        functools.partial(_conv_kernel, ctx_len=ctx_len),
        grid=(B,),
        in_specs=[col(COL_CB), col(COL_CC), col(COL_CX), pl.BlockSpec((3, CONV_CH), lambda b: (0, 0))],
        out_specs=pl.BlockSpec((None, T, CONV_CH), lambda b: (b, 0, 0)),
        out_shape=jax.ShapeDtypeStruct((B, T, CONV_CH), BF16),
        scratch_shapes=[pltpu.VMEM((T + 16, CONV_CH), F32)],
        compiler_params=_params(("parallel",)),
    )(proj, proj, proj, conv_w)


def _split2(v):
    hi = v.astype(BF16)
    lo = (v - hi.astype(F32)).astype(BF16)
    return hi, lo


def _gla_kernel(q_ref, k_ref, v_ref, gate_ref, lr_ref, w2_ref, bias_ref, ng_ref, o_ref,
                oacc_scr, qs_scr, u_scr, dec_scr, sin_scr, *, reverse, n_ctx_chunks):
    T = q_ref.shape[0]
    C = GLA_CHUNK
    n_tiles = T // SEQ_TILE
    n_chunks = T // C
    per_tile = SEQ_TILE // C
    nt_dims = (((1,), (1,)), ((), ()))
    tn_dims = (((0,), (0,)), ((), ()))

    r = lax.broadcasted_iota(jnp.int32, (SEQ_TILE, SEQ_TILE), 0)
    c = lax.broadcasted_iota(jnp.int32, (SEQ_TILE, SEQ_TILE), 1)
    same_chunk = (r // C) == (c // C)
    causal = same_chunk & ((c >= r) if reverse else (c <= r))
    cum_mat = jnp.where(causal, 1.0, 0.0).astype(BF16)
    same_head = (r // GLA_DK) == (c // GLA_DK)
    head_mean = jnp.where(same_head, 1.0 / GLA_DK, 0.0).astype(BF16)
    lane_head = lax.broadcasted_iota(jnp.int32, (C, GLA_W), 1) // GLA_DK

    def phase1(i, carry):
        rows = pl.ds(pl.multiple_of(i * SEQ_TILE, SEQ_TILE), SEQ_TILE)
        z = jnp.dot(lr_ref[rows, :], w2_ref[...], preferred_element_type=F32) + bias_ref[...]
        la = (jnp.minimum(z, 0.0) - jnp.log(1.0 + jnp.exp(-jnp.abs(z)))) / GLA_TAU
        hi, lo = _split2(la)
        b = jnp.dot(cum_mat, hi, preferred_element_type=F32) + jnp.dot(cum_mat, lo, preferred_element_type=F32)
        b3 = b.reshape(per_tile, C, GLA_W)
        total = b3[:, 0:1, :] if reverse else b3[:, C - 1:C, :]
        b_last = jnp.broadcast_to(total, (per_tile, C, GLA_W)).reshape(SEQ_TILE, GLA_W)
        b_ref = 0.5 * b_last
        q = q_ref[rows, :].astype(F32)
        k = k_ref[rows, :].astype(F32)
        v = v_ref[rows, :]
        e_fwd = jnp.exp(b - b_ref)
        e_bwd = jnp.exp(b_ref - b)
        e_half = jnp.exp(b_ref)
        qe = (q * e_fwd).astype(BF16)
        ke = (k * e_bwd).astype(BF16)
        ku = (k * (e_bwd * e_half)).astype(BF16)
        qs_scr[rows, :] = (q * (e_fwd * e_half)).astype(BF16)
        dec = e_half * e_half
        for h in range(GLA_HEADS):
            hs = slice(h * GLA_DK, (h + 1) * GLA_DK)
            s = lax.dot_general(qe[:, hs], ke[:, hs], nt_dims, preferred_element_type=F32)
            a = jnp.where(causal, s, 0.0).astype(BF16)
            oacc_scr[rows, hs] = jnp.dot(a, v[:, hs], preferred_element_type=F32)
        for ci in range(per_tile):
            cs = slice(ci * C, (ci + 1) * C)
            full = lax.dot_general(v[cs, :], ku[cs, :], tn_dims, preferred_element_type=F32)
            ut = full[0:C, :]
            for h in range(1, GLA_HEADS):
                ut = jnp.where(lane_head == h, full[h * C:(h + 1) * C, :], ut)
            u_scr[i * per_tile + ci] = ut
            dec_scr[i * per_tile + ci] = dec[ci * C:ci * C + 1, :]
        return carry

    lax.fori_loop(0, n_tiles, phase1, 0, unroll=4)

    def phase2(i, st):
        if reverse:
            ci = jnp.where(i < n_ctx_chunks, n_ctx_chunks - 1 - i, n_chunks - 1 + n_ctx_chunks - i)
        else:
            ci = i
        sin_scr[ci] = st.astype(BF16)
        return st * dec_scr[ci] + u_scr[ci]

    lax.fori_loop(0, n_chunks, phase2, jnp.zeros((C, GLA_W), F32))

    def phase3(i, carry):
        rows = pl.ds(pl.multiple_of(i * SEQ_TILE, SEQ_TILE), SEQ_TILE)
        qs = qs_scr[rows, :]
        parts = []
        for ci in range(per_tile):
            st = sin_scr[i * per_tile + ci]
            st_heads = jnp.where(same_head, jnp.concatenate([st] * GLA_HEADS, axis=0), jnp.zeros((), BF16))
            parts.append(lax.dot_general(qs[ci * C:(ci + 1) * C, :], st_heads, nt_dims, preferred_element_type=F32))
        o = oacc_scr[rows, :] + jnp.concatenate(parts, axis=0)
        ms = jnp.dot((o * o).astype(BF16), head_mean, preferred_element_type=F32)
        y = o * lax.rsqrt(ms + EPS) * ng_ref[...]
        o_ref[rows, :] = (y * _silu(gate_ref[rows, :].astype(F32))).astype(BF16)
        return carry

    lax.fori_loop(0, n_tiles, phase3, 0, unroll=4)


def _gla(proj, w2pad, bias, ng, ctx_len, reverse):
    B, T, _ = proj.shape
    assert T % SEQ_TILE == 0 and ctx_len % GLA_CHUNK == 0
    n_chunks = T // GLA_CHUNK
    col = lambda c0: pl.BlockSpec((None, T, GLA_W), lambda b: (b, 0, c0 // GLA_W))
    gate_col = COL_GGB if reverse else COL_GGF
    kern = functools.partial(_gla_kernel, reverse=reverse, n_ctx_chunks=ctx_len // GLA_CHUNK)
    return pl.pallas_call(
        kern,
        grid=(B,),
        in_specs=[col(COL_GQ), col(COL_GK), col(COL_GV), col(gate_col),
                  pl.BlockSpec((None, T, LANES), lambda b: (b, 0, COL_LR // LANES)),
                  pl.BlockSpec((LANES, GLA_W), lambda b: (0, 0)),
                  pl.BlockSpec((1, GLA_W), lambda b: (0, 0)),
                  pl.BlockSpec((1, GLA_W), lambda b: (0, 0))],
        out_specs=pl.BlockSpec((None, T, GLA_W), lambda b: (b, 0, 0)),
        out_shape=jax.ShapeDtypeStruct((B, T, GLA_W), BF16),
        scratch_shapes=[pltpu.VMEM((T, GLA_W), F32),
                        pltpu.VMEM((T, GLA_W), BF16),
                        pltpu.VMEM((n_chunks, GLA_CHUNK, GLA_W), F32),
                        pltpu.VMEM((n_chunks, 1, GLA_W), F32),
                        pltpu.VMEM((n_chunks, GLA_CHUNK, GLA_W), BF16)],
        compiler_params=_params(("parallel",)),
    )(proj, proj, proj, proj, proj, w2pad, bias, ng)


MXU_TILE = 256
FFN_CHUNK = 3 * MXU_TILE


def _swiglu_resident(h, wgu_ref, wd_ref):
    F = wd_ref.shape[0]
    y = None
    for c0 in range(0, F, FFN_CHUNK):
        c1 = min(c0 + FFN_CHUNK, F)
        g = jnp.dot(h, wgu_ref[:, c0:c1], preferred_element_type=F32)
        u = jnp.dot(h, wgu_ref[:, F + c0:F + c1], preferred_element_type=F32)
        part = jnp.dot((_silu(g) * u).astype(BF16), wd_ref[c0:c1, :], preferred_element_type=F32)
        y = part if y is None else y + part
    return y


def _outproj_kernel(*refs, tm, ctx_len, moe):
    if moe:
        (x_ref, att_ref, gf_ref, gb_ref, cv_ref, w_ref, g1_ref, cg1_ref, ng_ref, sc_ref, csc_ref, sh_ref, csh_ref,
         router_ref, xo_ref, hp_ref, route_ref, cnt_ref) = refs
    else:
        (x_ref, att_ref, gf_ref, gb_ref, cv_ref, w_ref, g1_ref, cg1_ref, ng_ref, sc_ref, csc_ref, sh_ref, csh_ref,
         g2_ref, cg2_ref, wgu_ref, wd_ref, xo_ref) = refs
    is_ctx = _row_is_ctx(pl.program_id(1), tm, ctx_len)
    gla = (gf_ref[...].astype(F32) + gb_ref[...].astype(F32)).astype(BF16)
    y = jnp.dot(att_ref[...], w_ref[:ATT_W, :], preferred_element_type=F32)
    y += jnp.dot(gla, w_ref[ATT_W:ATT_W + GLA_W, :], preferred_element_type=F32)
    y += jnp.dot(cv_ref[...], w_ref[ATT_W + GLA_W:, :], preferred_element_type=F32)
    x = x_ref[...] + jnp.where(is_ctx, cg1_ref[...], g1_ref[...]) * y
    if moe:
        xo_ref[...] = x
    n = x * lax.rsqrt(jnp.mean(x * x, axis=-1, keepdims=True) + EPS) * ng_ref[...]
    scale = jnp.where(is_ctx, csc_ref[...], sc_ref[...])
    shift = jnp.where(is_ctx, csh_ref[...], sh_ref[...])
    h = n * (1.0 + scale) + shift
    hi = h.astype(BF16)
    if not moe:
        xo_ref[...] = x + jnp.where(is_ctx, cg2_ref[...], g2_ref[...]) * _swiglu_resident(hi, wgu_ref, wd_ref)
    else:
        _store_packed(hp_ref, _pack_bf16_pairs(h))
        logits = jnp.dot(hi, router_ref[...], preferred_element_type=F32)
        lane = lax.broadcasted_iota(jnp.int32, logits.shape, 1)
        neg = -jnp.inf
        lg = jnp.where(lane < N_EXPERTS, logits, neg)
        m1 = jnp.max(lg, axis=-1, keepdims=True)
        i1 = jnp.min(jnp.where(lg == m1, lane, LANES), axis=-1, keepdims=True)
        lg2 = jnp.where(lane == i1, neg, lg)
        m2 = jnp.max(lg2, axis=-1, keepdims=True)
        i2 = jnp.min(jnp.where(lg2 == m2, lane, LANES), axis=-1, keepdims=True)
        e2 = jnp.exp(m2 - m1)
        gate1 = 1.0 / (1.0 + e2)
        gate2 = e2 / (1.0 + e2)
        first = jnp.where(lane == i1, 1.0, 0.0)
        second = jnp.where(lane == i2, 1.0, 0.0)
        rr = lax.broadcasted_iota(jnp.int32, (tm, tm), 0)
        cc = lax.broadcasted_iota(jnp.int32, (tm, tm), 1)
        earlier = jnp.where(cc < rr, 1.0, 0.0).astype(BF16)
        before = jnp.dot(earlier, jnp.concatenate([first, second], axis=1).astype(BF16), preferred_element_type=F32)
        n_first = jnp.sum(first, axis=0, keepdims=True)
        rank1 = jnp.sum(first * before[:, :LANES], axis=-1, keepdims=True)
        rank2 = jnp.sum(second * (before[:, LANES:] + n_first), axis=-1, keepdims=True)
        cnt_ref[...] = jnp.broadcast_to(n_first + jnp.sum(second, axis=0, keepdims=True), cnt_ref.shape)
        route_ref[...] = jnp.where(lane == 0, i1.astype(F32), jnp.where(lane == 1, i2.astype(F32),
                                   jnp.where(lane == 2, gate1, jnp.where(lane == 3, gate2,
                                             jnp.where(lane == 4, rank1, jnp.where(lane == 5, rank2, 0.0))))))


def _outproj(xt, att, gf, gb, cv, w_out, layer, ng, mod_l, mod_c, ctx_len, router=None, ffn=None):
    B, T, D = xt.shape
    tm = _pick_tile(T, 640)
    moe = router is not None
    assert moe != (ffn is not None)
    row = lambda w: pl.BlockSpec((None, tm, w), lambda b, j: (b, j, 0))
    const = lambda shape: pl.BlockSpec(shape, lambda b, j: (0,) * len(shape))
    in_specs = [row(D), row(ATT_W), row(GLA_W), row(GLA_W), row(CONV_CH),
                pl.BlockSpec((None, D, D), lambda b, j: (layer, 0, 0)),
                *_mod_specs(2, D), const((1, D)), *_mod_specs(4, D), *_mod_specs(3, D)]
    args = [xt, att, gf, gb, cv, w_out, mod_l, mod_c, ng, mod_l, mod_c, mod_l, mod_c]
    nt = T // tm
    if moe:
        in_specs.append(pl.BlockSpec((None, D, LANES), lambda b, j: (router[1], 0, 0)))
        args.append(router[0])
        out_specs = [row(D), pl.BlockSpec((PACK_CHUNKS, tm, LANES), lambda b, j: (0, b * nt + j, 0)), row(LANES),
                     pl.BlockSpec((None, 8, LANES), lambda b, j: (b * nt + j, 0, 0))]
        out_shape = [jax.ShapeDtypeStruct((B, T, D), F32), jax.ShapeDtypeStruct((PACK_CHUNKS, B * T, LANES), jnp.int32),
                     jax.ShapeDtypeStruct((B, T, LANES), F32), jax.ShapeDtypeStruct((B * nt, 8, LANES), F32)]
    else:
        w_gu, w_down, idx = ffn
        F = w_down.shape[1]
        assert F % MXU_TILE == 0
        resident = lambda shape: pl.BlockSpec((None,) + shape, lambda b, j: (idx, 0, 0), pipeline_mode=pl.Buffered(1))
        in_specs += [*_mod_specs(5, D), resident((D, 2 * F)), resident((F, D))]
        args += [mod_l, mod_c, w_gu, w_down]
        out_specs = row(D)
        out_shape = jax.ShapeDtypeStruct((B, T, D), F32)
    return pl.pallas_call(
        functools.partial(_outproj_kernel, tm=tm, ctx_len=ctx_len, moe=moe),
        grid=(B, T // tm),
        in_specs=in_specs, out_specs=out_specs, out_shape=out_shape,
        compiler_params=_params(("parallel", "parallel")),
    )(*args)


def _sc_gather(table, idx):
    _, width = table.shape
    m = idx.shape[0]
    workers = SC_CORES * SC_SUBCORES
    ch = SC_INDEX_CHUNK
    assert width == LANES and m % (workers * ch) == 0
    n_chunks = m // (workers * ch)
    nb = max(b for b in (4, 3, 2, 1) if n_chunks % b == 0)
    mesh = plsc.VectorSubcoreMesh(core_axis_name="c", subcore_axis_name="s",
                                  num_cores=SC_CORES, num_subcores=SC_SUBCORES)

    @functools.partial(
        pl.kernel, mesh=mesh, out_type=jax.ShapeDtypeStruct((m, width), table.dtype),
        scratch_types=[pltpu.VMEM((n_chunks, ch), jnp.int32), pltpu.VMEM((nb, ch, width), table.dtype)]
        + [pltpu.SemaphoreType.DMA] * (2 * nb))
    def gather(table_hbm, idx_hbm, out_hbm, idx_v, rows_v, *sems):
        wid = lax.axis_index("s") * SC_CORES + lax.axis_index("c")
        base = wid * (n_chunks * ch)
        pltpu.sync_copy(idx_hbm.at[wid], idx_v)

        def fetch(c, b):
            return pltpu.make_async_copy(table_hbm.at[idx_v.at[c]], rows_v.at[b], sems[b])

        def flush(c, b):
            return pltpu.make_async_copy(rows_v.at[b], out_hbm.at[pl.ds(base + c * ch, ch)], sems[nb + b])

        @pl.loop(0, n_chunks, step=nb)
        def _(c0):
            for b in range(nb):
                @pl.when(c0 > 0)
                def _():
                    flush(c0 - nb + b, b).wait()
                fetch(c0 + b, b).start()
            for b in range(nb):
                fetch(c0 + b, b).wait()
                flush(c0 + b, b).start()

        for b in range(nb):
            flush(n_chunks - nb + b, b).wait()

    return gather(table, idx.reshape(workers, n_chunks, ch))


def _sc_dispatch(table, dest_a, dest_b, dest_pad, n_out):
    m, width = table.shape
    n_pad = dest_pad.shape[0]
    workers = SC_CORES * SC_SUBCORES
    ch = SC_INDEX_CHUNK
    assert width == LANES and m % (workers * ch) == 0 and n_pad % (workers * ch) == 0
    n_chunks = m // (workers * ch)
    pad_chunks = n_pad // (workers * ch)
    nb = max(b for b in (4, 3, 2, 1) if n_chunks % b == 0)
    mesh = plsc.VectorSubcoreMesh(core_axis_name="c", subcore_axis_name="s",
                                  num_cores=SC_CORES, num_subcores=SC_SUBCORES)

    @functools.partial(
        pl.kernel, mesh=mesh, out_type=jax.ShapeDtypeStruct((n_out, width), table.dtype),
        scratch_types=[pltpu.VMEM((n_chunks, ch), jnp.int32), pltpu.VMEM((n_chunks, ch), jnp.int32),
                       pltpu.VMEM((pad_chunks, ch), jnp.int32), pltpu.VMEM((nb, ch, width), table.dtype),
                       pltpu.VMEM((ch, width), table.dtype)]
        + [pltpu.SemaphoreType.DMA] * (3 * nb + 1))
    def dispatch(table_hbm, zeros_hbm, da_hbm, db_hbm, dp_hbm, out_hbm, da_v, db_v, dp_v, rows_v, zero_v, *sems):
        wid = lax.axis_index("s") * SC_CORES + lax.axis_index("c")
        base = wid * (n_chunks * ch)
        pltpu.sync_copy(da_hbm.at[wid], da_v)
        pltpu.sync_copy(db_hbm.at[wid], db_v)
        pltpu.sync_copy(dp_hbm.at[wid], dp_v)
        pltpu.sync_copy(zeros_hbm, zero_v)

        def fetch(c, b):
            return pltpu.make_async_copy(table_hbm.at[pl.ds(base + c * ch, ch)], rows_v.at[b], sems[b])

        def put_a(c, b):
            return pltpu.make_async_copy(rows_v.at[b], out_hbm.at[da_v.at[c]], sems[nb + b])

        def put_b(c, b):
            return pltpu.make_async_copy(rows_v.at[b], out_hbm.at[db_v.at[c]], sems[2 * nb + b])

        def put_zero(p):
            return pltpu.make_async_copy(zero_v, out_hbm.at[dp_v.at[p]], sems[3 * nb])

        for p in range(pad_chunks):
            put_zero(p).start()

        @pl.loop(0, n_chunks, step=nb)
        def _(c0):
            for b in range(nb):
                @pl.when(c0 > 0)
                def _():
                    put_a(c0 - nb + b, b).wait()
                    put_b(c0 - nb + b, b).wait()
                fetch(c0 + b, b).start()
            for b in range(nb):
                fetch(c0 + b, b).wait()
                put_a(c0 + b, b).start()
                put_b(c0 + b, b).start()

        for b in range(nb):
            put_a(n_chunks - nb + b, b).wait()
            put_b(n_chunks - nb + b, b).wait()
        for p in range(pad_chunks):
            put_zero(p).wait()

    shape3 = lambda v: v.reshape(workers, -1, ch)
    return dispatch(table, jnp.zeros((ch, width), table.dtype), shape3(dest_a), shape3(dest_b), shape3(dest_pad))


def _routing(route, tile_counts, tile):
    B, T, _ = route.shape
    n = B * T
    assert (2 * n) % tile == 0
    info = route.reshape(n, LANES)[:, :6].astype(jnp.int32)
    per_tile = tile_counts[:, 0, :N_EXPERTS].astype(jnp.int32)
    counts = jnp.sum(per_tile, axis=0)
    padded = (counts + tile - 1) // tile * tile
    ends = jnp.cumsum(padded)
    starts = ends - padded
    base = jnp.repeat(starts[None, :] + jnp.cumsum(per_tile, axis=0) - per_tile, n // per_tile.shape[0], axis=0)
    experts = jnp.arange(N_EXPERTS, dtype=jnp.int32)[None, :]
    row_of = lambda e, rank: jnp.sum(jnp.where(e[:, None] == experts, base, 0), axis=1) + rank
    dest = jnp.concatenate([row_of(info[:, 0], info[:, 4]), row_of(info[:, 1], info[:, 5])])
    n_pad = N_EXPERTS * tile
    n_rows = 2 * n + n_pad
    seg_first = jnp.concatenate([starts + counts, ends[-1:]])
    seg_size = jnp.concatenate([padded - counts, (n_rows - ends[-1])[None]])
    seg_end = jnp.cumsum(seg_size)
    j = jnp.arange(n_pad, dtype=jnp.int32)
    seg = jnp.sum((j[:, None] >= seg_end[None, :]).astype(jnp.int32), axis=1)
    pad_rows = (seg_first[seg] + j - (seg_end - seg_size)[seg]).astype(jnp.int32)
    tile_start = jnp.arange(n_rows // tile, dtype=jnp.int32) * tile
    tile_expert = jnp.minimum(jnp.searchsorted(ends, tile_start, side="right"), N_EXPERTS - 1).astype(jnp.int32)
    tile_valid = (tile_start < ends[-1]).astype(jnp.int32)
    return dest, pad_rows, n_rows, tile_expert, tile_valid


def _expert_kernel(te_ref, tv_ref, xs_ref, wgu_ref, wd_ref, ys_ref, wgu_scr):
    i = pl.program_id(0)
    F = wd_ref.shape[0]

    @pl.when((i == 0) | (te_ref[i] != te_ref[jnp.maximum(i - 1, 0)]))
    def _new_expert():
        wgu_scr[...] = wgu_ref[...].astype(BF16)

    @pl.when(tv_ref[i] > 0)
    def _compute():
        lo, hi = _unpack_bf16_pairs(_load_packed(xs_ref))
        gu = (jnp.dot(lo.astype(BF16), wgu_scr[:PACK_W, :], preferred_element_type=F32)
              + jnp.dot(hi.astype(BF16), wgu_scr[PACK_W:, :], preferred_element_type=F32))
        act = (_silu(gu[:, :F]) * gu[:, F:]).astype(BF16)
        y = jnp.dot(act, wd_ref[...], preferred_element_type=F32)
        _store_packed(ys_ref, _pack_bf16_pairs(y))

    @pl.when(tv_ref[i] == 0)
    def _unused_tile():
        ys_ref[...] = jnp.zeros_like(ys_ref)


def _experts(xs, tile_expert, tile_valid, w_gu, w_down, layer):
    _, n_rows, _ = xs.shape
    _, _, F, D = w_down.shape
    tile = EXPERT_TILE
    rows = pl.BlockSpec((PACK_CHUNKS, tile, LANES), lambda i, te, tv: (0, i, 0))
    return pl.pallas_call(
        _expert_kernel,
        grid_spec=pltpu.PrefetchScalarGridSpec(
            num_scalar_prefetch=2,
            grid=(n_rows // tile,),
            in_specs=[rows,
                      pl.BlockSpec((None, None, D, 2 * F), lambda i, te, tv: (layer, te[i], 0, 0)),
                      pl.BlockSpec((None, None, F, D), lambda i, te, tv: (layer, te[i], 0, 0))],
            out_specs=rows,
            scratch_shapes=[pltpu.VMEM((D, 2 * F), BF16)]),
        out_shape=jax.ShapeDtypeStruct(xs.shape, jnp.int32),
        compiler_params=_params(("arbitrary",)),
    )(tile_expert, tile_valid, xs, w_gu, w_down)


def _combine_kernel(*refs, tm, ctx_len, skip, final):
    x_ref, g2_ref, cg2_ref, route_ref, ya_ref, yb_ref = refs[:6]
    fg_ref = refs[6] if final else None
    o_ref = refs[-1]
    is_ctx = _row_is_ctx(pl.program_id(1) + skip, tm, ctx_len)
    route = route_ref[...]
    lane = lax.broadcasted_iota(jnp.int32, route.shape, 1)
    gate_a = jnp.sum(jnp.where(lane == 2, route, 0.0), axis=-1, keepdims=True)
    gate_b = jnp.sum(jnp.where(lane == 3, route, 0.0), axis=-1, keepdims=True)
    unpack = lambda ref: jnp.concatenate(_unpack_bf16_pairs(_load_packed(ref)), axis=1)
    f = gate_a * unpack(ya_ref) + gate_b * unpack(yb_ref)
    x = x_ref[...] + jnp.where(is_ctx, cg2_ref[...], g2_ref[...]) * f
    if final:
        x = x * lax.rsqrt(jnp.mean(x * x, axis=-1, keepdims=True) + EPS) * fg_ref[...]
    o_ref[...] = x


def _combine(xt, mod_l, mod_c, route, yg, ctx_len, final_g=None):
    B, T, D = xt.shape
    final = final_g is not None
    if final:
        tm = SEQ_TILE
        assert ctx_len % tm == 0 and T % tm == 0
        skip = ctx_len // tm
    else:
        tm = _pick_tile(T, 640)
        skip = 0
    nt = T // tm
    row = lambda w: pl.BlockSpec((None, tm, w), lambda b, j: (b, j + skip, 0))
    slot = lambda s: pl.BlockSpec((PACK_CHUNKS, None, tm, LANES), lambda b, j: (0, s, b * nt + j + skip, 0))
    in_specs = [row(D), *_mod_specs(5, D), row(LANES), slot(0), slot(1)]
    args = [xt, mod_l, mod_c, route, yg, yg]
    if final:
        in_specs.append(pl.BlockSpec((1, D), lambda b, j: (0, 0)))
        args.append(final_g)
    return pl.pallas_call(
        functools.partial(_combine_kernel, tm=tm, ctx_len=ctx_len, skip=skip, final=final),
        grid=(B, nt - skip),
        in_specs=in_specs,
        out_specs=pl.BlockSpec((None, tm, D), lambda b, j: (b, j, 0)),
        out_shape=jax.ShapeDtypeStruct((B, T - skip * tm, D), F32),
        compiler_params=_params(("parallel", "parallel")),
    )(*args)


def _moe(hp, route, tile_counts, xt, mod_l, mod_c, w_gu, w_down, layer, ctx_len, final_g=None):
    B, T, _ = xt.shape
    n = B * T
    dest, pad_rows, n_rows, tile_expert, tile_valid = _routing(route, tile_counts, EXPERT_TILE)
    chunk = jnp.arange(PACK_CHUNKS, dtype=jnp.int32)[:, None] * n_rows
    xs = _sc_dispatch(hp.reshape(PACK_CHUNKS * n, LANES), (chunk + dest[None, :n]).reshape(-1),
                      (chunk + dest[None, n:]).reshape(-1), (chunk + pad_rows[None, :]).reshape(-1),
                      PACK_CHUNKS * n_rows)
    ys = _experts(xs.reshape(PACK_CHUNKS, n_rows, LANES), tile_expert, tile_valid, w_gu, w_down, layer)
    ys = ys.reshape(PACK_CHUNKS * n_rows, LANES)
    yg = _sc_gather(ys, (chunk + dest[None, :]).reshape(-1)).reshape(PACK_CHUNKS, 2, n, LANES)
    return _combine(xt, mod_l, mod_c, route, yg, ctx_len, final_g)


def _final_norm_kernel(x_ref, g_ref, o_ref):
    x = x_ref[...]
    o_ref[...] = x * lax.rsqrt(jnp.mean(x * x, axis=-1, keepdims=True) + EPS) * g_ref[...]


def _final_norm(xt, g, ctx_len):
    B, T, D = xt.shape
    S = T - ctx_len
    tm = SEQ_TILE
    assert ctx_len % tm == 0 and S % tm == 0
    skip = ctx_len // tm
    return pl.pallas_call(
        _final_norm_kernel,
        grid=(B, S // tm),
        in_specs=[pl.BlockSpec((None, tm, D), lambda b, j: (b, j + skip, 0)),
                  pl.BlockSpec((1, D), lambda b, j: (0, 0))],
        out_specs=pl.BlockSpec((None, tm, D), lambda b, j: (b, j, 0)),
        out_shape=jax.ShapeDtypeStruct((B, S, D), F32),
        compiler_params=_params(("parallel", "parallel")),
    )(xt, g)


def _rope_tables(seq, ctx_len):
    pos = jnp.arange(seq, dtype=jnp.int32)
    nf = HEAD_DIM // 4
    inv = ROPE_THETA ** (-jnp.arange(nf, dtype=F32) / nf)
    ang = jnp.concatenate([(pos // GRID_W).astype(F32)[:, None] * inv, (pos % GRID_W).astype(F32)[:, None] * inv], axis=-1)
    cos, sin = jnp.cos(ang), jnp.sin(ang)
    reps = LANES // HEAD_DIM
    cos_t = jnp.tile(jnp.concatenate([cos, cos], axis=-1), (1, reps))
    sin_t = jnp.tile(jnp.concatenate([-sin, sin], axis=-1), (1, reps))
    cos_t = jnp.concatenate([jnp.ones((ctx_len, LANES), F32), cos_t], axis=0)
    sin_t = jnp.concatenate([jnp.zeros((ctx_len, LANES), F32), sin_t], axis=0)
    return cos_t, sin_t


def _relayout_w_in(w_in):
    o_lr = ATT_W + 2 * KV_W + 5 * GLA_W
    q_scale = HEAD_DIM ** -0.5 * LOG2E
    gq_scale = GLA_DK ** -0.5
    parts = [w_in[..., :ATT_W] * q_scale, w_in[..., ATT_W:COL_GQ], w_in[..., COL_GQ:COL_GK] * gq_scale,
             w_in[..., COL_GK:o_lr], w_in[..., o_lr + 2 * GLA_LOWRANK:], w_in[..., o_lr:o_lr + 2 * GLA_LOWRANK],
             jnp.zeros(w_in.shape[:-1] + (LANES - 2 * GLA_LOWRANK,), w_in.dtype)]
    return jnp.concatenate(parts, axis=-1).astype(BF16)


def kernel(x, c, ctx, c_ctx, w_mod, b_mod, norm1_g, norm2_g, w_in, w_out, attn_sink, gla_w2_f, gla_b_f, gla_w2_b, gla_b_b, gla_norm_g, conv_w, ffn_w_gu, ffn_w_down, router_w, expert_w_gu, expert_w_down, final_norm_g):
    B, S, D = x.shape
    L = ctx.shape[1]
    depth = w_in.shape[0]
    assert D == D_MODEL and S % GRID_W == 0

    xt = jnp.concatenate([ctx, x], axis=1)
    rows = 16
    cc = jnp.concatenate([c, c_ctx[None, :], jnp.zeros((rows - B - 1, D), F32)], axis=0)
    mod = _adaln(cc, w_mod, b_mod)
    cos_t, sin_t = _rope_tables(S, L)

    w_in_p = _relayout_w_in(w_in)
    w_out_b = _to_bf16(w_out)
    zpad = jnp.zeros((depth, LANES - 2 * GLA_LOWRANK, GLA_W), F32)
    w2_f = jnp.concatenate([gla_w2_f, jnp.zeros_like(gla_w2_b), zpad], axis=1).astype(BF16)
    w2_b = jnp.concatenate([jnp.zeros_like(gla_w2_f), gla_w2_b, zpad], axis=1).astype(BF16)
    gla_ng = jnp.tile(gla_norm_g, (1, GLA_HEADS))
    ffn_gu_b = _to_bf16(ffn_w_gu)
    ffn_down_b = _to_bf16(ffn_w_down)
    exp_down_b = _to_bf16(expert_w_down)
    router_b = jnp.pad(router_w, ((0, 0), (0, 0), (0, LANES - N_EXPERTS))).astype(BF16)

    for l in range(depth):
        mod_l = mod[l, :B].reshape(B, 1, 6 * D)
        mod_c = mod[l, B:B + 1]
        proj = _inproj(xt, norm1_g[l][None, :], mod_l, mod_c, cos_t, sin_t, w_in_p, l, L)
        att = _attention(proj, attn_sink[l], L)
        cv = _short_conv(proj, conv_w[l], L)
        gf = _gla(proj, w2_f[l], gla_b_f[l][None, :], gla_ng[l][None, :], L, reverse=False)
        gb = _gla(proj, w2_b[l], gla_b_b[l][None, :], gla_ng[l][None, :], L, reverse=True)
        if l % 2 == 0:
            xt = _outproj(xt, att, gf, gb, cv, w_out_b, l, norm2_g[l][None, :], mod_l, mod_c, L,
                          ffn=(ffn_gu_b, ffn_down_b, l // 2))
        else:
            xt, hp, route, tile_counts = _outproj(xt, att, gf, gb, cv, w_out_b, l, norm2_g[l][None, :], mod_l, mod_c, L,
                                     router=(router_b, l // 2))
            xt = _moe(hp, route, tile_counts, xt, mod_l, mod_c, expert_w_gu, exp_down_b, l // 2, L,
                      final_g=final_norm_g[None, :] if l == depth - 1 else None)
    return xt if depth % 2 == 0 else _final_norm(xt, final_norm_g[None, :], L)
```

```python
import functools

import jax
import jax.numpy as jnp
from jax import lax
from jax.experimental import pallas as pl
from jax.experimental.pallas import tpu as pltpu
from jax.experimental.pallas import tpu_sc as plsc

F32 = jnp.float32
BF16 = jnp.bfloat16

D_MODEL = 1024
EPS = 1e-6
GRID_W = 64
ROPE_THETA = 10000.0
LOG2E = 1.4426950408889634
HEAD_DIM = 64
ATT_HEADS = 8
ATT_KV_HEADS = 2
ATT_GROUP = ATT_HEADS // ATT_KV_HEADS
ATT_W = ATT_HEADS * HEAD_DIM
KV_W = ATT_KV_HEADS * HEAD_DIM
WINDOW = 128
GLA_HEADS = 4
GLA_DK = 64
GLA_W = GLA_HEADS * GLA_DK
GLA_LOWRANK = 16
GLA_TAU = 16.0
GLA_CHUNK = 64
CONV_CH = 256
N_EXPERTS = 8

LANES = 128
SUBLANES_BF16 = 16

COL_Q = 0
COL_K = ATT_W
COL_V = COL_K + KV_W
ROPE_W = COL_V
COL_GQ = COL_V + KV_W
COL_GK = COL_GQ + GLA_W
COL_GV = COL_GK + GLA_W
COL_GGF = COL_GV + GLA_W
COL_GGB = COL_GGF + GLA_W
COL_CB = COL_GGB + GLA_W
COL_CC = COL_CB + CONV_CH
COL_CX = COL_CC + CONV_CH
COL_LR = COL_CX + CONV_CH
PROJ_W = COL_LR + LANES

SEQ_TILE = 256
VMEM_LIMIT = 56 * 1024 * 1024
CAST_BLOCK_BYTES = 4 * 1024 * 1024

PACK_W = D_MODEL // 2
PACK_CHUNKS = PACK_W // LANES
EXPERT_TILE = 512
SC_CORES = 2
SC_SUBCORES = 16
SC_INDEX_CHUNK = 128


def _silu(v):
    return v / (1.0 + jnp.exp(-v))


def _pick_tile(total, cap):
    best = None
    for t in range(SUBLANES_BF16, cap + 1, SUBLANES_BF16):
        if total % t == 0:
            best = t
    assert best is not None
    return best


def _params(sem):
    return pltpu.CompilerParams(dimension_semantics=sem, vmem_limit_bytes=VMEM_LIMIT)


def _pack_bf16_pairs(v):
    bits = lambda t: lax.bitcast_convert_type(t.astype(BF16).astype(F32), jnp.int32)
    return ((bits(v[:, :PACK_W]) >> 16) & 0xFFFF) | (bits(v[:, PACK_W:]) & -65536)


def _unpack_bf16_pairs(w):
    return lax.bitcast_convert_type(w << 16, F32), lax.bitcast_convert_type(w & -65536, F32)


CAST_STREAMS = 4


def _cast_kernel(*refs):
    o_ref = refs[-1]
    slab = refs[0].shape[0]
    for s, x_ref in enumerate(refs[:-1]):
        o_ref[s * slab:(s + 1) * slab, :] = x_ref[...].astype(BF16)


def _to_bf16(w):
    cols = w.shape[-1]
    rows = w.size // cols
    tr = _pick_tile(rows, max(SUBLANES_BF16, CAST_BLOCK_BYTES // (4 * cols)))
    streams = CAST_STREAMS if tr % (CAST_STREAMS * SUBLANES_BF16) == 0 else 1
    slab = tr // streams
    w2 = w.reshape(rows, cols)
    out = pl.pallas_call(
        _cast_kernel,
        grid=(rows // tr,),
        in_specs=[pl.BlockSpec((slab, cols), lambda i, s=s: (i * streams + s, 0)) for s in range(streams)],
        out_specs=pl.BlockSpec((tr, cols), lambda i: (i, 0)),
        out_shape=jax.ShapeDtypeStruct((rows, cols), BF16),
        compiler_params=_params(("parallel",)),
    )(*([w2] * streams))
    return out.reshape(w.shape)


def _load_packed(ref):
    return jnp.concatenate([ref[k] for k in range(PACK_CHUNKS)], axis=1)


def _store_packed(ref, packed):
    for k in range(PACK_CHUNKS):
        ref[k] = packed[:, k * LANES:(k + 1) * LANES]


def _adaln_kernel(c_ref, w_ref, b_ref, o_ref):
    s = _silu(c_ref[...]).astype(BF16)
    o_ref[...] = jnp.dot(s, w_ref[...].astype(BF16), preferred_element_type=F32) + b_ref[...]


def _adaln(cc, w_mod, b_mod):
    depth, d, n = w_mod.shape
    tn = 1024
    rows = cc.shape[0]
    return pl.pallas_call(
        _adaln_kernel,
        grid=(depth, n // tn),
        in_specs=[pl.BlockSpec((rows, d), lambda l, j: (0, 0)),
                  pl.BlockSpec((None, d, tn), lambda l, j: (l, 0, j)),
                  pl.BlockSpec((None, 1, tn), lambda l, j: (l, 0, j))],
        out_specs=pl.BlockSpec((None, rows, tn), lambda l, j: (l, 0, j)),
        out_shape=jax.ShapeDtypeStruct((depth, rows, n), F32),
        compiler_params=_params(("parallel", "parallel")),
    )(cc, w_mod, b_mod.reshape(depth, 1, n))


def _mod_specs(k, d):
    return [pl.BlockSpec((None, 1, d), lambda b, j, *_: (b, 0, k)),
            pl.BlockSpec((1, d), lambda b, j, *_: (0, k))]


def _row_is_ctx(j, tm, ctx_len):
    return (j * tm + lax.broadcasted_iota(jnp.int32, (tm, 1), 0)) < ctx_len


def _inproj_kernel(x_ref, g_ref, sc_ref, csc_ref, sh_ref, csh_ref, cos_ref, sin_ref, w_ref, o_ref, *, tm, ctx_len):
    x = x_ref[...]
    n = x * lax.rsqrt(jnp.mean(x * x, axis=-1, keepdims=True) + EPS) * g_ref[...]
    is_ctx = _row_is_ctx(pl.program_id(1), tm, ctx_len)
    scale = jnp.where(is_ctx, csc_ref[...], sc_ref[...])
    shift = jnp.where(is_ctx, csh_ref[...], sh_ref[...])
    h = (n * (1.0 + scale) + shift).astype(BF16)
    qk = jnp.dot(h, w_ref[:, :ROPE_W], preferred_element_type=F32)
    lower_half = (lax.broadcasted_iota(jnp.int32, (tm, LANES), 1) % HEAD_DIM) < (HEAD_DIM // 2)
    cos = cos_ref[...]
    sin = sin_ref[...]
    for i in range(ROPE_W // LANES):
        t = qk[:, i * LANES:(i + 1) * LANES]
        partner = jnp.where(lower_half, pltpu.roll(t, LANES - HEAD_DIM // 2, axis=1), pltpu.roll(t, HEAD_DIM // 2, axis=1))
        o_ref[:, i * LANES:(i + 1) * LANES] = (t * cos + partner * sin).astype(BF16)
    o_ref[:, ROPE_W:] = jnp.dot(h, w_ref[:, ROPE_W:], preferred_element_type=F32).astype(BF16)


def _inproj(xt, g, mod_l, mod_c, cos_t, sin_t, w, layer, ctx_len):
    B, T, D = xt.shape
    tm = _pick_tile(T, 640)
    kern = functools.partial(_inproj_kernel, tm=tm, ctx_len=ctx_len)
    return pl.pallas_call(
        kern,
        grid=(B, T // tm),
        in_specs=[pl.BlockSpec((None, tm, D), lambda b, j: (b, j, 0)),
                  pl.BlockSpec((1, D), lambda b, j: (0, 0)),
                  *_mod_specs(1, D), *_mod_specs(0, D),
                  pl.BlockSpec((tm, LANES), lambda b, j: (j, 0)),
                  pl.BlockSpec((tm, LANES), lambda b, j: (j, 0)),
                  pl.BlockSpec((None, D, PROJ_W), lambda b, j: (layer, 0, 0))],
        out_specs=pl.BlockSpec((None, tm, PROJ_W), lambda b, j: (b, j, 0)),
        out_shape=jax.ShapeDtypeStruct((B, T, PROJ_W), BF16),
        compiler_params=_params(("parallel", "parallel")),
    )(xt, g, mod_l, mod_c, mod_l, mod_c, cos_t, sin_t, w)


def _attn_kernel(sink_ref, q_ref, kc_ref, vc_ref, km_ref, vm_ref, kp_ref, vp_ref, kn_ref, vn_ref, o_ref, *, n_tiles):
    j = pl.program_id(1)
    nt_dims = (((1,), (1,)), ((), ()))
    tn_dims = (((0,), (0,)), ((), ()))

    def scores(rows, n_rows, g, k_ctx, k_win=None, mask_win=None):
        heads = [g * ATT_GROUP + hh for hh in range(ATT_GROUP)]
        q = jnp.concatenate([q_ref[rows, h * HEAD_DIM:(h + 1) * HEAD_DIM] for h in heads], axis=0)
        head_col = lax.broadcasted_iota(jnp.int32, (1, ATT_GROUP * n_rows), 1) // n_rows
        sink = jnp.full((1, ATT_GROUP * n_rows), sink_ref[heads[0]] * LOG2E, F32)
        for hh in range(1, ATT_GROUP):
            sink = jnp.where(head_col == hh, sink_ref[heads[hh]] * LOG2E, sink)
        s_c = lax.dot_general(k_ctx, q, nt_dims, preferred_element_type=F32)
        s_w = None
        if k_win is not None:
            s_w = jnp.where(mask_win, lax.dot_general(k_win, q, nt_dims, preferred_element_type=F32), -1e30)
        return sink, s_c, s_w

    def finish(item, v_ctx, v_win=None):
        sink, s_c, s_w = item
        m = jnp.maximum(jnp.max(s_c, axis=0, keepdims=True), sink)
        if s_w is not None:
            m = jnp.maximum(m, jnp.max(s_w, axis=0, keepdims=True))
        p_c = jnp.exp2(s_c - m)
        den = jnp.sum(p_c, axis=0, keepdims=True) + jnp.exp2(sink - m)
        o = lax.dot_general(v_ctx, p_c.astype(BF16), tn_dims, preferred_element_type=F32)
        if s_w is not None:
            p_w = jnp.exp2(s_w - m)
            den += jnp.sum(p_w, axis=0, keepdims=True)
            o += lax.dot_general(v_win, p_w.astype(BF16), tn_dims, preferred_element_type=F32)
        return o / den

    def store(rows, n_rows, outs):
        t = jnp.transpose(jnp.concatenate(outs, axis=0))
        for g in range(ATT_KV_HEADS):
            for hh in range(ATT_GROUP):
                h = g * ATT_GROUP + hh
                o_ref[rows, h * HEAD_DIM:(h + 1) * HEAD_DIM] = (
                    t[hh * n_rows:(hh + 1) * n_rows, g * HEAD_DIM:(g + 1) * HEAD_DIM].astype(BF16))

    def run(blocks):
        flat = [(bi, s, f) for bi, (_, _, work) in enumerate(blocks) for s, f in work]
        outs = [[] for _ in blocks]
        pending = None
        for bi, score_fn, finish_fn in flat + [(None, None, None)]:
            item = None if score_fn is None else score_fn()
            if pending is not None:
                pbi, pitem, pfinish = pending
                outs[pbi].append(pfinish(pitem))
                if len(outs[pbi]) == ATT_KV_HEADS:
                    store(blocks[pbi][0], blocks[pbi][1], outs[pbi])
            pending = None if score_fn is None else (bi, item, finish_fn)

    @pl.when(j == 0)
    def _context_queries():
        work = []
        for g in range(ATT_KV_HEADS):
            gs = slice(g * HEAD_DIM, (g + 1) * HEAD_DIM)
            work.append((functools.partial(scores, slice(0, SEQ_TILE), SEQ_TILE, g, kc_ref[:, gs]),
                         functools.partial(finish, v_ctx=vc_ref[:, gs])))
        run([(slice(0, SEQ_TILE), SEQ_TILE, work)])

    @pl.when(j > 0)
    def _latent_queries():
        w = lax.broadcasted_iota(jnp.int32, (3 * WINDOW, ATT_GROUP * WINDOW), 0)
        r = lax.broadcasted_iota(jnp.int32, (3 * WINDOW, ATT_GROUP * WINDOW), 1) % WINDOW
        band = (w >= r) & (w <= r + 2 * WINDOW)
        blocks = []
        for sub in range(SEQ_TILE // WINDOW):
            rows = slice(sub * WINDOW, (sub + 1) * WINDOW)
            if sub == 0:
                kprev, vprev = kp_ref[...], vp_ref[...]
                kmid, vmid = km_ref[:WINDOW], vm_ref[:WINDOW]
                knext, vnext = km_ref[WINDOW:], vm_ref[WINDOW:]
                w_lo = jnp.where(j > 1, 0, WINDOW)
                w_hi = 3 * WINDOW
            else:
                kprev, vprev = km_ref[:WINDOW], vm_ref[:WINDOW]
                kmid, vmid = km_ref[WINDOW:], vm_ref[WINDOW:]
                knext, vnext = kn_ref[...], vn_ref[...]
                w_lo = 0
                w_hi = jnp.where(j < n_tiles - 1, 3 * WINDOW, 2 * WINDOW)
            mask = band & (w >= w_lo) & (w < w_hi)
            work = []
            for g in range(ATT_KV_HEADS):
                gs = slice(g * HEAD_DIM, (g + 1) * HEAD_DIM)
                k_win = jnp.concatenate([kprev[:, gs], kmid[:, gs], knext[:, gs]], axis=0)
                v_win = jnp.concatenate([vprev[:, gs], vmid[:, gs], vnext[:, gs]], axis=0)
                work.append((functools.partial(scores, rows, WINDOW, g, kc_ref[:, gs], k_win, mask),
                             functools.partial(finish, v_ctx=vc_ref[:, gs], v_win=v_win)))
            blocks.append((rows, WINDOW, work))
        run(blocks)


def _attention(proj, sink, ctx_len):
    B, T, _ = proj.shape
    assert ctx_len == SEQ_TILE and T % SEQ_TILE == 0
    n_tiles = T // SEQ_TILE
    n_win = T // WINDOW
    kcol, vcol = COL_K // KV_W, COL_V // KV_W
    per_tile = SEQ_TILE // WINDOW
    tile = lambda col: pl.BlockSpec((None, SEQ_TILE, KV_W), lambda b, j: (b, j, col))
    ctx = lambda col: pl.BlockSpec((None, SEQ_TILE, KV_W), lambda b, j: (b, 0, col))
    prev = lambda col: pl.BlockSpec((None, WINDOW, KV_W), lambda b, j: (b, jnp.maximum(j * per_tile - 1, 0), col))
    nxt = lambda col: pl.BlockSpec((None, WINDOW, KV_W), lambda b, j: (b, jnp.minimum((j + 1) * per_tile, n_win - 1), col))
    return pl.pallas_call(
        functools.partial(_attn_kernel, n_tiles=n_tiles),
        grid=(B, n_tiles),
        in_specs=[pl.BlockSpec(memory_space=pltpu.SMEM),
                  pl.BlockSpec((None, SEQ_TILE, ATT_W), lambda b, j: (b, j, 0)),
                  ctx(kcol), ctx(vcol), tile(kcol), tile(vcol), prev(kcol), prev(vcol), nxt(kcol), nxt(vcol)],
        out_specs=pl.BlockSpec((None, SEQ_TILE, ATT_W), lambda b, j: (b, j, 0)),
        out_shape=jax.ShapeDtypeStruct((B, T, ATT_W), BF16),
        compiler_params=_params(("parallel", "parallel")),
    )(sink, *([proj] * 9))


def _conv_kernel(b_ref, c_ref, x_ref, w_ref, o_ref, u_scr, *, ctx_len):
    T = c_ref.shape[0]
    pad = 8
    u = c_ref[...].astype(F32) * x_ref[...].astype(F32)
    zero_row = jnp.zeros((1, CONV_CH), F32)
    u_scr[pad - 1:pad, :] = zero_row
    u_scr[pad:pad + T, :] = u
    u_scr[pad + T:pad + T + 1, :] = zero_row
    row = lax.broadcasted_iota(jnp.int32, (T, 1), 0)
    u_prev = jnp.where(row == ctx_len, 0.0, u_scr[pad - 1:pad - 1 + T, :])
    u_next = jnp.where(row == ctx_len - 1, 0.0, u_scr[pad + 1:pad + 1 + T, :])
    y = w_ref[0:1, :] * u_prev + w_ref[1:2, :] * u + w_ref[2:3, :] * u_next
    o_ref[...] = (b_ref[...].astype(F32) * y).astype(BF16)


def _short_conv(proj, conv_w, ctx_len):
    B, T, _ = proj.shape
    col = lambda c0: pl.BlockSpec((None, T, CONV_CH), lambda b: (b, 0, c0 // CONV_CH))
    return pl.pallas_call(
        functools.partial(_conv_kernel, ctx_len=ctx_len),
        grid=(B,),
        in_specs=[col(COL_CB), col(COL_CC), col(COL_CX), pl.BlockSpec((3, CONV_CH), lambda b: (0, 0))],
        out_specs=pl.BlockSpec((None, T, CONV_CH), lambda b: (b, 0, 0)),
        out_shape=jax.ShapeDtypeStruct((B, T, CONV_CH), BF16),
        scratch_shapes=[pltpu.VMEM((T + 16, CONV_CH), F32)],
        compiler_params=_params(("parallel",)),
    )(proj, proj, proj, conv_w)


def _split2(v):
    hi = v.astype(BF16)
    lo = (v - hi.astype(F32)).astype(BF16)
    return hi, lo


def _gla_kernel(q_ref, k_ref, v_ref, gate_ref, lr_ref, w2_ref, bias_ref, ng_ref, o_ref,
                oacc_scr, qs_scr, u_scr, dec_scr, sin_scr, *, reverse, n_ctx_chunks):
    T = q_ref.shape[0]
    C = GLA_CHUNK
    n_tiles = T // SEQ_TILE
    n_chunks = T // C
    per_tile = SEQ_TILE // C
    nt_dims = (((1,), (1,)), ((), ()))
    tn_dims = (((0,), (0,)), ((), ()))

    r = lax.broadcasted_iota(jnp.int32, (SEQ_TILE, SEQ_TILE), 0)
    c = lax.broadcasted_iota(jnp.int32, (SEQ_TILE, SEQ_TILE), 1)
    same_chunk = (r // C) == (c // C)
    causal = same_chunk & ((c >= r) if reverse else (c <= r))
    cum_mat = jnp.where(causal, 1.0, 0.0).astype(BF16)
    same_head = (r // GLA_DK) == (c // GLA_DK)
    head_mean = jnp.where(same_head, 1.0 / GLA_DK, 0.0).astype(BF16)
    lane_head = lax.broadcasted_iota(jnp.int32, (C, GLA_W), 1) // GLA_DK

    def phase1(i, carry):
        rows = pl.ds(pl.multiple_of(i * SEQ_TILE, SEQ_TILE), SEQ_TILE)
        z = jnp.dot(lr_ref[rows, :], w2_ref[...], preferred_element_type=F32) + bias_ref[...]
        la = (jnp.minimum(z, 0.0) - jnp.log(1.0 + jnp.exp(-jnp.abs(z)))) / GLA_TAU
        hi, lo = _split2(la)
        b = jnp.dot(cum_mat, hi, preferred_element_type=F32) + jnp.dot(cum_mat, lo, preferred_element_type=F32)
        b3 = b.reshape(per_tile, C, GLA_W)
        total = b3[:, 0:1, :] if reverse else b3[:, C - 1:C, :]
        b_last = jnp.broadcast_to(total, (per_tile, C, GLA_W)).reshape(SEQ_TILE, GLA_W)
        b_ref = 0.5 * b_last
        q = q_ref[rows, :].astype(F32)
        k = k_ref[rows, :].astype(F32)
        v = v_ref[rows, :]
        e_fwd = jnp.exp(b - b_ref)
        e_bwd = jnp.exp(b_ref - b)
        e_half = jnp.exp(b_ref)
        qe = (q * e_fwd).astype(BF16)
        ke = (k * e_bwd).astype(BF16)
        ku = (k * (e_bwd * e_half)).astype(BF16)
        qs_scr[rows, :] = (q * (e_fwd * e_half)).astype(BF16)
        dec = e_half * e_half
        for h in range(GLA_HEADS):
            hs = slice(h * GLA_DK, (h + 1) * GLA_DK)
            s = lax.dot_general(qe[:, hs], ke[:, hs], nt_dims, preferred_element_type=F32)
            a = jnp.where(causal, s, 0.0).astype(BF16)
            oacc_scr[rows, hs] = jnp.dot(a, v[:, hs], preferred_element_type=F32)
        for ci in range(per_tile):
            cs = slice(ci * C, (ci + 1) * C)
            full = lax.dot_general(v[cs, :], ku[cs, :], tn_dims, preferred_element_type=F32)
            ut = full[0:C, :]
            for h in range(1, GLA_HEADS):
                ut = jnp.where(lane_head == h, full[h * C:(h + 1) * C, :], ut)
            u_scr[i * per_tile + ci] = ut
            dec_scr[i * per_tile + ci] = dec[ci * C:ci * C + 1, :]
        return carry

    lax.fori_loop(0, n_tiles, phase1, 0, unroll=4)

    def phase2(i, st):
        if reverse:
            ci = jnp.where(i < n_ctx_chunks, n_ctx_chunks - 1 - i, n_chunks - 1 + n_ctx_chunks - i)
        else:
            ci = i
        sin_scr[ci] = st.astype(BF16)
        return st * dec_scr[ci] + u_scr[ci]

    lax.fori_loop(0, n_chunks, phase2, jnp.zeros((C, GLA_W), F32))

    def phase3(i, carry):
        rows = pl.ds(pl.multiple_of(i * SEQ_TILE, SEQ_TILE), SEQ_TILE)
        qs = qs_scr[rows, :]
        parts = []
        for ci in range(per_tile):
            st = sin_scr[i * per_tile + ci]
            st_heads = jnp.where(same_head, jnp.concatenate([st] * GLA_HEADS, axis=0), jnp.zeros((), BF16))
            parts.append(lax.dot_general(qs[ci * C:(ci + 1) * C, :], st_heads, nt_dims, preferred_element_type=F32))
        o = oacc_scr[rows, :] + jnp.concatenate(parts, axis=0)
        ms = jnp.dot((o * o).astype(BF16), head_mean, preferred_element_type=F32)
        y = o * lax.rsqrt(ms + EPS) * ng_ref[...]
        o_ref[rows, :] = (y * _silu(gate_ref[rows, :].astype(F32))).astype(BF16)
        return carry

    lax.fori_loop(0, n_tiles, phase3, 0, unroll=4)


def _gla(proj, w2pad, bias, ng, ctx_len, reverse):
    B, T, _ = proj.shape
    assert T % SEQ_TILE == 0 and ctx_len % GLA_CHUNK == 0
    n_chunks = T // GLA_CHUNK
    col = lambda c0: pl.BlockSpec((None, T, GLA_W), lambda b: (b, 0, c0 // GLA_W))
    gate_col = COL_GGB if reverse else COL_GGF
    kern = functools.partial(_gla_kernel, reverse=reverse, n_ctx_chunks=ctx_len // GLA_CHUNK)
    return pl.pallas_call(
        kern,
        grid=(B,),
        in_specs=[col(COL_GQ), col(COL_GK), col(COL_GV), col(gate_col),
                  pl.BlockSpec((None, T, LANES), lambda b: (b, 0, COL_LR // LANES)),
                  pl.BlockSpec((LANES, GLA_W), lambda b: (0, 0)),
                  pl.BlockSpec((1, GLA_W), lambda b: (0, 0)),
                  pl.BlockSpec((1, GLA_W), lambda b: (0, 0))],
        out_specs=pl.BlockSpec((None, T, GLA_W), lambda b: (b, 0, 0)),
        out_shape=jax.ShapeDtypeStruct((B, T, GLA_W), BF16),
        scratch_shapes=[pltpu.VMEM((T, GLA_W), F32),
                        pltpu.VMEM((T, GLA_W), BF16),
                        pltpu.VMEM((n_chunks, GLA_CHUNK, GLA_W), F32),
                        pltpu.VMEM((n_chunks, 1, GLA_W), F32),
                        pltpu.VMEM((n_chunks, GLA_CHUNK, GLA_W), BF16)],
        compiler_params=_params(("parallel",)),
    )(proj, proj, proj, proj, proj, w2pad, bias, ng)


MXU_TILE = 256
FFN_CHUNK = 3 * MXU_TILE


def _swiglu_resident(h, wgu_ref, wd_ref):
    F = wd_ref.shape[0]
    y = None
    for c0 in range(0, F, FFN_CHUNK):
        c1 = min(c0 + FFN_CHUNK, F)
        g = jnp.dot(h, wgu_ref[:, c0:c1], preferred_element_type=F32)
        u = jnp.dot(h, wgu_ref[:, F + c0:F + c1], preferred_element_type=F32)
        part = jnp.dot((_silu(g) * u).astype(BF16), wd_ref[c0:c1, :], preferred_element_type=F32)
        y = part if y is None else y + part
    return y


def _outproj_kernel(*refs, tm, ctx_len, moe):
    if moe:
        (x_ref, att_ref, gf_ref, gb_ref, cv_ref, w_ref, g1_ref, cg1_ref, ng_ref, sc_ref, csc_ref, sh_ref, csh_ref,
         router_ref, earlier_ref, xo_ref, hp_ref, route_ref, cnt_ref) = refs
    else:
        (x_ref, att_ref, gf_ref, gb_ref, cv_ref, w_ref, g1_ref, cg1_ref, ng_ref, sc_ref, csc_ref, sh_ref, csh_ref,
         g2_ref, cg2_ref, wgu_ref, wd_ref, xo_ref) = refs
    is_ctx = _row_is_ctx(pl.program_id(1), tm, ctx_len)
    gla = (gf_ref[...].astype(F32) + gb_ref[...].astype(F32)).astype(BF16)
    y = jnp.dot(att_ref[...], w_ref[:ATT_W, :], preferred_element_type=F32)
    y += jnp.dot(gla, w_ref[ATT_W:ATT_W + GLA_W, :], preferred_element_type=F32)
    y += jnp.dot(cv_ref[...], w_ref[ATT_W + GLA_W:, :], preferred_element_type=F32)
    x = x_ref[...] + jnp.where(is_ctx, cg1_ref[...], g1_ref[...]) * y
    if moe:
        xo_ref[...] = x
    n = x * lax.rsqrt(jnp.mean(x * x, axis=-1, keepdims=True) + EPS) * ng_ref[...]
    scale = jnp.where(is_ctx, csc_ref[...], sc_ref[...])
    shift = jnp.where(is_ctx, csh_ref[...], sh_ref[...])
    h = n * (1.0 + scale) + shift
    hi = h.astype(BF16)
    if not moe:
        xo_ref[...] = x + jnp.where(is_ctx, cg2_ref[...], g2_ref[...]) * _swiglu_resident(hi, wgu_ref, wd_ref)
    else:
        _store_packed(hp_ref, _pack_bf16_pairs(h))
        logits = jnp.dot(hi, router_ref[...], preferred_element_type=F32)
        lane = lax.broadcasted_iota(jnp.int32, logits.shape, 1)
        neg = -jnp.inf
        lg = jnp.where(lane < N_EXPERTS, logits, neg)
        m1 = jnp.max(lg, axis=-1, keepdims=True)
        i1 = jnp.min(jnp.where(lg == m1, lane, LANES), axis=-1, keepdims=True)
        lg2 = jnp.where(lane == i1, neg, lg)
        m2 = jnp.max(lg2, axis=-1, keepdims=True)
        i2 = jnp.min(jnp.where(lg2 == m2, lane, LANES), axis=-1, keepdims=True)
        e2 = jnp.exp(m2 - m1)
        gate1 = 1.0 / (1.0 + e2)
        gate2 = e2 / (1.0 + e2)
        first = jnp.where(lane == i1, 1.0, 0.0)
        second = jnp.where(lane == i2, 1.0, 0.0)
        before = jnp.dot(earlier_ref[...], jnp.concatenate([first, second], axis=1).astype(BF16),
                         preferred_element_type=F32)
        n_first = jnp.sum(first, axis=0, keepdims=True)
        rank1 = jnp.sum(first * before[:, :LANES], axis=-1, keepdims=True)
        rank2 = jnp.sum(second * (before[:, LANES:] + n_first), axis=-1, keepdims=True)
        cnt_ref[...] = jnp.broadcast_to(n_first + jnp.sum(second, axis=0, keepdims=True), cnt_ref.shape)
        route_ref[...] = jnp.where(lane == 0, i1.astype(F32), jnp.where(lane == 1, i2.astype(F32),
                                   jnp.where(lane == 2, gate1, jnp.where(lane == 3, gate2,
                                             jnp.where(lane == 4, rank1, jnp.where(lane == 5, rank2, 0.0))))))


def _outproj(xt, att, gf, gb, cv, w_out, layer, ng, mod_l, mod_c, ctx_len, router=None, ffn=None):
    B, T, D = xt.shape
    tm = _pick_tile(T, 640)
    moe = router is not None
    assert moe != (ffn is not None)
    row = lambda w: pl.BlockSpec((None, tm, w), lambda b, j: (b, j, 0))
    const = lambda shape: pl.BlockSpec(shape, lambda b, j: (0,) * len(shape))
    in_specs = [row(D), row(ATT_W), row(GLA_W), row(GLA_W), row(CONV_CH),
                pl.BlockSpec((None, D, D), lambda b, j: (layer, 0, 0)),
                *_mod_specs(2, D), const((1, D)), *_mod_specs(4, D), *_mod_specs(3, D)]
    args = [xt, att, gf, gb, cv, w_out, mod_l, mod_c, ng, mod_l, mod_c, mod_l, mod_c]
    nt = T // tm
    if moe:
        in_specs += [pl.BlockSpec((None, D, LANES), lambda b, j: (router[1], 0, 0)), const((tm, tm))]
        args += [router[0], jnp.tri(tm, k=-1, dtype=BF16)]
        out_specs = [row(D), pl.BlockSpec((PACK_CHUNKS, tm, LANES), lambda b, j: (0, b * nt + j, 0)), row(LANES),
                     pl.BlockSpec((None, 8, LANES), lambda b, j: (b * nt + j, 0, 0))]
        out_shape = [jax.ShapeDtypeStruct((B, T, D), F32), jax.ShapeDtypeStruct((PACK_CHUNKS, B * T, LANES), jnp.int32),
                     jax.ShapeDtypeStruct((B, T, LANES), F32), jax.ShapeDtypeStruct((B * nt, 8, LANES), F32)]
    else:
        w_gu, w_down, idx = ffn
        F = w_down.shape[1]
        assert F % MXU_TILE == 0
        resident = lambda shape: pl.BlockSpec((None,) + shape, lambda b, j: (idx, 0, 0), pipeline_mode=pl.Buffered(1))
        in_specs += [*_mod_specs(5, D), resident((D, 2 * F)), resident((F, D))]
        args += [mod_l, mod_c, w_gu, w_down]
        out_specs = row(D)
        out_shape = jax.ShapeDtypeStruct((B, T, D), F32)
    return pl.pallas_call(
        functools.partial(_outproj_kernel, tm=tm, ctx_len=ctx_len, moe=moe),
        grid=(B, T // tm),
        in_specs=in_specs, out_specs=out_specs, out_shape=out_shape,
        compiler_params=_params(("parallel", "parallel")),
    )(*args)


def _sc_gather(table, idx):
    _, width = table.shape
    m = idx.shape[0]
    workers = SC_CORES * SC_SUBCORES
    ch = SC_INDEX_CHUNK
    assert width == LANES and m % (workers * ch) == 0
    n_chunks = m // (workers * ch)
    nb = max(b for b in (4, 3, 2, 1) if n_chunks % b == 0)
    mesh = plsc.VectorSubcoreMesh(core_axis_name="c", subcore_axis_name="s",
                                  num_cores=SC_CORES, num_subcores=SC_SUBCORES)

    @functools.partial(
        pl.kernel, mesh=mesh, out_type=jax.ShapeDtypeStruct((m, width), table.dtype),
        scratch_types=[pltpu.VMEM((n_chunks, ch), jnp.int32), pltpu.VMEM((nb, ch, width), table.dtype)]
        + [pltpu.SemaphoreType.DMA] * (2 * nb))
    def gather(table_hbm, idx_hbm, out_hbm, idx_v, rows_v, *sems):
        wid = lax.axis_index("s") * SC_CORES + lax.axis_index("c")
        base = wid * (n_chunks * ch)
        pltpu.sync_copy(idx_hbm.at[wid], idx_v)

        def fetch(c, b):
            return pltpu.make_async_copy(table_hbm.at[idx_v.at[c]], rows_v.at[b], sems[b])

        def flush(c, b):
            return pltpu.make_async_copy(rows_v.at[b], out_hbm.at[pl.ds(base + c * ch, ch)], sems[nb + b])

        @pl.loop(0, n_chunks, step=nb)
        def _(c0):
            for b in range(nb):
                @pl.when(c0 > 0)
                def _():
                    flush(c0 - nb + b, b).wait()
                fetch(c0 + b, b).start()
            for b in range(nb):
                fetch(c0 + b, b).wait()
                flush(c0 + b, b).start()

        for b in range(nb):
            flush(n_chunks - nb + b, b).wait()

    return gather(table, idx.reshape(workers, n_chunks, ch))


def _sc_dispatch(table, dest_a, dest_b, dest_pad, n_out):
    m, width = table.shape
    n_pad = dest_pad.shape[0]
    workers = SC_CORES * SC_SUBCORES
    ch = SC_INDEX_CHUNK
    assert width == LANES and m % (workers * ch) == 0 and n_pad % (workers * ch) == 0
    n_chunks = m // (workers * ch)
    pad_chunks = n_pad // (workers * ch)
    nb = max(b for b in (4, 3, 2, 1) if n_chunks % b == 0)
    mesh = plsc.VectorSubcoreMesh(core_axis_name="c", subcore_axis_name="s",
                                  num_cores=SC_CORES, num_subcores=SC_SUBCORES)

    @functools.partial(
        pl.kernel, mesh=mesh, out_type=jax.ShapeDtypeStruct((n_out, width), table.dtype),
        scratch_types=[pltpu.VMEM((n_chunks, ch), jnp.int32), pltpu.VMEM((n_chunks, ch), jnp.int32),
                       pltpu.VMEM((pad_chunks, ch), jnp.int32), pltpu.VMEM((nb, ch, width), table.dtype),
                       pltpu.VMEM((ch, width), table.dtype)]
        + [pltpu.SemaphoreType.DMA] * (3 * nb + 1))
    def dispatch(table_hbm, zeros_hbm, da_hbm, db_hbm, dp_hbm, out_hbm, da_v, db_v, dp_v, rows_v, zero_v, *sems):
        wid = lax.axis_index("s") * SC_CORES + lax.axis_index("c")
        base = wid * (n_chunks * ch)
        pltpu.sync_copy(da_hbm.at[wid], da_v)
        pltpu.sync_copy(db_hbm.at[wid], db_v)
        pltpu.sync_copy(dp_hbm.at[wid], dp_v)
        pltpu.sync_copy(zeros_hbm, zero_v)

        def fetch(c, b):
            return pltpu.make_async_copy(table_hbm.at[pl.ds(base + c * ch, ch)], rows_v.at[b], sems[b])

        def put_a(c, b):
            return pltpu.make_async_copy(rows_v.at[b], out_hbm.at[da_v.at[c]], sems[nb + b])

        def put_b(c, b):
            return pltpu.make_async_copy(rows_v.at[b], out_hbm.at[db_v.at[c]], sems[2 * nb + b])

        def put_zero(p):
            return pltpu.make_async_copy(zero_v, out_hbm.at[dp_v.at[p]], sems[3 * nb])

        for p in range(pad_chunks):
            put_zero(p).start()

        @pl.loop(0, n_chunks, step=nb)
        def _(c0):
            for b in range(nb):
                @pl.when(c0 > 0)
                def _():
                    put_a(c0 - nb + b, b).wait()
                    put_b(c0 - nb + b, b).wait()
                fetch(c0 + b, b).start()
            for b in range(nb):
                fetch(c0 + b, b).wait()
                put_a(c0 + b, b).start()
                put_b(c0 + b, b).start()

        for b in range(nb):
            put_a(n_chunks - nb + b, b).wait()
            put_b(n_chunks - nb + b, b).wait()
        for p in range(pad_chunks):
            put_zero(p).wait()

    shape3 = lambda v: v.reshape(workers, -1, ch)
    return dispatch(table, jnp.zeros((ch, width), table.dtype), shape3(dest_a), shape3(dest_b), shape3(dest_pad))


def _routing(route, tile_counts, tile):
    B, T, _ = route.shape
    n = B * T
    assert (2 * n) % tile == 0
    info = route.reshape(n, LANES)[:, :6].astype(jnp.int32)
    per_tile = tile_counts[:, 0, :N_EXPERTS].astype(jnp.int32)
    counts = jnp.sum(per_tile, axis=0)
    padded = (counts + tile - 1) // tile * tile
    ends = jnp.cumsum(padded)
    starts = ends - padded
    base = jnp.repeat(starts[None, :] + jnp.cumsum(per_tile, axis=0) - per_tile, n // per_tile.shape[0], axis=0)
    experts = jnp.arange(N_EXPERTS, dtype=jnp.int32)[None, :]
    row_of = lambda e, rank: jnp.sum(jnp.where(e[:, None] == experts, base, 0), axis=1) + rank
    dest = jnp.concatenate([row_of(info[:, 0], info[:, 4]), row_of(info[:, 1], info[:, 5])])
    n_pad = N_EXPERTS * tile
    n_rows = 2 * n + n_pad
    seg_first = jnp.concatenate([starts + counts, ends[-1:]])
    seg_size = jnp.concatenate([padded - counts, (n_rows - ends[-1])[None]])
    seg_end = jnp.cumsum(seg_size)
    j = jnp.arange(n_pad, dtype=jnp.int32)
    seg = jnp.sum((j[:, None] >= seg_end[None, :]).astype(jnp.int32), axis=1)
    pad_rows = (seg_first[seg] + j - (seg_end - seg_size)[seg]).astype(jnp.int32)
    tile_start = jnp.arange(n_rows // tile, dtype=jnp.int32) * tile
    tile_expert = jnp.minimum(jnp.sum((tile_start[:, None] >= ends[None, :]).astype(jnp.int32), axis=1), N_EXPERTS - 1)
    tile_valid = (tile_start < ends[-1]).astype(jnp.int32)
    return dest, pad_rows, n_rows, tile_expert, tile_valid


def _expert_kernel(te_ref, tv_ref, xs_ref, wgu_ref, wd_ref, ys_ref, wgu_scr):
    i = pl.program_id(0)
    F = wd_ref.shape[0]

    @pl.when((i == 0) | (te_ref[i] != te_ref[jnp.maximum(i - 1, 0)]))
    def _new_expert():
        wgu_scr[...] = wgu_ref[...].astype(BF16)

    @pl.when(tv_ref[i] > 0)
    def _compute():
        lo, hi = _unpack_bf16_pairs(_load_packed(xs_ref))
        gu = (jnp.dot(lo.astype(BF16), wgu_scr[:PACK_W, :], preferred_element_type=F32)
              + jnp.dot(hi.astype(BF16), wgu_scr[PACK_W:, :], preferred_element_type=F32))
        act = (_silu(gu[:, :F]) * gu[:, F:]).astype(BF16)
        y = jnp.dot(act, wd_ref[...], preferred_element_type=F32)
        _store_packed(ys_ref, _pack_bf16_pairs(y))

    @pl.when(tv_ref[i] == 0)
    def _unused_tile():
        ys_ref[...] = jnp.zeros_like(ys_ref)


def _experts(xs, tile_expert, tile_valid, w_gu, w_down, layer):
    _, n_rows, _ = xs.shape
    _, _, F, D = w_down.shape
    tile = EXPERT_TILE
    rows = pl.BlockSpec((PACK_CHUNKS, tile, LANES), lambda i, te, tv: (0, i, 0))
    return pl.pallas_call(
        _expert_kernel,
        grid_spec=pltpu.PrefetchScalarGridSpec(
            num_scalar_prefetch=2,
            grid=(n_rows // tile,),
            in_specs=[rows,
                      pl.BlockSpec((None, None, D, 2 * F), lambda i, te, tv: (layer, te[i], 0, 0)),
                      pl.BlockSpec((None, None, F, D), lambda i, te, tv: (layer, te[i], 0, 0))],
            out_specs=rows,
            scratch_shapes=[pltpu.VMEM((D, 2 * F), BF16)]),
        out_shape=jax.ShapeDtypeStruct(xs.shape, jnp.int32),
        compiler_params=_params(("arbitrary",)),
    )(tile_expert, tile_valid, xs, w_gu, w_down)


def _combine_kernel(*refs, tm, ctx_len, skip, final):
    x_ref, g2_ref, cg2_ref, route_ref, ya_ref, yb_ref = refs[:6]
    fg_ref = refs[6] if final else None
    o_ref = refs[-1]
    is_ctx = _row_is_ctx(pl.program_id(1) + skip, tm, ctx_len)
    route = route_ref[...]
    lane = lax.broadcasted_iota(jnp.int32, route.shape, 1)
    gate_a = jnp.sum(jnp.where(lane == 2, route, 0.0), axis=-1, keepdims=True)
    gate_b = jnp.sum(jnp.where(lane == 3, route, 0.0), axis=-1, keepdims=True)
    unpack = lambda ref: jnp.concatenate(_unpack_bf16_pairs(_load_packed(ref)), axis=1)
    f = gate_a * unpack(ya_ref) + gate_b * unpack(yb_ref)
    x = x_ref[...] + jnp.where(is_ctx, cg2_ref[...], g2_ref[...]) * f
    if final:
        x = x * lax.rsqrt(jnp.mean(x * x, axis=-1, keepdims=True) + EPS) * fg_ref[...]
    o_ref[...] = x


def _combine(xt, mod_l, mod_c, route, yg, ctx_len, final_g=None):
    B, T, D = xt.shape
    final = final_g is not None
    if final:
        tm = SEQ_TILE
        assert ctx_len % tm == 0 and T % tm == 0
        skip = ctx_len // tm
    else:
        tm = _pick_tile(T, 640)
        skip = 0
    nt = T // tm
    row = lambda w: pl.BlockSpec((None, tm, w), lambda b, j: (b, j + skip, 0))
    slot = lambda s: pl.BlockSpec((PACK_CHUNKS, None, tm, LANES), lambda b, j: (0, s, b * nt + j + skip, 0))
    in_specs = [row(D), *_mod_specs(5, D), row(LANES), slot(0), slot(1)]
    args = [xt, mod_l, mod_c, route, yg, yg]
    if final:
        in_specs.append(pl.BlockSpec((1, D), lambda b, j: (0, 0)))
        args.append(final_g)
    return pl.pallas_call(
        functools.partial(_combine_kernel, tm=tm, ctx_len=ctx_len, skip=skip, final=final),
        grid=(B, nt - skip),
        in_specs=in_specs,
        out_specs=pl.BlockSpec((None, tm, D), lambda b, j: (b, j, 0)),
        out_shape=jax.ShapeDtypeStruct((B, T - skip * tm, D), F32),
        compiler_params=_params(("parallel", "parallel")),
    )(*args)


def _moe(hp, route, tile_counts, xt, mod_l, mod_c, w_gu, w_down, layer, ctx_len, final_g=None):
    B, T, _ = xt.shape
    n = B * T
    dest, pad_rows, n_rows, tile_expert, tile_valid = _routing(route, tile_counts, EXPERT_TILE)
    chunk = jnp.arange(PACK_CHUNKS, dtype=jnp.int32)[:, None] * n_rows
    xs = _sc_dispatch(hp.reshape(PACK_CHUNKS * n, LANES), (chunk + dest[None, :n]).reshape(-1),
                      (chunk + dest[None, n:]).reshape(-1), (chunk + pad_rows[None, :]).reshape(-1),
                      PACK_CHUNKS * n_rows)
    ys = _experts(xs.reshape(PACK_CHUNKS, n_rows, LANES), tile_expert, tile_valid, w_gu, w_down, layer)
    ys = ys.reshape(PACK_CHUNKS * n_rows, LANES)
    yg = _sc_gather(ys, (chunk + dest[None, :]).reshape(-1)).reshape(PACK_CHUNKS, 2, n, LANES)
    return _combine(xt, mod_l, mod_c, route, yg, ctx_len, final_g)


def _final_norm_kernel(x_ref, g_ref, o_ref):
    x = x_ref[...]
    o_ref[...] = x * lax.rsqrt(jnp.mean(x * x, axis=-1, keepdims=True) + EPS) * g_ref[...]


def _final_norm(xt, g, ctx_len):
    B, T, D = xt.shape
    S = T - ctx_len
    tm = SEQ_TILE
    assert ctx_len % tm == 0 and S % tm == 0
    skip = ctx_len // tm
    return pl.pallas_call(
        _final_norm_kernel,
        grid=(B, S // tm),
        in_specs=[pl.BlockSpec((None, tm, D), lambda b, j: (b, j + skip, 0)),
                  pl.BlockSpec((1, D), lambda b, j: (0, 0))],
        out_specs=pl.BlockSpec((None, tm, D), lambda b, j: (b, j, 0)),
        out_shape=jax.ShapeDtypeStruct((B, S, D), F32),
        compiler_params=_params(("parallel", "parallel")),
    )(xt, g)


def _rope_tables(seq, ctx_len):
    pos = jnp.arange(seq, dtype=jnp.int32)
    nf = HEAD_DIM // 4
    inv = ROPE_THETA ** (-jnp.arange(nf, dtype=F32) / nf)
    ang = jnp.concatenate([(pos // GRID_W).astype(F32)[:, None] * inv, (pos % GRID_W).astype(F32)[:, None] * inv], axis=-1)
    cos, sin = jnp.cos(ang), jnp.sin(ang)
    reps = LANES // HEAD_DIM
    cos_t = jnp.tile(jnp.concatenate([cos, cos], axis=-1), (1, reps))
    sin_t = jnp.tile(jnp.concatenate([-sin, sin], axis=-1), (1, reps))
    cos_t = jnp.concatenate([jnp.ones((ctx_len, LANES), F32), cos_t], axis=0)
    sin_t = jnp.concatenate([jnp.zeros((ctx_len, LANES), F32), sin_t], axis=0)
    return cos_t, sin_t


def _relayout_w_in(w_in):
    o_lr = ATT_W + 2 * KV_W + 5 * GLA_W
    q_scale = HEAD_DIM ** -0.5 * LOG2E
    gq_scale = GLA_DK ** -0.5
    parts = [w_in[..., :ATT_W] * q_scale, w_in[..., ATT_W:COL_GQ], w_in[..., COL_GQ:COL_GK] * gq_scale,
             w_in[..., COL_GK:o_lr], w_in[..., o_lr + 2 * GLA_LOWRANK:], w_in[..., o_lr:o_lr + 2 * GLA_LOWRANK],
             jnp.zeros(w_in.shape[:-1] + (LANES - 2 * GLA_LOWRANK,), w_in.dtype)]
    return jnp.concatenate(parts, axis=-1).astype(BF16)


def kernel(x, c, ctx, c_ctx, w_mod, b_mod, norm1_g, norm2_g, w_in, w_out, attn_sink, gla_w2_f, gla_b_f, gla_w2_b, gla_b_b, gla_norm_g, conv_w, ffn_w_gu, ffn_w_down, router_w, expert_w_gu, expert_w_down, final_norm_g):
    B, S, D = x.shape
    L = ctx.shape[1]
    depth = w_in.shape[0]
    assert D == D_MODEL and S % GRID_W == 0

    xt = jnp.concatenate([ctx, x], axis=1)
    rows = 16
    cc = jnp.concatenate([c, c_ctx[None, :], jnp.zeros((rows - B - 1, D), F32)], axis=0)
    mod = _adaln(cc, w_mod, b_mod)
    cos_t, sin_t = _rope_tables(S, L)

    w_in_p = _relayout_w_in(w_in)
    w_out_b = _to_bf16(w_out)
    zpad = jnp.zeros((depth, LANES - 2 * GLA_LOWRANK, GLA_W), F32)
    w2_f = jnp.concatenate([gla_w2_f, jnp.zeros_like(gla_w2_b), zpad], axis=1).astype(BF16)
    w2_b = jnp.concatenate([jnp.zeros_like(gla_w2_f), gla_w2_b, zpad], axis=1).astype(BF16)
    gla_ng = jnp.tile(gla_norm_g, (1, GLA_HEADS))
    ffn_gu_b = _to_bf16(ffn_w_gu)
    ffn_down_b = _to_bf16(ffn_w_down)
    exp_down_b = _to_bf16(expert_w_down)
    router_b = jnp.pad(router_w, ((0, 0), (0, 0), (0, LANES - N_EXPERTS))).astype(BF16)

    for l in range(depth):
        mod_l = mod[l, :B].reshape(B, 1, 6 * D)
        mod_c = mod[l, B:B + 1]
        proj = _inproj(xt, norm1_g[l][None, :], mod_l, mod_c, cos_t, sin_t, w_in_p, l, L)
        att = _attention(proj, attn_sink[l], L)
        cv = _short_conv(proj, conv_w[l], L)
        gf = _gla(proj, w2_f[l], gla_b_f[l][None, :], gla_ng[l][None, :], L, reverse=False)
        gb = _gla(proj, w2_b[l], gla_b_b[l][None, :], gla_ng[l][None, :], L, reverse=True)
        if l % 2 == 0:
            xt = _outproj(xt, att, gf, gb, cv, w_out_b, l, norm2_g[l][None, :], mod_l, mod_c, L,
                          ffn=(ffn_gu_b, ffn_down_b, l // 2))
        else:
            xt, hp, route, tile_counts = _outproj(xt, att, gf, gb, cv, w_out_b, l, norm2_g[l][None, :], mod_l, mod_c, L,
                                     router=(router_b, l // 2))
            xt = _moe(hp, route, tile_counts, xt, mod_l, mod_c, expert_w_gu, exp_down_b, l // 2, L,
                      final_g=final_norm_g[None, :] if l == depth - 1 else None)
    return xt if depth % 2 == 0 else _final_norm(xt, final_norm_g[None, :], L)
```

```python
import functools

import jax
import jax.numpy as jnp
from jax import lax
from jax.experimental import pallas as pl
from jax.experimental.pallas import tpu as pltpu
from jax.experimental.pallas import tpu_sc as plsc

F32 = jnp.float32
BF16 = jnp.bfloat16

D_MODEL = 1024
EPS = 1e-6
GRID_W = 64
ROPE_THETA = 10000.0
LOG2E = 1.4426950408889634
HEAD_DIM = 64
ATT_HEADS = 8
ATT_KV_HEADS = 2
ATT_GROUP = ATT_HEADS // ATT_KV_HEADS
ATT_W = ATT_HEADS * HEAD_DIM
KV_W = ATT_KV_HEADS * HEAD_DIM
WINDOW = 128
GLA_HEADS = 4
GLA_DK = 64
GLA_W = GLA_HEADS * GLA_DK
GLA_LOWRANK = 16
GLA_TAU = 16.0
GLA_CHUNK = 64
CONV_CH = 256
N_EXPERTS = 8

LANES = 128
SUBLANES_BF16 = 16

COL_Q = 0
COL_K = ATT_W
COL_V = COL_K + KV_W
ROPE_W = COL_V
COL_GQ = COL_V + KV_W
COL_GK = COL_GQ + GLA_W
COL_GV = COL_GK + GLA_W
COL_GGF = COL_GV + GLA_W
COL_GGB = COL_GGF + GLA_W
COL_CB = COL_GGB + GLA_W
COL_CC = COL_CB + CONV_CH
COL_CX = COL_CC + CONV_CH
COL_LR = COL_CX + CONV_CH
PROJ_W = COL_LR + LANES

SEQ_TILE = 256
VMEM_LIMIT = 56 * 1024 * 1024
CAST_BLOCK_BYTES = 4 * 1024 * 1024

PACK_W = D_MODEL // 2
PACK_CHUNKS = PACK_W // LANES
EXPERT_TILE = 512
SC_CORES = 2
SC_SUBCORES = 16
SC_INDEX_CHUNK = 128


def _silu(v):
    return v / (1.0 + jnp.exp(-v))


def _pick_tile(total, cap):
    best = None
    for t in range(SUBLANES_BF16, cap + 1, SUBLANES_BF16):
        if total % t == 0:
            best = t
    assert best is not None
    return best


def _params(sem):
    return pltpu.CompilerParams(dimension_semantics=sem, vmem_limit_bytes=VMEM_LIMIT)


def _pack_bf16_pairs(v):
    bits = lambda t: lax.bitcast_convert_type(t.astype(BF16).astype(F32), jnp.int32)
    return ((bits(v[:, :PACK_W]) >> 16) & 0xFFFF) | (bits(v[:, PACK_W:]) & -65536)


def _unpack_bf16_pairs(w):
    return lax.bitcast_convert_type(w << 16, F32), lax.bitcast_convert_type(w & -65536, F32)


CAST_STREAMS = 4


def _cast_kernel(*refs):
    o_ref = refs[-1]
    slab = refs[0].shape[0]
    for s, x_ref in enumerate(refs[:-1]):
        o_ref[s * slab:(s + 1) * slab, :] = x_ref[...].astype(BF16)


def _to_bf16(w, lead=None, after=None):
    cols = w.shape[-1]
    rows_in = w.size // cols
    rows = rows_in if lead is None else rows_in // w.shape[0]
    tr = _pick_tile(rows, max(SUBLANES_BF16, CAST_BLOCK_BYTES // (4 * cols)))
    streams = CAST_STREAMS if tr % (CAST_STREAMS * SUBLANES_BF16) == 0 else 1
    slab = tr // streams
    first = 0 if lead is None else lead * (rows // slab)
    w2 = w.reshape(rows_in, cols)
    if after is not None:
        w2, _ = lax.optimization_barrier((w2, after))
    out = pl.pallas_call(
        _cast_kernel,
        grid=(rows // tr,),
        in_specs=[pl.BlockSpec((slab, cols), lambda i, s=s: (first + i * streams + s, 0)) for s in range(streams)],
        out_specs=pl.BlockSpec((tr, cols), lambda i: (i, 0)),
        out_shape=jax.ShapeDtypeStruct((rows, cols), BF16),
        compiler_params=_params(("parallel",)),
    )(*([w2] * streams))
    return out.reshape(w.shape if lead is None else (1,) + w.shape[1:])


def _load_packed(ref):
    return jnp.concatenate([ref[k] for k in range(PACK_CHUNKS)], axis=1)


def _store_packed(ref, packed):
    for k in range(PACK_CHUNKS):
        ref[k] = packed[:, k * LANES:(k + 1) * LANES]


def _adaln_kernel(c_ref, w_ref, b_ref, o_ref):
    s = _silu(c_ref[...]).astype(BF16)
    o_ref[...] = jnp.dot(s, w_ref[...].astype(BF16), preferred_element_type=F32) + b_ref[...]


def _adaln(cc, w_mod, b_mod):
    depth, d, n = w_mod.shape
    tn = 1024
    rows = cc.shape[0]
    return pl.pallas_call(
        _adaln_kernel,
        grid=(depth, n // tn),
        in_specs=[pl.BlockSpec((rows, d), lambda l, j: (0, 0)),
                  pl.BlockSpec((None, d, tn), lambda l, j: (l, 0, j)),
                  pl.BlockSpec((None, 1, tn), lambda l, j: (l, 0, j))],
        out_specs=pl.BlockSpec((None, rows, tn), lambda l, j: (l, 0, j)),
        out_shape=jax.ShapeDtypeStruct((depth, rows, n), F32),
        compiler_params=_params(("parallel", "parallel")),
    )(cc, w_mod, b_mod.reshape(depth, 1, n))


def _mod_specs(k, d):
    return [pl.BlockSpec((None, 1, d), lambda b, j, *_: (b, 0, k)),
            pl.BlockSpec((1, d), lambda b, j, *_: (0, k))]


def _row_is_ctx(j, tm, ctx_len):
    return (j * tm + lax.broadcasted_iota(jnp.int32, (tm, 1), 0)) < ctx_len


def _inproj_kernel(x_ref, g_ref, sc_ref, csc_ref, sh_ref, csh_ref, cos_ref, sin_ref, w_ref, o_ref, *, tm, ctx_len):
    x = x_ref[...]
    n = x * lax.rsqrt(jnp.mean(x * x, axis=-1, keepdims=True) + EPS) * g_ref[...]
    is_ctx = _row_is_ctx(pl.program_id(1), tm, ctx_len)
    scale = jnp.where(is_ctx, csc_ref[...], sc_ref[...])
    shift = jnp.where(is_ctx, csh_ref[...], sh_ref[...])
    h = (n * (1.0 + scale) + shift).astype(BF16)
    qk = jnp.dot(h, w_ref[:, :ROPE_W], preferred_element_type=F32)
    lower_half = (lax.broadcasted_iota(jnp.int32, (tm, LANES), 1) % HEAD_DIM) < (HEAD_DIM // 2)
    cos = cos_ref[...]
    sin = sin_ref[...]
    for i in range(ROPE_W // LANES):
        t = qk[:, i * LANES:(i + 1) * LANES]
        partner = jnp.where(lower_half, pltpu.roll(t, LANES - HEAD_DIM // 2, axis=1), pltpu.roll(t, HEAD_DIM // 2, axis=1))
        o_ref[:, i * LANES:(i + 1) * LANES] = (t * cos + partner * sin).astype(BF16)
    o_ref[:, ROPE_W:] = jnp.dot(h, w_ref[:, ROPE_W:], preferred_element_type=F32).astype(BF16)


def _inproj(xt, g, mod_l, mod_c, cos_t, sin_t, w, layer, ctx_len):
    B, T, D = xt.shape
    tm = _pick_tile(T, 640)
    kern = functools.partial(_inproj_kernel, tm=tm, ctx_len=ctx_len)
    return pl.pallas_call(
        kern,
        grid=(B, T // tm),
        in_specs=[pl.BlockSpec((None, tm, D), lambda b, j: (b, j, 0)),
                  pl.BlockSpec((1, D), lambda b, j: (0, 0)),
                  *_mod_specs(1, D), *_mod_specs(0, D),
                  pl.BlockSpec((tm, LANES), lambda b, j: (j, 0)),
                  pl.BlockSpec((tm, LANES), lambda b, j: (j, 0)),
                  pl.BlockSpec((None, D, PROJ_W), lambda b, j: (layer, 0, 0))],
        out_specs=pl.BlockSpec((None, tm, PROJ_W), lambda b, j: (b, j, 0)),
        out_shape=jax.ShapeDtypeStruct((B, T, PROJ_W), BF16),
        compiler_params=_params(("parallel", "parallel")),
    )(xt, g, mod_l, mod_c, mod_l, mod_c, cos_t, sin_t, w)


def _attn_kernel(sink_ref, q_ref, kc_ref, vc_ref, km_ref, vm_ref, kp_ref, vp_ref, kn_ref, vn_ref, o_ref, *, n_tiles):
    j = pl.program_id(1)
    nt_dims = (((1,), (1,)), ((), ()))
    tn_dims = (((0,), (0,)), ((), ()))

    def scores(rows, n_rows, g, k_ctx, k_win=None, mask_win=None):
        heads = [g * ATT_GROUP + hh for hh in range(ATT_GROUP)]
        q = jnp.concatenate([q_ref[rows, h * HEAD_DIM:(h + 1) * HEAD_DIM] for h in heads], axis=0)
        head_col = lax.broadcasted_iota(jnp.int32, (1, ATT_GROUP * n_rows), 1) // n_rows
        sink = jnp.full((1, ATT_GROUP * n_rows), sink_ref[heads[0]] * LOG2E, F32)
        for hh in range(1, ATT_GROUP):
            sink = jnp.where(head_col == hh, sink_ref[heads[hh]] * LOG2E, sink)
        s_c = lax.dot_general(k_ctx, q, nt_dims, preferred_element_type=F32)
        s_w = None
        if k_win is not None:
            s_w = jnp.where(mask_win, lax.dot_general(k_win, q, nt_dims, preferred_element_type=F32), -1e30)
        return sink, s_c, s_w

    def finish(item, v_ctx, v_win=None):
        sink, s_c, s_w = item
        m = jnp.maximum(jnp.max(s_c, axis=0, keepdims=True), sink)
        if s_w is not None:
            m = jnp.maximum(m, jnp.max(s_w, axis=0, keepdims=True))
        p_c = jnp.exp2(s_c - m)
        den = jnp.sum(p_c, axis=0, keepdims=True) + jnp.exp2(sink - m)
        o = lax.dot_general(v_ctx, p_c.astype(BF16), tn_dims, preferred_element_type=F32)
        if s_w is not None:
            p_w = jnp.exp2(s_w - m)
            den += jnp.sum(p_w, axis=0, keepdims=True)
            o += lax.dot_general(v_win, p_w.astype(BF16), tn_dims, preferred_element_type=F32)
        return o / den

    def store(rows, n_rows, outs):
        t = jnp.transpose(jnp.concatenate(outs, axis=0))
        for g in range(ATT_KV_HEADS):
            for hh in range(ATT_GROUP):
                h = g * ATT_GROUP + hh
                o_ref[rows, h * HEAD_DIM:(h + 1) * HEAD_DIM] = (
                    t[hh * n_rows:(hh + 1) * n_rows, g * HEAD_DIM:(g + 1) * HEAD_DIM].astype(BF16))

    def run(blocks):
        flat = [(bi, s, f) for bi, (_, _, work) in enumerate(blocks) for s, f in work]
        outs = [[] for _ in blocks]
        pending = None
        for bi, score_fn, finish_fn in flat + [(None, None, None)]:
            item = None if score_fn is None else score_fn()
            if pending is not None:
                pbi, pitem, pfinish = pending
                outs[pbi].append(pfinish(pitem))
                if len(outs[pbi]) == ATT_KV_HEADS:
                    store(blocks[pbi][0], blocks[pbi][1], outs[pbi])
            pending = None if score_fn is None else (bi, item, finish_fn)

    @pl.when(j == 0)
    def _context_queries():
        work = []
        for g in range(ATT_KV_HEADS):
            gs = slice(g * HEAD_DIM, (g + 1) * HEAD_DIM)
            work.append((functools.partial(scores, slice(0, SEQ_TILE), SEQ_TILE, g, kc_ref[:, gs]),
                         functools.partial(finish, v_ctx=vc_ref[:, gs])))
        run([(slice(0, SEQ_TILE), SEQ_TILE, work)])

    @pl.when(j > 0)
    def _latent_queries():
        w = lax.broadcasted_iota(jnp.int32, (3 * WINDOW, ATT_GROUP * WINDOW), 0)
        r = lax.broadcasted_iota(jnp.int32, (3 * WINDOW, ATT_GROUP * WINDOW), 1) % WINDOW
        band = (w >= r) & (w <= r + 2 * WINDOW)
        blocks = []
        for sub in range(SEQ_TILE // WINDOW):
            rows = slice(sub * WINDOW, (sub + 1) * WINDOW)
            if sub == 0:
                kprev, vprev = kp_ref[...], vp_ref[...]
                kmid, vmid = km_ref[:WINDOW], vm_ref[:WINDOW]
                knext, vnext = km_ref[WINDOW:], vm_ref[WINDOW:]
                w_lo = jnp.where(j > 1, 0, WINDOW)
                w_hi = 3 * WINDOW
            else:
                kprev, vprev = km_ref[:WINDOW], vm_ref[:WINDOW]
                kmid, vmid = km_ref[WINDOW:], vm_ref[WINDOW:]
                knext, vnext = kn_ref[...], vn_ref[...]
                w_lo = 0
                w_hi = jnp.where(j < n_tiles - 1, 3 * WINDOW, 2 * WINDOW)
            mask = band & (w >= w_lo) & (w < w_hi)
            work = []
            for g in range(ATT_KV_HEADS):
                gs = slice(g * HEAD_DIM, (g + 1) * HEAD_DIM)
                k_win = jnp.concatenate([kprev[:, gs], kmid[:, gs], knext[:, gs]], axis=0)
                v_win = jnp.concatenate([vprev[:, gs], vmid[:, gs], vnext[:, gs]], axis=0)
                work.append((functools.partial(scores, rows, WINDOW, g, kc_ref[:, gs], k_win, mask),
                             functools.partial(finish, v_ctx=vc_ref[:, gs], v_win=v_win)))
            blocks.append((rows, WINDOW, work))
        run(blocks)


def _attention(proj, sink, ctx_len):
    B, T, _ = proj.shape
    assert ctx_len == SEQ_TILE and T % SEQ_TILE == 0
    n_tiles = T // SEQ_TILE
    n_win = T // WINDOW
    kcol, vcol = COL_K // KV_W, COL_V // KV_W
    per_tile = SEQ_TILE // WINDOW
    tile = lambda col: pl.BlockSpec((None, SEQ_TILE, KV_W), lambda b, j: (b, j, col))
    ctx = lambda col: pl.BlockSpec((None, SEQ_TILE, KV_W), lambda b, j: (b, 0, col))
    prev = lambda col: pl.BlockSpec((None, WINDOW, KV_W), lambda b, j: (b, jnp.maximum(j * per_tile - 1, 0), col))
    nxt = lambda col: pl.BlockSpec((None, WINDOW, KV_W), lambda b, j: (b, jnp.minimum((j + 1) * per_tile, n_win - 1), col))
    return pl.pallas_call(
        functools.partial(_attn_kernel, n_tiles=n_tiles),
        grid=(B, n_tiles),
        in_specs=[pl.BlockSpec(memory_space=pltpu.SMEM),
                  pl.BlockSpec((None, SEQ_TILE, ATT_W), lambda b, j: (b, j, 0)),
                  ctx(kcol), ctx(vcol), tile(kcol), tile(vcol), prev(kcol), prev(vcol), nxt(kcol), nxt(vcol)],
        out_specs=pl.BlockSpec((None, SEQ_TILE, ATT_W), lambda b, j: (b, j, 0)),
        out_shape=jax.ShapeDtypeStruct((B, T, ATT_W), BF16),
        compiler_params=_params(("parallel", "parallel")),
    )(sink, *([proj] * 9))


def _conv_kernel(b_ref, c_ref, x_ref, w_ref, o_ref, u_scr, *, ctx_len):
    T = c_ref.shape[0]
    pad = 8
    u = c_ref[...].astype(F32) * x_ref[...].astype(F32)
    zero_row = jnp.zeros((1, CONV_CH), F32)
    u_scr[pad - 1:pad, :] = zero_row
    u_scr[pad:pad + T, :] = u
    u_scr[pad + T:pad + T + 1, :] = zero_row
    row = lax.broadcasted_iota(jnp.int32, (T, 1), 0)
    u_prev = jnp.where(row == ctx_len, 0.0, u_scr[pad - 1:pad - 1 + T, :])
    u_next = jnp.where(row == ctx_len - 1, 0.0, u_scr[pad + 1:pad + 1 + T, :])
    y = w_ref[0:1, :] * u_prev + w_ref[1:2, :] * u + w_ref[2:3, :] * u_next
    o_ref[...] = (b_ref[...].astype(F32) * y).astype(BF16)


def _short_conv(proj, conv_w, ctx_len):
    B, T, _ = proj.shape
    col = lambda c0: pl.BlockSpec((None, T, CONV_CH), lambda b: (b, 0, c0 // CONV_CH))
    return pl.pallas_call(
        functools.partial(_conv_kernel, ctx_len=ctx_len),
        grid=(B,),
        in_specs=[col(COL_CB), col(COL_CC), col(COL_CX), pl.BlockSpec((3, CONV_CH), lambda b: (0, 0))],
        out_specs=pl.BlockSpec((None, T, CONV_CH), lambda b: (b, 0, 0)),
        out_shape=jax.ShapeDtypeStruct((B, T, CONV_CH), BF16),
        scratch_shapes=[pltpu.VMEM((T + 16, CONV_CH), F32)],
        compiler_params=_params(("parallel",)),
    )(proj, proj, proj, conv_w)


def _split2(v):
    hi = v.astype(BF16)
    lo = (v - hi.astype(F32)).astype(BF16)
    return hi, lo


def _gla_kernel(q_ref, k_ref, v_ref, gate_ref, lr_ref, w2_ref, bias_ref, ng_ref, o_ref,
                oacc_scr, qs_scr, u_scr, dec_scr, sin_scr, *, reverse, n_ctx_chunks):
    T = q_ref.shape[0]
    C = GLA_CHUNK
    n_tiles = T // SEQ_TILE
    n_chunks = T // C
    per_tile = SEQ_TILE // C
    nt_dims = (((1,), (1,)), ((), ()))
    tn_dims = (((0,), (0,)), ((), ()))

    r = lax.broadcasted_iota(jnp.int32, (SEQ_TILE, SEQ_TILE), 0)
    c = lax.broadcasted_iota(jnp.int32, (SEQ_TILE, SEQ_TILE), 1)
    same_chunk = (r // C) == (c // C)
    causal = same_chunk & ((c >= r) if reverse else (c <= r))
    cum_mat = jnp.where(causal, 1.0, 0.0).astype(BF16)
    same_head = (r // GLA_DK) == (c // GLA_DK)
    head_mean = jnp.where(same_head, 1.0 / GLA_DK, 0.0).astype(BF16)
    lane_head = lax.broadcasted_iota(jnp.int32, (C, GLA_W), 1) // GLA_DK

    def phase1(i, carry):
        rows = pl.ds(pl.multiple_of(i * SEQ_TILE, SEQ_TILE), SEQ_TILE)
        z = jnp.dot(lr_ref[rows, :], w2_ref[...], preferred_element_type=F32) + bias_ref[...]
        la = (jnp.minimum(z, 0.0) - jnp.log(1.0 + jnp.exp(-jnp.abs(z)))) / GLA_TAU
        hi, lo = _split2(la)
        b = jnp.dot(cum_mat, hi, preferred_element_type=F32) + jnp.dot(cum_mat, lo, preferred_element_type=F32)
        b3 = b.reshape(per_tile, C, GLA_W)
        total = b3[:, 0:1, :] if reverse else b3[:, C - 1:C, :]
        b_last = jnp.broadcast_to(total, (per_tile, C, GLA_W)).reshape(SEQ_TILE, GLA_W)
        b_ref = 0.5 * b_last
        q = q_ref[rows, :].astype(F32)
        k = k_ref[rows, :].astype(F32)
        v = v_ref[rows, :]
        e_fwd = jnp.exp(b - b_ref)
        e_bwd = jnp.exp(b_ref - b)
        e_half = jnp.exp(b_ref)
        qe = (q * e_fwd).astype(BF16)
        ke = (k * e_bwd).astype(BF16)
        ku = (k * (e_bwd * e_half)).astype(BF16)
        qs_scr[rows, :] = (q * (e_fwd * e_half)).astype(BF16)
        dec = e_half * e_half
        for h in range(GLA_HEADS):
            hs = slice(h * GLA_DK, (h + 1) * GLA_DK)
            s = lax.dot_general(qe[:, hs], ke[:, hs], nt_dims, preferred_element_type=F32)
            a = jnp.where(causal, s, 0.0).astype(BF16)
            oacc_scr[rows, hs] = jnp.dot(a, v[:, hs], preferred_element_type=F32)
        for ci in range(per_tile):
            cs = slice(ci * C, (ci + 1) * C)
            full = lax.dot_general(v[cs, :], ku[cs, :], tn_dims, preferred_element_type=F32)
            ut = full[0:C, :]
            for h in range(1, GLA_HEADS):
                ut = jnp.where(lane_head == h, full[h * C:(h + 1) * C, :], ut)
            u_scr[i * per_tile + ci] = ut
            dec_scr[i * per_tile + ci] = dec[ci * C:ci * C + 1, :]
        return carry

    lax.fori_loop(0, n_tiles, phase1, 0, unroll=4)

    def phase2(i, st):
        if reverse:
            ci = jnp.where(i < n_ctx_chunks, n_ctx_chunks - 1 - i, n_chunks - 1 + n_ctx_chunks - i)
        else:
            ci = i
        sin_scr[ci] = st.astype(BF16)
        return st * dec_scr[ci] + u_scr[ci]

    lax.fori_loop(0, n_chunks, phase2, jnp.zeros((C, GLA_W), F32))

    def phase3(i, carry):
        rows = pl.ds(pl.multiple_of(i * SEQ_TILE, SEQ_TILE), SEQ_TILE)
        qs = qs_scr[rows, :]
        parts = []
        for ci in range(per_tile):
            st = sin_scr[i * per_tile + ci]
            st_heads = jnp.where(same_head, jnp.concatenate([st] * GLA_HEADS, axis=0), jnp.zeros((), BF16))
            parts.append(lax.dot_general(qs[ci * C:(ci + 1) * C, :], st_heads, nt_dims, preferred_element_type=F32))
        o = oacc_scr[rows, :] + jnp.concatenate(parts, axis=0)
        ms = jnp.dot((o * o).astype(BF16), head_mean, preferred_element_type=F32)
        y = o * lax.rsqrt(ms + EPS) * ng_ref[...]
        o_ref[rows, :] = (y * _silu(gate_ref[rows, :].astype(F32))).astype(BF16)
        return carry

    lax.fori_loop(0, n_tiles, phase3, 0, unroll=4)


def _gla(proj, w2pad, bias, ng, ctx_len, reverse):
    B, T, _ = proj.shape
    assert T % SEQ_TILE == 0 and ctx_len % GLA_CHUNK == 0
    n_chunks = T // GLA_CHUNK
    col = lambda c0: pl.BlockSpec((None, T, GLA_W), lambda b: (b, 0, c0 // GLA_W))
    gate_col = COL_GGB if reverse else COL_GGF
    kern = functools.partial(_gla_kernel, reverse=reverse, n_ctx_chunks=ctx_len // GLA_CHUNK)
    return pl.pallas_call(
        kern,
        grid=(B,),
        in_specs=[col(COL_GQ), col(COL_GK), col(COL_GV), col(gate_col),
                  pl.BlockSpec((None, T, LANES), lambda b: (b, 0, COL_LR // LANES)),
                  pl.BlockSpec((LANES, GLA_W), lambda b: (0, 0)),
                  pl.BlockSpec((1, GLA_W), lambda b: (0, 0)),
                  pl.BlockSpec((1, GLA_W), lambda b: (0, 0))],
        out_specs=pl.BlockSpec((None, T, GLA_W), lambda b: (b, 0, 0)),
        out_shape=jax.ShapeDtypeStruct((B, T, GLA_W), BF16),
        scratch_shapes=[pltpu.VMEM((T, GLA_W), F32),
                        pltpu.VMEM((T, GLA_W), BF16),
                        pltpu.VMEM((n_chunks, GLA_CHUNK, GLA_W), F32),
                        pltpu.VMEM((n_chunks, 1, GLA_W), F32),
                        pltpu.VMEM((n_chunks, GLA_CHUNK, GLA_W), BF16)],
        compiler_params=_params(("parallel",)),
    )(proj, proj, proj, proj, proj, w2pad, bias, ng)


MXU_TILE = 256
FFN_CHUNK = 3 * MXU_TILE


def _swiglu_resident(h, wgu_ref, wd_ref):
    F = wd_ref.shape[0]
    y = None
    for c0 in range(0, F, FFN_CHUNK):
        c1 = min(c0 + FFN_CHUNK, F)
        g = jnp.dot(h, wgu_ref[:, c0:c1], preferred_element_type=F32)
        u = jnp.dot(h, wgu_ref[:, F + c0:F + c1], preferred_element_type=F32)
        part = jnp.dot((_silu(g) * u).astype(BF16), wd_ref[c0:c1, :], preferred_element_type=F32)
        y = part if y is None else y + part
    return y


def _outproj_kernel(*refs, tm, ctx_len, moe):
    if moe:
        (x_ref, att_ref, gf_ref, gb_ref, cv_ref, w_ref, g1_ref, cg1_ref, ng_ref, sc_ref, csc_ref, sh_ref, csh_ref,
         router_ref, earlier_ref, xo_ref, hp_ref, route_ref, cnt_ref) = refs
    else:
        (x_ref, att_ref, gf_ref, gb_ref, cv_ref, w_ref, g1_ref, cg1_ref, ng_ref, sc_ref, csc_ref, sh_ref, csh_ref,
         g2_ref, cg2_ref, wgu_ref, wd_ref, xo_ref) = refs
    is_ctx = _row_is_ctx(pl.program_id(1), tm, ctx_len)
    gla = (gf_ref[...].astype(F32) + gb_ref[...].astype(F32)).astype(BF16)
    y = jnp.dot(att_ref[...], w_ref[:ATT_W, :], preferred_element_type=F32)
    y += jnp.dot(gla, w_ref[ATT_W:ATT_W + GLA_W, :], preferred_element_type=F32)
    y += jnp.dot(cv_ref[...], w_ref[ATT_W + GLA_W:, :], preferred_element_type=F32)
    x = x_ref[...] + jnp.where(is_ctx, cg1_ref[...], g1_ref[...]) * y
    if moe:
        xo_ref[...] = x
    n = x * lax.rsqrt(jnp.mean(x * x, axis=-1, keepdims=True) + EPS) * ng_ref[...]
    scale = jnp.where(is_ctx, csc_ref[...], sc_ref[...])
    shift = jnp.where(is_ctx, csh_ref[...], sh_ref[...])
    h = n * (1.0 + scale) + shift
    hi = h.astype(BF16)
    if not moe:
        xo_ref[...] = x + jnp.where(is_ctx, cg2_ref[...], g2_ref[...]) * _swiglu_resident(hi, wgu_ref, wd_ref)
    else:
        _store_packed(hp_ref, _pack_bf16_pairs(h))
        logits = jnp.dot(hi, router_ref[...], preferred_element_type=F32)
        lane = lax.broadcasted_iota(jnp.int32, logits.shape, 1)
        neg = -jnp.inf
        lg = jnp.where(lane < N_EXPERTS, logits, neg)
        m1 = jnp.max(lg, axis=-1, keepdims=True)
        i1 = jnp.min(jnp.where(lg == m1, lane, LANES), axis=-1, keepdims=True)
        lg2 = jnp.where(lane == i1, neg, lg)
        m2 = jnp.max(lg2, axis=-1, keepdims=True)
        i2 = jnp.min(jnp.where(lg2 == m2, lane, LANES), axis=-1, keepdims=True)
        e2 = jnp.exp(m2 - m1)
        gate1 = 1.0 / (1.0 + e2)
        gate2 = e2 / (1.0 + e2)
        first = jnp.where(lane == i1, 1.0, 0.0)
        second = jnp.where(lane == i2, 1.0, 0.0)
        before = jnp.dot(earlier_ref[...], jnp.concatenate([first, second], axis=1).astype(BF16),
                         preferred_element_type=F32)
        n_first = jnp.sum(first, axis=0, keepdims=True)
        rank1 = jnp.sum(first * before[:, :LANES], axis=-1, keepdims=True)
        rank2 = jnp.sum(second * (before[:, LANES:] + n_first), axis=-1, keepdims=True)
        cnt_ref[...] = jnp.broadcast_to(n_first + jnp.sum(second, axis=0, keepdims=True), cnt_ref.shape)
        route_ref[...] = jnp.where(lane == 0, i1.astype(F32), jnp.where(lane == 1, i2.astype(F32),
                                   jnp.where(lane == 2, gate1, jnp.where(lane == 3, gate2,
                                             jnp.where(lane == 4, rank1, jnp.where(lane == 5, rank2, 0.0))))))


def _outproj(xt, att, gf, gb, cv, w_out, layer, ng, mod_l, mod_c, ctx_len, router=None, ffn=None):
    B, T, D = xt.shape
    tm = _pick_tile(T, 640)
    moe = router is not None
    assert moe != (ffn is not None)
    row = lambda w: pl.BlockSpec((None, tm, w), lambda b, j: (b, j, 0))
    const = lambda shape: pl.BlockSpec(shape, lambda b, j: (0,) * len(shape))
    in_specs = [row(D), row(ATT_W), row(GLA_W), row(GLA_W), row(CONV_CH),
                pl.BlockSpec((None, D, D), lambda b, j: (layer, 0, 0)),
                *_mod_specs(2, D), const((1, D)), *_mod_specs(4, D), *_mod_specs(3, D)]
    args = [xt, att, gf, gb, cv, w_out, mod_l, mod_c, ng, mod_l, mod_c, mod_l, mod_c]
    nt = T // tm
    if moe:
        in_specs += [pl.BlockSpec((None, D, LANES), lambda b, j: (router[1], 0, 0)), const((tm, tm))]
        args += [router[0], jnp.tri(tm, k=-1, dtype=BF16)]
        out_specs = [row(D), pl.BlockSpec((PACK_CHUNKS, tm, LANES), lambda b, j: (0, b * nt + j, 0)), row(LANES),
                     pl.BlockSpec((None, 8, LANES), lambda b, j: (b * nt + j, 0, 0))]
        out_shape = [jax.ShapeDtypeStruct((B, T, D), F32), jax.ShapeDtypeStruct((PACK_CHUNKS, B * T, LANES), jnp.int32),
                     jax.ShapeDtypeStruct((B, T, LANES), F32), jax.ShapeDtypeStruct((B * nt, 8, LANES), F32)]
    else:
        w_gu, w_down, idx = ffn
        F = w_down.shape[1]
        assert F % MXU_TILE == 0
        resident = lambda shape: pl.BlockSpec((None,) + shape, lambda b, j: (idx, 0, 0), pipeline_mode=pl.Buffered(1))
        in_specs += [*_mod_specs(5, D), resident((D, 2 * F)), resident((F, D))]
        args += [mod_l, mod_c, w_gu, w_down]
        out_specs = row(D)
        out_shape = jax.ShapeDtypeStruct((B, T, D), F32)
    return pl.pallas_call(
        functools.partial(_outproj_kernel, tm=tm, ctx_len=ctx_len, moe=moe),
        grid=(B, T // tm),
        in_specs=in_specs, out_specs=out_specs, out_shape=out_shape,
        compiler_params=_params(("parallel", "parallel")),
    )(*args)


def _sc_gather(table, idx):
    _, width = table.shape
    m = idx.shape[0]
    workers = SC_CORES * SC_SUBCORES
    ch = SC_INDEX_CHUNK
    assert width == LANES and m % (workers * ch) == 0
    n_chunks = m // (workers * ch)
    nb = max(b for b in (4, 3, 2, 1) if n_chunks % b == 0)
    mesh = plsc.VectorSubcoreMesh(core_axis_name="c", subcore_axis_name="s",
                                  num_cores=SC_CORES, num_subcores=SC_SUBCORES)

    @functools.partial(
        pl.kernel, mesh=mesh, out_type=jax.ShapeDtypeStruct((m, width), table.dtype),
        scratch_types=[pltpu.VMEM((n_chunks, ch), jnp.int32), pltpu.VMEM((nb, ch, width), table.dtype)]
        + [pltpu.SemaphoreType.DMA] * (2 * nb))
    def gather(table_hbm, idx_hbm, out_hbm, idx_v, rows_v, *sems):
        wid = lax.axis_index("s") * SC_CORES + lax.axis_index("c")
        base = wid * (n_chunks * ch)
        pltpu.sync_copy(idx_hbm.at[wid], idx_v)

        def fetch(c, b):
            return pltpu.make_async_copy(table_hbm.at[idx_v.at[c]], rows_v.at[b], sems[b])

        def flush(c, b):
            return pltpu.make_async_copy(rows_v.at[b], out_hbm.at[pl.ds(base + c * ch, ch)], sems[nb + b])

        @pl.loop(0, n_chunks, step=nb)
        def _(c0):
            for b in range(nb):
                @pl.when(c0 > 0)
                def _():
                    flush(c0 - nb + b, b).wait()
                fetch(c0 + b, b).start()
            for b in range(nb):
                fetch(c0 + b, b).wait()
                flush(c0 + b, b).start()

        for b in range(nb):
            flush(n_chunks - nb + b, b).wait()

    return gather(table, idx.reshape(workers, n_chunks, ch))


def _sc_dispatch(table, dest_a, dest_b, dest_pad, n_out):
    m, width = table.shape
    n_pad = dest_pad.shape[0]
    workers = SC_CORES * SC_SUBCORES
    ch = SC_INDEX_CHUNK
    assert width == LANES and m % (workers * ch) == 0 and n_pad % (workers * ch) == 0
    n_chunks = m // (workers * ch)
    pad_chunks = n_pad // (workers * ch)
    nb = max(b for b in (4, 3, 2, 1) if n_chunks % b == 0)
    mesh = plsc.VectorSubcoreMesh(core_axis_name="c", subcore_axis_name="s",
                                  num_cores=SC_CORES, num_subcores=SC_SUBCORES)

    @functools.partial(
        pl.kernel, mesh=mesh, out_type=jax.ShapeDtypeStruct((n_out, width), table.dtype),
        scratch_types=[pltpu.VMEM((n_chunks, ch), jnp.int32), pltpu.VMEM((n_chunks, ch), jnp.int32),
                       pltpu.VMEM((pad_chunks, ch), jnp.int32), pltpu.VMEM((nb, ch, width), table.dtype),
                       pltpu.VMEM((ch, width), table.dtype)]
        + [pltpu.SemaphoreType.DMA] * (3 * nb + 1))
    def dispatch(table_hbm, zeros_hbm, da_hbm, db_hbm, dp_hbm, out_hbm, da_v, db_v, dp_v, rows_v, zero_v, *sems):
        wid = lax.axis_index("s") * SC_CORES + lax.axis_index("c")
        base = wid * (n_chunks * ch)
        pltpu.sync_copy(da_hbm.at[wid], da_v)
        pltpu.sync_copy(db_hbm.at[wid], db_v)
        pltpu.sync_copy(dp_hbm.at[wid], dp_v)
        pltpu.sync_copy(zeros_hbm, zero_v)

        def fetch(c, b):
            return pltpu.make_async_copy(table_hbm.at[pl.ds(base + c * ch, ch)], rows_v.at[b], sems[b])

        def put_a(c, b):
            return pltpu.make_async_copy(rows_v.at[b], out_hbm.at[da_v.at[c]], sems[nb + b])

        def put_b(c, b):
            return pltpu.make_async_copy(rows_v.at[b], out_hbm.at[db_v.at[c]], sems[2 * nb + b])

        def put_zero(p):
            return pltpu.make_async_copy(zero_v, out_hbm.at[dp_v.at[p]], sems[3 * nb])

        for p in range(pad_chunks):
            put_zero(p).start()

        @pl.loop(0, n_chunks, step=nb)
        def _(c0):
            for b in range(nb):
                @pl.when(c0 > 0)
                def _():
                    put_a(c0 - nb + b, b).wait()
                    put_b(c0 - nb + b, b).wait()
                fetch(c0 + b, b).start()
            for b in range(nb):
                fetch(c0 + b, b).wait()
                put_a(c0 + b, b).start()
                put_b(c0 + b, b).start()

        for b in range(nb):
            put_a(n_chunks - nb + b, b).wait()
            put_b(n_chunks - nb + b, b).wait()
        for p in range(pad_chunks):
            put_zero(p).wait()

    shape3 = lambda v: v.reshape(workers, -1, ch)
    return dispatch(table, jnp.zeros((ch, width), table.dtype), shape3(dest_a), shape3(dest_b), shape3(dest_pad))


def _routing(route, tile_counts, tile):
    B, T, _ = route.shape
    n = B * T
    assert (2 * n) % tile == 0
    info = route.reshape(n, LANES)[:, :6].astype(jnp.int32)
    per_tile = tile_counts[:, 0, :N_EXPERTS].astype(jnp.int32)
    counts = jnp.sum(per_tile, axis=0)
    padded = (counts + tile - 1) // tile * tile
    ends = jnp.cumsum(padded)
    starts = ends - padded
    base = jnp.repeat(starts[None, :] + jnp.cumsum(per_tile, axis=0) - per_tile, n // per_tile.shape[0], axis=0)
    experts = jnp.arange(N_EXPERTS, dtype=jnp.int32)[None, :]
    row_of = lambda e, rank: jnp.sum(jnp.where(e[:, None] == experts, base, 0), axis=1) + rank
    dest = jnp.concatenate([row_of(info[:, 0], info[:, 4]), row_of(info[:, 1], info[:, 5])])
    n_pad = N_EXPERTS * tile
    n_rows = 2 * n + n_pad
    seg_first = jnp.concatenate([starts + counts, ends[-1:]])
    seg_size = jnp.concatenate([padded - counts, (n_rows - ends[-1])[None]])
    seg_end = jnp.cumsum(seg_size)
    j = jnp.arange(n_pad, dtype=jnp.int32)
    seg = jnp.sum((j[:, None] >= seg_end[None, :]).astype(jnp.int32), axis=1)
    pad_rows = (seg_first[seg] + j - (seg_end - seg_size)[seg]).astype(jnp.int32)
    tile_start = jnp.arange(n_rows // tile, dtype=jnp.int32) * tile
    tile_expert = jnp.minimum(jnp.sum((tile_start[:, None] >= ends[None, :]).astype(jnp.int32), axis=1), N_EXPERTS - 1)
    tile_valid = (tile_start < ends[-1]).astype(jnp.int32)
    return dest, pad_rows, n_rows, tile_expert, tile_valid


def _expert_kernel(te_ref, tv_ref, xs_ref, wgu_ref, wd_ref, ys_ref, wgu_scr):
    i = pl.program_id(0)
    F = wd_ref.shape[0]

    @pl.when((i == 0) | (te_ref[i] != te_ref[jnp.maximum(i - 1, 0)]))
    def _new_expert():
        wgu_scr[...] = wgu_ref[...].astype(BF16)

    @pl.when(tv_ref[i] > 0)
    def _compute():
        lo, hi = _unpack_bf16_pairs(_load_packed(xs_ref))
        gu = (jnp.dot(lo.astype(BF16), wgu_scr[:PACK_W, :], preferred_element_type=F32)
              + jnp.dot(hi.astype(BF16), wgu_scr[PACK_W:, :], preferred_element_type=F32))
        act = (_silu(gu[:, :F]) * gu[:, F:]).astype(BF16)
        y = jnp.dot(act, wd_ref[...], preferred_element_type=F32)
        _store_packed(ys_ref, _pack_bf16_pairs(y))

    @pl.when(tv_ref[i] == 0)
    def _unused_tile():
        ys_ref[...] = jnp.zeros_like(ys_ref)


def _experts(xs, tile_expert, tile_valid, w_gu, layer, w_down):
    _, n_rows, _ = xs.shape
    _, _, F, D = w_down.shape
    tile = EXPERT_TILE
    rows = pl.BlockSpec((PACK_CHUNKS, tile, LANES), lambda i, te, tv: (0, i, 0))
    return pl.pallas_call(
        _expert_kernel,
        grid_spec=pltpu.PrefetchScalarGridSpec(
            num_scalar_prefetch=2,
            grid=(n_rows // tile,),
            in_specs=[rows,
                      pl.BlockSpec((None, None, D, 2 * F), lambda i, te, tv: (layer, te[i], 0, 0)),
                      pl.BlockSpec((None, None, F, D), lambda i, te, tv: (0, te[i], 0, 0))],
            out_specs=rows,
            scratch_shapes=[pltpu.VMEM((D, 2 * F), BF16)]),
        out_shape=jax.ShapeDtypeStruct(xs.shape, jnp.int32),
        compiler_params=_params(("arbitrary",)),
    )(tile_expert, tile_valid, xs, w_gu, w_down)


def _combine_kernel(*refs, tm, ctx_len, skip, final):
    x_ref, g2_ref, cg2_ref, route_ref, ya_ref, yb_ref = refs[:6]
    fg_ref = refs[6] if final else None
    o_ref = refs[-1]
    is_ctx = _row_is_ctx(pl.program_id(1) + skip, tm, ctx_len)
    route = route_ref[...]
    lane = lax.broadcasted_iota(jnp.int32, route.shape, 1)
    gate_a = jnp.sum(jnp.where(lane == 2, route, 0.0), axis=-1, keepdims=True)
    gate_b = jnp.sum(jnp.where(lane == 3, route, 0.0), axis=-1, keepdims=True)
    unpack = lambda ref: jnp.concatenate(_unpack_bf16_pairs(_load_packed(ref)), axis=1)
    f = gate_a * unpack(ya_ref) + gate_b * unpack(yb_ref)
    x = x_ref[...] + jnp.where(is_ctx, cg2_ref[...], g2_ref[...]) * f
    if final:
        x = x * lax.rsqrt(jnp.mean(x * x, axis=-1, keepdims=True) + EPS) * fg_ref[...]
    o_ref[...] = x


def _combine(xt, mod_l, mod_c, route, yg, ctx_len, final_g=None):
    B, T, D = xt.shape
    final = final_g is not None
    if final:
        tm = SEQ_TILE
        assert ctx_len % tm == 0 and T % tm == 0
        skip = ctx_len // tm
    else:
        tm = _pick_tile(T, 640)
        skip = 0
    nt = T // tm
    row = lambda w: pl.BlockSpec((None, tm, w), lambda b, j: (b, j + skip, 0))
    slot = lambda s: pl.BlockSpec((PACK_CHUNKS, None, tm, LANES), lambda b, j: (0, s, b * nt + j + skip, 0))
    in_specs = [row(D), *_mod_specs(5, D), row(LANES), slot(0), slot(1)]
    args = [xt, mod_l, mod_c, route, yg, yg]
    if final:
        in_specs.append(pl.BlockSpec((1, D), lambda b, j: (0, 0)))
        args.append(final_g)
    return pl.pallas_call(
        functools.partial(_combine_kernel, tm=tm, ctx_len=ctx_len, skip=skip, final=final),
        grid=(B, nt - skip),
        in_specs=in_specs,
        out_specs=pl.BlockSpec((None, tm, D), lambda b, j: (b, j, 0)),
        out_shape=jax.ShapeDtypeStruct((B, T - skip * tm, D), F32),
        compiler_params=_params(("parallel", "parallel")),
    )(*args)


def _moe(hp, route, tile_counts, xt, mod_l, mod_c, w_gu, layer, w_down, ctx_len, final_g=None):
    B, T, _ = xt.shape
    n = B * T
    dest, pad_rows, n_rows, tile_expert, tile_valid = _routing(route, tile_counts, EXPERT_TILE)
    chunk = jnp.arange(PACK_CHUNKS, dtype=jnp.int32)[:, None] * n_rows
    xs = _sc_dispatch(hp.reshape(PACK_CHUNKS * n, LANES), (chunk + dest[None, :n]).reshape(-1),
                      (chunk + dest[None, n:]).reshape(-1), (chunk + pad_rows[None, :]).reshape(-1),
                      PACK_CHUNKS * n_rows)
    ys = _experts(xs.reshape(PACK_CHUNKS, n_rows, LANES), tile_expert, tile_valid, w_gu, layer, w_down)
    ys = ys.reshape(PACK_CHUNKS * n_rows, LANES)
    yg = _sc_gather(ys, (chunk + dest[None, :]).reshape(-1)).reshape(PACK_CHUNKS, 2, n, LANES)
    return _combine(xt, mod_l, mod_c, route, yg, ctx_len, final_g)


def _final_norm_kernel(x_ref, g_ref, o_ref):
    x = x_ref[...]
    o_ref[...] = x * lax.rsqrt(jnp.mean(x * x, axis=-1, keepdims=True) + EPS) * g_ref[...]


def _final_norm(xt, g, ctx_len):
    B, T, D = xt.shape
    S = T - ctx_len
    tm = SEQ_TILE
    assert ctx_len % tm == 0 and S % tm == 0
    skip = ctx_len // tm
    return pl.pallas_call(
        _final_norm_kernel,
        grid=(B, S // tm),
        in_specs=[pl.BlockSpec((None, tm, D), lambda b, j: (b, j + skip, 0)),
                  pl.BlockSpec((1, D), lambda b, j: (0, 0))],
        out_specs=pl.BlockSpec((None, tm, D), lambda b, j: (b, j, 0)),
        out_shape=jax.ShapeDtypeStruct((B, S, D), F32),
        compiler_params=_params(("parallel", "parallel")),
    )(xt, g)


def _rope_tables(seq, ctx_len):
    pos = jnp.arange(seq, dtype=jnp.int32)
    nf = HEAD_DIM // 4
    inv = ROPE_THETA ** (-jnp.arange(nf, dtype=F32) / nf)
    ang = jnp.concatenate([(pos // GRID_W).astype(F32)[:, None] * inv, (pos % GRID_W).astype(F32)[:, None] * inv], axis=-1)
    cos, sin = jnp.cos(ang), jnp.sin(ang)
    reps = LANES // HEAD_DIM
    cos_t = jnp.tile(jnp.concatenate([cos, cos], axis=-1), (1, reps))
    sin_t = jnp.tile(jnp.concatenate([-sin, sin], axis=-1), (1, reps))
    cos_t = jnp.concatenate([jnp.ones((ctx_len, LANES), F32), cos_t], axis=0)
    sin_t = jnp.concatenate([jnp.zeros((ctx_len, LANES), F32), sin_t], axis=0)
    return cos_t, sin_t


def _relayout_w_in(w_in):
    o_lr = ATT_W + 2 * KV_W + 5 * GLA_W
    q_scale = HEAD_DIM ** -0.5 * LOG2E
    gq_scale = GLA_DK ** -0.5
    parts = [w_in[..., :ATT_W] * q_scale, w_in[..., ATT_W:COL_GQ], w_in[..., COL_GQ:COL_GK] * gq_scale,
             w_in[..., COL_GK:o_lr], w_in[..., o_lr + 2 * GLA_LOWRANK:], w_in[..., o_lr:o_lr + 2 * GLA_LOWRANK],
             jnp.zeros(w_in.shape[:-1] + (LANES - 2 * GLA_LOWRANK,), w_in.dtype)]
    return jnp.concatenate(parts, axis=-1).astype(BF16)


def kernel(x, c, ctx, c_ctx, w_mod, b_mod, norm1_g, norm2_g, w_in, w_out, attn_sink, gla_w2_f, gla_b_f, gla_w2_b, gla_b_b, gla_norm_g, conv_w, ffn_w_gu, ffn_w_down, router_w, expert_w_gu, expert_w_down, final_norm_g):
    B, S, D = x.shape
    L = ctx.shape[1]
    depth = w_in.shape[0]
    assert D == D_MODEL and S % GRID_W == 0

    xt = jnp.concatenate([ctx, x], axis=1)
    rows = 16
    cc = jnp.concatenate([c, c_ctx[None, :], jnp.zeros((rows - B - 1, D), F32)], axis=0)
    mod = _adaln(cc, w_mod, b_mod)
    cos_t, sin_t = _rope_tables(S, L)

    w_in_p = _relayout_w_in(w_in)
    w_out_b = _to_bf16(w_out)
    zpad = jnp.zeros((depth, LANES - 2 * GLA_LOWRANK, GLA_W), F32)
    w2_f = jnp.concatenate([gla_w2_f, jnp.zeros_like(gla_w2_b), zpad], axis=1).astype(BF16)
    w2_b = jnp.concatenate([jnp.zeros_like(gla_w2_f), gla_w2_b, zpad], axis=1).astype(BF16)
    gla_ng = jnp.tile(gla_norm_g, (1, GLA_HEADS))
    dense_w = {0: (_to_bf16(ffn_w_gu, lead=0), _to_bf16(ffn_w_down, lead=0))}
    router_b = jnp.pad(router_w, ((0, 0), (0, 0), (0, LANES - N_EXPERTS))).astype(BF16)

    for l in range(depth):
        mod_l = mod[l, :B].reshape(B, 1, 6 * D)
        mod_c = mod[l, B:B + 1]
        proj = _inproj(xt, norm1_g[l][None, :], mod_l, mod_c, cos_t, sin_t, w_in_p, l, L)
        att = _attention(proj, attn_sink[l], L)
        cv = _short_conv(proj, conv_w[l], L)
        gf = _gla(proj, w2_f[l], gla_b_f[l][None, :], gla_ng[l][None, :], L, reverse=False)
        gb = _gla(proj, w2_b[l], gla_b_b[l][None, :], gla_ng[l][None, :], L, reverse=True)
        if l % 2 == 0:
            xt = _outproj(xt, att, gf, gb, cv, w_out_b, l, norm2_g[l][None, :], mod_l, mod_c, L,
                          ffn=(*dense_w.pop(l), 0))
        else:
            xt, hp, route, tile_counts = _outproj(xt, att, gf, gb, cv, w_out_b, l, norm2_g[l][None, :], mod_l, mod_c, L,
                                     router=(router_b, l // 2))
            exp_down_b = _to_bf16(expert_w_down, lead=l // 2, after=route)
            if l + 1 < depth:
                dense_w[l + 1] = (_to_bf16(ffn_w_gu, lead=(l + 1) // 2, after=route),
                                  _to_bf16(ffn_w_down, lead=(l + 1) // 2, after=route))
            xt = _moe(hp, route, tile_counts, xt, mod_l, mod_c, expert_w_gu, l // 2, exp_down_b, L,
                      final_g=final_norm_g[None, :] if l == depth - 1 else None)
    return xt if depth % 2 == 0 else _final_norm(xt, final_norm_g[None, :], L)
```

```python
import functools

import jax
import jax.numpy as jnp
from jax import lax
from jax.experimental import pallas as pl
from jax.experimental.pallas import tpu as pltpu
from jax.experimental.pallas import tpu_sc as plsc

F32 = jnp.float32
BF16 = jnp.bfloat16

D_MODEL = 1024
EPS = 1e-6
GRID_W = 64
ROPE_THETA = 10000.0
LOG2E = 1.4426950408889634
HEAD_DIM = 64
ATT_HEADS = 8
ATT_KV_HEADS = 2
ATT_GROUP = ATT_HEADS // ATT_KV_HEADS
ATT_W = ATT_HEADS * HEAD_DIM
KV_W = ATT_KV_HEADS * HEAD_DIM
WINDOW = 128
GLA_HEADS = 4
GLA_DK = 64
GLA_W = GLA_HEADS * GLA_DK
GLA_LOWRANK = 16
GLA_TAU = 16.0
GLA_CHUNK = 64
CONV_CH = 256
N_EXPERTS = 8

LANES = 128
SUBLANES_BF16 = 16

COL_Q = 0
COL_K = ATT_W
COL_V = COL_K + KV_W
ROPE_W = COL_V
COL_GQ = COL_V + KV_W
COL_GK = COL_GQ + GLA_W
COL_GV = COL_GK + GLA_W
COL_GGF = COL_GV + GLA_W
COL_GGB = COL_GGF + GLA_W
COL_CB = COL_GGB + GLA_W
COL_CC = COL_CB + CONV_CH
COL_CX = COL_CC + CONV_CH
COL_LR = COL_CX + CONV_CH
PROJ_W = COL_LR + LANES

SEQ_TILE = 256
VMEM_LIMIT = 56 * 1024 * 1024
CAST_BLOCK_BYTES = 4 * 1024 * 1024

PACK_W = D_MODEL // 2
PACK_CHUNKS = PACK_W // LANES
EXPERT_TILE = 512
SC_CORES = 2
SC_SUBCORES = 16
SC_INDEX_CHUNK = 128


def _silu(v):
    return v / (1.0 + jnp.exp(-v))


def _pick_tile(total, cap):
    best = None
    for t in range(SUBLANES_BF16, cap + 1, SUBLANES_BF16):
        if total % t == 0:
            best = t
    assert best is not None
    return best


def _params(sem):
    return pltpu.CompilerParams(dimension_semantics=sem, vmem_limit_bytes=VMEM_LIMIT)


def _pack_bf16_pairs(v):
    bits = lambda t: lax.bitcast_convert_type(t.astype(BF16).astype(F32), jnp.int32)
    return ((bits(v[:, :PACK_W]) >> 16) & 0xFFFF) | (bits(v[:, PACK_W:]) & -65536)


def _unpack_bf16_pairs(w):
    return lax.bitcast_convert_type(w << 16, F32), lax.bitcast_convert_type(w & -65536, F32)


CAST_STREAMS = 4


def _cast_kernel(*refs):
    o_ref = refs[-1]
    slab = refs[0].shape[0]
    for s, x_ref in enumerate(refs[:-1]):
        o_ref[s * slab:(s + 1) * slab, :] = x_ref[...].astype(BF16)


def _to_bf16(w, lead=None, after=None):
    cols = w.shape[-1]
    rows_in = w.size // cols
    rows = rows_in if lead is None else rows_in // w.shape[0]
    tr = _pick_tile(rows, max(SUBLANES_BF16, CAST_BLOCK_BYTES // (4 * cols)))
    streams = CAST_STREAMS if tr % (CAST_STREAMS * SUBLANES_BF16) == 0 else 1
    slab = tr // streams
    first = 0 if lead is None else lead * (rows // slab)
    w2 = w.reshape(rows_in, cols)
    if after is not None:
        w2, _ = lax.optimization_barrier((w2, after))
    out = pl.pallas_call(
        _cast_kernel,
        grid=(rows // tr,),
        in_specs=[pl.BlockSpec((slab, cols), lambda i, s=s: (first + i * streams + s, 0)) for s in range(streams)],
        out_specs=pl.BlockSpec((tr, cols), lambda i: (i, 0)),
        out_shape=jax.ShapeDtypeStruct((rows, cols), BF16),
        compiler_params=_params(("parallel",)),
    )(*([w2] * streams))
    return out.reshape(w.shape if lead is None else (1,) + w.shape[1:])


def _load_packed(ref):
    return jnp.concatenate([ref[k] for k in range(PACK_CHUNKS)], axis=1)


def _store_packed(ref, packed):
    for k in range(PACK_CHUNKS):
        ref[k] = packed[:, k * LANES:(k + 1) * LANES]


def _adaln_kernel(c_ref, w_ref, b_ref, o_ref):
    s = _silu(c_ref[...]).astype(BF16)
    o_ref[...] = jnp.dot(s, w_ref[...].astype(BF16), preferred_element_type=F32) + b_ref[...]


def _adaln(cc, w_mod, b_mod):
    depth, d, n = w_mod.shape
    tn = 1024
    rows = cc.shape[0]
    return pl.pallas_call(
        _adaln_kernel,
        grid=(depth, n // tn),
        in_specs=[pl.BlockSpec((rows, d), lambda l, j: (0, 0)),
                  pl.BlockSpec((None, d, tn), lambda l, j: (l, 0, j)),
                  pl.BlockSpec((None, 1, tn), lambda l, j: (l, 0, j))],
        out_specs=pl.BlockSpec((None, rows, tn), lambda l, j: (l, 0, j)),
        out_shape=jax.ShapeDtypeStruct((depth, rows, n), F32),
        compiler_params=_params(("parallel", "parallel")),
    )(cc, w_mod, b_mod.reshape(depth, 1, n))


def _mod_specs(k, d):
    return [pl.BlockSpec((None, 1, d), lambda b, j, *_: (b, 0, k)),
            pl.BlockSpec((1, d), lambda b, j, *_: (0, k))]


def _row_is_ctx(j, tm, ctx_len):
    return (j * tm + lax.broadcasted_iota(jnp.int32, (tm, 1), 0)) < ctx_len


def _inproj_kernel(x_ref, g_ref, sc_ref, csc_ref, sh_ref, csh_ref, cos_ref, sin_ref, w_ref, o_ref, *, tm, ctx_len):
    x = x_ref[...]
    n = x * lax.rsqrt(jnp.mean(x * x, axis=-1, keepdims=True) + EPS) * g_ref[...]
    is_ctx = _row_is_ctx(pl.program_id(1), tm, ctx_len)
    scale = jnp.where(is_ctx, csc_ref[...], sc_ref[...])
    shift = jnp.where(is_ctx, csh_ref[...], sh_ref[...])
    h = (n * (1.0 + scale) + shift).astype(BF16)
    qk = jnp.dot(h, w_ref[:, :ROPE_W], preferred_element_type=F32)
    lower_half = (lax.broadcasted_iota(jnp.int32, (tm, LANES), 1) % HEAD_DIM) < (HEAD_DIM // 2)
    cos = cos_ref[...]
    sin = sin_ref[...]
    for i in range(ROPE_W // LANES):
        t = qk[:, i * LANES:(i + 1) * LANES]
        partner = jnp.where(lower_half, pltpu.roll(t, LANES - HEAD_DIM // 2, axis=1), pltpu.roll(t, HEAD_DIM // 2, axis=1))
        o_ref[:, i * LANES:(i + 1) * LANES] = (t * cos + partner * sin).astype(BF16)
    o_ref[:, ROPE_W:] = jnp.dot(h, w_ref[:, ROPE_W:], preferred_element_type=F32).astype(BF16)


def _inproj(xt, g, mod_l, mod_c, cos_t, sin_t, w, layer, ctx_len):
    B, T, D = xt.shape
    tm = _pick_tile(T, 640)
    kern = functools.partial(_inproj_kernel, tm=tm, ctx_len=ctx_len)
    return pl.pallas_call(
        kern,
        grid=(B, T // tm),
        in_specs=[pl.BlockSpec((None, tm, D), lambda b, j: (b, j, 0)),
                  pl.BlockSpec((1, D), lambda b, j: (0, 0)),
                  *_mod_specs(1, D), *_mod_specs(0, D),
                  pl.BlockSpec((tm, LANES), lambda b, j: (j, 0)),
                  pl.BlockSpec((tm, LANES), lambda b, j: (j, 0)),
                  pl.BlockSpec((None, D, PROJ_W), lambda b, j: (layer, 0, 0))],
        out_specs=pl.BlockSpec((None, tm, PROJ_W), lambda b, j: (b, j, 0)),
        out_shape=jax.ShapeDtypeStruct((B, T, PROJ_W), BF16),
        compiler_params=_params(("parallel", "parallel")),
    )(xt, g, mod_l, mod_c, mod_l, mod_c, cos_t, sin_t, w)


def _attn_kernel(sink_ref, q_ref, kc_ref, vc_ref, km_ref, vm_ref, kp_ref, vp_ref, kn_ref, vn_ref, o_ref, *, n_tiles):
    j = pl.program_id(1)
    nt_dims = (((1,), (1,)), ((), ()))
    tn_dims = (((0,), (0,)), ((), ()))

    def scores(rows, n_rows, g, k_ctx, k_win=None, mask_win=None):
        heads = [g * ATT_GROUP + hh for hh in range(ATT_GROUP)]
        q = jnp.concatenate([q_ref[rows, h * HEAD_DIM:(h + 1) * HEAD_DIM] for h in heads], axis=0)
        head_col = lax.broadcasted_iota(jnp.int32, (1, ATT_GROUP * n_rows), 1) // n_rows
        sink = jnp.full((1, ATT_GROUP * n_rows), sink_ref[heads[0]] * LOG2E, F32)
        for hh in range(1, ATT_GROUP):
            sink = jnp.where(head_col == hh, sink_ref[heads[hh]] * LOG2E, sink)
        s_c = lax.dot_general(k_ctx, q, nt_dims, preferred_element_type=F32)
        s_w = None
        if k_win is not None:
            s_w = jnp.where(mask_win, lax.dot_general(k_win, q, nt_dims, preferred_element_type=F32), -1e30)
        return sink, s_c, s_w

    def finish(item, v_ctx, v_win=None):
        sink, s_c, s_w = item
        m = jnp.maximum(jnp.max(s_c, axis=0, keepdims=True), sink)
        if s_w is not None:
            m = jnp.maximum(m, jnp.max(s_w, axis=0, keepdims=True))
        p_c = jnp.exp2(s_c - m)
        den = jnp.sum(p_c, axis=0, keepdims=True) + jnp.exp2(sink - m)
        o = lax.dot_general(v_ctx, p_c.astype(BF16), tn_dims, preferred_element_type=F32)
        if s_w is not None:
            p_w = jnp.exp2(s_w - m)
            den += jnp.sum(p_w, axis=0, keepdims=True)
            o += lax.dot_general(v_win, p_w.astype(BF16), tn_dims, preferred_element_type=F32)
        return o / den

    def store(rows, n_rows, outs):
        t = jnp.transpose(jnp.concatenate(outs, axis=0))
        for g in range(ATT_KV_HEADS):
            for hh in range(ATT_GROUP):
                h = g * ATT_GROUP + hh
                o_ref[rows, h * HEAD_DIM:(h + 1) * HEAD_DIM] = (
                    t[hh * n_rows:(hh + 1) * n_rows, g * HEAD_DIM:(g + 1) * HEAD_DIM].astype(BF16))

    def run(blocks):
        flat = [(bi, s, f) for bi, (_, _, work) in enumerate(blocks) for s, f in work]
        outs = [[] for _ in blocks]
        pending = None
        for bi, score_fn, finish_fn in flat + [(None, None, None)]:
            item = None if score_fn is None else score_fn()
            if pending is not None:
                pbi, pitem, pfinish = pending
                outs[pbi].append(pfinish(pitem))
                if len(outs[pbi]) == ATT_KV_HEADS:
                    store(blocks[pbi][0], blocks[pbi][1], outs[pbi])
            pending = None if score_fn is None else (bi, item, finish_fn)

    @pl.when(j == 0)
    def _context_queries():
        work = []
        for g in range(ATT_KV_HEADS):
            gs = slice(g * HEAD_DIM, (g + 1) * HEAD_DIM)
            work.append((functools.partial(scores, slice(0, SEQ_TILE), SEQ_TILE, g, kc_ref[:, gs]),
                         functools.partial(finish, v_ctx=vc_ref[:, gs])))
        run([(slice(0, SEQ_TILE), SEQ_TILE, work)])

    @pl.when(j > 0)
    def _latent_queries():
        w = lax.broadcasted_iota(jnp.int32, (3 * WINDOW, ATT_GROUP * WINDOW), 0)
        r = lax.broadcasted_iota(jnp.int32, (3 * WINDOW, ATT_GROUP * WINDOW), 1) % WINDOW
        band = (w >= r) & (w <= r + 2 * WINDOW)
        blocks = []
        for sub in range(SEQ_TILE // WINDOW):
            rows = slice(sub * WINDOW, (sub + 1) * WINDOW)
            if sub == 0:
                kprev, vprev = kp_ref[...], vp_ref[...]
                kmid, vmid = km_ref[:WINDOW], vm_ref[:WINDOW]
                knext, vnext = km_ref[WINDOW:], vm_ref[WINDOW:]
                w_lo = jnp.where(j > 1, 0, WINDOW)
                w_hi = 3 * WINDOW
            else:
                kprev, vprev = km_ref[:WINDOW], vm_ref[:WINDOW]
                kmid, vmid = km_ref[WINDOW:], vm_ref[WINDOW:]
                knext, vnext = kn_ref[...], vn_ref[...]
                w_lo = 0
                w_hi = jnp.where(j < n_tiles - 1, 3 * WINDOW, 2 * WINDOW)
            mask = band & (w >= w_lo) & (w < w_hi)
            work = []
            for g in range(ATT_KV_HEADS):
                gs = slice(g * HEAD_DIM, (g + 1) * HEAD_DIM)
                k_win = jnp.concatenate([kprev[:, gs], kmid[:, gs], knext[:, gs]], axis=0)
                v_win = jnp.concatenate([vprev[:, gs], vmid[:, gs], vnext[:, gs]], axis=0)
                work.append((functools.partial(scores, rows, WINDOW, g, kc_ref[:, gs], k_win, mask),
                             functools.partial(finish, v_ctx=vc_ref[:, gs], v_win=v_win)))
            blocks.append((rows, WINDOW, work))
        run(blocks)


def _attention(proj, sink, ctx_len):
    B, T, _ = proj.shape
    assert ctx_len == SEQ_TILE and T % SEQ_TILE == 0
    n_tiles = T // SEQ_TILE
    n_win = T // WINDOW
    kcol, vcol = COL_K // KV_W, COL_V // KV_W
    per_tile = SEQ_TILE // WINDOW
    tile = lambda col: pl.BlockSpec((None, SEQ_TILE, KV_W), lambda b, j: (b, j, col))
    ctx = lambda col: pl.BlockSpec((None, SEQ_TILE, KV_W), lambda b, j: (b, 0, col))
    prev = lambda col: pl.BlockSpec((None, WINDOW, KV_W), lambda b, j: (b, jnp.maximum(j * per_tile - 1, 0), col))
    nxt = lambda col: pl.BlockSpec((None, WINDOW, KV_W), lambda b, j: (b, jnp.minimum((j + 1) * per_tile, n_win - 1), col))
    return pl.pallas_call(
        functools.partial(_attn_kernel, n_tiles=n_tiles),
        grid=(B, n_tiles),
        in_specs=[pl.BlockSpec(memory_space=pltpu.SMEM),
                  pl.BlockSpec((None, SEQ_TILE, ATT_W), lambda b, j: (b, j, 0)),
                  ctx(kcol), ctx(vcol), tile(kcol), tile(vcol), prev(kcol), prev(vcol), nxt(kcol), nxt(vcol)],
        out_specs=pl.BlockSpec((None, SEQ_TILE, ATT_W), lambda b, j: (b, j, 0)),
        out_shape=jax.ShapeDtypeStruct((B, T, ATT_W), BF16),
        compiler_params=_params(("parallel", "parallel")),
    )(sink, *([proj] * 9))


def _conv_kernel(b_ref, c_ref, x_ref, w_ref, o_ref, u_scr, *, ctx_len):
    T = c_ref.shape[0]
    pad = 8
    u = c_ref[...].astype(F32) * x_ref[...].astype(F32)
    zero_row = jnp.zeros((1, CONV_CH), F32)
    u_scr[pad - 1:pad, :] = zero_row
    u_scr[pad:pad + T, :] = u
    u_scr[pad + T:pad + T + 1, :] = zero_row
    row = lax.broadcasted_iota(jnp.int32, (T, 1), 0)
    u_prev = jnp.where(row == ctx_len, 0.0, u_scr[pad - 1:pad - 1 + T, :])
    u_next = jnp.where(row == ctx_len - 1, 0.0, u_scr[pad + 1:pad + 1 + T, :])
    y = w_ref[0:1, :] * u_prev + w_ref[1:2, :] * u + w_ref[2:3, :] * u_next
    o_ref[...] = (b_ref[...].astype(F32) * y).astype(BF16)


def _short_conv(proj, conv_w, ctx_len):
    B, T, _ = proj.shape
    col = lambda c0: pl.BlockSpec((None, T, CONV_CH), lambda b: (b, 0, c0 // CONV_CH))
    return pl.pallas_call(
        functools.partial(_conv_kernel, ctx_len=ctx_len),
        grid=(B,),
        in_specs=[col(COL_CB), col(COL_CC), col(COL_CX), pl.BlockSpec((3, CONV_CH), lambda b: (0, 0))],
        out_specs=pl.BlockSpec((None, T, CONV_CH), lambda b: (b, 0, 0)),
        out_shape=jax.ShapeDtypeStruct((B, T, CONV_CH), BF16),
        scratch_shapes=[pltpu.VMEM((T + 16, CONV_CH), F32)],
        compiler_params=_params(("parallel",)),
    )(proj, proj, proj, conv_w)


def _split2(v):
    hi = v.astype(BF16)
    lo = (v - hi.astype(F32)).astype(BF16)
    return hi, lo


def _gla_kernel(q_ref, k_ref, v_ref, gate_ref, lr_ref, w2_ref, bias_ref, ng_ref, o_ref,
                oacc_scr, qs_scr, u_scr, dec_scr, sin_scr, *, reverse, n_ctx_chunks):
    T = q_ref.shape[0]
    C = GLA_CHUNK
    n_tiles = T // SEQ_TILE
    n_chunks = T // C
    per_tile = SEQ_TILE // C
    nt_dims = (((1,), (1,)), ((), ()))
    tn_dims = (((0,), (0,)), ((), ()))

    r = lax.broadcasted_iota(jnp.int32, (SEQ_TILE, SEQ_TILE), 0)
    c = lax.broadcasted_iota(jnp.int32, (SEQ_TILE, SEQ_TILE), 1)
    same_chunk = (r // C) == (c // C)
    causal = same_chunk & ((c >= r) if reverse else (c <= r))
    cum_mat = jnp.where(causal, 1.0, 0.0).astype(BF16)
    same_head = (r // GLA_DK) == (c // GLA_DK)
    head_mean = jnp.where(same_head, 1.0 / GLA_DK, 0.0).astype(BF16)
    lane_head = lax.broadcasted_iota(jnp.int32, (C, GLA_W), 1) // GLA_DK

    def prepare(i):
        rows = pl.ds(pl.multiple_of(i * SEQ_TILE, SEQ_TILE), SEQ_TILE)
        z = jnp.dot(lr_ref[rows, :], w2_ref[...], preferred_element_type=F32) + bias_ref[...]
        la = (jnp.minimum(z, 0.0) - jnp.log(1.0 + jnp.exp(-jnp.abs(z)))) / GLA_TAU
        hi, lo = _split2(la)
        b = jnp.dot(cum_mat, hi, preferred_element_type=F32) + jnp.dot(cum_mat, lo, preferred_element_type=F32)
        b3 = b.reshape(per_tile, C, GLA_W)
        total = b3[:, 0:1, :] if reverse else b3[:, C - 1:C, :]
        b_last = jnp.broadcast_to(total, (per_tile, C, GLA_W)).reshape(SEQ_TILE, GLA_W)
        b_ref = 0.5 * b_last
        q = q_ref[rows, :].astype(F32)
        k = k_ref[rows, :].astype(F32)
        e_fwd = jnp.exp(b - b_ref)
        e_bwd = jnp.exp(b_ref - b)
        e_half = jnp.exp(b_ref)
        qe = (q * e_fwd).astype(BF16)
        ke = (k * e_bwd).astype(BF16)
        ku = (k * (e_bwd * e_half)).astype(BF16)
        qs_scr[rows, :] = (q * (e_fwd * e_half)).astype(BF16)
        return rows, qe, ke, ku, e_half * e_half

    def products(i, prepared):
        rows, qe, ke, ku, dec = prepared
        v = v_ref[rows, :]
        for h in range(GLA_HEADS):
            hs = slice(h * GLA_DK, (h + 1) * GLA_DK)
            s = lax.dot_general(qe[:, hs], ke[:, hs], nt_dims, preferred_element_type=F32)
            a = jnp.where(causal, s, 0.0).astype(BF16)
            oacc_scr[rows, hs] = jnp.dot(a, v[:, hs], preferred_element_type=F32)
        for ci in range(per_tile):
            cs = slice(ci * C, (ci + 1) * C)
            full = lax.dot_general(v[cs, :], ku[cs, :], tn_dims, preferred_element_type=F32)
            ut = full[0:C, :]
            for h in range(1, GLA_HEADS):
                ut = jnp.where(lane_head == h, full[h * C:(h + 1) * C, :], ut)
            u_scr[i * per_tile + ci] = ut
            dec_scr[i * per_tile + ci] = dec[ci * C:ci * C + 1, :]

    def phase1(first, count):
        pending = None
        for t in range(count + 1):
            nxt = (first + t, prepare(first + t)) if t < count else None
            if pending is not None:
                products(*pending)
            pending = nxt

    group = 6
    lax.fori_loop(0, n_tiles // group, lambda g, c: (phase1(g * group, group), c)[1], 0)
    if n_tiles % group:
        phase1((n_tiles // group) * group, n_tiles % group)


    def phase2(i, st):
        if reverse:
            ci = jnp.where(i < n_ctx_chunks, n_ctx_chunks - 1 - i, n_chunks - 1 + n_ctx_chunks - i)
        else:
            ci = i
        sin_scr[ci] = st.astype(BF16)
        return st * dec_scr[ci] + u_scr[ci]

    lax.fori_loop(0, n_chunks, phase2, jnp.zeros((C, GLA_W), F32))

    def phase3(i, carry):
        rows = pl.ds(pl.multiple_of(i * SEQ_TILE, SEQ_TILE), SEQ_TILE)
        qs = qs_scr[rows, :]
        parts = []
        for ci in range(per_tile):
            st = sin_scr[i * per_tile + ci]
            st_heads = jnp.where(same_head, jnp.concatenate([st] * GLA_HEADS, axis=0), jnp.zeros((), BF16))
            parts.append(lax.dot_general(qs[ci * C:(ci + 1) * C, :], st_heads, nt_dims, preferred_element_type=F32))
        o = oacc_scr[rows, :] + jnp.concatenate(parts, axis=0)
        ms = jnp.dot((o * o).astype(BF16), head_mean, preferred_element_type=F32)
        y = o * lax.rsqrt(ms + EPS) * ng_ref[...]
        o_ref[rows, :] = (y * _silu(gate_ref[rows, :].astype(F32))).astype(BF16)
        return carry

    lax.fori_loop(0, n_tiles, phase3, 0, unroll=4)


def _gla(proj, w2pad, bias, ng, ctx_len, reverse):
    B, T, _ = proj.shape
    assert T % SEQ_TILE == 0 and ctx_len % GLA_CHUNK == 0
    n_chunks = T // GLA_CHUNK
    col = lambda c0: pl.BlockSpec((None, T, GLA_W), lambda b: (b, 0, c0 // GLA_W))
    gate_col = COL_GGB if reverse else COL_GGF
    kern = functools.partial(_gla_kernel, reverse=reverse, n_ctx_chunks=ctx_len // GLA_CHUNK)
    return pl.pallas_call(
        kern,
        grid=(B,),
        in_specs=[col(COL_GQ), col(COL_GK), col(COL_GV), col(gate_col),
                  pl.BlockSpec((None, T, LANES), lambda b: (b, 0, COL_LR // LANES)),
                  pl.BlockSpec((LANES, GLA_W), lambda b: (0, 0)),
                  pl.BlockSpec((1, GLA_W), lambda b: (0, 0)),
                  pl.BlockSpec((1, GLA_W), lambda b: (0, 0))],
        out_specs=pl.BlockSpec((None, T, GLA_W), lambda b: (b, 0, 0)),
        out_shape=jax.ShapeDtypeStruct((B, T, GLA_W), BF16),
        scratch_shapes=[pltpu.VMEM((T, GLA_W), F32),
                        pltpu.VMEM((T, GLA_W), BF16),
                        pltpu.VMEM((n_chunks, GLA_CHUNK, GLA_W), F32),
                        pltpu.VMEM((n_chunks, 1, GLA_W), F32),
                        pltpu.VMEM((n_chunks, GLA_CHUNK, GLA_W), BF16)],
        compiler_params=_params(("parallel",)),
    )(proj, proj, proj, proj, proj, w2pad, bias, ng)


MXU_TILE = 256
FFN_CHUNK = 3 * MXU_TILE


def _swiglu_resident(h, wgu_ref, wd_ref):
    F = wd_ref.shape[0]
    y = None
    for c0 in range(0, F, FFN_CHUNK):
        c1 = min(c0 + FFN_CHUNK, F)
        g = jnp.dot(h, wgu_ref[:, c0:c1], preferred_element_type=F32)
        u = jnp.dot(h, wgu_ref[:, F + c0:F + c1], preferred_element_type=F32)
        part = jnp.dot((_silu(g) * u).astype(BF16), wd_ref[c0:c1, :], preferred_element_type=F32)
        y = part if y is None else y + part
    return y


def _outproj_kernel(*refs, tm, ctx_len, moe):
    if moe:
        (x_ref, att_ref, gf_ref, gb_ref, cv_ref, w_ref, g1_ref, cg1_ref, ng_ref, sc_ref, csc_ref, sh_ref, csh_ref,
         router_ref, earlier_ref, xo_ref, hp_ref, route_ref, cnt_ref) = refs
    else:
        (x_ref, att_ref, gf_ref, gb_ref, cv_ref, w_ref, g1_ref, cg1_ref, ng_ref, sc_ref, csc_ref, sh_ref, csh_ref,
         g2_ref, cg2_ref, wgu_ref, wd_ref, xo_ref) = refs
    is_ctx = _row_is_ctx(pl.program_id(1), tm, ctx_len)
    gla = (gf_ref[...].astype(F32) + gb_ref[...].astype(F32)).astype(BF16)
    y = jnp.dot(att_ref[...], w_ref[:ATT_W, :], preferred_element_type=F32)
    y += jnp.dot(gla, w_ref[ATT_W:ATT_W + GLA_W, :], preferred_element_type=F32)
    y += jnp.dot(cv_ref[...], w_ref[ATT_W + GLA_W:, :], preferred_element_type=F32)
    x = x_ref[...] + jnp.where(is_ctx, cg1_ref[...], g1_ref[...]) * y
    if moe:
        xo_ref[...] = x
    n = x * lax.rsqrt(jnp.mean(x * x, axis=-1, keepdims=True) + EPS) * ng_ref[...]
    scale = jnp.where(is_ctx, csc_ref[...], sc_ref[...])
    shift = jnp.where(is_ctx, csh_ref[...], sh_ref[...])
    h = n * (1.0 + scale) + shift
    hi = h.astype(BF16)
    if not moe:
        xo_ref[...] = x + jnp.where(is_ctx, cg2_ref[...], g2_ref[...]) * _swiglu_resident(hi, wgu_ref, wd_ref)
    else:
        _store_packed(hp_ref, _pack_bf16_pairs(h))
        logits = jnp.dot(hi, router_ref[...], preferred_element_type=F32)
        lane = lax.broadcasted_iota(jnp.int32, logits.shape, 1)
        neg = -jnp.inf
        lg = jnp.where(lane < N_EXPERTS, logits, neg)
        m1 = jnp.max(lg, axis=-1, keepdims=True)
        i1 = jnp.min(jnp.where(lg == m1, lane, LANES), axis=-1, keepdims=True)
        lg2 = jnp.where(lane == i1, neg, lg)
        m2 = jnp.max(lg2, axis=-1, keepdims=True)
        i2 = jnp.min(jnp.where(lg2 == m2, lane, LANES), axis=-1, keepdims=True)
        e2 = jnp.exp(m2 - m1)
        gate1 = 1.0 / (1.0 + e2)
        gate2 = e2 / (1.0 + e2)
        first = jnp.where(lane == i1, 1.0, 0.0)
        second = jnp.where(lane == i2, 1.0, 0.0)
        before = jnp.dot(earlier_ref[...], jnp.concatenate([first, second], axis=1).astype(BF16),
                         preferred_element_type=F32)
        n_first = jnp.sum(first, axis=0, keepdims=True)
        rank1 = jnp.sum(first * before[:, :LANES], axis=-1, keepdims=True)
        rank2 = jnp.sum(second * (before[:, LANES:] + n_first), axis=-1, keepdims=True)
        cnt_ref[...] = jnp.broadcast_to(n_first + jnp.sum(second, axis=0, keepdims=True), cnt_ref.shape)
        route_ref[...] = jnp.where(lane == 0, i1.astype(F32), jnp.where(lane == 1, i2.astype(F32),
                                   jnp.where(lane == 2, gate1, jnp.where(lane == 3, gate2,
                                             jnp.where(lane == 4, rank1, jnp.where(lane == 5, rank2, 0.0))))))


def _outproj(xt, att, gf, gb, cv, w_out, layer, ng, mod_l, mod_c, ctx_len, router=None, ffn=None):
    B, T, D = xt.shape
    tm = _pick_tile(T, 640)
    moe = router is not None
    assert moe != (ffn is not None)
    row = lambda w: pl.BlockSpec((None, tm, w), lambda b, j: (b, j, 0))
    const = lambda shape: pl.BlockSpec(shape, lambda b, j: (0,) * len(shape))
    in_specs = [row(D), row(ATT_W), row(GLA_W), row(GLA_W), row(CONV_CH),
                pl.BlockSpec((None, D, D), lambda b, j: (layer, 0, 0)),
                *_mod_specs(2, D), const((1, D)), *_mod_specs(4, D), *_mod_specs(3, D)]
    args = [xt, att, gf, gb, cv, w_out, mod_l, mod_c, ng, mod_l, mod_c, mod_l, mod_c]
    nt = T // tm
    if moe:
        in_specs += [pl.BlockSpec((None, D, LANES), lambda b, j: (router[1], 0, 0)), const((tm, tm))]
        args += [router[0], jnp.tri(tm, k=-1, dtype=BF16)]
        out_specs = [row(D), pl.BlockSpec((PACK_CHUNKS, tm, LANES), lambda b, j: (0, b * nt + j, 0)), row(LANES),
                     pl.BlockSpec((None, 8, LANES), lambda b, j: (b * nt + j, 0, 0))]
        out_shape = [jax.ShapeDtypeStruct((B, T, D), F32), jax.ShapeDtypeStruct((PACK_CHUNKS, B * T, LANES), jnp.int32),
                     jax.ShapeDtypeStruct((B, T, LANES), F32), jax.ShapeDtypeStruct((B * nt, 8, LANES), F32)]
    else:
        w_gu, w_down, idx = ffn
        F = w_down.shape[1]
        assert F % MXU_TILE == 0
        resident = lambda shape: pl.BlockSpec((None,) + shape, lambda b, j: (idx, 0, 0), pipeline_mode=pl.Buffered(1))
        in_specs += [*_mod_specs(5, D), resident((D, 2 * F)), resident((F, D))]
        args += [mod_l, mod_c, w_gu, w_down]
        out_specs = row(D)
        out_shape = jax.ShapeDtypeStruct((B, T, D), F32)
    return pl.pallas_call(
        functools.partial(_outproj_kernel, tm=tm, ctx_len=ctx_len, moe=moe),
        grid=(B, T // tm),
        in_specs=in_specs, out_specs=out_specs, out_shape=out_shape,
        compiler_params=_params(("parallel", "parallel")),
    )(*args)


def _sc_gather(table, idx):
    _, width = table.shape
    m = idx.shape[0]
    workers = SC_CORES * SC_SUBCORES
    ch = SC_INDEX_CHUNK
    assert width == LANES and m % (workers * ch) == 0
    n_chunks = m // (workers * ch)
    nb = max(b for b in (4, 3, 2, 1) if n_chunks % b == 0)
    mesh = plsc.VectorSubcoreMesh(core_axis_name="c", subcore_axis_name="s",
                                  num_cores=SC_CORES, num_subcores=SC_SUBCORES)

    @functools.partial(
        pl.kernel, mesh=mesh, out_type=jax.ShapeDtypeStruct((m, width), table.dtype),
        scratch_types=[pltpu.VMEM((n_chunks, ch), jnp.int32), pltpu.VMEM((nb, ch, width), table.dtype)]
        + [pltpu.SemaphoreType.DMA] * (2 * nb))
    def gather(table_hbm, idx_hbm, out_hbm, idx_v, rows_v, *sems):
        wid = lax.axis_index("s") * SC_CORES + lax.axis_index("c")
        base = wid * (n_chunks * ch)
        pltpu.sync_copy(idx_hbm.at[wid], idx_v)

        def fetch(c, b):
            return pltpu.make_async_copy(table_hbm.at[idx_v.at[c]], rows_v.at[b], sems[b])

        def flush(c, b):
            return pltpu.make_async_copy(rows_v.at[b], out_hbm.at[pl.ds(base + c * ch, ch)], sems[nb + b])

        @pl.loop(0, n_chunks, step=nb)
        def _(c0):
            for b in range(nb):
                @pl.when(c0 > 0)
                def _():
                    flush(c0 - nb + b, b).wait()
                fetch(c0 + b, b).start()
            for b in range(nb):
                fetch(c0 + b, b).wait()
                flush(c0 + b, b).start()

        for b in range(nb):
            flush(n_chunks - nb + b, b).wait()

    return gather(table, idx.reshape(workers, n_chunks, ch))


def _sc_dispatch(table, dest_a, dest_b, dest_pad, n_out):
    m, width = table.shape
    n_pad = dest_pad.shape[0]
    workers = SC_CORES * SC_SUBCORES
    ch = SC_INDEX_CHUNK
    assert width == LANES and m % (workers * ch) == 0 and n_pad % (workers * ch) == 0
    n_chunks = m // (workers * ch)
    pad_chunks = n_pad // (workers * ch)
    nb = max(b for b in (4, 3, 2, 1) if n_chunks % b == 0)
    mesh = plsc.VectorSubcoreMesh(core_axis_name="c", subcore_axis_name="s",
                                  num_cores=SC_CORES, num_subcores=SC_SUBCORES)

    @functools.partial(
        pl.kernel, mesh=mesh, out_type=jax.ShapeDtypeStruct((n_out, width), table.dtype),
        scratch_types=[pltpu.VMEM((n_chunks, ch), jnp.int32), pltpu.VMEM((n_chunks, ch), jnp.int32),
                       pltpu.VMEM((pad_chunks, ch), jnp.int32), pltpu.VMEM((nb, ch, width), table.dtype),
                       pltpu.VMEM((ch, width), table.dtype)]
        + [pltpu.SemaphoreType.DMA] * (3 * nb + 1))
    def dispatch(table_hbm, zeros_hbm, da_hbm, db_hbm, dp_hbm, out_hbm, da_v, db_v, dp_v, rows_v, zero_v, *sems):
        wid = lax.axis_index("s") * SC_CORES + lax.axis_index("c")
        base = wid * (n_chunks * ch)
        pltpu.sync_copy(da_hbm.at[wid], da_v)
        pltpu.sync_copy(db_hbm.at[wid], db_v)
        pltpu.sync_copy(dp_hbm.at[wid], dp_v)
        pltpu.sync_copy(zeros_hbm, zero_v)

        def fetch(c, b):
            return pltpu.make_async_copy(table_hbm.at[pl.ds(base + c * ch, ch)], rows_v.at[b], sems[b])

        def put_a(c, b):
            return pltpu.make_async_copy(rows_v.at[b], out_hbm.at[da_v.at[c]], sems[nb + b])

        def put_b(c, b):
            return pltpu.make_async_copy(rows_v.at[b], out_hbm.at[db_v.at[c]], sems[2 * nb + b])

        def put_zero(p):
            return pltpu.make_async_copy(zero_v, out_hbm.at[dp_v.at[p]], sems[3 * nb])

        for p in range(pad_chunks):
            put_zero(p).start()

        @pl.loop(0, n_chunks, step=nb)
        def _(c0):
            for b in range(nb):
                @pl.when(c0 > 0)
                def _():
                    put_a(c0 - nb + b, b).wait()
                    put_b(c0 - nb + b, b).wait()
                fetch(c0 + b, b).start()
            for b in range(nb):
                fetch(c0 + b, b).wait()
                put_a(c0 + b, b).start()
                put_b(c0 + b, b).start()

        for b in range(nb):
            put_a(n_chunks - nb + b, b).wait()
            put_b(n_chunks - nb + b, b).wait()
        for p in range(pad_chunks):
            put_zero(p).wait()

    shape3 = lambda v: v.reshape(workers, -1, ch)
    return dispatch(table, jnp.zeros((ch, width), table.dtype), shape3(dest_a), shape3(dest_b), shape3(dest_pad))


def _routing(route, tile_counts, tile):
    B, T, _ = route.shape
    n = B * T
    assert (2 * n) % tile == 0
    info = route.reshape(n, LANES)[:, :6].astype(jnp.int32)
    per_tile = tile_counts[:, 0, :N_EXPERTS].astype(jnp.int32)
    counts = jnp.sum(per_tile, axis=0)
    padded = (counts + tile - 1) // tile * tile
    ends = jnp.cumsum(padded)
    starts = ends - padded
    base = jnp.repeat(starts[None, :] + jnp.cumsum(per_tile, axis=0) - per_tile, n // per_tile.shape[0], axis=0)
    experts = jnp.arange(N_EXPERTS, dtype=jnp.int32)[None, :]
    row_of = lambda e, rank: jnp.sum(jnp.where(e[:, None] == experts, base, 0), axis=1) + rank
    dest = jnp.concatenate([row_of(info[:, 0], info[:, 4]), row_of(info[:, 1], info[:, 5])])
    n_pad = N_EXPERTS * tile
    n_rows = 2 * n + n_pad
    seg_first = jnp.concatenate([starts + counts, ends[-1:]])
    seg_size = jnp.concatenate([padded - counts, (n_rows - ends[-1])[None]])
    seg_end = jnp.cumsum(seg_size)
    j = jnp.arange(n_pad, dtype=jnp.int32)
    seg = jnp.sum((j[:, None] >= seg_end[None, :]).astype(jnp.int32), axis=1)
    pad_rows = (seg_first[seg] + j - (seg_end - seg_size)[seg]).astype(jnp.int32)
    tile_start = jnp.arange(n_rows // tile, dtype=jnp.int32) * tile
    tile_expert = jnp.minimum(jnp.sum((tile_start[:, None] >= ends[None, :]).astype(jnp.int32), axis=1), N_EXPERTS - 1)
    tile_valid = (tile_start < ends[-1]).astype(jnp.int32)
    return dest, pad_rows, n_rows, tile_expert, tile_valid


def _expert_kernel(te_ref, tv_ref, xs_ref, wgu_ref, wd_ref, ys_ref, wgu_scr):
    i = pl.program_id(0)
    F = wd_ref.shape[0]

    @pl.when((i == 0) | (te_ref[i] != te_ref[jnp.maximum(i - 1, 0)]))
    def _new_expert():
        wgu_scr[...] = wgu_ref[...].astype(BF16)

    @pl.when(tv_ref[i] > 0)
    def _compute():
        lo, hi = _unpack_bf16_pairs(_load_packed(xs_ref))
        gu = (jnp.dot(lo.astype(BF16), wgu_scr[:PACK_W, :], preferred_element_type=F32)
              + jnp.dot(hi.astype(BF16), wgu_scr[PACK_W:, :], preferred_element_type=F32))
        act = (_silu(gu[:, :F]) * gu[:, F:]).astype(BF16)
        y = jnp.dot(act, wd_ref[...], preferred_element_type=F32)
        _store_packed(ys_ref, _pack_bf16_pairs(y))

    @pl.when(tv_ref[i] == 0)
    def _unused_tile():
        ys_ref[...] = jnp.zeros_like(ys_ref)


def _experts(xs, tile_expert, tile_valid, w_gu, layer, w_down):
    _, n_rows, _ = xs.shape
    _, _, F, D = w_down.shape
    tile = EXPERT_TILE
    rows = pl.BlockSpec((PACK_CHUNKS, tile, LANES), lambda i, te, tv: (0, i, 0))
    return pl.pallas_call(
        _expert_kernel,
        grid_spec=pltpu.PrefetchScalarGridSpec(
            num_scalar_prefetch=2,
            grid=(n_rows // tile,),
            in_specs=[rows,
                      pl.BlockSpec((None, None, D, 2 * F), lambda i, te, tv: (layer, te[i], 0, 0)),
                      pl.BlockSpec((None, None, F, D), lambda i, te, tv: (0, te[i], 0, 0))],
            out_specs=rows,
            scratch_shapes=[pltpu.VMEM((D, 2 * F), BF16)]),
        out_shape=jax.ShapeDtypeStruct(xs.shape, jnp.int32),
        compiler_params=_params(("arbitrary",)),
    )(tile_expert, tile_valid, xs, w_gu, w_down)


def _combine_kernel(*refs, tm, ctx_len, skip, final):
    x_ref, g2_ref, cg2_ref, route_ref, ya_ref, yb_ref = refs[:6]
    fg_ref = refs[6] if final else None
    o_ref = refs[-1]
    is_ctx = _row_is_ctx(pl.program_id(1) + skip, tm, ctx_len)
    route = route_ref[...]
    lane = lax.broadcasted_iota(jnp.int32, route.shape, 1)
    gate_a = jnp.sum(jnp.where(lane == 2, route, 0.0), axis=-1, keepdims=True)
    gate_b = jnp.sum(jnp.where(lane == 3, route, 0.0), axis=-1, keepdims=True)
    unpack = lambda ref: jnp.concatenate(_unpack_bf16_pairs(_load_packed(ref)), axis=1)
    f = gate_a * unpack(ya_ref) + gate_b * unpack(yb_ref)
    x = x_ref[...] + jnp.where(is_ctx, cg2_ref[...], g2_ref[...]) * f
    if final:
        x = x * lax.rsqrt(jnp.mean(x * x, axis=-1, keepdims=True) + EPS) * fg_ref[...]
    o_ref[...] = x


def _combine(xt, mod_l, mod_c, route, yg, ctx_len, final_g=None):
    B, T, D = xt.shape
    final = final_g is not None
    if final:
        tm = SEQ_TILE
        assert ctx_len % tm == 0 and T % tm == 0
        skip = ctx_len // tm
    else:
        tm = _pick_tile(T, 640)
        skip = 0
    nt = T // tm
    row = lambda w: pl.BlockSpec((None, tm, w), lambda b, j: (b, j + skip, 0))
    slot = lambda s: pl.BlockSpec((PACK_CHUNKS, None, tm, LANES), lambda b, j: (0, s, b * nt + j + skip, 0))
    in_specs = [row(D), *_mod_specs(5, D), row(LANES), slot(0), slot(1)]
    args = [xt, mod_l, mod_c, route, yg, yg]
    if final:
        in_specs.append(pl.BlockSpec((1, D), lambda b, j: (0, 0)))
        args.append(final_g)
    return pl.pallas_call(
        functools.partial(_combine_kernel, tm=tm, ctx_len=ctx_len, skip=skip, final=final),
        grid=(B, nt - skip),
        in_specs=in_specs,
        out_specs=pl.BlockSpec((None, tm, D), lambda b, j: (b, j, 0)),
        out_shape=jax.ShapeDtypeStruct((B, T - skip * tm, D), F32),
        compiler_params=_params(("parallel", "parallel")),
    )(*args)


def _moe(hp, route, tile_counts, xt, mod_l, mod_c, w_gu, layer, w_down, ctx_len, final_g=None):
    B, T, _ = xt.shape
    n = B * T
    dest, pad_rows, n_rows, tile_expert, tile_valid = _routing(route, tile_counts, EXPERT_TILE)
    chunk = jnp.arange(PACK_CHUNKS, dtype=jnp.int32)[:, None] * n_rows
    xs = _sc_dispatch(hp.reshape(PACK_CHUNKS * n, LANES), (chunk + dest[None, :n]).reshape(-1),
                      (chunk + dest[None, n:]).reshape(-1), (chunk + pad_rows[None, :]).reshape(-1),
                      PACK_CHUNKS * n_rows)
    ys = _experts(xs.reshape(PACK_CHUNKS, n_rows, LANES), tile_expert, tile_valid, w_gu, layer, w_down)
    ys = ys.reshape(PACK_CHUNKS * n_rows, LANES)
    yg = _sc_gather(ys, (chunk + dest[None, :]).reshape(-1)).reshape(PACK_CHUNKS, 2, n, LANES)
    return _combine(xt, mod_l, mod_c, route, yg, ctx_len, final_g)


def _final_norm_kernel(x_ref, g_ref, o_ref):
    x = x_ref[...]
    o_ref[...] = x * lax.rsqrt(jnp.mean(x * x, axis=-1, keepdims=True) + EPS) * g_ref[...]


def _final_norm(xt, g, ctx_len):
    B, T, D = xt.shape
    S = T - ctx_len
    tm = SEQ_TILE
    assert ctx_len % tm == 0 and S % tm == 0
    skip = ctx_len // tm
    return pl.pallas_call(
        _final_norm_kernel,
        grid=(B, S // tm),
        in_specs=[pl.BlockSpec((None, tm, D), lambda b, j: (b, j + skip, 0)),
                  pl.BlockSpec((1, D), lambda b, j: (0, 0))],
        out_specs=pl.BlockSpec((None, tm, D), lambda b, j: (b, j, 0)),
        out_shape=jax.ShapeDtypeStruct((B, S, D), F32),
        compiler_params=_params(("parallel", "parallel")),
    )(xt, g)


def _rope_tables(seq, ctx_len):
    pos = jnp.arange(seq, dtype=jnp.int32)
    nf = HEAD_DIM // 4
    inv = ROPE_THETA ** (-jnp.arange(nf, dtype=F32) / nf)
    ang = jnp.concatenate([(pos // GRID_W).astype(F32)[:, None] * inv, (pos % GRID_W).astype(F32)[:, None] * inv], axis=-1)
    cos, sin = jnp.cos(ang), jnp.sin(ang)
    reps = LANES // HEAD_DIM
    cos_t = jnp.tile(jnp.concatenate([cos, cos], axis=-1), (1, reps))
    sin_t = jnp.tile(jnp.concatenate([-sin, sin], axis=-1), (1, reps))
    cos_t = jnp.concatenate([jnp.ones((ctx_len, LANES), F32), cos_t], axis=0)
    sin_t = jnp.concatenate([jnp.zeros((ctx_len, LANES), F32), sin_t], axis=0)
    return cos_t, sin_t


def _relayout_w_in(w_in):
    o_lr = ATT_W + 2 * KV_W + 5 * GLA_W
    q_scale = HEAD_DIM ** -0.5 * LOG2E
    gq_scale = GLA_DK ** -0.5
    parts = [w_in[..., :ATT_W] * q_scale, w_in[..., ATT_W:COL_GQ], w_in[..., COL_GQ:COL_GK] * gq_scale,
             w_in[..., COL_GK:o_lr], w_in[..., o_lr + 2 * GLA_LOWRANK:], w_in[..., o_lr:o_lr + 2 * GLA_LOWRANK],
             jnp.zeros(w_in.shape[:-1] + (LANES - 2 * GLA_LOWRANK,), w_in.dtype)]
    return jnp.concatenate(parts, axis=-1).astype(BF16)


def kernel(x, c, ctx, c_ctx, w_mod, b_mod, norm1_g, norm2_g, w_in, w_out, attn_sink, gla_w2_f, gla_b_f, gla_w2_b, gla_b_b, gla_norm_g, conv_w, ffn_w_gu, ffn_w_down, router_w, expert_w_gu, expert_w_down, final_norm_g):
    B, S, D = x.shape
    L = ctx.shape[1]
    depth = w_in.shape[0]
    assert D == D_MODEL and S % GRID_W == 0

    xt = jnp.concatenate([ctx, x], axis=1)
    rows = 16
    cc = jnp.concatenate([c, c_ctx[None, :], jnp.zeros((rows - B - 1, D), F32)], axis=0)
    mod = _adaln(cc, w_mod, b_mod)
    cos_t, sin_t = _rope_tables(S, L)

    w_in_p = _relayout_w_in(w_in)
    w_out_b = _to_bf16(w_out)
    zpad = jnp.zeros((depth, LANES - 2 * GLA_LOWRANK, GLA_W), F32)
    w2_f = jnp.concatenate([gla_w2_f, jnp.zeros_like(gla_w2_b), zpad], axis=1).astype(BF16)
    w2_b = jnp.concatenate([jnp.zeros_like(gla_w2_f), gla_w2_b, zpad], axis=1).astype(BF16)
    gla_ng = jnp.tile(gla_norm_g, (1, GLA_HEADS))
    dense_w = {0: (_to_bf16(ffn_w_gu, lead=0), _to_bf16(ffn_w_down, lead=0))}
    router_b = jnp.pad(router_w, ((0, 0), (0, 0), (0, LANES - N_EXPERTS))).astype(BF16)

    for l in range(depth):
        mod_l = mod[l, :B].reshape(B, 1, 6 * D)
        mod_c = mod[l, B:B + 1]
        proj = _inproj(xt, norm1_g[l][None, :], mod_l, mod_c, cos_t, sin_t, w_in_p, l, L)
        att = _attention(proj, attn_sink[l], L)
        cv = _short_conv(proj, conv_w[l], L)
        gf = _gla(proj, w2_f[l], gla_b_f[l][None, :], gla_ng[l][None, :], L, reverse=False)
        gb = _gla(proj, w2_b[l], gla_b_b[l][None, :], gla_ng[l][None, :], L, reverse=True)
        if l % 2 == 0:
            xt = _outproj(xt, att, gf, gb, cv, w_out_b, l, norm2_g[l][None, :], mod_l, mod_c, L,
                          ffn=(*dense_w.pop(l), 0))
        else:
            xt, hp, route, tile_counts = _outproj(xt, att, gf, gb, cv, w_out_b, l, norm2_g[l][None, :], mod_l, mod_c, L,
                                     router=(router_b, l // 2))
            exp_down_b = _to_bf16(expert_w_down, lead=l // 2, after=route)
            if l + 1 < depth:
                dense_w[l + 1] = (_to_bf16(ffn_w_gu, lead=(l + 1) // 2, after=route),
                                  _to_bf16(ffn_w_down, lead=(l + 1) // 2, after=route))
            xt = _moe(hp, route, tile_counts, xt, mod_l, mod_c, expert_w_gu, l // 2, exp_down_b, L,
                      final_g=final_norm_g[None, :] if l == depth - 1 else None)
    return xt if depth % 2 == 0 else _final_norm(xt, final_norm_g[None, :], L)
```

```python
import functools

import jax
import jax.numpy as jnp
from jax import lax
from jax.experimental import pallas as pl
from jax.experimental.pallas import tpu as pltpu
from jax.experimental.pallas import tpu_sc as plsc

F32 = jnp.float32
BF16 = jnp.bfloat16

D_MODEL = 1024
EPS = 1e-6
GRID_W = 64
ROPE_THETA = 10000.0
LOG2E = 1.4426950408889634
HEAD_DIM = 64
ATT_HEADS = 8
ATT_KV_HEADS = 2
ATT_GROUP = ATT_HEADS // ATT_KV_HEADS
ATT_W = ATT_HEADS * HEAD_DIM
KV_W = ATT_KV_HEADS * HEAD_DIM
WINDOW = 128
GLA_HEADS = 4
GLA_DK = 64
GLA_W = GLA_HEADS * GLA_DK
GLA_LOWRANK = 16
GLA_TAU = 16.0
GLA_CHUNK = 64
CONV_CH = 256
N_EXPERTS = 8

LANES = 128
SUBLANES_BF16 = 16

COL_Q = 0
COL_K = ATT_W
COL_V = COL_K + KV_W
ROPE_W = COL_V
COL_GQ = COL_V + KV_W
COL_GK = COL_GQ + GLA_W
COL_GV = COL_GK + GLA_W
COL_GGF = COL_GV + GLA_W
COL_GGB = COL_GGF + GLA_W
COL_CB = COL_GGB + GLA_W
COL_CC = COL_CB + CONV_CH
COL_CX = COL_CC + CONV_CH
COL_LR = COL_CX + CONV_CH
PROJ_W = COL_LR + LANES

SEQ_TILE = 256
VMEM_LIMIT = 56 * 1024 * 1024
CAST_BLOCK_BYTES = 4 * 1024 * 1024

PACK_W = D_MODEL // 2
PACK_CHUNKS = PACK_W // LANES
EXPERT_TILE = 512
SC_CORES = 2
SC_SUBCORES = 16
SC_INDEX_CHUNK = 128


def _silu(v):
    return v / (1.0 + jnp.exp(-v))


def _pick_tile(total, cap):
    best = None
    for t in range(SUBLANES_BF16, cap + 1, SUBLANES_BF16):
        if total % t == 0:
            best = t
    assert best is not None
    return best


def _params(sem):
    return pltpu.CompilerParams(dimension_semantics=sem, vmem_limit_bytes=VMEM_LIMIT)


def _pack_bf16_pairs(v):
    bits = lambda t: lax.bitcast_convert_type(t.astype(BF16).astype(F32), jnp.int32)
    return ((bits(v[:, :PACK_W]) >> 16) & 0xFFFF) | (bits(v[:, PACK_W:]) & -65536)


def _unpack_bf16_pairs(w):
    return lax.bitcast_convert_type(w << 16, F32), lax.bitcast_convert_type(w & -65536, F32)


CAST_STREAMS = 4


def _cast_kernel(*refs):
    o_ref = refs[-1]
    slab = refs[0].shape[0]
    for s, x_ref in enumerate(refs[:-1]):
        o_ref[s * slab:(s + 1) * slab, :] = x_ref[...].astype(BF16)


def _to_bf16(w, lead=None, after=None):
    cols = w.shape[-1]
    rows_in = w.size // cols
    rows = rows_in if lead is None else rows_in // w.shape[0]
    tr = _pick_tile(rows, max(SUBLANES_BF16, CAST_BLOCK_BYTES // (4 * cols)))
    streams = CAST_STREAMS if tr % (CAST_STREAMS * SUBLANES_BF16) == 0 else 1
    slab = tr // streams
    first = 0 if lead is None else lead * (rows // slab)
    w2 = w.reshape(rows_in, cols)
    if after is not None:
        w2, _ = lax.optimization_barrier((w2, after))
    out = pl.pallas_call(
        _cast_kernel,
        grid=(rows // tr,),
        in_specs=[pl.BlockSpec((slab, cols), lambda i, s=s: (first + i * streams + s, 0)) for s in range(streams)],
        out_specs=pl.BlockSpec((tr, cols), lambda i: (i, 0)),
        out_shape=jax.ShapeDtypeStruct((rows, cols), BF16),
        compiler_params=_params(("parallel",)),
    )(*([w2] * streams))
    return out.reshape(w.shape if lead is None else (1,) + w.shape[1:])


def _load_packed(ref):
    return jnp.concatenate([ref[k] for k in range(PACK_CHUNKS)], axis=1)


def _store_packed(ref, packed):
    for k in range(PACK_CHUNKS):
        ref[k] = packed[:, k * LANES:(k + 1) * LANES]


def _adaln_kernel(c_ref, w_ref, b_ref, o_ref):
    s = _silu(c_ref[...]).astype(BF16)
    o_ref[...] = jnp.dot(s, w_ref[...].astype(BF16), preferred_element_type=F32) + b_ref[...]


def _adaln(cc, w_mod, b_mod):
    depth, d, n = w_mod.shape
    tn = 1024
    rows = cc.shape[0]
    return pl.pallas_call(
        _adaln_kernel,
        grid=(depth, n // tn),
        in_specs=[pl.BlockSpec((rows, d), lambda l, j: (0, 0)),
                  pl.BlockSpec((None, d, tn), lambda l, j: (l, 0, j)),
                  pl.BlockSpec((None, 1, tn), lambda l, j: (l, 0, j))],
        out_specs=pl.BlockSpec((None, rows, tn), lambda l, j: (l, 0, j)),
        out_shape=jax.ShapeDtypeStruct((depth, rows, n), F32),
        compiler_params=_params(("parallel", "parallel")),
    )(cc, w_mod, b_mod.reshape(depth, 1, n))


def _mod_specs(k, d):
    return [pl.BlockSpec((None, 1, d), lambda b, j, *_: (b, 0, k)),
            pl.BlockSpec((1, d), lambda b, j, *_: (0, k))]


def _row_is_ctx(j, tm, ctx_len):
    return (j * tm + lax.broadcasted_iota(jnp.int32, (tm, 1), 0)) < ctx_len


def _inproj_kernel(x_ref, g_ref, sc_ref, csc_ref, sh_ref, csh_ref, cos_ref, sin_ref, w_ref, o_ref, *, tm, ctx_len):
    x = x_ref[...]
    n = x * lax.rsqrt(jnp.mean(x * x, axis=-1, keepdims=True) + EPS) * g_ref[...]
    is_ctx = _row_is_ctx(pl.program_id(1), tm, ctx_len)
    scale = jnp.where(is_ctx, csc_ref[...], sc_ref[...])
    shift = jnp.where(is_ctx, csh_ref[...], sh_ref[...])
    h = (n * (1.0 + scale) + shift).astype(BF16)
    qk = jnp.dot(h, w_ref[:, :ROPE_W], preferred_element_type=F32)
    lower_half = (lax.broadcasted_iota(jnp.int32, (tm, LANES), 1) % HEAD_DIM) < (HEAD_DIM // 2)
    cos = cos_ref[...]
    sin = sin_ref[...]
    for i in range(ROPE_W // LANES):
        t = qk[:, i * LANES:(i + 1) * LANES]
        partner = jnp.where(lower_half, pltpu.roll(t, LANES - HEAD_DIM // 2, axis=1), pltpu.roll(t, HEAD_DIM // 2, axis=1))
        o_ref[:, i * LANES:(i + 1) * LANES] = (t * cos + partner * sin).astype(BF16)
    o_ref[:, ROPE_W:] = jnp.dot(h, w_ref[:, ROPE_W:], preferred_element_type=F32).astype(BF16)


def _inproj(xt, g, mod_l, mod_c, cos_t, sin_t, w, layer, ctx_len):
    B, T, D = xt.shape
    tm = _pick_tile(T, 640)
    kern = functools.partial(_inproj_kernel, tm=tm, ctx_len=ctx_len)
    return pl.pallas_call(
        kern,
        grid=(B, T // tm),
        in_specs=[pl.BlockSpec((None, tm, D), lambda b, j: (b, j, 0)),
                  pl.BlockSpec((1, D), lambda b, j: (0, 0)),
                  *_mod_specs(1, D), *_mod_specs(0, D),
                  pl.BlockSpec((tm, LANES), lambda b, j: (j, 0)),
                  pl.BlockSpec((tm, LANES), lambda b, j: (j, 0)),
                  pl.BlockSpec((None, D, PROJ_W), lambda b, j: (layer, 0, 0))],
        out_specs=pl.BlockSpec((None, tm, PROJ_W), lambda b, j: (b, j, 0)),
        out_shape=jax.ShapeDtypeStruct((B, T, PROJ_W), BF16),
        compiler_params=_params(("parallel", "parallel")),
    )(xt, g, mod_l, mod_c, mod_l, mod_c, cos_t, sin_t, w)


def _attn_kernel(sink_ref, q_ref, kc_ref, vc_ref, km_ref, vm_ref, kp_ref, vp_ref, kn_ref, vn_ref, o_ref, *, n_tiles):
    j = pl.program_id(1)
    nt_dims = (((1,), (1,)), ((), ()))
    tn_dims = (((0,), (0,)), ((), ()))

    def scores(rows, n_rows, g, k_ctx, k_win=None, mask_win=None):
        heads = [g * ATT_GROUP + hh for hh in range(ATT_GROUP)]
        q = jnp.concatenate([q_ref[rows, h * HEAD_DIM:(h + 1) * HEAD_DIM] for h in heads], axis=0)
        head_col = lax.broadcasted_iota(jnp.int32, (1, ATT_GROUP * n_rows), 1) // n_rows
        sink = jnp.full((1, ATT_GROUP * n_rows), sink_ref[heads[0]] * LOG2E, F32)
        for hh in range(1, ATT_GROUP):
            sink = jnp.where(head_col == hh, sink_ref[heads[hh]] * LOG2E, sink)
        s_c = lax.dot_general(k_ctx, q, nt_dims, preferred_element_type=F32)
        s_w = None
        if k_win is not None:
            s_w = jnp.where(mask_win, lax.dot_general(k_win, q, nt_dims, preferred_element_type=F32), -1e30)
        return sink, s_c, s_w

    def finish(item, v_ctx, v_win=None):
        sink, s_c, s_w = item
        m = jnp.maximum(jnp.max(s_c, axis=0, keepdims=True), sink)
        if s_w is not None:
            m = jnp.maximum(m, jnp.max(s_w, axis=0, keepdims=True))
        p_c = jnp.exp2(s_c - m)
        den = jnp.sum(p_c, axis=0, keepdims=True) + jnp.exp2(sink - m)
        o = lax.dot_general(v_ctx, p_c.astype(BF16), tn_dims, preferred_element_type=F32)
        if s_w is not None:
            p_w = jnp.exp2(s_w - m)
            den += jnp.sum(p_w, axis=0, keepdims=True)
            o += lax.dot_general(v_win, p_w.astype(BF16), tn_dims, preferred_element_type=F32)
        return o / den

    def store(rows, n_rows, outs):
        t = jnp.transpose(jnp.concatenate(outs, axis=0))
        for g in range(ATT_KV_HEADS):
            for hh in range(ATT_GROUP):
                h = g * ATT_GROUP + hh
                o_ref[rows, h * HEAD_DIM:(h + 1) * HEAD_DIM] = (
                    t[hh * n_rows:(hh + 1) * n_rows, g * HEAD_DIM:(g + 1) * HEAD_DIM].astype(BF16))

    def run(blocks):
        flat = [(bi, s, f) for bi, (_, _, work) in enumerate(blocks) for s, f in work]
        outs = [[] for _ in blocks]
        pending = None
        for bi, score_fn, finish_fn in flat + [(None, None, None)]:
            item = None if score_fn is None else score_fn()
            if pending is not None:
                pbi, pitem, pfinish = pending
                outs[pbi].append(pfinish(pitem))
                if len(outs[pbi]) == ATT_KV_HEADS:
                    store(blocks[pbi][0], blocks[pbi][1], outs[pbi])
            pending = None if score_fn is None else (bi, item, finish_fn)

    @pl.when(j == 0)
    def _context_queries():
        work = []
        for g in range(ATT_KV_HEADS):
            gs = slice(g * HEAD_DIM, (g + 1) * HEAD_DIM)
            work.append((functools.partial(scores, slice(0, SEQ_TILE), SEQ_TILE, g, kc_ref[:, gs]),
                         functools.partial(finish, v_ctx=vc_ref[:, gs])))
        run([(slice(0, SEQ_TILE), SEQ_TILE, work)])

    @pl.when(j > 0)
    def _latent_queries():
        w = lax.broadcasted_iota(jnp.int32, (3 * WINDOW, ATT_GROUP * WINDOW), 0)
        r = lax.broadcasted_iota(jnp.int32, (3 * WINDOW, ATT_GROUP * WINDOW), 1) % WINDOW
        band = (w >= r) & (w <= r + 2 * WINDOW)
        blocks = []
        for sub in range(SEQ_TILE // WINDOW):
            rows = slice(sub * WINDOW, (sub + 1) * WINDOW)
            if sub == 0:
                kprev, vprev = kp_ref[...], vp_ref[...]
                kmid, vmid = km_ref[:WINDOW], vm_ref[:WINDOW]
                knext, vnext = km_ref[WINDOW:], vm_ref[WINDOW:]
                w_lo = jnp.where(j > 1, 0, WINDOW)
                w_hi = 3 * WINDOW
            else:
                kprev, vprev = km_ref[:WINDOW], vm_ref[:WINDOW]
                kmid, vmid = km_ref[WINDOW:], vm_ref[WINDOW:]
                knext, vnext = kn_ref[...], vn_ref[...]
                w_lo = 0
                w_hi = jnp.where(j < n_tiles - 1, 3 * WINDOW, 2 * WINDOW)
            mask = band & (w >= w_lo) & (w < w_hi)
            work = []
            for g in range(ATT_KV_HEADS):
                gs = slice(g * HEAD_DIM, (g + 1) * HEAD_DIM)
                k_win = jnp.concatenate([kprev[:, gs], kmid[:, gs], knext[:, gs]], axis=0)
                v_win = jnp.concatenate([vprev[:, gs], vmid[:, gs], vnext[:, gs]], axis=0)
                work.append((functools.partial(scores, rows, WINDOW, g, kc_ref[:, gs], k_win, mask),
                             functools.partial(finish, v_ctx=vc_ref[:, gs], v_win=v_win)))
            blocks.append((rows, WINDOW, work))
        run(blocks)


def _attention(proj, sink, ctx_len):
    B, T, _ = proj.shape
    assert ctx_len == SEQ_TILE and T % SEQ_TILE == 0
    n_tiles = T // SEQ_TILE
    n_win = T // WINDOW
    kcol, vcol = COL_K // KV_W, COL_V // KV_W
    per_tile = SEQ_TILE // WINDOW
    tile = lambda col: pl.BlockSpec((None, SEQ_TILE, KV_W), lambda b, j: (b, j, col))
    ctx = lambda col: pl.BlockSpec((None, SEQ_TILE, KV_W), lambda b, j: (b, 0, col))
    prev = lambda col: pl.BlockSpec((None, WINDOW, KV_W), lambda b, j: (b, jnp.maximum(j * per_tile - 1, 0), col))
    nxt = lambda col: pl.BlockSpec((None, WINDOW, KV_W), lambda b, j: (b, jnp.minimum((j + 1) * per_tile, n_win - 1), col))
    return pl.pallas_call(
        functools.partial(_attn_kernel, n_tiles=n_tiles),
        grid=(B, n_tiles),
        in_specs=[pl.BlockSpec(memory_space=pltpu.SMEM),
                  pl.BlockSpec((None, SEQ_TILE, ATT_W), lambda b, j: (b, j, 0)),
                  ctx(kcol), ctx(vcol), tile(kcol), tile(vcol), prev(kcol), prev(vcol), nxt(kcol), nxt(vcol)],
        out_specs=pl.BlockSpec((None, SEQ_TILE, ATT_W), lambda b, j: (b, j, 0)),
        out_shape=jax.ShapeDtypeStruct((B, T, ATT_W), BF16),
        compiler_params=_params(("parallel", "parallel")),
    )(sink, *([proj] * 9))


def _conv_kernel(b_ref, c_ref, x_ref, w_ref, o_ref, u_scr, *, ctx_len):
    T = c_ref.shape[0]
    pad = 8
    u = c_ref[...].astype(F32) * x_ref[...].astype(F32)
    zero_row = jnp.zeros((1, CONV_CH), F32)
    u_scr[pad - 1:pad, :] = zero_row
    u_scr[pad:pad + T, :] = u
    u_scr[pad + T:pad + T + 1, :] = zero_row
    row = lax.broadcasted_iota(jnp.int32, (T, 1), 0)
    u_prev = jnp.where(row == ctx_len, 0.0, u_scr[pad - 1:pad - 1 + T, :])
    u_next = jnp.where(row == ctx_len - 1, 0.0, u_scr[pad + 1:pad + 1 + T, :])
    y = w_ref[0:1, :] * u_prev + w_ref[1:2, :] * u + w_ref[2:3, :] * u_next
    o_ref[...] = (b_ref[...].astype(F32) * y).astype(BF16)


def _short_conv(proj, conv_w, ctx_len):
    B, T, _ = proj.shape
    col = lambda c0: pl.BlockSpec((None, T, CONV_CH), lambda b: (b, 0, c0 // CONV_CH))
    return pl.pallas_call(
        functools.partial(_conv_kernel, ctx_len=ctx_len),
        grid=(B,),
        in_specs=[col(COL_CB), col(COL_CC), col(COL_CX), pl.BlockSpec((3, CONV_CH), lambda b: (0, 0))],
        out_specs=pl.BlockSpec((None, T, CONV_CH), lambda b: (b, 0, 0)),
        out_shape=jax.ShapeDtypeStruct((B, T, CONV_CH), BF16),
        scratch_shapes=[pltpu.VMEM((T + 16, CONV_CH), F32)],
        compiler_params=_params(("parallel",)),
    )(proj, proj, proj, conv_w)


def _split2(v):
    hi = v.astype(BF16)
    lo = (v - hi.astype(F32)).astype(BF16)
    return hi, lo


def _gla_kernel(q_ref, k_ref, v_ref, gate_ref, lr_ref, w2_ref, bias_ref, ng_ref, o_ref,
                oacc_scr, qs_scr, u_scr, dec_scr, sin_scr, *, reverse, n_ctx_chunks):
    T = q_ref.shape[0]
    C = GLA_CHUNK
    n_tiles = T // SEQ_TILE
    n_chunks = T // C
    per_tile = SEQ_TILE // C
    nt_dims = (((1,), (1,)), ((), ()))
    tn_dims = (((0,), (0,)), ((), ()))

    r = lax.broadcasted_iota(jnp.int32, (SEQ_TILE, SEQ_TILE), 0)
    c = lax.broadcasted_iota(jnp.int32, (SEQ_TILE, SEQ_TILE), 1)
    same_chunk = (r // C) == (c // C)
    causal = same_chunk & ((c >= r) if reverse else (c <= r))
    cum_mat = jnp.where(causal, 1.0, 0.0).astype(BF16)
    same_head = (r // GLA_DK) == (c // GLA_DK)
    head_mean = jnp.where(same_head, 1.0 / GLA_DK, 0.0).astype(BF16)
    lane_head = lax.broadcasted_iota(jnp.int32, (C, GLA_W), 1) // GLA_DK

    def prepare(i):
        rows = pl.ds(pl.multiple_of(i * SEQ_TILE, SEQ_TILE), SEQ_TILE)
        z = jnp.dot(lr_ref[rows, :], w2_ref[...], preferred_element_type=F32) + bias_ref[...]
        la = (jnp.minimum(z, 0.0) - jnp.log(1.0 + jnp.exp(-jnp.abs(z)))) / GLA_TAU
        hi, lo = _split2(la)
        b = jnp.dot(cum_mat, hi, preferred_element_type=F32) + jnp.dot(cum_mat, lo, preferred_element_type=F32)
        b3 = b.reshape(per_tile, C, GLA_W)
        total = b3[:, 0:1, :] if reverse else b3[:, C - 1:C, :]
        b_last = jnp.broadcast_to(total, (per_tile, C, GLA_W)).reshape(SEQ_TILE, GLA_W)
        b_ref = 0.5 * b_last
        q = q_ref[rows, :].astype(F32)
        k = k_ref[rows, :].astype(F32)
        e_fwd = jnp.exp(b - b_ref)
        e_bwd = jnp.exp(b_ref - b)
        e_half = jnp.exp(b_ref)
        qe = (q * e_fwd).astype(BF16)
        ke = (k * e_bwd).astype(BF16)
        ku = (k * (e_bwd * e_half)).astype(BF16)
        qs_scr[rows, :] = (q * (e_fwd * e_half)).astype(BF16)
        return rows, qe, ke, ku, e_half * e_half

    def products(i, prepared):
        rows, qe, ke, ku, dec = prepared
        v = v_ref[rows, :]
        for h in range(GLA_HEADS):
            hs = slice(h * GLA_DK, (h + 1) * GLA_DK)
            s = lax.dot_general(qe[:, hs], ke[:, hs], nt_dims, preferred_element_type=F32)
            a = jnp.where(causal, s, 0.0).astype(BF16)
            oacc_scr[rows, hs] = jnp.dot(a, v[:, hs], preferred_element_type=F32)
        for ci in range(per_tile):
            cs = slice(ci * C, (ci + 1) * C)
            full = lax.dot_general(v[cs, :], ku[cs, :], tn_dims, preferred_element_type=F32)
            ut = full[0:C, :]
            for h in range(1, GLA_HEADS):
                ut = jnp.where(lane_head == h, full[h * C:(h + 1) * C, :], ut)
            u_scr[i * per_tile + ci] = ut
            dec_scr[i * per_tile + ci] = dec[ci * C:ci * C + 1, :]

    def phase1(first, count):
        pending = None
        for t in range(count + 1):
            nxt = (first + t, prepare(first + t)) if t < count else None
            if pending is not None:
                products(*pending)
            pending = nxt

    group = 6
    lax.fori_loop(0, n_tiles // group, lambda g, c: (phase1(g * group, group), c)[1], 0)
    if n_tiles % group:
        phase1((n_tiles // group) * group, n_tiles % group)


    def phase2(i, st):
        if reverse:
            ci = jnp.where(i < n_ctx_chunks, n_ctx_chunks - 1 - i, n_chunks - 1 + n_ctx_chunks - i)
        else:
            ci = i
        sin_scr[ci] = st.astype(BF16)
        return st * dec_scr[ci] + u_scr[ci]

    lax.fori_loop(0, n_chunks, phase2, jnp.zeros((C, GLA_W), F32))

    def carried_in(i):
        rows = pl.ds(pl.multiple_of(i * SEQ_TILE, SEQ_TILE), SEQ_TILE)
        qs = qs_scr[rows, :]
        parts = []
        for ci in range(per_tile):
            st = sin_scr[i * per_tile + ci]
            st_heads = jnp.where(same_head, jnp.concatenate([st] * GLA_HEADS, axis=0), jnp.zeros((), BF16))
            parts.append(lax.dot_general(qs[ci * C:(ci + 1) * C, :], st_heads, nt_dims, preferred_element_type=F32))
        return rows, jnp.concatenate(parts, axis=0)

    def finish(rows, o_inter):
        o = oacc_scr[rows, :] + o_inter
        ms = jnp.dot((o * o).astype(BF16), head_mean, preferred_element_type=F32)
        y = o * lax.rsqrt(ms + EPS) * ng_ref[...]
        o_ref[rows, :] = (y * _silu(gate_ref[rows, :].astype(F32))).astype(BF16)

    def phase3(first, count):
        pending = None
        for t in range(count + 1):
            nxt = carried_in(first + t) if t < count else None
            if pending is not None:
                finish(*pending)
            pending = nxt

    lax.fori_loop(0, n_tiles // group, lambda g, c: (phase3(g * group, group), c)[1], 0)
    if n_tiles % group:
        phase3((n_tiles // group) * group, n_tiles % group)


def _gla(proj, w2pad, bias, ng, ctx_len, reverse):
    B, T, _ = proj.shape
    assert T % SEQ_TILE == 0 and ctx_len % GLA_CHUNK == 0
    n_chunks = T // GLA_CHUNK
    col = lambda c0: pl.BlockSpec((None, T, GLA_W), lambda b: (b, 0, c0 // GLA_W))
    gate_col = COL_GGB if reverse else COL_GGF
    kern = functools.partial(_gla_kernel, reverse=reverse, n_ctx_chunks=ctx_len // GLA_CHUNK)
    return pl.pallas_call(
        kern,
        grid=(B,),
        in_specs=[col(COL_GQ), col(COL_GK), col(COL_GV), col(gate_col),
                  pl.BlockSpec((None, T, LANES), lambda b: (b, 0, COL_LR // LANES)),
                  pl.BlockSpec((LANES, GLA_W), lambda b: (0, 0)),
                  pl.BlockSpec((1, GLA_W), lambda b: (0, 0)),
                  pl.BlockSpec((1, GLA_W), lambda b: (0, 0))],
        out_specs=pl.BlockSpec((None, T, GLA_W), lambda b: (b, 0, 0)),
        out_shape=jax.ShapeDtypeStruct((B, T, GLA_W), BF16),
        scratch_shapes=[pltpu.VMEM((T, GLA_W), F32),
                        pltpu.VMEM((T, GLA_W), BF16),
                        pltpu.VMEM((n_chunks, GLA_CHUNK, GLA_W), F32),
                        pltpu.VMEM((n_chunks, 1, GLA_W), F32),
                        pltpu.VMEM((n_chunks, GLA_CHUNK, GLA_W), BF16)],
        compiler_params=_params(("parallel",)),
    )(proj, proj, proj, proj, proj, w2pad, bias, ng)


MXU_TILE = 256
FFN_CHUNK = 3 * MXU_TILE


def _swiglu_resident(h, wgu_ref, wd_ref):
    F = wd_ref.shape[0]
    y = None
    for c0 in range(0, F, FFN_CHUNK):
        c1 = min(c0 + FFN_CHUNK, F)
        g = jnp.dot(h, wgu_ref[:, c0:c1], preferred_element_type=F32)
        u = jnp.dot(h, wgu_ref[:, F + c0:F + c1], preferred_element_type=F32)
        part = jnp.dot((_silu(g) * u).astype(BF16), wd_ref[c0:c1, :], preferred_element_type=F32)
        y = part if y is None else y + part
    return y


def _outproj_kernel(*refs, tm, ctx_len, moe):
    if moe:
        (x_ref, att_ref, gf_ref, gb_ref, cv_ref, w_ref, g1_ref, cg1_ref, ng_ref, sc_ref, csc_ref, sh_ref, csh_ref,
         router_ref, earlier_ref, xo_ref, hp_ref, route_ref, cnt_ref) = refs
    else:
        (x_ref, att_ref, gf_ref, gb_ref, cv_ref, w_ref, g1_ref, cg1_ref, ng_ref, sc_ref, csc_ref, sh_ref, csh_ref,
         g2_ref, cg2_ref, wgu_ref, wd_ref, xo_ref) = refs
    is_ctx = _row_is_ctx(pl.program_id(1), tm, ctx_len)
    gla = (gf_ref[...].astype(F32) + gb_ref[...].astype(F32)).astype(BF16)
    y = jnp.dot(att_ref[...], w_ref[:ATT_W, :], preferred_element_type=F32)
    y += jnp.dot(gla, w_ref[ATT_W:ATT_W + GLA_W, :], preferred_element_type=F32)
    y += jnp.dot(cv_ref[...], w_ref[ATT_W + GLA_W:, :], preferred_element_type=F32)
    x = x_ref[...] + jnp.where(is_ctx, cg1_ref[...], g1_ref[...]) * y
    if moe:
        xo_ref[...] = x
    n = x * lax.rsqrt(jnp.mean(x * x, axis=-1, keepdims=True) + EPS) * ng_ref[...]
    scale = jnp.where(is_ctx, csc_ref[...], sc_ref[...])
    shift = jnp.where(is_ctx, csh_ref[...], sh_ref[...])
    h = n * (1.0 + scale) + shift
    hi = h.astype(BF16)
    if not moe:
        xo_ref[...] = x + jnp.where(is_ctx, cg2_ref[...], g2_ref[...]) * _swiglu_resident(hi, wgu_ref, wd_ref)
    else:
        _store_packed(hp_ref, _pack_bf16_pairs(h))
        logits = jnp.dot(hi, router_ref[...], preferred_element_type=F32)
        lane = lax.broadcasted_iota(jnp.int32, logits.shape, 1)
        neg = -jnp.inf
        lg = jnp.where(lane < N_EXPERTS, logits, neg)
        m1 = jnp.max(lg, axis=-1, keepdims=True)
        i1 = jnp.min(jnp.where(lg == m1, lane, LANES), axis=-1, keepdims=True)
        lg2 = jnp.where(lane == i1, neg, lg)
        m2 = jnp.max(lg2, axis=-1, keepdims=True)
        i2 = jnp.min(jnp.where(lg2 == m2, lane, LANES), axis=-1, keepdims=True)
        e2 = jnp.exp(m2 - m1)
        gate1 = 1.0 / (1.0 + e2)
        gate2 = e2 / (1.0 + e2)
        first = jnp.where(lane == i1, 1.0, 0.0)
        second = jnp.where(lane == i2, 1.0, 0.0)
        before = jnp.dot(earlier_ref[...], jnp.concatenate([first, second], axis=1).astype(BF16),
                         preferred_element_type=F32)
        n_first = jnp.sum(first, axis=0, keepdims=True)
        rank1 = jnp.sum(first * before[:, :LANES], axis=-1, keepdims=True)
        rank2 = jnp.sum(second * (before[:, LANES:] + n_first), axis=-1, keepdims=True)
        cnt_ref[...] = jnp.broadcast_to(n_first + jnp.sum(second, axis=0, keepdims=True), cnt_ref.shape)
        route_ref[...] = jnp.where(lane == 0, i1.astype(F32), jnp.where(lane == 1, i2.astype(F32),
                                   jnp.where(lane == 2, gate1, jnp.where(lane == 3, gate2,
                                             jnp.where(lane == 4, rank1, jnp.where(lane == 5, rank2, 0.0))))))


def _outproj(xt, att, gf, gb, cv, w_out, layer, ng, mod_l, mod_c, ctx_len, router=None, ffn=None):
    B, T, D = xt.shape
    tm = _pick_tile(T, 640)
    moe = router is not None
    assert moe != (ffn is not None)
    row = lambda w: pl.BlockSpec((None, tm, w), lambda b, j: (b, j, 0))
    const = lambda shape: pl.BlockSpec(shape, lambda b, j: (0,) * len(shape))
    in_specs = [row(D), row(ATT_W), row(GLA_W), row(GLA_W), row(CONV_CH),
                pl.BlockSpec((None, D, D), lambda b, j: (layer, 0, 0)),
                *_mod_specs(2, D), const((1, D)), *_mod_specs(4, D), *_mod_specs(3, D)]
    args = [xt, att, gf, gb, cv, w_out, mod_l, mod_c, ng, mod_l, mod_c, mod_l, mod_c]
    nt = T // tm
    if moe:
        in_specs += [pl.BlockSpec((None, D, LANES), lambda b, j: (router[1], 0, 0)), const((tm, tm))]
        args += [router[0], jnp.tri(tm, k=-1, dtype=BF16)]
        out_specs = [row(D), pl.BlockSpec((PACK_CHUNKS, tm, LANES), lambda b, j: (0, b * nt + j, 0)), row(LANES),
                     pl.BlockSpec((None, 8, LANES), lambda b, j: (b * nt + j, 0, 0))]
        out_shape = [jax.ShapeDtypeStruct((B, T, D), F32), jax.ShapeDtypeStruct((PACK_CHUNKS, B * T, LANES), jnp.int32),
                     jax.ShapeDtypeStruct((B, T, LANES), F32), jax.ShapeDtypeStruct((B * nt, 8, LANES), F32)]
    else:
        w_gu, w_down, idx = ffn
        F = w_down.shape[1]
        assert F % MXU_TILE == 0
        resident = lambda shape: pl.BlockSpec((None,) + shape, lambda b, j: (idx, 0, 0), pipeline_mode=pl.Buffered(1))
        in_specs += [*_mod_specs(5, D), resident((D, 2 * F)), resident((F, D))]
        args += [mod_l, mod_c, w_gu, w_down]
        out_specs = row(D)
        out_shape = jax.ShapeDtypeStruct((B, T, D), F32)
    return pl.pallas_call(
        functools.partial(_outproj_kernel, tm=tm, ctx_len=ctx_len, moe=moe),
        grid=(B, T // tm),
        in_specs=in_specs, out_specs=out_specs, out_shape=out_shape,
        compiler_params=_params(("parallel", "parallel")),
    )(*args)


def _sc_gather(table, idx):
    _, width = table.shape
    m = idx.shape[0]
    workers = SC_CORES * SC_SUBCORES
    ch = SC_INDEX_CHUNK
    assert width == LANES and m % (workers * ch) == 0
    n_chunks = m // (workers * ch)
    nb = max(b for b in (4, 3, 2, 1) if n_chunks % b == 0)
    mesh = plsc.VectorSubcoreMesh(core_axis_name="c", subcore_axis_name="s",
                                  num_cores=SC_CORES, num_subcores=SC_SUBCORES)

    @functools.partial(
        pl.kernel, mesh=mesh, out_type=jax.ShapeDtypeStruct((m, width), table.dtype),
        scratch_types=[pltpu.VMEM((n_chunks, ch), jnp.int32), pltpu.VMEM((nb, ch, width), table.dtype)]
        + [pltpu.SemaphoreType.DMA] * (2 * nb))
    def gather(table_hbm, idx_hbm, out_hbm, idx_v, rows_v, *sems):
        wid = lax.axis_index("s") * SC_CORES + lax.axis_index("c")
        base = wid * (n_chunks * ch)
        pltpu.sync_copy(idx_hbm.at[wid], idx_v)

        def fetch(c, b):
            return pltpu.make_async_copy(table_hbm.at[idx_v.at[c]], rows_v.at[b], sems[b])

        def flush(c, b):
            return pltpu.make_async_copy(rows_v.at[b], out_hbm.at[pl.ds(base + c * ch, ch)], sems[nb + b])

        @pl.loop(0, n_chunks, step=nb)
        def _(c0):
            for b in range(nb):
                @pl.when(c0 > 0)
                def _():
                    flush(c0 - nb + b, b).wait()
                fetch(c0 + b, b).start()
            for b in range(nb):
                fetch(c0 + b, b).wait()
                flush(c0 + b, b).start()

        for b in range(nb):
            flush(n_chunks - nb + b, b).wait()

    return gather(table, idx.reshape(workers, n_chunks, ch))


def _sc_dispatch(table, dest_a, dest_b, dest_pad, n_out):
    m, width = table.shape
    n_pad = dest_pad.shape[0]
    workers = SC_CORES * SC_SUBCORES
    ch = SC_INDEX_CHUNK
    assert width == LANES and m % (workers * ch) == 0 and n_pad % (workers * ch) == 0
    n_chunks = m // (workers * ch)
    pad_chunks = n_pad // (workers * ch)
    nb = max(b for b in (4, 3, 2, 1) if n_chunks % b == 0)
    mesh = plsc.VectorSubcoreMesh(core_axis_name="c", subcore_axis_name="s",
                                  num_cores=SC_CORES, num_subcores=SC_SUBCORES)

    @functools.partial(
        pl.kernel, mesh=mesh, out_type=jax.ShapeDtypeStruct((n_out, width), table.dtype),
        scratch_types=[pltpu.VMEM((n_chunks, ch), jnp.int32), pltpu.VMEM((n_chunks, ch), jnp.int32),
                       pltpu.VMEM((pad_chunks, ch), jnp.int32), pltpu.VMEM((nb, ch, width), table.dtype),
                       pltpu.VMEM((ch, width), table.dtype)]
        + [pltpu.SemaphoreType.DMA] * (3 * nb + 1))
    def dispatch(table_hbm, zeros_hbm, da_hbm, db_hbm, dp_hbm, out_hbm, da_v, db_v, dp_v, rows_v, zero_v, *sems):
        wid = lax.axis_index("s") * SC_CORES + lax.axis_index("c")
        base = wid * (n_chunks * ch)
        pltpu.sync_copy(da_hbm.at[wid], da_v)
        pltpu.sync_copy(db_hbm.at[wid], db_v)
        pltpu.sync_copy(dp_hbm.at[wid], dp_v)
        pltpu.sync_copy(zeros_hbm, zero_v)

        def fetch(c, b):
            return pltpu.make_async_copy(table_hbm.at[pl.ds(base + c * ch, ch)], rows_v.at[b], sems[b])

        def put_a(c, b):
            return pltpu.make_async_copy(rows_v.at[b], out_hbm.at[da_v.at[c]], sems[nb + b])

        def put_b(c, b):
            return pltpu.make_async_copy(rows_v.at[b], out_hbm.at[db_v.at[c]], sems[2 * nb + b])

        def put_zero(p):
            return pltpu.make_async_copy(zero_v, out_hbm.at[dp_v.at[p]], sems[3 * nb])

        for p in range(pad_chunks):
            put_zero(p).start()

        @pl.loop(0, n_chunks, step=nb)
        def _(c0):
            for b in range(nb):
                @pl.when(c0 > 0)
                def _():
                    put_a(c0 - nb + b, b).wait()
                    put_b(c0 - nb + b, b).wait()
                fetch(c0 + b, b).start()
            for b in range(nb):
                fetch(c0 + b, b).wait()
                put_a(c0 + b, b).start()
                put_b(c0 + b, b).start()

        for b in range(nb):
            put_a(n_chunks - nb + b, b).wait()
            put_b(n_chunks - nb + b, b).wait()
        for p in range(pad_chunks):
            put_zero(p).wait()

    shape3 = lambda v: v.reshape(workers, -1, ch)
    return dispatch(table, jnp.zeros((ch, width), table.dtype), shape3(dest_a), shape3(dest_b), shape3(dest_pad))


def _routing(route, tile_counts, tile):
    B, T, _ = route.shape
    n = B * T
    assert (2 * n) % tile == 0
    info = route.reshape(n, LANES)[:, :6].astype(jnp.int32)
    per_tile = tile_counts[:, 0, :N_EXPERTS].astype(jnp.int32)
    counts = jnp.sum(per_tile, axis=0)
    padded = (counts + tile - 1) // tile * tile
    ends = jnp.cumsum(padded)
    starts = ends - padded
    base = jnp.repeat(starts[None, :] + jnp.cumsum(per_tile, axis=0) - per_tile, n // per_tile.shape[0], axis=0)
    experts = jnp.arange(N_EXPERTS, dtype=jnp.int32)[None, :]
    row_of = lambda e, rank: jnp.sum(jnp.where(e[:, None] == experts, base, 0), axis=1) + rank
    dest = jnp.concatenate([row_of(info[:, 0], info[:, 4]), row_of(info[:, 1], info[:, 5])])
    n_pad = N_EXPERTS * tile
    n_rows = 2 * n + n_pad
    seg_first = jnp.concatenate([starts + counts, ends[-1:]])
    seg_size = jnp.concatenate([padded - counts, (n_rows - ends[-1])[None]])
    seg_end = jnp.cumsum(seg_size)
    j = jnp.arange(n_pad, dtype=jnp.int32)
    seg = jnp.sum((j[:, None] >= seg_end[None, :]).astype(jnp.int32), axis=1)
    pad_rows = (seg_first[seg] + j - (seg_end - seg_size)[seg]).astype(jnp.int32)
    tile_start = jnp.arange(n_rows // tile, dtype=jnp.int32) * tile
    tile_expert = jnp.minimum(jnp.sum((tile_start[:, None] >= ends[None, :]).astype(jnp.int32), axis=1), N_EXPERTS - 1)
    tile_valid = (tile_start < ends[-1]).astype(jnp.int32)
    return dest, pad_rows, n_rows, tile_expert, tile_valid


def _expert_kernel(te_ref, tv_ref, xs_ref, wgu_ref, wd_ref, ys_ref, wgu_scr):
    i = pl.program_id(0)
    F = wd_ref.shape[0]

    @pl.when((i == 0) | (te_ref[i] != te_ref[jnp.maximum(i - 1, 0)]))
    def _new_expert():
        wgu_scr[...] = wgu_ref[...].astype(BF16)

    @pl.when(tv_ref[i] > 0)
    def _compute():
        lo, hi = _unpack_bf16_pairs(_load_packed(xs_ref))
        gu = (jnp.dot(lo.astype(BF16), wgu_scr[:PACK_W, :], preferred_element_type=F32)
              + jnp.dot(hi.astype(BF16), wgu_scr[PACK_W:, :], preferred_element_type=F32))
        act = (_silu(gu[:, :F]) * gu[:, F:]).astype(BF16)
        y = jnp.dot(act, wd_ref[...], preferred_element_type=F32)
        _store_packed(ys_ref, _pack_bf16_pairs(y))

    @pl.when(tv_ref[i] == 0)
    def _unused_tile():
        ys_ref[...] = jnp.zeros_like(ys_ref)


def _experts(xs, tile_expert, tile_valid, w_gu, layer, w_down):
    _, n_rows, _ = xs.shape
    _, _, F, D = w_down.shape
    tile = EXPERT_TILE
    rows = pl.BlockSpec((PACK_CHUNKS, tile, LANES), lambda i, te, tv: (0, i, 0))
    return pl.pallas_call(
        _expert_kernel,
        grid_spec=pltpu.PrefetchScalarGridSpec(
            num_scalar_prefetch=2,
            grid=(n_rows // tile,),
            in_specs=[rows,
                      pl.BlockSpec((None, None, D, 2 * F), lambda i, te, tv: (layer, te[i], 0, 0)),
                      pl.BlockSpec((None, None, F, D), lambda i, te, tv: (0, te[i], 0, 0))],
            out_specs=rows,
            scratch_shapes=[pltpu.VMEM((D, 2 * F), BF16)]),
        out_shape=jax.ShapeDtypeStruct(xs.shape, jnp.int32),
        compiler_params=_params(("arbitrary",)),
    )(tile_expert, tile_valid, xs, w_gu, w_down)


def _combine_kernel(*refs, tm, ctx_len, skip, final):
    x_ref, g2_ref, cg2_ref, route_ref, ya_ref, yb_ref = refs[:6]
    fg_ref = refs[6] if final else None
    o_ref = refs[-1]
    is_ctx = _row_is_ctx(pl.program_id(1) + skip, tm, ctx_len)
    route = route_ref[...]
    lane = lax.broadcasted_iota(jnp.int32, route.shape, 1)
    gate_a = jnp.sum(jnp.where(lane == 2, route, 0.0), axis=-1, keepdims=True)
    gate_b = jnp.sum(jnp.where(lane == 3, route, 0.0), axis=-1, keepdims=True)
    unpack = lambda ref: jnp.concatenate(_unpack_bf16_pairs(_load_packed(ref)), axis=1)
    f = gate_a * unpack(ya_ref) + gate_b * unpack(yb_ref)
    x = x_ref[...] + jnp.where(is_ctx, cg2_ref[...], g2_ref[...]) * f
    if final:
        x = x * lax.rsqrt(jnp.mean(x * x, axis=-1, keepdims=True) + EPS) * fg_ref[...]
    o_ref[...] = x


def _combine(xt, mod_l, mod_c, route, yg, ctx_len, final_g=None):
    B, T, D = xt.shape
    final = final_g is not None
    if final:
        tm = SEQ_TILE
        assert ctx_len % tm == 0 and T % tm == 0
        skip = ctx_len // tm
    else:
        tm = _pick_tile(T, 640)
        skip = 0
    nt = T // tm
    row = lambda w: pl.BlockSpec((None, tm, w), lambda b, j: (b, j + skip, 0))
    slot = lambda s: pl.BlockSpec((PACK_CHUNKS, None, tm, LANES), lambda b, j: (0, s, b * nt + j + skip, 0))
    in_specs = [row(D), *_mod_specs(5, D), row(LANES), slot(0), slot(1)]
    args = [xt, mod_l, mod_c, route, yg, yg]
    if final:
        in_specs.append(pl.BlockSpec((1, D), lambda b, j: (0, 0)))
        args.append(final_g)
    return pl.pallas_call(
        functools.partial(_combine_kernel, tm=tm, ctx_len=ctx_len, skip=skip, final=final),
        grid=(B, nt - skip),
        in_specs=in_specs,
        out_specs=pl.BlockSpec((None, tm, D), lambda b, j: (b, j, 0)),
        out_shape=jax.ShapeDtypeStruct((B, T - skip * tm, D), F32),
        compiler_params=_params(("parallel", "parallel")),
    )(*args)


def _moe(hp, route, tile_counts, xt, mod_l, mod_c, w_gu, layer, w_down, ctx_len, final_g=None):
    B, T, _ = xt.shape
    n = B * T
    dest, pad_rows, n_rows, tile_expert, tile_valid = _routing(route, tile_counts, EXPERT_TILE)
    chunk = jnp.arange(PACK_CHUNKS, dtype=jnp.int32)[:, None] * n_rows
    xs = _sc_dispatch(hp.reshape(PACK_CHUNKS * n, LANES), (chunk + dest[None, :n]).reshape(-1),
                      (chunk + dest[None, n:]).reshape(-1), (chunk + pad_rows[None, :]).reshape(-1),
                      PACK_CHUNKS * n_rows)
    ys = _experts(xs.reshape(PACK_CHUNKS, n_rows, LANES), tile_expert, tile_valid, w_gu, layer, w_down)
    ys = ys.reshape(PACK_CHUNKS * n_rows, LANES)
    yg = _sc_gather(ys, (chunk + dest[None, :]).reshape(-1)).reshape(PACK_CHUNKS, 2, n, LANES)
    return _combine(xt, mod_l, mod_c, route, yg, ctx_len, final_g)


def _final_norm_kernel(x_ref, g_ref, o_ref):
    x = x_ref[...]
    o_ref[...] = x * lax.rsqrt(jnp.mean(x * x, axis=-1, keepdims=True) + EPS) * g_ref[...]


def _final_norm(xt, g, ctx_len):
    B, T, D = xt.shape
    S = T - ctx_len
    tm = SEQ_TILE
    assert ctx_len % tm == 0 and S % tm == 0
    skip = ctx_len // tm
    return pl.pallas_call(
        _final_norm_kernel,
        grid=(B, S // tm),
        in_specs=[pl.BlockSpec((None, tm, D), lambda b, j: (b, j + skip, 0)),
                  pl.BlockSpec((1, D), lambda b, j: (0, 0))],
        out_specs=pl.BlockSpec((None, tm, D), lambda b, j: (b, j, 0)),
        out_shape=jax.ShapeDtypeStruct((B, S, D), F32),
        compiler_params=_params(("parallel", "parallel")),
    )(xt, g)


def _rope_tables(seq, ctx_len):
    pos = jnp.arange(seq, dtype=jnp.int32)
    nf = HEAD_DIM // 4
    inv = ROPE_THETA ** (-jnp.arange(nf, dtype=F32) / nf)
    ang = jnp.concatenate([(pos // GRID_W).astype(F32)[:, None] * inv, (pos % GRID_W).astype(F32)[:, None] * inv], axis=-1)
    cos, sin = jnp.cos(ang), jnp.sin(ang)
    reps = LANES // HEAD_DIM
    cos_t = jnp.tile(jnp.concatenate([cos, cos], axis=-1), (1, reps))
    sin_t = jnp.tile(jnp.concatenate([-sin, sin], axis=-1), (1, reps))
    cos_t = jnp.concatenate([jnp.ones((ctx_len, LANES), F32), cos_t], axis=0)
    sin_t = jnp.concatenate([jnp.zeros((ctx_len, LANES), F32), sin_t], axis=0)
    return cos_t, sin_t


def _relayout_w_in(w_in):
    o_lr = ATT_W + 2 * KV_W + 5 * GLA_W
    q_scale = HEAD_DIM ** -0.5 * LOG2E
    gq_scale = GLA_DK ** -0.5
    parts = [w_in[..., :ATT_W] * q_scale, w_in[..., ATT_W:COL_GQ], w_in[..., COL_GQ:COL_GK] * gq_scale,
             w_in[..., COL_GK:o_lr], w_in[..., o_lr + 2 * GLA_LOWRANK:], w_in[..., o_lr:o_lr + 2 * GLA_LOWRANK],
             jnp.zeros(w_in.shape[:-1] + (LANES - 2 * GLA_LOWRANK,), w_in.dtype)]
    return jnp.concatenate(parts, axis=-1).astype(BF16)


def kernel(x, c, ctx, c_ctx, w_mod, b_mod, norm1_g, norm2_g, w_in, w_out, attn_sink, gla_w2_f, gla_b_f, gla_w2_b, gla_b_b, gla_norm_g, conv_w, ffn_w_gu, ffn_w_down, router_w, expert_w_gu, expert_w_down, final_norm_g):
    B, S, D = x.shape
    L = ctx.shape[1]
    depth = w_in.shape[0]
    assert D == D_MODEL and S % GRID_W == 0

    xt = jnp.concatenate([ctx, x], axis=1)
    rows = 16
    cc = jnp.concatenate([c, c_ctx[None, :], jnp.zeros((rows - B - 1, D), F32)], axis=0)
    mod = _adaln(cc, w_mod, b_mod)
    cos_t, sin_t = _rope_tables(S, L)

    w_in_p = _relayout_w_in(w_in)
    w_out_b = _to_bf16(w_out)
    zpad = jnp.zeros((depth, LANES - 2 * GLA_LOWRANK, GLA_W), F32)
    w2_f = jnp.concatenate([gla_w2_f, jnp.zeros_like(gla_w2_b), zpad], axis=1).astype(BF16)
    w2_b = jnp.concatenate([jnp.zeros_like(gla_w2_f), gla_w2_b, zpad], axis=1).astype(BF16)
    gla_ng = jnp.tile(gla_norm_g, (1, GLA_HEADS))
    dense_w = {0: (_to_bf16(ffn_w_gu, lead=0), _to_bf16(ffn_w_down, lead=0))}
    router_b = jnp.pad(router_w, ((0, 0), (0, 0), (0, LANES - N_EXPERTS))).astype(BF16)

    for l in range(depth):
        mod_l = mod[l, :B].reshape(B, 1, 6 * D)
        mod_c = mod[l, B:B + 1]
        proj = _inproj(xt, norm1_g[l][None, :], mod_l, mod_c, cos_t, sin_t, w_in_p, l, L)
        att = _attention(proj, attn_sink[l], L)
        cv = _short_conv(proj, conv_w[l], L)
        gf = _gla(proj, w2_f[l], gla_b_f[l][None, :], gla_ng[l][None, :], L, reverse=False)
        gb = _gla(proj, w2_b[l], gla_b_b[l][None, :], gla_ng[l][None, :], L, reverse=True)
        if l % 2 == 0:
            xt = _outproj(xt, att, gf, gb, cv, w_out_b, l, norm2_g[l][None, :], mod_l, mod_c, L,
                          ffn=(*dense_w.pop(l), 0))
        else:
            xt, hp, route, tile_counts = _outproj(xt, att, gf, gb, cv, w_out_b, l, norm2_g[l][None, :], mod_l, mod_c, L,
                                     router=(router_b, l // 2))
            exp_down_b = _to_bf16(expert_w_down, lead=l // 2, after=route)
            if l + 1 < depth:
                dense_w[l + 1] = (_to_bf16(ffn_w_gu, lead=(l + 1) // 2, after=route),
                                  _to_bf16(ffn_w_down, lead=(l + 1) // 2, after=route))
            xt = _moe(hp, route, tile_counts, xt, mod_l, mod_c, expert_w_gu, l // 2, exp_down_b, L,
                      final_g=final_norm_g[None, :] if l == depth - 1 else None)
    return xt if depth % 2 == 0 else _final_norm(xt, final_norm_g[None, :], L)
```

```python
import functools

import jax
import jax.numpy as jnp
from jax import lax
from jax.experimental import pallas as pl
from jax.experimental.pallas import tpu as pltpu
from jax.experimental.pallas import tpu_sc as plsc

F32 = jnp.float32
BF16 = jnp.bfloat16

D_MODEL = 1024
EPS = 1e-6
GRID_W = 64
ROPE_THETA = 10000.0
LOG2E = 1.4426950408889634
HEAD_DIM = 64
ATT_HEADS = 8
ATT_KV_HEADS = 2
ATT_GROUP = ATT_HEADS // ATT_KV_HEADS
ATT_W = ATT_HEADS * HEAD_DIM
KV_W = ATT_KV_HEADS * HEAD_DIM
WINDOW = 128
GLA_HEADS = 4
GLA_DK = 64
GLA_W = GLA_HEADS * GLA_DK
GLA_LOWRANK = 16
GLA_TAU = 16.0
GLA_CHUNK = 64
CONV_CH = 256
N_EXPERTS = 8

LANES = 128
SUBLANES_BF16 = 16

COL_Q = 0
COL_K = ATT_W
COL_V = COL_K + KV_W
ROPE_W = COL_V
COL_GQ = COL_V + KV_W
COL_GK = COL_GQ + GLA_W
COL_GV = COL_GK + GLA_W
COL_GGF = COL_GV + GLA_W
COL_GGB = COL_GGF + GLA_W
COL_CB = COL_GGB + GLA_W
COL_CC = COL_CB + CONV_CH
COL_CX = COL_CC + CONV_CH
COL_LR = COL_CX + CONV_CH
PROJ_W = COL_LR + LANES

SEQ_TILE = 256
VMEM_LIMIT = 56 * 1024 * 1024
CAST_BLOCK_BYTES = 4 * 1024 * 1024

PACK_W = D_MODEL // 2
PACK_CHUNKS = PACK_W // LANES
EXPERT_TILE = 512
SC_CORES = 2
SC_SUBCORES = 16
SC_INDEX_CHUNK = 128


def _silu(v):
    return v / (1.0 + jnp.exp(-v))


def _pick_tile(total, cap):
    best = None
    for t in range(SUBLANES_BF16, cap + 1, SUBLANES_BF16):
        if total % t == 0:
            best = t
    assert best is not None
    return best


def _params(sem):
    return pltpu.CompilerParams(dimension_semantics=sem, vmem_limit_bytes=VMEM_LIMIT)


def _pack_bf16_pairs(v):
    bits = lambda t: lax.bitcast_convert_type(t.astype(BF16).astype(F32), jnp.int32)
    return ((bits(v[:, :PACK_W]) >> 16) & 0xFFFF) | (bits(v[:, PACK_W:]) & -65536)


def _unpack_bf16_pairs(w):
    return lax.bitcast_convert_type(w << 16, F32), lax.bitcast_convert_type(w & -65536, F32)


CAST_STREAMS = 4


def _cast_kernel(*refs):
    o_ref = refs[-1]
    slab = refs[0].shape[0]
    for s, x_ref in enumerate(refs[:-1]):
        o_ref[s * slab:(s + 1) * slab, :] = x_ref[...].astype(BF16)


def _to_bf16(w, lead=None, after=None):
    cols = w.shape[-1]
    rows_in = w.size // cols
    rows = rows_in if lead is None else rows_in // w.shape[0]
    tr = _pick_tile(rows, max(SUBLANES_BF16, CAST_BLOCK_BYTES // (4 * cols)))
    streams = CAST_STREAMS if tr % (CAST_STREAMS * SUBLANES_BF16) == 0 else 1
    slab = tr // streams
    first = 0 if lead is None else lead * (rows // slab)
    w2 = w.reshape(rows_in, cols)
    if after is not None:
        w2, _ = lax.optimization_barrier((w2, after))
    out = pl.pallas_call(
        _cast_kernel,
        grid=(rows // tr,),
        in_specs=[pl.BlockSpec((slab, cols), lambda i, s=s: (first + i * streams + s, 0)) for s in range(streams)],
        out_specs=pl.BlockSpec((tr, cols), lambda i: (i, 0)),
        out_shape=jax.ShapeDtypeStruct((rows, cols), BF16),
        compiler_params=_params(("parallel",)),
    )(*([w2] * streams))
    return out.reshape(w.shape if lead is None else (1,) + w.shape[1:])


def _load_packed(ref):
    return jnp.concatenate([ref[k] for k in range(PACK_CHUNKS)], axis=1)


def _store_packed(ref, packed):
    for k in range(PACK_CHUNKS):
        ref[k] = packed[:, k * LANES:(k + 1) * LANES]


def _adaln_kernel(c_ref, w_ref, b_ref, o_ref):
    s = _silu(c_ref[...]).astype(BF16)
    o_ref[...] = jnp.dot(s, w_ref[...].astype(BF16), preferred_element_type=F32) + b_ref[...]


def _adaln(cc, w_mod, b_mod):
    depth, d, n = w_mod.shape
    tn = 1024
    rows = cc.shape[0]
    return pl.pallas_call(
        _adaln_kernel,
        grid=(depth, n // tn),
        in_specs=[pl.BlockSpec((rows, d), lambda l, j: (0, 0)),
                  pl.BlockSpec((None, d, tn), lambda l, j: (l, 0, j)),
                  pl.BlockSpec((None, 1, tn), lambda l, j: (l, 0, j))],
        out_specs=pl.BlockSpec((None, rows, tn), lambda l, j: (l, 0, j)),
        out_shape=jax.ShapeDtypeStruct((depth, rows, n), F32),
        compiler_params=_params(("parallel", "parallel")),
    )(cc, w_mod, b_mod.reshape(depth, 1, n))


def _mod_specs(k, d):
    return [pl.BlockSpec((None, 1, d), lambda b, j, *_: (b, 0, k)),
            pl.BlockSpec((1, d), lambda b, j, *_: (0, k))]


def _row_is_ctx(j, tm, ctx_len):
    return (j * tm + lax.broadcasted_iota(jnp.int32, (tm, 1), 0)) < ctx_len


def _inproj_kernel(x_ref, g_ref, sc_ref, csc_ref, sh_ref, csh_ref, cos_ref, sin_ref, w_ref, o_ref, *, tm, ctx_len):
    x = x_ref[...]
    n = x * lax.rsqrt(jnp.mean(x * x, axis=-1, keepdims=True) + EPS) * g_ref[...]
    is_ctx = _row_is_ctx(pl.program_id(1), tm, ctx_len)
    scale = jnp.where(is_ctx, csc_ref[...], sc_ref[...])
    shift = jnp.where(is_ctx, csh_ref[...], sh_ref[...])
    h = (n * (1.0 + scale) + shift).astype(BF16)
    qk = jnp.dot(h, w_ref[:, :ROPE_W], preferred_element_type=F32)
    lower_half = (lax.broadcasted_iota(jnp.int32, (tm, LANES), 1) % HEAD_DIM) < (HEAD_DIM // 2)
    cos = cos_ref[...]
    sin = sin_ref[...]
    for i in range(ROPE_W // LANES):
        t = qk[:, i * LANES:(i + 1) * LANES]
        partner = jnp.where(lower_half, pltpu.roll(t, LANES - HEAD_DIM // 2, axis=1), pltpu.roll(t, HEAD_DIM // 2, axis=1))
        o_ref[:, i * LANES:(i + 1) * LANES] = (t * cos + partner * sin).astype(BF16)
    o_ref[:, ROPE_W:] = jnp.dot(h, w_ref[:, ROPE_W:], preferred_element_type=F32).astype(BF16)


def _inproj(xt, g, mod_l, mod_c, cos_t, sin_t, w, layer, ctx_len):
    B, T, D = xt.shape
    tm = _pick_tile(T, 640)
    kern = functools.partial(_inproj_kernel, tm=tm, ctx_len=ctx_len)
    return pl.pallas_call(
        kern,
        grid=(B, T // tm),
        in_specs=[pl.BlockSpec((None, tm, D), lambda b, j: (b, j, 0)),
                  pl.BlockSpec((1, D), lambda b, j: (0, 0)),
                  *_mod_specs(1, D), *_mod_specs(0, D),
                  pl.BlockSpec((tm, LANES), lambda b, j: (j, 0)),
                  pl.BlockSpec((tm, LANES), lambda b, j: (j, 0)),
                  pl.BlockSpec((None, D, PROJ_W), lambda b, j: (layer, 0, 0))],
        out_specs=pl.BlockSpec((None, tm, PROJ_W), lambda b, j: (b, j, 0)),
        out_shape=jax.ShapeDtypeStruct((B, T, PROJ_W), BF16),
        compiler_params=_params(("parallel", "parallel")),
    )(xt, g, mod_l, mod_c, mod_l, mod_c, cos_t, sin_t, w)


def _attn_kernel(sink_ref, q_ref, kc_ref, vc_ref, km_ref, vm_ref, kp_ref, vp_ref, kn_ref, vn_ref, o_ref, *, n_tiles):
    j = pl.program_id(1)
    nt_dims = (((1,), (1,)), ((), ()))
    tn_dims = (((0,), (0,)), ((), ()))

    def scores(rows, n_rows, g, k_ctx, k_win=None, mask_win=None):
        heads = [g * ATT_GROUP + hh for hh in range(ATT_GROUP)]
        q = jnp.concatenate([q_ref[rows, h * HEAD_DIM:(h + 1) * HEAD_DIM] for h in heads], axis=0)
        head_col = lax.broadcasted_iota(jnp.int32, (1, ATT_GROUP * n_rows), 1) // n_rows
        sink = jnp.full((1, ATT_GROUP * n_rows), sink_ref[heads[0]] * LOG2E, F32)
        for hh in range(1, ATT_GROUP):
            sink = jnp.where(head_col == hh, sink_ref[heads[hh]] * LOG2E, sink)
        s_c = lax.dot_general(k_ctx, q, nt_dims, preferred_element_type=F32)
        s_w = None
        if k_win is not None:
            s_w = jnp.where(mask_win, lax.dot_general(k_win, q, nt_dims, preferred_element_type=F32), -1e30)
        return sink, s_c, s_w

    def finish(item, v_ctx, v_win=None):
        sink, s_c, s_w = item
        m = jnp.maximum(jnp.max(s_c, axis=0, keepdims=True), sink)
        if s_w is not None:
            m = jnp.maximum(m, jnp.max(s_w, axis=0, keepdims=True))
        p_c = jnp.exp2(s_c - m)
        den = jnp.sum(p_c, axis=0, keepdims=True) + jnp.exp2(sink - m)
        o = lax.dot_general(v_ctx, p_c.astype(BF16), tn_dims, preferred_element_type=F32)
        if s_w is not None:
            p_w = jnp.exp2(s_w - m)
            den += jnp.sum(p_w, axis=0, keepdims=True)
            o += lax.dot_general(v_win, p_w.astype(BF16), tn_dims, preferred_element_type=F32)
        return o / den

    def store(rows, n_rows, outs):
        t = jnp.transpose(jnp.concatenate(outs, axis=0))
        for g in range(ATT_KV_HEADS):
            for hh in range(ATT_GROUP):
                h = g * ATT_GROUP + hh
                o_ref[rows, h * HEAD_DIM:(h + 1) * HEAD_DIM] = (
                    t[hh * n_rows:(hh + 1) * n_rows, g * HEAD_DIM:(g + 1) * HEAD_DIM].astype(BF16))

    def run(blocks):
        flat = [(bi, s, f) for bi, (_, _, work) in enumerate(blocks) for s, f in work]
        outs = [[] for _ in blocks]
        pending = None
        for bi, score_fn, finish_fn in flat + [(None, None, None)]:
            item = None if score_fn is None else score_fn()
            if pending is not None:
                pbi, pitem, pfinish = pending
                outs[pbi].append(pfinish(pitem))
                if len(outs[pbi]) == ATT_KV_HEADS:
                    store(blocks[pbi][0], blocks[pbi][1], outs[pbi])
            pending = None if score_fn is None else (bi, item, finish_fn)

    @pl.when(j == 0)
    def _context_queries():
        work = []
        for g in range(ATT_KV_HEADS):
            gs = slice(g * HEAD_DIM, (g + 1) * HEAD_DIM)
            work.append((functools.partial(scores, slice(0, SEQ_TILE), SEQ_TILE, g, kc_ref[:, gs]),
                         functools.partial(finish, v_ctx=vc_ref[:, gs])))
        run([(slice(0, SEQ_TILE), SEQ_TILE, work)])

    @pl.when(j > 0)
    def _latent_queries():
        w = lax.broadcasted_iota(jnp.int32, (3 * WINDOW, ATT_GROUP * WINDOW), 0)
        r = lax.broadcasted_iota(jnp.int32, (3 * WINDOW, ATT_GROUP * WINDOW), 1) % WINDOW
        band = (w >= r) & (w <= r + 2 * WINDOW)
        blocks = []
        for sub in range(SEQ_TILE // WINDOW):
            rows = slice(sub * WINDOW, (sub + 1) * WINDOW)
            if sub == 0:
                kprev, vprev = kp_ref[...], vp_ref[...]
                kmid, vmid = km_ref[:WINDOW], vm_ref[:WINDOW]
                knext, vnext = km_ref[WINDOW:], vm_ref[WINDOW:]
                w_lo = jnp.where(j > 1, 0, WINDOW)
                w_hi = 3 * WINDOW
            else:
                kprev, vprev = km_ref[:WINDOW], vm_ref[:WINDOW]
                kmid, vmid = km_ref[WINDOW:], vm_ref[WINDOW:]
                knext, vnext = kn_ref[...], vn_ref[...]
                w_lo = 0
                w_hi = jnp.where(j < n_tiles - 1, 3 * WINDOW, 2 * WINDOW)
            mask = band & (w >= w_lo) & (w < w_hi)
            work = []
            for g in range(ATT_KV_HEADS):
                gs = slice(g * HEAD_DIM, (g + 1) * HEAD_DIM)
                k_win = jnp.concatenate([kprev[:, gs], kmid[:, gs], knext[:, gs]], axis=0)
                v_win = jnp.concatenate([vprev[:, gs], vmid[:, gs], vnext[:, gs]], axis=0)
                work.append((functools.partial(scores, rows, WINDOW, g, kc_ref[:, gs], k_win, mask),
                             functools.partial(finish, v_ctx=vc_ref[:, gs], v_win=v_win)))
            blocks.append((rows, WINDOW, work))
        run(blocks)


def _attention(proj, sink, ctx_len):
    B, T, _ = proj.shape
    assert ctx_len == SEQ_TILE and T % SEQ_TILE == 0
    n_tiles = T // SEQ_TILE
    n_win = T // WINDOW
    kcol, vcol = COL_K // KV_W, COL_V // KV_W
    per_tile = SEQ_TILE // WINDOW
    tile = lambda col: pl.BlockSpec((None, SEQ_TILE, KV_W), lambda b, j: (b, j, col))
    ctx = lambda col: pl.BlockSpec((None, SEQ_TILE, KV_W), lambda b, j: (b, 0, col))
    prev = lambda col: pl.BlockSpec((None, WINDOW, KV_W), lambda b, j: (b, jnp.maximum(j * per_tile - 1, 0), col))
    nxt = lambda col: pl.BlockSpec((None, WINDOW, KV_W), lambda b, j: (b, jnp.minimum((j + 1) * per_tile, n_win - 1), col))
    return pl.pallas_call(
        functools.partial(_attn_kernel, n_tiles=n_tiles),
        grid=(B, n_tiles),
        in_specs=[pl.BlockSpec(memory_space=pltpu.SMEM),
                  pl.BlockSpec((None, SEQ_TILE, ATT_W), lambda b, j: (b, j, 0)),
                  ctx(kcol), ctx(vcol), tile(kcol), tile(vcol), prev(kcol), prev(vcol), nxt(kcol), nxt(vcol)],
        out_specs=pl.BlockSpec((None, SEQ_TILE, ATT_W), lambda b, j: (b, j, 0)),
        out_shape=jax.ShapeDtypeStruct((B, T, ATT_W), BF16),
        compiler_params=_params(("parallel", "parallel")),
    )(sink, *([proj] * 9))


def _conv_kernel(b_ref, c_ref, x_ref, w_ref, o_ref, u_scr, *, ctx_len):
    T = c_ref.shape[0]
    pad = 8
    u = c_ref[...].astype(F32) * x_ref[...].astype(F32)
    zero_row = jnp.zeros((1, CONV_CH), F32)
    u_scr[pad - 1:pad, :] = zero_row
    u_scr[pad:pad + T, :] = u
    u_scr[pad + T:pad + T + 1, :] = zero_row
    row = lax.broadcasted_iota(jnp.int32, (T, 1), 0)
    u_prev = jnp.where(row == ctx_len, 0.0, u_scr[pad - 1:pad - 1 + T, :])
    u_next = jnp.where(row == ctx_len - 1, 0.0, u_scr[pad + 1:pad + 1 + T, :])
    y = w_ref[0:1, :] * u_prev + w_ref[1:2, :] * u + w_ref[2:3, :] * u_next
    o_ref[...] = (b_ref[...].astype(F32) * y).astype(BF16)


def _short_conv(proj, conv_w, ctx_len):
    B, T, _ = proj.shape
    col = lambda c0: pl.BlockSpec((None, T, CONV_CH), lambda b: (b, 0, c0 // CONV_CH))
    return pl.pallas_call(
        functools.partial(_conv_kernel, ctx_len=ctx_len),
        grid=(B,),
        in_specs=[col(COL_CB), col(COL_CC), col(COL_CX), pl.BlockSpec((3, CONV_CH), lambda b: (0, 0))],
        out_specs=pl.BlockSpec((None, T, CONV_CH), lambda b: (b, 0, 0)),
        out_shape=jax.ShapeDtypeStruct((B, T, CONV_CH), BF16),
        scratch_shapes=[pltpu.VMEM((T + 16, CONV_CH), F32)],
        compiler_params=_params(("parallel",)),
    )(proj, proj, proj, conv_w)


def _split2(v):
    hi = v.astype(BF16)
    lo = (v - hi.astype(F32)).astype(BF16)
    return hi, lo


def _gla_kernel(q_ref, k_ref, v_ref, gate_ref, lr_ref, w2_ref, bias_ref, ng_ref, o_ref,
                oacc_scr, qs_scr, u_scr, dec_scr, sin_scr, *, reverse, n_ctx_chunks):
    T = q_ref.shape[0]
    C = GLA_CHUNK
    n_tiles = T // SEQ_TILE
    n_chunks = T // C
    per_tile = SEQ_TILE // C
    nt_dims = (((1,), (1,)), ((), ()))
    tn_dims = (((0,), (0,)), ((), ()))

    r = lax.broadcasted_iota(jnp.int32, (SEQ_TILE, SEQ_TILE), 0)
    c = lax.broadcasted_iota(jnp.int32, (SEQ_TILE, SEQ_TILE), 1)
    same_chunk = (r // C) == (c // C)
    causal = same_chunk & ((c >= r) if reverse else (c <= r))
    cum_mat = jnp.where(causal, 1.0, 0.0).astype(BF16)
    same_head = (r // GLA_DK) == (c // GLA_DK)
    head_mean = jnp.where(same_head, 1.0 / GLA_DK, 0.0).astype(BF16)
    lane_head = lax.broadcasted_iota(jnp.int32, (C, GLA_W), 1) // GLA_DK

    def prepare(i):
        rows = pl.ds(pl.multiple_of(i * SEQ_TILE, SEQ_TILE), SEQ_TILE)
        z = jnp.dot(lr_ref[rows, :], w2_ref[...], preferred_element_type=F32) + bias_ref[...]
        la = (jnp.minimum(z, 0.0) - jnp.log(1.0 + jnp.exp(-jnp.abs(z)))) / GLA_TAU
        hi, lo = _split2(la)
        b = jnp.dot(cum_mat, hi, preferred_element_type=F32) + jnp.dot(cum_mat, lo, preferred_element_type=F32)
        b3 = b.reshape(per_tile, C, GLA_W)
        total = b3[:, 0:1, :] if reverse else b3[:, C - 1:C, :]
        b_last = jnp.broadcast_to(total, (per_tile, C, GLA_W)).reshape(SEQ_TILE, GLA_W)
        b_ref = 0.5 * b_last
        q = q_ref[rows, :].astype(F32)
        k = k_ref[rows, :].astype(F32)
        e_fwd = jnp.exp(b - b_ref)
        e_bwd = jnp.exp(b_ref - b)
        e_half = jnp.exp(b_ref)
        qe = (q * e_fwd).astype(BF16)
        ke = (k * e_bwd).astype(BF16)
        ku = (k * (e_bwd * e_half)).astype(BF16)
        qs_scr[rows, :] = (q * (e_fwd * e_half)).astype(BF16)
        return rows, qe, ke, ku, e_half * e_half

    def products(i, prepared):
        rows, qe, ke, ku, dec = prepared
        v = v_ref[rows, :]
        for h in range(GLA_HEADS):
            hs = slice(h * GLA_DK, (h + 1) * GLA_DK)
            s = lax.dot_general(qe[:, hs], ke[:, hs], nt_dims, preferred_element_type=F32)
            a = jnp.where(causal, s, 0.0).astype(BF16)
            oacc_scr[rows, hs] = jnp.dot(a, v[:, hs], preferred_element_type=F32)
        for ci in range(per_tile):
            cs = slice(ci * C, (ci + 1) * C)
            full = lax.dot_general(v[cs, :], ku[cs, :], tn_dims, preferred_element_type=F32)
            ut = full[0:C, :]
            for h in range(1, GLA_HEADS):
                ut = jnp.where(lane_head == h, full[h * C:(h + 1) * C, :], ut)
            u_scr[i * per_tile + ci] = ut
            dec_scr[i * per_tile + ci] = dec[ci * C:ci * C + 1, :]

    def phase1(first, count):
        pending = None
        for t in range(count + 1):
            nxt = (first + t, prepare(first + t)) if t < count else None
            if pending is not None:
                products(*pending)
            pending = nxt

    group = 9
    lax.fori_loop(0, n_tiles // group, lambda g, c: (phase1(g * group, group), c)[1], 0)
    if n_tiles % group:
        phase1((n_tiles // group) * group, n_tiles % group)


    def phase2(i, st):
        if reverse:
            ci = jnp.where(i < n_ctx_chunks, n_ctx_chunks - 1 - i, n_chunks - 1 + n_ctx_chunks - i)
        else:
            ci = i
        sin_scr[ci] = st.astype(BF16)
        return st * dec_scr[ci] + u_scr[ci]

    lax.fori_loop(0, n_chunks, phase2, jnp.zeros((C, GLA_W), F32))

    def carried_in(i):
        rows = pl.ds(pl.multiple_of(i * SEQ_TILE, SEQ_TILE), SEQ_TILE)
        qs = qs_scr[rows, :]
        parts = []
        for ci in range(per_tile):
            st = sin_scr[i * per_tile + ci]
            st_heads = jnp.where(same_head, jnp.concatenate([st] * GLA_HEADS, axis=0), jnp.zeros((), BF16))
            parts.append(lax.dot_general(qs[ci * C:(ci + 1) * C, :], st_heads, nt_dims, preferred_element_type=F32))
        return rows, jnp.concatenate(parts, axis=0)

    def finish(rows, o_inter):
        o = oacc_scr[rows, :] + o_inter
        ms = jnp.dot((o * o).astype(BF16), head_mean, preferred_element_type=F32)
        y = o * lax.rsqrt(ms + EPS) * ng_ref[...]
        o_ref[rows, :] = (y * _silu(gate_ref[rows, :].astype(F32))).astype(BF16)

    def phase3(first, count):
        pending = None
        for t in range(count + 1):
            nxt = carried_in(first + t) if t < count else None
            if pending is not None:
                finish(*pending)
            pending = nxt

    lax.fori_loop(0, n_tiles // group, lambda g, c: (phase3(g * group, group), c)[1], 0)
    if n_tiles % group:
        phase3((n_tiles // group) * group, n_tiles % group)


def _gla(proj, w2pad, bias, ng, ctx_len, reverse):
    B, T, _ = proj.shape
    assert T % SEQ_TILE == 0 and ctx_len % GLA_CHUNK == 0
    n_chunks = T // GLA_CHUNK
    col = lambda c0: pl.BlockSpec((None, T, GLA_W), lambda b: (b, 0, c0 // GLA_W))
    gate_col = COL_GGB if reverse else COL_GGF
    kern = functools.partial(_gla_kernel, reverse=reverse, n_ctx_chunks=ctx_len // GLA_CHUNK)
    return pl.pallas_call(
        kern,
        grid=(B,),
        in_specs=[col(COL_GQ), col(COL_GK), col(COL_GV), col(gate_col),
                  pl.BlockSpec((None, T, LANES), lambda b: (b, 0, COL_LR // LANES)),
                  pl.BlockSpec((LANES, GLA_W), lambda b: (0, 0)),
                  pl.BlockSpec((1, GLA_W), lambda b: (0, 0)),
                  pl.BlockSpec((1, GLA_W), lambda b: (0, 0))],
        out_specs=pl.BlockSpec((None, T, GLA_W), lambda b: (b, 0, 0)),
        out_shape=jax.ShapeDtypeStruct((B, T, GLA_W), BF16),
        scratch_shapes=[pltpu.VMEM((T, GLA_W), F32),
                        pltpu.VMEM((T, GLA_W), BF16),
                        pltpu.VMEM((n_chunks, GLA_CHUNK, GLA_W), F32),
                        pltpu.VMEM((n_chunks, 1, GLA_W), F32),
                        pltpu.VMEM((n_chunks, GLA_CHUNK, GLA_W), BF16)],
        compiler_params=_params(("parallel",)),
    )(proj, proj, proj, proj, proj, w2pad, bias, ng)


MXU_TILE = 256
FFN_CHUNK = 3 * MXU_TILE


def _swiglu_resident(h, wgu_ref, wd_ref):
    F = wd_ref.shape[0]
    y = None
    for c0 in range(0, F, FFN_CHUNK):
        c1 = min(c0 + FFN_CHUNK, F)
        g = jnp.dot(h, wgu_ref[:, c0:c1], preferred_element_type=F32)
        u = jnp.dot(h, wgu_ref[:, F + c0:F + c1], preferred_element_type=F32)
        part = jnp.dot((_silu(g) * u).astype(BF16), wd_ref[c0:c1, :], preferred_element_type=F32)
        y = part if y is None else y + part
    return y


def _outproj_kernel(*refs, tm, ctx_len, moe):
    if moe:
        (x_ref, att_ref, gf_ref, gb_ref, cv_ref, w_ref, g1_ref, cg1_ref, ng_ref, sc_ref, csc_ref, sh_ref, csh_ref,
         router_ref, earlier_ref, xo_ref, hp_ref, route_ref, cnt_ref) = refs
    else:
        (x_ref, att_ref, gf_ref, gb_ref, cv_ref, w_ref, g1_ref, cg1_ref, ng_ref, sc_ref, csc_ref, sh_ref, csh_ref,
         g2_ref, cg2_ref, wgu_ref, wd_ref, xo_ref) = refs
    is_ctx = _row_is_ctx(pl.program_id(1), tm, ctx_len)
    gla = (gf_ref[...].astype(F32) + gb_ref[...].astype(F32)).astype(BF16)
    y = jnp.dot(att_ref[...], w_ref[:ATT_W, :], preferred_element_type=F32)
    y += jnp.dot(gla, w_ref[ATT_W:ATT_W + GLA_W, :], preferred_element_type=F32)
    y += jnp.dot(cv_ref[...], w_ref[ATT_W + GLA_W:, :], preferred_element_type=F32)
    x = x_ref[...] + jnp.where(is_ctx, cg1_ref[...], g1_ref[...]) * y
    if moe:
        xo_ref[...] = x
    n = x * lax.rsqrt(jnp.mean(x * x, axis=-1, keepdims=True) + EPS) * ng_ref[...]
    scale = jnp.where(is_ctx, csc_ref[...], sc_ref[...])
    shift = jnp.where(is_ctx, csh_ref[...], sh_ref[...])
    h = n * (1.0 + scale) + shift
    hi = h.astype(BF16)
    if not moe:
        xo_ref[...] = x + jnp.where(is_ctx, cg2_ref[...], g2_ref[...]) * _swiglu_resident(hi, wgu_ref, wd_ref)
    else:
        _store_packed(hp_ref, _pack_bf16_pairs(h))
        logits = jnp.dot(hi, router_ref[...], preferred_element_type=F32)
        lane = lax.broadcasted_iota(jnp.int32, logits.shape, 1)
        neg = -jnp.inf
        lg = jnp.where(lane < N_EXPERTS, logits, neg)
        m1 = jnp.max(lg, axis=-1, keepdims=True)
        i1 = jnp.min(jnp.where(lg == m1, lane, LANES), axis=-1, keepdims=True)
        lg2 = jnp.where(lane == i1, neg, lg)
        m2 = jnp.max(lg2, axis=-1, keepdims=True)
        i2 = jnp.min(jnp.where(lg2 == m2, lane, LANES), axis=-1, keepdims=True)
        e2 = jnp.exp(m2 - m1)
        gate1 = 1.0 / (1.0 + e2)
        gate2 = e2 / (1.0 + e2)
        first = jnp.where(lane == i1, 1.0, 0.0)
        second = jnp.where(lane == i2, 1.0, 0.0)
        before = jnp.dot(earlier_ref[...], jnp.concatenate([first, second], axis=1).astype(BF16),
                         preferred_element_type=F32)
        n_first = jnp.sum(first, axis=0, keepdims=True)
        rank1 = jnp.sum(first * before[:, :LANES], axis=-1, keepdims=True)
        rank2 = jnp.sum(second * (before[:, LANES:] + n_first), axis=-1, keepdims=True)
        cnt_ref[...] = jnp.broadcast_to(n_first + jnp.sum(second, axis=0, keepdims=True), cnt_ref.shape)
        route_ref[...] = jnp.where(lane == 0, i1.astype(F32), jnp.where(lane == 1, i2.astype(F32),
                                   jnp.where(lane == 2, gate1, jnp.where(lane == 3, gate2,
                                             jnp.where(lane == 4, rank1, jnp.where(lane == 5, rank2, 0.0))))))


def _outproj(xt, att, gf, gb, cv, w_out, layer, ng, mod_l, mod_c, ctx_len, router=None, ffn=None):
    B, T, D = xt.shape
    tm = _pick_tile(T, 640)
    moe = router is not None
    assert moe != (ffn is not None)
    row = lambda w: pl.BlockSpec((None, tm, w), lambda b, j: (b, j, 0))
    const = lambda shape: pl.BlockSpec(shape, lambda b, j: (0,) * len(shape))
    in_specs = [row(D), row(ATT_W), row(GLA_W), row(GLA_W), row(CONV_CH),
                pl.BlockSpec((None, D, D), lambda b, j: (layer, 0, 0)),
                *_mod_specs(2, D), const((1, D)), *_mod_specs(4, D), *_mod_specs(3, D)]
    args = [xt, att, gf, gb, cv, w_out, mod_l, mod_c, ng, mod_l, mod_c, mod_l, mod_c]
    nt = T // tm
    if moe:
        in_specs += [pl.BlockSpec((None, D, LANES), lambda b, j: (router[1], 0, 0)), const((tm, tm))]
        args += [router[0], jnp.tri(tm, k=-1, dtype=BF16)]
        out_specs = [row(D), pl.BlockSpec((PACK_CHUNKS, tm, LANES), lambda b, j: (0, b * nt + j, 0)), row(LANES),
                     pl.BlockSpec((None, 8, LANES), lambda b, j: (b * nt + j, 0, 0))]
        out_shape = [jax.ShapeDtypeStruct((B, T, D), F32), jax.ShapeDtypeStruct((PACK_CHUNKS, B * T, LANES), jnp.int32),
                     jax.ShapeDtypeStruct((B, T, LANES), F32), jax.ShapeDtypeStruct((B * nt, 8, LANES), F32)]
    else:
        w_gu, w_down, idx = ffn
        F = w_down.shape[1]
        assert F % MXU_TILE == 0
        resident = lambda shape: pl.BlockSpec((None,) + shape, lambda b, j: (idx, 0, 0), pipeline_mode=pl.Buffered(1))
        in_specs += [*_mod_specs(5, D), resident((D, 2 * F)), resident((F, D))]
        args += [mod_l, mod_c, w_gu, w_down]
        out_specs = row(D)
        out_shape = jax.ShapeDtypeStruct((B, T, D), F32)
    return pl.pallas_call(
        functools.partial(_outproj_kernel, tm=tm, ctx_len=ctx_len, moe=moe),
        grid=(B, T // tm),
        in_specs=in_specs, out_specs=out_specs, out_shape=out_shape,
        compiler_params=_params(("parallel", "parallel")),
    )(*args)


def _sc_gather(table, idx):
    _, width = table.shape
    m = idx.shape[0]
    workers = SC_CORES * SC_SUBCORES
    ch = SC_INDEX_CHUNK
    assert width == LANES and m % (workers * ch) == 0
    n_chunks = m // (workers * ch)
    nb = max(b for b in (4, 3, 2, 1) if n_chunks % b == 0)
    mesh = plsc.VectorSubcoreMesh(core_axis_name="c", subcore_axis_name="s",
                                  num_cores=SC_CORES, num_subcores=SC_SUBCORES)

    @functools.partial(
        pl.kernel, mesh=mesh, out_type=jax.ShapeDtypeStruct((m, width), table.dtype),
        scratch_types=[pltpu.VMEM((n_chunks, ch), jnp.int32), pltpu.VMEM((nb, ch, width), table.dtype)]
        + [pltpu.SemaphoreType.DMA] * (2 * nb))
    def gather(table_hbm, idx_hbm, out_hbm, idx_v, rows_v, *sems):
        wid = lax.axis_index("s") * SC_CORES + lax.axis_index("c")
        base = wid * (n_chunks * ch)
        pltpu.sync_copy(idx_hbm.at[wid], idx_v)

        def fetch(c, b):
            return pltpu.make_async_copy(table_hbm.at[idx_v.at[c]], rows_v.at[b], sems[b])

        def flush(c, b):
            return pltpu.make_async_copy(rows_v.at[b], out_hbm.at[pl.ds(base + c * ch, ch)], sems[nb + b])

        @pl.loop(0, n_chunks, step=nb)
        def _(c0):
            for b in range(nb):
                @pl.when(c0 > 0)
                def _():
                    flush(c0 - nb + b, b).wait()
                fetch(c0 + b, b).start()
            for b in range(nb):
                fetch(c0 + b, b).wait()
                flush(c0 + b, b).start()

        for b in range(nb):
            flush(n_chunks - nb + b, b).wait()

    return gather(table, idx.reshape(workers, n_chunks, ch))


def _sc_dispatch(table, dest_a, dest_b, dest_pad, n_out):
    m, width = table.shape
    n_pad = dest_pad.shape[0]
    workers = SC_CORES * SC_SUBCORES
    ch = SC_INDEX_CHUNK
    assert width == LANES and m % (workers * ch) == 0 and n_pad % (workers * ch) == 0
    n_chunks = m // (workers * ch)
    pad_chunks = n_pad // (workers * ch)
    nb = max(b for b in (4, 3, 2, 1) if n_chunks % b == 0)
    mesh = plsc.VectorSubcoreMesh(core_axis_name="c", subcore_axis_name="s",
                                  num_cores=SC_CORES, num_subcores=SC_SUBCORES)

    @functools.partial(
        pl.kernel, mesh=mesh, out_type=jax.ShapeDtypeStruct((n_out, width), table.dtype),
        scratch_types=[pltpu.VMEM((n_chunks, ch), jnp.int32), pltpu.VMEM((n_chunks, ch), jnp.int32),
                       pltpu.VMEM((pad_chunks, ch), jnp.int32), pltpu.VMEM((nb, ch, width), table.dtype),
                       pltpu.VMEM((ch, width), table.dtype)]
        + [pltpu.SemaphoreType.DMA] * (3 * nb + 1))
    def dispatch(table_hbm, zeros_hbm, da_hbm, db_hbm, dp_hbm, out_hbm, da_v, db_v, dp_v, rows_v, zero_v, *sems):
        wid = lax.axis_index("s") * SC_CORES + lax.axis_index("c")
        base = wid * (n_chunks * ch)
        pltpu.sync_copy(da_hbm.at[wid], da_v)
        pltpu.sync_copy(db_hbm.at[wid], db_v)
        pltpu.sync_copy(dp_hbm.at[wid], dp_v)
        pltpu.sync_copy(zeros_hbm, zero_v)

        def fetch(c, b):
            return pltpu.make_async_copy(table_hbm.at[pl.ds(base + c * ch, ch)], rows_v.at[b], sems[b])

        def put_a(c, b):
            return pltpu.make_async_copy(rows_v.at[b], out_hbm.at[da_v.at[c]], sems[nb + b])

        def put_b(c, b):
            return pltpu.make_async_copy(rows_v.at[b], out_hbm.at[db_v.at[c]], sems[2 * nb + b])

        def put_zero(p):
            return pltpu.make_async_copy(zero_v, out_hbm.at[dp_v.at[p]], sems[3 * nb])

        for p in range(pad_chunks):
            put_zero(p).start()

        @pl.loop(0, n_chunks, step=nb)
        def _(c0):
            for b in range(nb):
                @pl.when(c0 > 0)
                def _():
                    put_a(c0 - nb + b, b).wait()
                    put_b(c0 - nb + b, b).wait()
                fetch(c0 + b, b).start()
            for b in range(nb):
                fetch(c0 + b, b).wait()
                put_a(c0 + b, b).start()
                put_b(c0 + b, b).start()

        for b in range(nb):
            put_a(n_chunks - nb + b, b).wait()
            put_b(n_chunks - nb + b, b).wait()
        for p in range(pad_chunks):
            put_zero(p).wait()

    shape3 = lambda v: v.reshape(workers, -1, ch)
    return dispatch(table, jnp.zeros((ch, width), table.dtype), shape3(dest_a), shape3(dest_b), shape3(dest_pad))


def _routing(route, tile_counts, tile):
    B, T, _ = route.shape
    n = B * T
    assert (2 * n) % tile == 0
    info = route.reshape(n, LANES)[:, :6].astype(jnp.int32)
    per_tile = tile_counts[:, 0, :N_EXPERTS].astype(jnp.int32)
    counts = jnp.sum(per_tile, axis=0)
    padded = (counts + tile - 1) // tile * tile
    ends = jnp.cumsum(padded)
    starts = ends - padded
    base = jnp.repeat(starts[None, :] + jnp.cumsum(per_tile, axis=0) - per_tile, n // per_tile.shape[0], axis=0)
    experts = jnp.arange(N_EXPERTS, dtype=jnp.int32)[None, :]
    row_of = lambda e, rank: jnp.sum(jnp.where(e[:, None] == experts, base, 0), axis=1) + rank
    dest = jnp.concatenate([row_of(info[:, 0], info[:, 4]), row_of(info[:, 1], info[:, 5])])
    n_pad = N_EXPERTS * tile
    n_rows = 2 * n + n_pad
    seg_first = jnp.concatenate([starts + counts, ends[-1:]])
    seg_size = jnp.concatenate([padded - counts, (n_rows - ends[-1])[None]])
    seg_end = jnp.cumsum(seg_size)
    j = jnp.arange(n_pad, dtype=jnp.int32)
    seg = jnp.sum((j[:, None] >= seg_end[None, :]).astype(jnp.int32), axis=1)
    pad_rows = (seg_first[seg] + j - (seg_end - seg_size)[seg]).astype(jnp.int32)
    tile_start = jnp.arange(n_rows // tile, dtype=jnp.int32) * tile
    tile_expert = jnp.minimum(jnp.sum((tile_start[:, None] >= ends[None, :]).astype(jnp.int32), axis=1), N_EXPERTS - 1)
    tile_valid = (tile_start < ends[-1]).astype(jnp.int32)
    return dest, pad_rows, n_rows, tile_expert, tile_valid


def _expert_kernel(te_ref, tv_ref, xs_ref, wgu_ref, wd_ref, ys_ref, wgu_scr):
    i = pl.program_id(0)
    F = wd_ref.shape[0]

    @pl.when((i == 0) | (te_ref[i] != te_ref[jnp.maximum(i - 1, 0)]))
    def _new_expert():
        wgu_scr[...] = wgu_ref[...].astype(BF16)

    @pl.when(tv_ref[i] > 0)
    def _compute():
        lo, hi = _unpack_bf16_pairs(_load_packed(xs_ref))
        gu = (jnp.dot(lo.astype(BF16), wgu_scr[:PACK_W, :], preferred_element_type=F32)
              + jnp.dot(hi.astype(BF16), wgu_scr[PACK_W:, :], preferred_element_type=F32))
        act = (_silu(gu[:, :F]) * gu[:, F:]).astype(BF16)
        y = jnp.dot(act, wd_ref[...], preferred_element_type=F32)
        _store_packed(ys_ref, _pack_bf16_pairs(y))

    @pl.when(tv_ref[i] == 0)
    def _unused_tile():
        ys_ref[...] = jnp.zeros_like(ys_ref)


def _experts(xs, tile_expert, tile_valid, w_gu, layer, w_down):
    _, n_rows, _ = xs.shape
    _, _, F, D = w_down.shape
    tile = EXPERT_TILE
    rows = pl.BlockSpec((PACK_CHUNKS, tile, LANES), lambda i, te, tv: (0, i, 0))
    return pl.pallas_call(
        _expert_kernel,
        grid_spec=pltpu.PrefetchScalarGridSpec(
            num_scalar_prefetch=2,
            grid=(n_rows // tile,),
            in_specs=[rows,
                      pl.BlockSpec((None, None, D, 2 * F), lambda i, te, tv: (layer, te[i], 0, 0)),
                      pl.BlockSpec((None, None, F, D), lambda i, te, tv: (0, te[i], 0, 0))],
            out_specs=rows,
            scratch_shapes=[pltpu.VMEM((D, 2 * F), BF16)]),
        out_shape=jax.ShapeDtypeStruct(xs.shape, jnp.int32),
        compiler_params=_params(("arbitrary",)),
    )(tile_expert, tile_valid, xs, w_gu, w_down)


def _combine_kernel(*refs, tm, ctx_len, skip, final):
    x_ref, g2_ref, cg2_ref, route_ref, ya_ref, yb_ref = refs[:6]
    fg_ref = refs[6] if final else None
    o_ref = refs[-1]
    is_ctx = _row_is_ctx(pl.program_id(1) + skip, tm, ctx_len)
    route = route_ref[...]
    lane = lax.broadcasted_iota(jnp.int32, route.shape, 1)
    gate_a = jnp.sum(jnp.where(lane == 2, route, 0.0), axis=-1, keepdims=True)
    gate_b = jnp.sum(jnp.where(lane == 3, route, 0.0), axis=-1, keepdims=True)
    unpack = lambda ref: jnp.concatenate(_unpack_bf16_pairs(_load_packed(ref)), axis=1)
    f = gate_a * unpack(ya_ref) + gate_b * unpack(yb_ref)
    x = x_ref[...] + jnp.where(is_ctx, cg2_ref[...], g2_ref[...]) * f
    if final:
        x = x * lax.rsqrt(jnp.mean(x * x, axis=-1, keepdims=True) + EPS) * fg_ref[...]
    o_ref[...] = x


def _combine(xt, mod_l, mod_c, route, yg, ctx_len, final_g=None):
    B, T, D = xt.shape
    final = final_g is not None
    if final:
        tm = SEQ_TILE
        assert ctx_len % tm == 0 and T % tm == 0
        skip = ctx_len // tm
    else:
        tm = _pick_tile(T, 640)
        skip = 0
    nt = T // tm
    row = lambda w: pl.BlockSpec((None, tm, w), lambda b, j: (b, j + skip, 0))
    slot = lambda s: pl.BlockSpec((PACK_CHUNKS, None, tm, LANES), lambda b, j: (0, s, b * nt + j + skip, 0))
    in_specs = [row(D), *_mod_specs(5, D), row(LANES), slot(0), slot(1)]
    args = [xt, mod_l, mod_c, route, yg, yg]
    if final:
        in_specs.append(pl.BlockSpec((1, D), lambda b, j: (0, 0)))
        args.append(final_g)
    return pl.pallas_call(
        functools.partial(_combine_kernel, tm=tm, ctx_len=ctx_len, skip=skip, final=final),
        grid=(B, nt - skip),
        in_specs=in_specs,
        out_specs=pl.BlockSpec((None, tm, D), lambda b, j: (b, j, 0)),
        out_shape=jax.ShapeDtypeStruct((B, T - skip * tm, D), F32),
        compiler_params=_params(("parallel", "parallel")),
    )(*args)


def _moe(hp, route, tile_counts, xt, mod_l, mod_c, w_gu, layer, w_down, ctx_len, final_g=None):
    B, T, _ = xt.shape
    n = B * T
    dest, pad_rows, n_rows, tile_expert, tile_valid = _routing(route, tile_counts, EXPERT_TILE)
    chunk = jnp.arange(PACK_CHUNKS, dtype=jnp.int32)[:, None] * n_rows
    xs = _sc_dispatch(hp.reshape(PACK_CHUNKS * n, LANES), (chunk + dest[None, :n]).reshape(-1),
                      (chunk + dest[None, n:]).reshape(-1), (chunk + pad_rows[None, :]).reshape(-1),
                      PACK_CHUNKS * n_rows)
    ys = _experts(xs.reshape(PACK_CHUNKS, n_rows, LANES), tile_expert, tile_valid, w_gu, layer, w_down)
    ys = ys.reshape(PACK_CHUNKS * n_rows, LANES)
    yg = _sc_gather(ys, (chunk + dest[None, :]).reshape(-1)).reshape(PACK_CHUNKS, 2, n, LANES)
    return _combine(xt, mod_l, mod_c, route, yg, ctx_len, final_g)


def _final_norm_kernel(x_ref, g_ref, o_ref):
    x = x_ref[...]
    o_ref[...] = x * lax.rsqrt(jnp.mean(x * x, axis=-1, keepdims=True) + EPS) * g_ref[...]


def _final_norm(xt, g, ctx_len):
    B, T, D = xt.shape
    S = T - ctx_len
    tm = SEQ_TILE
    assert ctx_len % tm == 0 and S % tm == 0
    skip = ctx_len // tm
    return pl.pallas_call(
        _final_norm_kernel,
        grid=(B, S // tm),
        in_specs=[pl.BlockSpec((None, tm, D), lambda b, j: (b, j + skip, 0)),
                  pl.BlockSpec((1, D), lambda b, j: (0, 0))],
        out_specs=pl.BlockSpec((None, tm, D), lambda b, j: (b, j, 0)),
        out_shape=jax.ShapeDtypeStruct((B, S, D), F32),
        compiler_params=_params(("parallel", "parallel")),
    )(xt, g)


def _rope_tables(seq, ctx_len):
    pos = jnp.arange(seq, dtype=jnp.int32)
    nf = HEAD_DIM // 4
    inv = ROPE_THETA ** (-jnp.arange(nf, dtype=F32) / nf)
    ang = jnp.concatenate([(pos // GRID_W).astype(F32)[:, None] * inv, (pos % GRID_W).astype(F32)[:, None] * inv], axis=-1)
    cos, sin = jnp.cos(ang), jnp.sin(ang)
    reps = LANES // HEAD_DIM
    cos_t = jnp.tile(jnp.concatenate([cos, cos], axis=-1), (1, reps))
    sin_t = jnp.tile(jnp.concatenate([-sin, sin], axis=-1), (1, reps))
    cos_t = jnp.concatenate([jnp.ones((ctx_len, LANES), F32), cos_t], axis=0)
    sin_t = jnp.concatenate([jnp.zeros((ctx_len, LANES), F32), sin_t], axis=0)
    return cos_t, sin_t


def _relayout_w_in(w_in):
    o_lr = ATT_W + 2 * KV_W + 5 * GLA_W
    q_scale = HEAD_DIM ** -0.5 * LOG2E
    gq_scale = GLA_DK ** -0.5
    parts = [w_in[..., :ATT_W] * q_scale, w_in[..., ATT_W:COL_GQ], w_in[..., COL_GQ:COL_GK] * gq_scale,
             w_in[..., COL_GK:o_lr], w_in[..., o_lr + 2 * GLA_LOWRANK:], w_in[..., o_lr:o_lr + 2 * GLA_LOWRANK],
             jnp.zeros(w_in.shape[:-1] + (LANES - 2 * GLA_LOWRANK,), w_in.dtype)]
    return jnp.concatenate(parts, axis=-1).astype(BF16)


def kernel(x, c, ctx, c_ctx, w_mod, b_mod, norm1_g, norm2_g, w_in, w_out, attn_sink, gla_w2_f, gla_b_f, gla_w2_b, gla_b_b, gla_norm_g, conv_w, ffn_w_gu, ffn_w_down, router_w, expert_w_gu, expert_w_down, final_norm_g):
    B, S, D = x.shape
    L = ctx.shape[1]
    depth = w_in.shape[0]
    assert D == D_MODEL and S % GRID_W == 0

    xt = jnp.concatenate([ctx, x], axis=1)
    rows = 16
    cc = jnp.concatenate([c, c_ctx[None, :], jnp.zeros((rows - B - 1, D), F32)], axis=0)
    mod = _adaln(cc, w_mod, b_mod)
    cos_t, sin_t = _rope_tables(S, L)

    w_in_p = _relayout_w_in(w_in)
    w_out_b = _to_bf16(w_out)
    zpad = jnp.zeros((depth, LANES - 2 * GLA_LOWRANK, GLA_W), F32)
    w2_f = jnp.concatenate([gla_w2_f, jnp.zeros_like(gla_w2_b), zpad], axis=1).astype(BF16)
    w2_b = jnp.concatenate([jnp.zeros_like(gla_w2_f), gla_w2_b, zpad], axis=1).astype(BF16)
    gla_ng = jnp.tile(gla_norm_g, (1, GLA_HEADS))
    dense_w = {0: (_to_bf16(ffn_w_gu, lead=0), _to_bf16(ffn_w_down, lead=0))}
    router_b = jnp.pad(router_w, ((0, 0), (0, 0), (0, LANES - N_EXPERTS))).astype(BF16)

    for l in range(depth):
        mod_l = mod[l, :B].reshape(B, 1, 6 * D)
        mod_c = mod[l, B:B + 1]
        proj = _inproj(xt, norm1_g[l][None, :], mod_l, mod_c, cos_t, sin_t, w_in_p, l, L)
        att = _attention(proj, attn_sink[l], L)
        cv = _short_conv(proj, conv_w[l], L)
        gf = _gla(proj, w2_f[l], gla_b_f[l][None, :], gla_ng[l][None, :], L, reverse=False)
        gb = _gla(proj, w2_b[l], gla_b_b[l][None, :], gla_ng[l][None, :], L, reverse=True)
        if l % 2 == 0:
            xt = _outproj(xt, att, gf, gb, cv, w_out_b, l, norm2_g[l][None, :], mod_l, mod_c, L,
                          ffn=(*dense_w.pop(l), 0))
        else:
            xt, hp, route, tile_counts = _outproj(xt, att, gf, gb, cv, w_out_b, l, norm2_g[l][None, :], mod_l, mod_c, L,
                                     router=(router_b, l // 2))
            exp_down_b = _to_bf16(expert_w_down, lead=l // 2, after=route)
            if l + 1 < depth:
                dense_w[l + 1] = (_to_bf16(ffn_w_gu, lead=(l + 1) // 2, after=route),
                                  _to_bf16(ffn_w_down, lead=(l + 1) // 2, after=route))
            xt = _moe(hp, route, tile_counts, xt, mod_l, mod_c, expert_w_gu, l // 2, exp_down_b, L,
                      final_g=final_norm_g[None, :] if l == depth - 1 else None)
    return xt if depth % 2 == 0 else _final_norm(xt, final_norm_g[None, :], L)
```

```python
import functools

import jax
import jax.numpy as jnp
from jax import lax
from jax.experimental import pallas as pl
from jax.experimental.pallas import tpu as pltpu
from jax.experimental.pallas import tpu_sc as plsc

F32 = jnp.float32
BF16 = jnp.bfloat16

D_MODEL = 1024
EPS = 1e-6
GRID_W = 64
ROPE_THETA = 10000.0
LOG2E = 1.4426950408889634
HEAD_DIM = 64
ATT_HEADS = 8
ATT_KV_HEADS = 2
ATT_GROUP = ATT_HEADS // ATT_KV_HEADS
ATT_W = ATT_HEADS * HEAD_DIM
KV_W = ATT_KV_HEADS * HEAD_DIM
WINDOW = 128
GLA_HEADS = 4
GLA_DK = 64
GLA_W = GLA_HEADS * GLA_DK
GLA_LOWRANK = 16
GLA_TAU = 16.0
GLA_CHUNK = 64
CONV_CH = 256
N_EXPERTS = 8

LANES = 128
SUBLANES_BF16 = 16

COL_Q = 0
COL_K = ATT_W
COL_V = COL_K + KV_W
ROPE_W = COL_V
COL_GQ = COL_V + KV_W
COL_GK = COL_GQ + GLA_W
COL_GV = COL_GK + GLA_W
COL_GGF = COL_GV + GLA_W
COL_GGB = COL_GGF + GLA_W
COL_CB = COL_GGB + GLA_W
COL_CC = COL_CB + CONV_CH
COL_CX = COL_CC + CONV_CH
COL_LR = COL_CX + CONV_CH
PROJ_W = COL_LR + LANES

SEQ_TILE = 256
VMEM_LIMIT = 56 * 1024 * 1024
CAST_BLOCK_BYTES = 4 * 1024 * 1024

PACK_W = D_MODEL // 2
PACK_CHUNKS = PACK_W // LANES
EXPERT_TILE = 512
SC_CORES = 2
SC_SUBCORES = 16
SC_INDEX_CHUNK = 128


def _silu(v):
    return v / (1.0 + jnp.exp(-v))


def _pick_tile(total, cap):
    best = None
    for t in range(SUBLANES_BF16, cap + 1, SUBLANES_BF16):
        if total % t == 0:
            best = t
    assert best is not None
    return best


def _params(sem):
    return pltpu.CompilerParams(dimension_semantics=sem, vmem_limit_bytes=VMEM_LIMIT)


def _pack_bf16_pairs(v):
    bits = lambda t: lax.bitcast_convert_type(t.astype(BF16).astype(F32), jnp.int32)
    return ((bits(v[:, :PACK_W]) >> 16) & 0xFFFF) | (bits(v[:, PACK_W:]) & -65536)


def _unpack_bf16_pairs(w):
    return lax.bitcast_convert_type(w << 16, F32), lax.bitcast_convert_type(w & -65536, F32)


CAST_STREAMS = 4


def _cast_kernel(*refs):
    o_ref = refs[-1]
    slab = refs[0].shape[0]
    for s, x_ref in enumerate(refs[:-1]):
        o_ref[s * slab:(s + 1) * slab, :] = x_ref[...].astype(BF16)


def _to_bf16(w, lead=None, after=None):
    cols = w.shape[-1]
    rows_in = w.size // cols
    rows = rows_in if lead is None else rows_in // w.shape[0]
    tr = _pick_tile(rows, max(SUBLANES_BF16, CAST_BLOCK_BYTES // (4 * cols)))
    streams = CAST_STREAMS if tr % (CAST_STREAMS * SUBLANES_BF16) == 0 else 1
    slab = tr // streams
    first = 0 if lead is None else lead * (rows // slab)
    w2 = w.reshape(rows_in, cols)
    if after is not None:
        w2, _ = lax.optimization_barrier((w2, after))
    out = pl.pallas_call(
        _cast_kernel,
        grid=(rows // tr,),
        in_specs=[pl.BlockSpec((slab, cols), lambda i, s=s: (first + i * streams + s, 0)) for s in range(streams)],
        out_specs=pl.BlockSpec((tr, cols), lambda i: (i, 0)),
        out_shape=jax.ShapeDtypeStruct((rows, cols), BF16),
        compiler_params=_params(("parallel",)),
    )(*([w2] * streams))
    return out.reshape(w.shape if lead is None else (1,) + w.shape[1:])


def _load_packed(ref):
    return jnp.concatenate([ref[k] for k in range(PACK_CHUNKS)], axis=1)


def _store_packed(ref, packed):
    for k in range(PACK_CHUNKS):
        ref[k] = packed[:, k * LANES:(k + 1) * LANES]


def _adaln_kernel(c_ref, w_ref, b_ref, o_ref):
    s = _silu(c_ref[...]).astype(BF16)
    o_ref[...] = jnp.dot(s, w_ref[...].astype(BF16), preferred_element_type=F32) + b_ref[...]


def _adaln(cc, w_mod, b_mod):
    depth, d, n = w_mod.shape
    tn = 1024
    rows = cc.shape[0]
    return pl.pallas_call(
        _adaln_kernel,
        grid=(depth, n // tn),
        in_specs=[pl.BlockSpec((rows, d), lambda l, j: (0, 0)),
                  pl.BlockSpec((None, d, tn), lambda l, j: (l, 0, j)),
                  pl.BlockSpec((None, 1, tn), lambda l, j: (l, 0, j))],
        out_specs=pl.BlockSpec((None, rows, tn), lambda l, j: (l, 0, j)),
        out_shape=jax.ShapeDtypeStruct((depth, rows, n), F32),
        compiler_params=_params(("parallel", "parallel")),
    )(cc, w_mod, b_mod.reshape(depth, 1, n))


def _mod_specs(k, d):
    return [pl.BlockSpec((None, 1, d), lambda b, j, *_: (b, 0, k)),
            pl.BlockSpec((1, d), lambda b, j, *_: (0, k))]


def _row_is_ctx(j, tm, ctx_len):
    return (j * tm + lax.broadcasted_iota(jnp.int32, (tm, 1), 0)) < ctx_len


def _inproj_kernel(x_ref, g_ref, sc_ref, csc_ref, sh_ref, csh_ref, cos_ref, sin_ref, w_ref, o_ref, *, tm, ctx_len):
    x = x_ref[...]
    n = x * lax.rsqrt(jnp.mean(x * x, axis=-1, keepdims=True) + EPS) * g_ref[...]
    is_ctx = _row_is_ctx(pl.program_id(1), tm, ctx_len)
    scale = jnp.where(is_ctx, csc_ref[...], sc_ref[...])
    shift = jnp.where(is_ctx, csh_ref[...], sh_ref[...])
    h = (n * (1.0 + scale) + shift).astype(BF16)
    qk = jnp.dot(h, w_ref[:, :ROPE_W], preferred_element_type=F32)
    lower_half = (lax.broadcasted_iota(jnp.int32, (tm, LANES), 1) % HEAD_DIM) < (HEAD_DIM // 2)
    cos = cos_ref[...]
    sin = sin_ref[...]
    for i in range(ROPE_W // LANES):
        t = qk[:, i * LANES:(i + 1) * LANES]
        partner = jnp.where(lower_half, pltpu.roll(t, LANES - HEAD_DIM // 2, axis=1), pltpu.roll(t, HEAD_DIM // 2, axis=1))
        o_ref[:, i * LANES:(i + 1) * LANES] = (t * cos + partner * sin).astype(BF16)
    o_ref[:, ROPE_W:] = jnp.dot(h, w_ref[:, ROPE_W:], preferred_element_type=F32).astype(BF16)


def _inproj(xt, g, mod_l, mod_c, cos_t, sin_t, w, layer, ctx_len):
    B, T, D = xt.shape
    tm = _pick_tile(T, 640)
    kern = functools.partial(_inproj_kernel, tm=tm, ctx_len=ctx_len)
    return pl.pallas_call(
        kern,
        grid=(B, T // tm),
        in_specs=[pl.BlockSpec((None, tm, D), lambda b, j: (b, j, 0)),
                  pl.BlockSpec((1, D), lambda b, j: (0, 0)),
                  *_mod_specs(1, D), *_mod_specs(0, D),
                  pl.BlockSpec((tm, LANES), lambda b, j: (j, 0)),
                  pl.BlockSpec((tm, LANES), lambda b, j: (j, 0)),
                  pl.BlockSpec((None, D, PROJ_W), lambda b, j: (layer, 0, 0))],
        out_specs=pl.BlockSpec((None, tm, PROJ_W), lambda b, j: (b, j, 0)),
        out_shape=jax.ShapeDtypeStruct((B, T, PROJ_W), BF16),
        compiler_params=_params(("parallel", "parallel")),
    )(xt, g, mod_l, mod_c, mod_l, mod_c, cos_t, sin_t, w)


def _attn_kernel(sink_ref, q_ref, kc_ref, vc_ref, km_ref, vm_ref, kp_ref, vp_ref, kn_ref, vn_ref, o_ref, *, n_tiles):
    j = pl.program_id(1)
    nt_dims = (((1,), (1,)), ((), ()))
    tn_dims = (((0,), (0,)), ((), ()))

    def scores(rows, n_rows, g, k_ctx, k_win=None, mask_win=None):
        heads = [g * ATT_GROUP + hh for hh in range(ATT_GROUP)]
        q = jnp.concatenate([q_ref[rows, h * HEAD_DIM:(h + 1) * HEAD_DIM] for h in heads], axis=0)
        head_col = lax.broadcasted_iota(jnp.int32, (1, ATT_GROUP * n_rows), 1) // n_rows
        sink = jnp.full((1, ATT_GROUP * n_rows), sink_ref[heads[0]] * LOG2E, F32)
        for hh in range(1, ATT_GROUP):
            sink = jnp.where(head_col == hh, sink_ref[heads[hh]] * LOG2E, sink)
        s_c = lax.dot_general(k_ctx, q, nt_dims, preferred_element_type=F32)
        s_w = None
        if k_win is not None:
            s_w = jnp.where(mask_win, lax.dot_general(k_win, q, nt_dims, preferred_element_type=F32), -1e30)
        return sink, s_c, s_w

    def finish(item, v_ctx, v_win=None):
        sink, s_c, s_w = item
        m = jnp.maximum(jnp.max(s_c, axis=0, keepdims=True), sink)
        if s_w is not None:
            m = jnp.maximum(m, jnp.max(s_w, axis=0, keepdims=True))
        p_c = jnp.exp2(s_c - m)
        den = jnp.sum(p_c, axis=0, keepdims=True) + jnp.exp2(sink - m)
        o = lax.dot_general(v_ctx, p_c.astype(BF16), tn_dims, preferred_element_type=F32)
        if s_w is not None:
            p_w = jnp.exp2(s_w - m)
            den += jnp.sum(p_w, axis=0, keepdims=True)
            o += lax.dot_general(v_win, p_w.astype(BF16), tn_dims, preferred_element_type=F32)
        return o / den

    def store(rows, n_rows, outs):
        t = jnp.transpose(jnp.concatenate(outs, axis=0))
        for g in range(ATT_KV_HEADS):
            for hh in range(ATT_GROUP):
                h = g * ATT_GROUP + hh
                o_ref[rows, h * HEAD_DIM:(h + 1) * HEAD_DIM] = (
                    t[hh * n_rows:(hh + 1) * n_rows, g * HEAD_DIM:(g + 1) * HEAD_DIM].astype(BF16))

    PIPE_DEPTH = 2

    def run(blocks):
        flat = [(bi, s, f) for bi, (_, _, work) in enumerate(blocks) for s, f in work]
        outs = [[] for _ in blocks]
        pending = []
        for bi, score_fn, finish_fn in flat + [(None, None, None)] * PIPE_DEPTH:
            if score_fn is not None:
                pending.append((bi, score_fn(), finish_fn))
            if len(pending) > PIPE_DEPTH or (score_fn is None and pending):
                pbi, pitem, pfinish = pending.pop(0)
                outs[pbi].append(pfinish(pitem))
                if len(outs[pbi]) == ATT_KV_HEADS:
                    store(blocks[pbi][0], blocks[pbi][1], outs[pbi])

    @pl.when(j == 0)
    def _context_queries():
        work = []
        for g in range(ATT_KV_HEADS):
            gs = slice(g * HEAD_DIM, (g + 1) * HEAD_DIM)
            work.append((functools.partial(scores, slice(0, SEQ_TILE), SEQ_TILE, g, kc_ref[:, gs]),
                         functools.partial(finish, v_ctx=vc_ref[:, gs])))
        run([(slice(0, SEQ_TILE), SEQ_TILE, work)])

    @pl.when(j > 0)
    def _latent_queries():
        w = lax.broadcasted_iota(jnp.int32, (3 * WINDOW, ATT_GROUP * WINDOW), 0)
        r = lax.broadcasted_iota(jnp.int32, (3 * WINDOW, ATT_GROUP * WINDOW), 1) % WINDOW
        band = (w >= r) & (w <= r + 2 * WINDOW)
        blocks = []
        for sub in range(SEQ_TILE // WINDOW):
            rows = slice(sub * WINDOW, (sub + 1) * WINDOW)
            if sub == 0:
                kprev, vprev = kp_ref[...], vp_ref[...]
                kmid, vmid = km_ref[:WINDOW], vm_ref[:WINDOW]
                knext, vnext = km_ref[WINDOW:], vm_ref[WINDOW:]
                w_lo = jnp.where(j > 1, 0, WINDOW)
                w_hi = 3 * WINDOW
            else:
                kprev, vprev = km_ref[:WINDOW], vm_ref[:WINDOW]
                kmid, vmid = km_ref[WINDOW:], vm_ref[WINDOW:]
                knext, vnext = kn_ref[...], vn_ref[...]
                w_lo = 0
                w_hi = jnp.where(j < n_tiles - 1, 3 * WINDOW, 2 * WINDOW)
            mask = band & (w >= w_lo) & (w < w_hi)
            work = []
            for g in range(ATT_KV_HEADS):
                gs = slice(g * HEAD_DIM, (g + 1) * HEAD_DIM)
                k_win = jnp.concatenate([kprev[:, gs], kmid[:, gs], knext[:, gs]], axis=0)
                v_win = jnp.concatenate([vprev[:, gs], vmid[:, gs], vnext[:, gs]], axis=0)
                work.append((functools.partial(scores, rows, WINDOW, g, kc_ref[:, gs], k_win, mask),
                             functools.partial(finish, v_ctx=vc_ref[:, gs], v_win=v_win)))
            blocks.append((rows, WINDOW, work))
        run(blocks)


def _attention(proj, sink, ctx_len):
    B, T, _ = proj.shape
    assert ctx_len == SEQ_TILE and T % SEQ_TILE == 0
    n_tiles = T // SEQ_TILE
    n_win = T // WINDOW
    kcol, vcol = COL_K // KV_W, COL_V // KV_W
    per_tile = SEQ_TILE // WINDOW
    tile = lambda col: pl.BlockSpec((None, SEQ_TILE, KV_W), lambda b, j: (b, j, col))
    ctx = lambda col: pl.BlockSpec((None, SEQ_TILE, KV_W), lambda b, j: (b, 0, col))
    prev = lambda col: pl.BlockSpec((None, WINDOW, KV_W), lambda b, j: (b, jnp.maximum(j * per_tile - 1, 0), col))
    nxt = lambda col: pl.BlockSpec((None, WINDOW, KV_W), lambda b, j: (b, jnp.minimum((j + 1) * per_tile, n_win - 1), col))
    return pl.pallas_call(
        functools.partial(_attn_kernel, n_tiles=n_tiles),
        grid=(B, n_tiles),
        in_specs=[pl.BlockSpec(memory_space=pltpu.SMEM),
                  pl.BlockSpec((None, SEQ_TILE, ATT_W), lambda b, j: (b, j, 0)),
                  ctx(kcol), ctx(vcol), tile(kcol), tile(vcol), prev(kcol), prev(vcol), nxt(kcol), nxt(vcol)],
        out_specs=pl.BlockSpec((None, SEQ_TILE, ATT_W), lambda b, j: (b, j, 0)),
        out_shape=jax.ShapeDtypeStruct((B, T, ATT_W), BF16),
        compiler_params=_params(("parallel", "parallel")),
    )(sink, *([proj] * 9))


def _conv_kernel(b_ref, c_ref, x_ref, w_ref, o_ref, u_scr, *, ctx_len):
    T = c_ref.shape[0]
    pad = 8
    u = c_ref[...].astype(F32) * x_ref[...].astype(F32)
    zero_row = jnp.zeros((1, CONV_CH), F32)
    u_scr[pad - 1:pad, :] = zero_row
    u_scr[pad:pad + T, :] = u
    u_scr[pad + T:pad + T + 1, :] = zero_row
    row = lax.broadcasted_iota(jnp.int32, (T, 1), 0)
    u_prev = jnp.where(row == ctx_len, 0.0, u_scr[pad - 1:pad - 1 + T, :])
    u_next = jnp.where(row == ctx_len - 1, 0.0, u_scr[pad + 1:pad + 1 + T, :])
    y = w_ref[0:1, :] * u_prev + w_ref[1:2, :] * u + w_ref[2:3, :] * u_next
    o_ref[...] = (b_ref[...].astype(F32) * y).astype(BF16)


def _short_conv(proj, conv_w, ctx_len):
    B, T, _ = proj.shape
    col = lambda c0: pl.BlockSpec((None, T, CONV_CH), lambda b: (b, 0, c0 // CONV_CH))
    return pl.pallas_call(
        functools.partial(_conv_kernel, ctx_len=ctx_len),
        grid=(B,),
        in_specs=[col(COL_CB), col(COL_CC), col(COL_CX), pl.BlockSpec((3, CONV_CH), lambda b: (0, 0))],
        out_specs=pl.BlockSpec((None, T, CONV_CH), lambda b: (b, 0, 0)),
        out_shape=jax.ShapeDtypeStruct((B, T, CONV_CH), BF16),
        scratch_shapes=[pltpu.VMEM((T + 16, CONV_CH), F32)],
        compiler_params=_params(("parallel",)),
    )(proj, proj, proj, conv_w)


def _split2(v):
    hi = v.astype(BF16)
    lo = (v - hi.astype(F32)).astype(BF16)
    return hi, lo


def _gla_kernel(q_ref, k_ref, v_ref, gate_ref, lr_ref, w2_ref, bias_ref, ng_ref, o_ref,
                oacc_scr, qs_scr, u_scr, dec_scr, sin_scr, *, reverse, n_ctx_chunks):
    T = q_ref.shape[0]
    C = GLA_CHUNK
    n_tiles = T // SEQ_TILE
    n_chunks = T // C
    per_tile = SEQ_TILE // C
    nt_dims = (((1,), (1,)), ((), ()))
    tn_dims = (((0,), (0,)), ((), ()))

    r = lax.broadcasted_iota(jnp.int32, (SEQ_TILE, SEQ_TILE), 0)
    c = lax.broadcasted_iota(jnp.int32, (SEQ_TILE, SEQ_TILE), 1)
    same_chunk = (r // C) == (c // C)
    causal = same_chunk & ((c >= r) if reverse else (c <= r))
    cum_mat = jnp.where(causal, 1.0, 0.0).astype(BF16)
    same_head = (r // GLA_DK) == (c // GLA_DK)
    head_mean = jnp.where(same_head, 1.0 / GLA_DK, 0.0).astype(BF16)
    lane_head = lax.broadcasted_iota(jnp.int32, (C, GLA_W), 1) // GLA_DK

    def prepare(i):
        rows = pl.ds(pl.multiple_of(i * SEQ_TILE, SEQ_TILE), SEQ_TILE)
        z = jnp.dot(lr_ref[rows, :], w2_ref[...], preferred_element_type=F32) + bias_ref[...]
        la = (jnp.minimum(z, 0.0) - jnp.log(1.0 + jnp.exp(-jnp.abs(z)))) / GLA_TAU
        hi, lo = _split2(la)
        b = jnp.dot(cum_mat, hi, preferred_element_type=F32) + jnp.dot(cum_mat, lo, preferred_element_type=F32)
        b3 = b.reshape(per_tile, C, GLA_W)
        total = b3[:, 0:1, :] if reverse else b3[:, C - 1:C, :]
        b_last = jnp.broadcast_to(total, (per_tile, C, GLA_W)).reshape(SEQ_TILE, GLA_W)
        b_ref = 0.5 * b_last
        q = q_ref[rows, :].astype(F32)
        k = k_ref[rows, :].astype(F32)
        e_fwd = jnp.exp(b - b_ref)
        e_bwd = jnp.exp(b_ref - b)
        e_half = jnp.exp(b_ref)
        qe = (q * e_fwd).astype(BF16)
        ke = (k * e_bwd).astype(BF16)
        ku = (k * (e_bwd * e_half)).astype(BF16)
        qs_scr[rows, :] = (q * (e_fwd * e_half)).astype(BF16)
        return rows, qe, ke, ku, e_half * e_half

    def products(i, prepared):
        rows, qe, ke, ku, dec = prepared
        v = v_ref[rows, :]
        for h in range(GLA_HEADS):
            hs = slice(h * GLA_DK, (h + 1) * GLA_DK)
            s = lax.dot_general(qe[:, hs], ke[:, hs], nt_dims, preferred_element_type=F32)
            a = jnp.where(causal, s, 0.0).astype(BF16)
            oacc_scr[rows, hs] = jnp.dot(a, v[:, hs], preferred_element_type=F32)
        for ci in range(per_tile):
            cs = slice(ci * C, (ci + 1) * C)
            full = lax.dot_general(v[cs, :], ku[cs, :], tn_dims, preferred_element_type=F32)
            ut = full[0:C, :]
            for h in range(1, GLA_HEADS):
                ut = jnp.where(lane_head == h, full[h * C:(h + 1) * C, :], ut)
            u_scr[i * per_tile + ci] = ut
            dec_scr[i * per_tile + ci] = dec[ci * C:ci * C + 1, :]

    def phase1(first, count):
        pending = None
        for t in range(count + 1):
            nxt = (first + t, prepare(first + t)) if t < count else None
            if pending is not None:
                products(*pending)
            pending = nxt

    group = 9
    lax.fori_loop(0, n_tiles // group, lambda g, c: (phase1(g * group, group), c)[1], 0)
    if n_tiles % group:
        phase1((n_tiles // group) * group, n_tiles % group)


    def phase2(i, st):
        if reverse:
            ci = jnp.where(i < n_ctx_chunks, n_ctx_chunks - 1 - i, n_chunks - 1 + n_ctx_chunks - i)
        else:
            ci = i
        sin_scr[ci] = st.astype(BF16)
        return st * dec_scr[ci] + u_scr[ci]

    lax.fori_loop(0, n_chunks, phase2, jnp.zeros((C, GLA_W), F32))

    def carried_in(i):
        rows = pl.ds(pl.multiple_of(i * SEQ_TILE, SEQ_TILE), SEQ_TILE)
        qs = qs_scr[rows, :]
        parts = []
        for ci in range(per_tile):
            st = sin_scr[i * per_tile + ci]
            st_heads = jnp.where(same_head, jnp.concatenate([st] * GLA_HEADS, axis=0), jnp.zeros((), BF16))
            parts.append(lax.dot_general(qs[ci * C:(ci + 1) * C, :], st_heads, nt_dims, preferred_element_type=F32))
        return rows, jnp.concatenate(parts, axis=0)

    def finish(rows, o_inter):
        o = oacc_scr[rows, :] + o_inter
        ms = jnp.dot((o * o).astype(BF16), head_mean, preferred_element_type=F32)
        y = o * lax.rsqrt(ms + EPS) * ng_ref[...]
        o_ref[rows, :] = (y * _silu(gate_ref[rows, :].astype(F32))).astype(BF16)

    def phase3(first, count):
        pending = None
        for t in range(count + 1):
            nxt = carried_in(first + t) if t < count else None
            if pending is not None:
                finish(*pending)
            pending = nxt

    lax.fori_loop(0, n_tiles // group, lambda g, c: (phase3(g * group, group), c)[1], 0)
    if n_tiles % group:
        phase3((n_tiles // group) * group, n_tiles % group)


def _gla(proj, w2pad, bias, ng, ctx_len, reverse):
    B, T, _ = proj.shape
    assert T % SEQ_TILE == 0 and ctx_len % GLA_CHUNK == 0
    n_chunks = T // GLA_CHUNK
    col = lambda c0: pl.BlockSpec((None, T, GLA_W), lambda b: (b, 0, c0 // GLA_W))
    gate_col = COL_GGB if reverse else COL_GGF
    kern = functools.partial(_gla_kernel, reverse=reverse, n_ctx_chunks=ctx_len // GLA_CHUNK)
    return pl.pallas_call(
        kern,
        grid=(B,),
        in_specs=[col(COL_GQ), col(COL_GK), col(COL_GV), col(gate_col),
                  pl.BlockSpec((None, T, LANES), lambda b: (b, 0, COL_LR // LANES)),
                  pl.BlockSpec((LANES, GLA_W), lambda b: (0, 0)),
                  pl.BlockSpec((1, GLA_W), lambda b: (0, 0)),
                  pl.BlockSpec((1, GLA_W), lambda b: (0, 0))],
        out_specs=pl.BlockSpec((None, T, GLA_W), lambda b: (b, 0, 0)),
        out_shape=jax.ShapeDtypeStruct((B, T, GLA_W), BF16),
        scratch_shapes=[pltpu.VMEM((T, GLA_W), F32),
                        pltpu.VMEM((T, GLA_W), BF16),
                        pltpu.VMEM((n_chunks, GLA_CHUNK, GLA_W), F32),
                        pltpu.VMEM((n_chunks, 1, GLA_W), F32),
                        pltpu.VMEM((n_chunks, GLA_CHUNK, GLA_W), BF16)],
        compiler_params=_params(("parallel",)),
    )(proj, proj, proj, proj, proj, w2pad, bias, ng)


MXU_TILE = 256
FFN_CHUNK = 3 * MXU_TILE


def _swiglu_resident(h, wgu_ref, wd_ref):
    F = wd_ref.shape[0]
    y = None
    for c0 in range(0, F, FFN_CHUNK):
        c1 = min(c0 + FFN_CHUNK, F)
        g = jnp.dot(h, wgu_ref[:, c0:c1], preferred_element_type=F32)
        u = jnp.dot(h, wgu_ref[:, F + c0:F + c1], preferred_element_type=F32)
        part = jnp.dot((_silu(g) * u).astype(BF16), wd_ref[c0:c1, :], preferred_element_type=F32)
        y = part if y is None else y + part
    return y


def _outproj_kernel(*refs, tm, ctx_len, moe):
    if moe:
        (x_ref, att_ref, gf_ref, gb_ref, cv_ref, w_ref, g1_ref, cg1_ref, ng_ref, sc_ref, csc_ref, sh_ref, csh_ref,
         router_ref, earlier_ref, xo_ref, hp_ref, route_ref, cnt_ref) = refs
    else:
        (x_ref, att_ref, gf_ref, gb_ref, cv_ref, w_ref, g1_ref, cg1_ref, ng_ref, sc_ref, csc_ref, sh_ref, csh_ref,
         g2_ref, cg2_ref, wgu_ref, wd_ref, xo_ref) = refs
    is_ctx = _row_is_ctx(pl.program_id(1), tm, ctx_len)
    gla = (gf_ref[...].astype(F32) + gb_ref[...].astype(F32)).astype(BF16)
    y = jnp.dot(att_ref[...], w_ref[:ATT_W, :], preferred_element_type=F32)
    y += jnp.dot(gla, w_ref[ATT_W:ATT_W + GLA_W, :], preferred_element_type=F32)
    y += jnp.dot(cv_ref[...], w_ref[ATT_W + GLA_W:, :], preferred_element_type=F32)
    x = x_ref[...] + jnp.where(is_ctx, cg1_ref[...], g1_ref[...]) * y
    if moe:
        xo_ref[...] = x
    n = x * lax.rsqrt(jnp.mean(x * x, axis=-1, keepdims=True) + EPS) * ng_ref[...]
    scale = jnp.where(is_ctx, csc_ref[...], sc_ref[...])
    shift = jnp.where(is_ctx, csh_ref[...], sh_ref[...])
    h = n * (1.0 + scale) + shift
    hi = h.astype(BF16)
    if not moe:
        xo_ref[...] = x + jnp.where(is_ctx, cg2_ref[...], g2_ref[...]) * _swiglu_resident(hi, wgu_ref, wd_ref)
    else:
        _store_packed(hp_ref, _pack_bf16_pairs(h))
        logits = jnp.dot(hi, router_ref[...], preferred_element_type=F32)
        lane = lax.broadcasted_iota(jnp.int32, logits.shape, 1)
        neg = -jnp.inf
        lg = jnp.where(lane < N_EXPERTS, logits, neg)
        m1 = jnp.max(lg, axis=-1, keepdims=True)
        i1 = jnp.min(jnp.where(lg == m1, lane, LANES), axis=-1, keepdims=True)
        lg2 = jnp.where(lane == i1, neg, lg)
        m2 = jnp.max(lg2, axis=-1, keepdims=True)
        i2 = jnp.min(jnp.where(lg2 == m2, lane, LANES), axis=-1, keepdims=True)
        e2 = jnp.exp(m2 - m1)
        gate1 = 1.0 / (1.0 + e2)
        gate2 = e2 / (1.0 + e2)
        first = jnp.where(lane == i1, 1.0, 0.0)
        second = jnp.where(lane == i2, 1.0, 0.0)
        before = jnp.dot(earlier_ref[...], jnp.concatenate([first, second], axis=1).astype(BF16),
                         preferred_element_type=F32)
        n_first = jnp.sum(first, axis=0, keepdims=True)
        rank1 = jnp.sum(first * before[:, :LANES], axis=-1, keepdims=True)
        rank2 = jnp.sum(second * (before[:, LANES:] + n_first), axis=-1, keepdims=True)
        cnt_ref[...] = jnp.broadcast_to(n_first + jnp.sum(second, axis=0, keepdims=True), cnt_ref.shape)
        route_ref[...] = jnp.where(lane == 0, i1.astype(F32), jnp.where(lane == 1, i2.astype(F32),
                                   jnp.where(lane == 2, gate1, jnp.where(lane == 3, gate2,
                                             jnp.where(lane == 4, rank1, jnp.where(lane == 5, rank2, 0.0))))))


def _outproj(xt, att, gf, gb, cv, w_out, layer, ng, mod_l, mod_c, ctx_len, router=None, ffn=None):
    B, T, D = xt.shape
    tm = _pick_tile(T, 640)
    moe = router is not None
    assert moe != (ffn is not None)
    row = lambda w: pl.BlockSpec((None, tm, w), lambda b, j: (b, j, 0))
    const = lambda shape: pl.BlockSpec(shape, lambda b, j: (0,) * len(shape))
    in_specs = [row(D), row(ATT_W), row(GLA_W), row(GLA_W), row(CONV_CH),
                pl.BlockSpec((None, D, D), lambda b, j: (layer, 0, 0)),
                *_mod_specs(2, D), const((1, D)), *_mod_specs(4, D), *_mod_specs(3, D)]
    args = [xt, att, gf, gb, cv, w_out, mod_l, mod_c, ng, mod_l, mod_c, mod_l, mod_c]
    nt = T // tm
    if moe:
        in_specs += [pl.BlockSpec((None, D, LANES), lambda b, j: (router[1], 0, 0)), const((tm, tm))]
        args += [router[0], jnp.tri(tm, k=-1, dtype=BF16)]
        out_specs = [row(D), pl.BlockSpec((PACK_CHUNKS, tm, LANES), lambda b, j: (0, b * nt + j, 0)), row(LANES),
                     pl.BlockSpec((None, 8, LANES), lambda b, j: (b * nt + j, 0, 0))]
        out_shape = [jax.ShapeDtypeStruct((B, T, D), F32), jax.ShapeDtypeStruct((PACK_CHUNKS, B * T, LANES), jnp.int32),
                     jax.ShapeDtypeStruct((B, T, LANES), F32), jax.ShapeDtypeStruct((B * nt, 8, LANES), F32)]
    else:
        w_gu, w_down, idx = ffn
        F = w_down.shape[1]
        assert F % MXU_TILE == 0
        resident = lambda shape: pl.BlockSpec((None,) + shape, lambda b, j: (idx, 0, 0), pipeline_mode=pl.Buffered(1))
        in_specs += [*_mod_specs(5, D), resident((D, 2 * F)), resident((F, D))]
        args += [mod_l, mod_c, w_gu, w_down]
        out_specs = row(D)
        out_shape = jax.ShapeDtypeStruct((B, T, D), F32)
    return pl.pallas_call(
        functools.partial(_outproj_kernel, tm=tm, ctx_len=ctx_len, moe=moe),
        grid=(B, T // tm),
        in_specs=in_specs, out_specs=out_specs, out_shape=out_shape,
        compiler_params=_params(("parallel", "parallel")),
    )(*args)


def _sc_gather(table, idx):
    _, width = table.shape
    m = idx.shape[0]
    workers = SC_CORES * SC_SUBCORES
    ch = SC_INDEX_CHUNK
    assert width == LANES and m % (workers * ch) == 0
    n_chunks = m // (workers * ch)
    nb = max(b for b in (4, 3, 2, 1) if n_chunks % b == 0)
    mesh = plsc.VectorSubcoreMesh(core_axis_name="c", subcore_axis_name="s",
                                  num_cores=SC_CORES, num_subcores=SC_SUBCORES)

    @functools.partial(
        pl.kernel, mesh=mesh, out_type=jax.ShapeDtypeStruct((m, width), table.dtype),
        scratch_types=[pltpu.VMEM((n_chunks, ch), jnp.int32), pltpu.VMEM((nb, ch, width), table.dtype)]
        + [pltpu.SemaphoreType.DMA] * (2 * nb))
    def gather(table_hbm, idx_hbm, out_hbm, idx_v, rows_v, *sems):
        wid = lax.axis_index("s") * SC_CORES + lax.axis_index("c")
        base = wid * (n_chunks * ch)
        pltpu.sync_copy(idx_hbm.at[wid], idx_v)

        def fetch(c, b):
            return pltpu.make_async_copy(table_hbm.at[idx_v.at[c]], rows_v.at[b], sems[b])

        def flush(c, b):
            return pltpu.make_async_copy(rows_v.at[b], out_hbm.at[pl.ds(base + c * ch, ch)], sems[nb + b])

        @pl.loop(0, n_chunks, step=nb)
        def _(c0):
            for b in range(nb):
                @pl.when(c0 > 0)
                def _():
                    flush(c0 - nb + b, b).wait()
                fetch(c0 + b, b).start()
            for b in range(nb):
                fetch(c0 + b, b).wait()
                flush(c0 + b, b).start()

        for b in range(nb):
            flush(n_chunks - nb + b, b).wait()

    return gather(table, idx.reshape(workers, n_chunks, ch))


def _sc_dispatch(table, dest_a, dest_b, dest_pad, n_out):
    m, width = table.shape
    n_pad = dest_pad.shape[0]
    workers = SC_CORES * SC_SUBCORES
    ch = SC_INDEX_CHUNK
    assert width == LANES and m % (workers * ch) == 0 and n_pad % (workers * ch) == 0
    n_chunks = m // (workers * ch)
    pad_chunks = n_pad // (workers * ch)
    nb = max(b for b in (4, 3, 2, 1) if n_chunks % b == 0)
    mesh = plsc.VectorSubcoreMesh(core_axis_name="c", subcore_axis_name="s",
                                  num_cores=SC_CORES, num_subcores=SC_SUBCORES)

    @functools.partial(
        pl.kernel, mesh=mesh, out_type=jax.ShapeDtypeStruct((n_out, width), table.dtype),
        scratch_types=[pltpu.VMEM((n_chunks, ch), jnp.int32), pltpu.VMEM((n_chunks, ch), jnp.int32),
                       pltpu.VMEM((pad_chunks, ch), jnp.int32), pltpu.VMEM((nb, ch, width), table.dtype),
                       pltpu.VMEM((ch, width), table.dtype)]
        + [pltpu.SemaphoreType.DMA] * (3 * nb + 1))
    def dispatch(table_hbm, zeros_hbm, da_hbm, db_hbm, dp_hbm, out_hbm, da_v, db_v, dp_v, rows_v, zero_v, *sems):
        wid = lax.axis_index("s") * SC_CORES + lax.axis_index("c")
        base = wid * (n_chunks * ch)
        pltpu.sync_copy(da_hbm.at[wid], da_v)
        pltpu.sync_copy(db_hbm.at[wid], db_v)
        pltpu.sync_copy(dp_hbm.at[wid], dp_v)
        pltpu.sync_copy(zeros_hbm, zero_v)

        def fetch(c, b):
            return pltpu.make_async_copy(table_hbm.at[pl.ds(base + c * ch, ch)], rows_v.at[b], sems[b])

        def put_a(c, b):
            return pltpu.make_async_copy(rows_v.at[b], out_hbm.at[da_v.at[c]], sems[nb + b])

        def put_b(c, b):
            return pltpu.make_async_copy(rows_v.at[b], out_hbm.at[db_v.at[c]], sems[2 * nb + b])

        def put_zero(p):
            return pltpu.make_async_copy(zero_v, out_hbm.at[dp_v.at[p]], sems[3 * nb])

        for p in range(pad_chunks):
            put_zero(p).start()

        @pl.loop(0, n_chunks, step=nb)
        def _(c0):
            for b in range(nb):
                @pl.when(c0 > 0)
                def _():
                    put_a(c0 - nb + b, b).wait()
                    put_b(c0 - nb + b, b).wait()
                fetch(c0 + b, b).start()
            for b in range(nb):
                fetch(c0 + b, b).wait()
                put_a(c0 + b, b).start()
                put_b(c0 + b, b).start()

        for b in range(nb):
            put_a(n_chunks - nb + b, b).wait()
            put_b(n_chunks - nb + b, b).wait()
        for p in range(pad_chunks):
            put_zero(p).wait()

    shape3 = lambda v: v.reshape(workers, -1, ch)
    return dispatch(table, jnp.zeros((ch, width), table.dtype), shape3(dest_a), shape3(dest_b), shape3(dest_pad))


def _routing(route, tile_counts, tile):
    B, T, _ = route.shape
    n = B * T
    assert (2 * n) % tile == 0
    info = route.reshape(n, LANES)[:, :6].astype(jnp.int32)
    per_tile = tile_counts[:, 0, :N_EXPERTS].astype(jnp.int32)
    counts = jnp.sum(per_tile, axis=0)
    padded = (counts + tile - 1) // tile * tile
    ends = jnp.cumsum(padded)
    starts = ends - padded
    base = jnp.repeat(starts[None, :] + jnp.cumsum(per_tile, axis=0) - per_tile, n // per_tile.shape[0], axis=0)
    experts = jnp.arange(N_EXPERTS, dtype=jnp.int32)[None, :]
    row_of = lambda e, rank: jnp.sum(jnp.where(e[:, None] == experts, base, 0), axis=1) + rank
    dest = jnp.concatenate([row_of(info[:, 0], info[:, 4]), row_of(info[:, 1], info[:, 5])])
    n_pad = N_EXPERTS * tile
    n_rows = 2 * n + n_pad
    seg_first = jnp.concatenate([starts + counts, ends[-1:]])
    seg_size = jnp.concatenate([padded - counts, (n_rows - ends[-1])[None]])
    seg_end = jnp.cumsum(seg_size)
    j = jnp.arange(n_pad, dtype=jnp.int32)
    seg = jnp.sum((j[:, None] >= seg_end[None, :]).astype(jnp.int32), axis=1)
    pad_rows = (seg_first[seg] + j - (seg_end - seg_size)[seg]).astype(jnp.int32)
    tile_start = jnp.arange(n_rows // tile, dtype=jnp.int32) * tile
    tile_expert = jnp.minimum(jnp.sum((tile_start[:, None] >= ends[None, :]).astype(jnp.int32), axis=1), N_EXPERTS - 1)
    tile_valid = (tile_start < ends[-1]).astype(jnp.int32)
    return dest, pad_rows, n_rows, tile_expert, tile_valid


def _expert_kernel(te_ref, tv_ref, xs_ref, wgu_ref, wd_ref, ys_ref, wgu_scr):
    i = pl.program_id(0)
    F = wd_ref.shape[0]

    @pl.when((i == 0) | (te_ref[i] != te_ref[jnp.maximum(i - 1, 0)]))
    def _new_expert():
        wgu_scr[...] = wgu_ref[...].astype(BF16)

    @pl.when(tv_ref[i] > 0)
    def _compute():
        lo, hi = _unpack_bf16_pairs(_load_packed(xs_ref))
        gu = (jnp.dot(lo.astype(BF16), wgu_scr[:PACK_W, :], preferred_element_type=F32)
              + jnp.dot(hi.astype(BF16), wgu_scr[PACK_W:, :], preferred_element_type=F32))
        act = (_silu(gu[:, :F]) * gu[:, F:]).astype(BF16)
        y = jnp.dot(act, wd_ref[...], preferred_element_type=F32)
        _store_packed(ys_ref, _pack_bf16_pairs(y))

    @pl.when(tv_ref[i] == 0)
    def _unused_tile():
        ys_ref[...] = jnp.zeros_like(ys_ref)


def _experts(xs, tile_expert, tile_valid, w_gu, layer, w_down):
    _, n_rows, _ = xs.shape
    _, _, F, D = w_down.shape
    tile = EXPERT_TILE
    rows = pl.BlockSpec((PACK_CHUNKS, tile, LANES), lambda i, te, tv: (0, i, 0))
    return pl.pallas_call(
        _expert_kernel,
        grid_spec=pltpu.PrefetchScalarGridSpec(
            num_scalar_prefetch=2,
            grid=(n_rows // tile,),
            in_specs=[rows,
                      pl.BlockSpec((None, None, D, 2 * F), lambda i, te, tv: (layer, te[i], 0, 0)),
                      pl.BlockSpec((None, None, F, D), lambda i, te, tv: (0, te[i], 0, 0))],
            out_specs=rows,
            scratch_shapes=[pltpu.VMEM((D, 2 * F), BF16)]),
        out_shape=jax.ShapeDtypeStruct(xs.shape, jnp.int32),
        compiler_params=_params(("arbitrary",)),
    )(tile_expert, tile_valid, xs, w_gu, w_down)


def _combine_kernel(*refs, tm, ctx_len, skip, final):
    x_ref, g2_ref, cg2_ref, route_ref, ya_ref, yb_ref = refs[:6]
    fg_ref = refs[6] if final else None
    o_ref = refs[-1]
    is_ctx = _row_is_ctx(pl.program_id(1) + skip, tm, ctx_len)
    route = route_ref[...]
    lane = lax.broadcasted_iota(jnp.int32, route.shape, 1)
    gate_a = jnp.sum(jnp.where(lane == 2, route, 0.0), axis=-1, keepdims=True)
    gate_b = jnp.sum(jnp.where(lane == 3, route, 0.0), axis=-1, keepdims=True)
    unpack = lambda ref: jnp.concatenate(_unpack_bf16_pairs(_load_packed(ref)), axis=1)
    f = gate_a * unpack(ya_ref) + gate_b * unpack(yb_ref)
    x = x_ref[...] + jnp.where(is_ctx, cg2_ref[...], g2_ref[...]) * f
    if final:
        x = x * lax.rsqrt(jnp.mean(x * x, axis=-1, keepdims=True) + EPS) * fg_ref[...]
    o_ref[...] = x


def _combine(xt, mod_l, mod_c, route, yg, ctx_len, final_g=None):
    B, T, D = xt.shape
    final = final_g is not None
    if final:
        tm = SEQ_TILE
        assert ctx_len % tm == 0 and T % tm == 0
        skip = ctx_len // tm
    else:
        tm = _pick_tile(T, 640)
        skip = 0
    nt = T // tm
    row = lambda w: pl.BlockSpec((None, tm, w), lambda b, j: (b, j + skip, 0))
    slot = lambda s: pl.BlockSpec((PACK_CHUNKS, None, tm, LANES), lambda b, j: (0, s, b * nt + j + skip, 0))
    in_specs = [row(D), *_mod_specs(5, D), row(LANES), slot(0), slot(1)]
    args = [xt, mod_l, mod_c, route, yg, yg]
    if final:
        in_specs.append(pl.BlockSpec((1, D), lambda b, j: (0, 0)))
        args.append(final_g)
    return pl.pallas_call(
        functools.partial(_combine_kernel, tm=tm, ctx_len=ctx_len, skip=skip, final=final),
        grid=(B, nt - skip),
        in_specs=in_specs,
        out_specs=pl.BlockSpec((None, tm, D), lambda b, j: (b, j, 0)),
        out_shape=jax.ShapeDtypeStruct((B, T - skip * tm, D), F32),
        compiler_params=_params(("parallel", "parallel")),
    )(*args)


def _moe(hp, route, tile_counts, xt, mod_l, mod_c, w_gu, layer, w_down, ctx_len, final_g=None):
    B, T, _ = xt.shape
    n = B * T
    dest, pad_rows, n_rows, tile_expert, tile_valid = _routing(route, tile_counts, EXPERT_TILE)
    chunk = jnp.arange(PACK_CHUNKS, dtype=jnp.int32)[:, None] * n_rows
    xs = _sc_dispatch(hp.reshape(PACK_CHUNKS * n, LANES), (chunk + dest[None, :n]).reshape(-1),
                      (chunk + dest[None, n:]).reshape(-1), (chunk + pad_rows[None, :]).reshape(-1),
                      PACK_CHUNKS * n_rows)
    ys = _experts(xs.reshape(PACK_CHUNKS, n_rows, LANES), tile_expert, tile_valid, w_gu, layer, w_down)
    ys = ys.reshape(PACK_CHUNKS * n_rows, LANES)
    yg = _sc_gather(ys, (chunk + dest[None, :]).reshape(-1)).reshape(PACK_CHUNKS, 2, n, LANES)
    return _combine(xt, mod_l, mod_c, route, yg, ctx_len, final_g)


def _final_norm_kernel(x_ref, g_ref, o_ref):
    x = x_ref[...]
    o_ref[...] = x * lax.rsqrt(jnp.mean(x * x, axis=-1, keepdims=True) + EPS) * g_ref[...]


def _final_norm(xt, g, ctx_len):
    B, T, D = xt.shape
    S = T - ctx_len
    tm = SEQ_TILE
    assert ctx_len % tm == 0 and S % tm == 0
    skip = ctx_len // tm
    return pl.pallas_call(
        _final_norm_kernel,
        grid=(B, S // tm),
        in_specs=[pl.BlockSpec((None, tm, D), lambda b, j: (b, j + skip, 0)),
                  pl.BlockSpec((1, D), lambda b, j: (0, 0))],
        out_specs=pl.BlockSpec((None, tm, D), lambda b, j: (b, j, 0)),
        out_shape=jax.ShapeDtypeStruct((B, S, D), F32),
        compiler_params=_params(("parallel", "parallel")),
    )(xt, g)


def _rope_tables(seq, ctx_len):
    pos = jnp.arange(seq, dtype=jnp.int32)
    nf = HEAD_DIM // 4
    inv = ROPE_THETA ** (-jnp.arange(nf, dtype=F32) / nf)
    ang = jnp.concatenate([(pos // GRID_W).astype(F32)[:, None] * inv, (pos % GRID_W).astype(F32)[:, None] * inv], axis=-1)
    cos, sin = jnp.cos(ang), jnp.sin(ang)
    reps = LANES // HEAD_DIM
    cos_t = jnp.tile(jnp.concatenate([cos, cos], axis=-1), (1, reps))
    sin_t = jnp.tile(jnp.concatenate([-sin, sin], axis=-1), (1, reps))
    cos_t = jnp.concatenate([jnp.ones((ctx_len, LANES), F32), cos_t], axis=0)
    sin_t = jnp.concatenate([jnp.zeros((ctx_len, LANES), F32), sin_t], axis=0)
    return cos_t, sin_t


def _relayout_w_in(w_in):
    o_lr = ATT_W + 2 * KV_W + 5 * GLA_W
    q_scale = HEAD_DIM ** -0.5 * LOG2E
    gq_scale = GLA_DK ** -0.5
    parts = [w_in[..., :ATT_W] * q_scale, w_in[..., ATT_W:COL_GQ], w_in[..., COL_GQ:COL_GK] * gq_scale,
             w_in[..., COL_GK:o_lr], w_in[..., o_lr + 2 * GLA_LOWRANK:], w_in[..., o_lr:o_lr + 2 * GLA_LOWRANK],
             jnp.zeros(w_in.shape[:-1] + (LANES - 2 * GLA_LOWRANK,), w_in.dtype)]
    return jnp.concatenate(parts, axis=-1).astype(BF16)


def kernel(x, c, ctx, c_ctx, w_mod, b_mod, norm1_g, norm2_g, w_in, w_out, attn_sink, gla_w2_f, gla_b_f, gla_w2_b, gla_b_b, gla_norm_g, conv_w, ffn_w_gu, ffn_w_down, router_w, expert_w_gu, expert_w_down, final_norm_g):
    B, S, D = x.shape
    L = ctx.shape[1]
    depth = w_in.shape[0]
    assert D == D_MODEL and S % GRID_W == 0

    xt = jnp.concatenate([ctx, x], axis=1)
    rows = 16
    cc = jnp.concatenate([c, c_ctx[None, :], jnp.zeros((rows - B - 1, D), F32)], axis=0)
    mod = _adaln(cc, w_mod, b_mod)
    cos_t, sin_t = _rope_tables(S, L)

    w_in_p = _relayout_w_in(w_in)
    w_out_b = _to_bf16(w_out)
    zpad = jnp.zeros((depth, LANES - 2 * GLA_LOWRANK, GLA_W), F32)
    w2_f = jnp.concatenate([gla_w2_f, jnp.zeros_like(gla_w2_b), zpad], axis=1).astype(BF16)
    w2_b = jnp.concatenate([jnp.zeros_like(gla_w2_f), gla_w2_b, zpad], axis=1).astype(BF16)
    gla_ng = jnp.tile(gla_norm_g, (1, GLA_HEADS))
    dense_w = {0: (_to_bf16(ffn_w_gu, lead=0), _to_bf16(ffn_w_down, lead=0))}
    router_b = jnp.pad(router_w, ((0, 0), (0, 0), (0, LANES - N_EXPERTS))).astype(BF16)

    for l in range(depth):
        mod_l = mod[l, :B].reshape(B, 1, 6 * D)
        mod_c = mod[l, B:B + 1]
        proj = _inproj(xt, norm1_g[l][None, :], mod_l, mod_c, cos_t, sin_t, w_in_p, l, L)
        att = _attention(proj, attn_sink[l], L)
        cv = _short_conv(proj, conv_w[l], L)
        gf = _gla(proj, w2_f[l], gla_b_f[l][None, :], gla_ng[l][None, :], L, reverse=False)
        gb = _gla(proj, w2_b[l], gla_b_b[l][None, :], gla_ng[l][None, :], L, reverse=True)
        if l % 2 == 0:
            xt = _outproj(xt, att, gf, gb, cv, w_out_b, l, norm2_g[l][None, :], mod_l, mod_c, L,
                          ffn=(*dense_w.pop(l), 0))
        else:
            xt, hp, route, tile_counts = _outproj(xt, att, gf, gb, cv, w_out_b, l, norm2_g[l][None, :], mod_l, mod_c, L,
                                     router=(router_b, l // 2))
            exp_down_b = _to_bf16(expert_w_down, lead=l // 2, after=route)
            if l + 1 < depth:
                dense_w[l + 1] = (_to_bf16(ffn_w_gu, lead=(l + 1) // 2, after=route),
                                  _to_bf16(ffn_w_down, lead=(l + 1) // 2, after=route))
            xt = _moe(hp, route, tile_counts, xt, mod_l, mod_c, expert_w_gu, l // 2, exp_down_b, L,
                      final_g=final_norm_g[None, :] if l == depth - 1 else None)
    return xt if depth % 2 == 0 else _final_norm(xt, final_norm_g[None, :], L)
```
